```python
import math
import jax, jax.numpy as jnp
from jax import lax
import numpy as np

D_MODEL = 1024
BATCH = 8
SEQ = 4096
DEPTH = 4

CHUNK = 64
N_MIXERS = 2
N_A_LAYERS = (DEPTH + N_MIXERS - 1) // N_MIXERS
N_B_LAYERS = DEPTH // N_MIXERS

A_BLOCK = 128
A_DFF = 2 * D_MODEL
A_GROUPS = 8
A_GROUP_DIM = A_DFF // A_GROUPS

SSM_EXPAND = 2
D_INNER = SSM_EXPAND * D_MODEL
SSM_HEAD_DIM = 64
SSM_HEADS = D_INNER // SSM_HEAD_DIM
SSM_GROUPS = 4
SSM_HEADS_PER_GROUP = SSM_HEADS // SSM_GROUPS
SSM_STATE = 128
SSM_CONV = 4
SSM_CHUNK = CHUNK
CONV_DIM = D_INNER + 2 * SSM_GROUPS * SSM_STATE
IN_PROJ_B = 2 * D_INNER + 2 * SSM_GROUPS * SSM_STATE + SSM_HEADS

N_EXPERTS = 32
TOP_K = 4
D_FF_EXPERT = D_MODEL
SWIGLU_LIMIT = 7.0
SWIGLU_ALPHA = 1.702
MOE_BLOCK = 256

DN_ALPHA = (2 * DEPTH) ** 0.25
DN_BETA = (8 * DEPTH) ** -0.25
LN_EPS = 1e-5
RMS_EPS = 1e-5

kernel_name = "hybrid_gmlp_ssd_moe_deepnorm"


def layernorm(x, g, b):
    xf = x.astype(jnp.float32)
    mu = jnp.mean(xf, axis=-1, keepdims=True)
    var = jnp.mean(jnp.square(xf - mu), axis=-1, keepdims=True)
    return ((xf - mu) * lax.rsqrt(var + LN_EPS) * g.astype(jnp.float32) + b.astype(jnp.float32)).astype(x.dtype)


def mixer_a(x, w_in, b_in, ln_g, ln_b, w_s, b_s, w_out, b_out):
    bsz, s_len, _ = x.shape
    h = jax.nn.gelu(x @ w_in + b_in, approximate=False)
    u, v = jnp.split(h, 2, axis=-1)
    v = layernorm(v, ln_g, ln_b)
    nb = s_len // A_BLOCK
    v = v.reshape(bsz, nb, A_BLOCK, A_GROUPS, A_GROUP_DIM)
    pos = jnp.arange(A_BLOCK)
    mask = (pos[None, :] // CHUNK) <= (pos[:, None] // CHUNK)
    w_masked = jnp.where(mask[None], w_s, 0.0)
    sv = jnp.einsum('gij,bnjgd->bnigd', w_masked, v) + b_s.T[None, None, :, :, None]
    out = u * sv.reshape(bsz, s_len, A_DFF)
    return out @ w_out + b_out


def ssd(xs, dt, a, bm, cm):
    bsz, s_len = xs.shape[:2]
    nc = s_len // SSM_CHUNK
    L, G, R, P, N = SSM_CHUNK, SSM_GROUPS, SSM_HEADS_PER_GROUP, SSM_HEAD_DIM, SSM_STATE
    x = xs.reshape(bsz, nc, L, G, R, P)
    bc = bm.reshape(bsz, nc, L, G, N)
    cc = cm.reshape(bsz, nc, L, G, N)
    dt = dt.reshape(bsz, nc, L, G, R)
    xdt = x * dt[..., None]
    adt = jnp.moveaxis(dt * a.reshape(G, R), 2, -1)
    acum = jnp.cumsum(adt, axis=-1)
    tril = jnp.tril(jnp.ones((L, L), dtype=bool))
    seg = acum[..., :, None] - acum[..., None, :]
    decay = jnp.exp(jnp.where(tril, seg, -jnp.inf))
    cb = jnp.einsum('bclgn,bcsgn->bcgls', cc, bc)
    m = cb[:, :, :, None] * decay
    y_diag = jnp.einsum('bcgrls,bcsgrp->bclgrp', m, xdt)

    def step(state, inp):
        c_c, b_c, x_c, acum_c = inp
        y_off = jnp.einsum('blgn,bgrpn->blgrp', c_c, state) * jnp.moveaxis(jnp.exp(acum_c), -1, 1)[..., None]
        w = jnp.exp(acum_c[..., -1:] - acum_c)
        contrib = jnp.einsum('blgn,blgrp->bgrpn', b_c, x_c * jnp.moveaxis(w, -1, 1)[..., None])
        new_state = state * jnp.exp(acum_c[..., -1])[..., None, None] + contrib
        return new_state, y_off

    scan_in = (jnp.moveaxis(cc, 1, 0), jnp.moveaxis(bc, 1, 0), jnp.moveaxis(xdt, 1, 0), jnp.moveaxis(acum, 1, 0))
    state0 = jnp.zeros((bsz, G, R, P, N), jnp.float32)
    _, y_off = lax.scan(step, state0, scan_in)
    y_off = jnp.moveaxis(y_off, 0, 1)
    return (y_diag + y_off).reshape(bsz, s_len, SSM_HEADS, SSM_HEAD_DIM)


def mixer_b(x, w_in, conv_w, conv_b, dt_bias, a_log, d_skip, norm_w, w_out):
    bsz, s_len, _ = x.shape
    zxbcdt = x @ w_in
    z, xbc, dt = jnp.split(zxbcdt, [D_INNER, D_INNER + CONV_DIM], axis=-1)
    xbc = lax.conv_general_dilated(xbc, conv_w[:, None, :], window_strides=(1,),
                                   padding=[(SSM_CONV - 1, 0)],
                                   dimension_numbers=('NWC', 'WIO', 'NWC'),
                                   feature_group_count=CONV_DIM)
    xbc = jax.nn.silu(xbc + conv_b)
    xs, bm, cm = jnp.split(xbc, [D_INNER, D_INNER + SSM_GROUPS * SSM_STATE], axis=-1)
    dt = jax.nn.softplus((dt + dt_bias).astype(jnp.float32))
    a = -jnp.exp(a_log.astype(jnp.float32))
    xs32 = xs.astype(jnp.float32)
    y = ssd(xs32, dt, a, bm.astype(jnp.float32), cm.astype(jnp.float32))
    y = y + d_skip.astype(jnp.float32)[:, None] * xs32.reshape(bsz, s_len, SSM_HEADS, SSM_HEAD_DIM)
    y = y.reshape(bsz, s_len, D_INNER) * jax.nn.silu(z.astype(jnp.float32))
    yg = y.reshape(bsz, s_len, SSM_GROUPS, D_INNER // SSM_GROUPS)
    yg = yg * lax.rsqrt(jnp.mean(jnp.square(yg), axis=-1, keepdims=True) + RMS_EPS)
    y = yg.reshape(bsz, s_len, D_INNER) * norm_w.astype(jnp.float32)
    return y.astype(x.dtype) @ w_out


def clamped_swiglu(h):
    g, lin = jnp.split(h, 2, axis=-1)
    g = jnp.minimum(g, SWIGLU_LIMIT)
    lin = jnp.clip(lin, -SWIGLU_LIMIT, SWIGLU_LIMIT)
    return g * jax.nn.sigmoid(SWIGLU_ALPHA * g) * (lin + 1.0)


def moe(x, w_r, b_r, w1, b1, w2, b2):
    bsz, s_len, d = x.shape
    n_tok = bsz * s_len
    xt = x.reshape(n_tok, d)
    logits = (xt @ w_r + b_r).astype(jnp.float32)
    top_v, top_i = lax.top_k(logits, TOP_K)
    gates = jax.nn.softmax(top_v, axis=-1)
    n_assign = n_tok * TOP_K
    e_flat = top_i.reshape(-1).astype(jnp.int32)
    tok_flat = jnp.arange(n_assign, dtype=jnp.int32) // TOP_K
    gate_flat = gates.reshape(-1)
    order = jnp.argsort(e_flat)
    e_sorted = e_flat[order]
    counts = jnp.zeros((N_EXPERTS,), jnp.int32).at[e_flat].add(1)
    padded = (counts + MOE_BLOCK - 1) // MOE_BLOCK * MOE_BLOCK
    ustart = jnp.cumsum(counts) - counts
    pend = jnp.cumsum(padded)
    pstart = pend - padded
    dest = pstart[e_sorted] + jnp.arange(n_assign, dtype=jnp.int32) - ustart[e_sorted]
    n_blocks = -(-n_assign // MOE_BLOCK) + N_EXPERTS
    n_rows = n_blocks * MOE_BLOCK
    row_tok = jnp.zeros((n_rows,), jnp.int32).at[dest].set(tok_flat[order])
    row_gate = jnp.zeros((n_rows,), jnp.float32).at[dest].set(gate_flat[order])
    block_start = jnp.arange(n_blocks, dtype=jnp.int32) * MOE_BLOCK
    block_expert = jnp.minimum(jnp.searchsorted(pend, block_start, side='right'), N_EXPERTS - 1)
    xb = xt[row_tok].reshape(n_blocks, MOE_BLOCK, d)

    def expert_block(args):
        xblk, e = args
        h = xblk @ w1[e] + b1[e]
        return clamped_swiglu(h) @ w2[e] + b2[e]

    yb = lax.map(expert_block, (xb, block_expert)).reshape(n_rows, d)
    y = jnp.zeros((n_tok, d), x.dtype).at[row_tok].add(yb * row_gate[:, None].astype(x.dtype))
    return y.reshape(bsz, s_len, d)


def setup_inputs(seed: int = 0) -> dict:
    key = jax.random.key(seed)
    ks = jax.random.split(key, 28)

    def nrm(k, shape, scale):
        return jax.random.normal(k, shape, jnp.float32) * scale

    dt0 = jnp.exp(jax.random.uniform(ks[12], (N_B_LAYERS, SSM_HEADS), jnp.float32)
                  * (math.log(0.1) - math.log(0.001)) + math.log(0.001))
    dt_bias = dt0 + jnp.log(-jnp.expm1(-dt0))
    return {
        "x": nrm(ks[0], (BATCH, SEQ, D_MODEL), 1.0),
        "a_w_in": nrm(ks[1], (N_A_LAYERS, D_MODEL, 2 * A_DFF), D_MODEL ** -0.5),
        "a_b_in": nrm(ks[2], (N_A_LAYERS, 2 * A_DFF), 0.02),
        "a_ln_g": 1.0 + nrm(ks[3], (N_A_LAYERS, A_DFF), 0.02),
        "a_ln_b": nrm(ks[4], (N_A_LAYERS, A_DFF), 0.02),
        "a_w_s": nrm(ks[5], (N_A_LAYERS, A_GROUPS, A_BLOCK, A_BLOCK), A_BLOCK ** -0.5),
        "a_b_s": 1.0 + nrm(ks[6], (N_A_LAYERS, A_GROUPS, A_BLOCK), 0.02),
        "a_w_out": nrm(ks[7], (N_A_LAYERS, A_DFF, D_MODEL), DN_BETA * A_DFF ** -0.5),
        "a_b_out": nrm(ks[8], (N_A_LAYERS, D_MODEL), 0.02),
        "b_w_in": nrm(ks[9], (N_B_LAYERS, D_MODEL, IN_PROJ_B), D_MODEL ** -0.5),
        "b_conv_w": nrm(ks[10], (N_B_LAYERS, SSM_CONV, CONV_DIM), SSM_CONV ** -0.5),
        "b_conv_b": nrm(ks[11], (N_B_LAYERS, CONV_DIM), 0.02),
        "b_dt_bias": dt_bias,
        "b_a_log": jnp.log(jax.random.uniform(ks[13], (N_B_LAYERS, SSM_HEADS), jnp.float32, 1.0, 16.0)),
        "b_d": 1.0 + nrm(ks[14], (N_B_LAYERS, SSM_HEADS), 0.02),
        "b_norm_w": 1.0 + nrm(ks[15], (N_B_LAYERS, D_INNER), 0.02),
        "b_w_out": nrm(ks[16], (N_B_LAYERS, D_INNER, D_MODEL), DN_BETA * D_INNER ** -0.5),
        "moe_w_router": nrm(ks[17], (DEPTH, D_MODEL, N_EXPERTS), D_MODEL ** -0.5),
        "moe_b_router": nrm(ks[18], (DEPTH, N_EXPERTS), 0.01),
        "moe_w1": nrm(ks[19], (DEPTH, N_EXPERTS, D_MODEL, 2 * D_FF_EXPERT), D_MODEL ** -0.5),
        "moe_b1": nrm(ks[20], (DEPTH, N_EXPERTS, 2 * D_FF_EXPERT), 0.02),
        "moe_w2": nrm(ks[21], (DEPTH, N_EXPERTS, D_FF_EXPERT, D_MODEL), DN_BETA * D_FF_EXPERT ** -0.5),
        "moe_b2": nrm(ks[22], (DEPTH, N_EXPERTS, D_MODEL), 0.02),
        "ln1_g": 1.0 + nrm(ks[23], (DEPTH, D_MODEL), 0.02),
        "ln1_b": nrm(ks[24], (DEPTH, D_MODEL), 0.02),
        "ln2_g": 1.0 + nrm(ks[25], (DEPTH, D_MODEL), 0.02),
        "ln2_b": nrm(ks[26], (DEPTH, D_MODEL), 0.02),
    }


def reference(x, a_w_in, a_b_in, a_ln_g, a_ln_b, a_w_s, a_b_s, a_w_out, a_b_out,
              b_w_in, b_conv_w, b_conv_b, b_dt_bias, b_a_log, b_d, b_norm_w, b_w_out,
              moe_w_router, moe_b_router, moe_w1, moe_b1, moe_w2, moe_b2,
              ln1_g, ln1_b, ln2_g, ln2_b):
    for i in range(DEPTH):
        j = i // N_MIXERS
        if i % N_MIXERS == 0:
            m = mixer_a(x, a_w_in[j], a_b_in[j], a_ln_g[j], a_ln_b[j], a_w_s[j], a_b_s[j],
                        a_w_out[j], a_b_out[j])
        else:
            m = mixer_b(x, b_w_in[j], b_conv_w[j], b_conv_b[j], b_dt_bias[j], b_a_log[j], b_d[j],
                        b_norm_w[j], b_w_out[j])
        x = layernorm(DN_ALPHA * x + m, ln1_g[i], ln1_b[i])
        f = moe(x, moe_w_router[i], moe_b_router[i], moe_w1[i], moe_b1[i], moe_w2[i], moe_b2[i])
        x = layernorm(DN_ALPHA * x + f, ln2_g[i], ln2_b[i])
    return x
```

```python
import functools
import math

import jax
import jax.numpy as jnp
from jax import lax
from jax.experimental import pallas as pl
from jax.experimental.pallas import tpu as pltpu
from jax.experimental.pallas import tpu_sc as plsc

F32 = jnp.float32
BF16 = jnp.bfloat16

DEPTH = 4
N_EXPERTS = 32
TOP_K = 4
MOE_BLOCK = 256
SWIGLU_LIMIT = 7.0
SWIGLU_ALPHA = 1.702
A_BLOCK = 128
A_GROUPS = 8
CHUNK = 64
SSM_HEADS = 32
SSM_HEAD_DIM = 64
SSM_GROUPS = 4
SSM_STATE = 128
SSM_CONV = 4
DN_ALPHA = (2 * DEPTH) ** 0.25
LN_EPS = 1e-5
RMS_EPS = 1e-5

LANES = 128
TOKEN_TILE = 256
VMEM_LIMIT = 56 * 1024 * 1024
SC_CHUNK = 64
NEG_BIG = -1e30


def _ln(x, g, b):
    mu = jnp.mean(x, axis=-1, keepdims=True)
    xc = x - mu
    var = jnp.mean(xc * xc, axis=-1, keepdims=True)
    return xc * lax.rsqrt(var + LN_EPS) * g + b


def _gelu(x):
    return 0.5 * x * (1.0 + lax.erf(x * (1.0 / math.sqrt(2.0))))


def _route(x1, wr_ref, br_ref, ri_ref, rg_ref):
    logits = jnp.dot(x1, wr_ref[...], preferred_element_type=F32,
                     precision=lax.Precision.HIGHEST) + br_ref[...]
    lane = lax.broadcasted_iota(jnp.int32, logits.shape, 1)
    lane_f = lane.astype(F32)
    vals, idxs = [], []
    l = logits
    for _ in range(TOP_K):
        m = jnp.max(l, axis=-1, keepdims=True)
        i = jnp.min(jnp.where(l == m, lane_f, float(LANES)), axis=-1, keepdims=True)
        vals.append(m)
        idxs.append(i)
        l = jnp.where(lane_f == i, -jnp.inf, l)
    es = [jnp.exp(v - vals[0]) for v in vals]
    tot = es[0] + es[1] + es[2] + es[3]
    ri = jnp.zeros(logits.shape, F32)
    rg = jnp.zeros(logits.shape, F32)
    for k in range(TOP_K):
        ri = jnp.where(lane == k, idxs[k], ri)
        rg = jnp.where(lane == k, es[k] / tot, rg)
    ri_ref[...] = ri.astype(jnp.int32)
    rg_ref[...] = rg


def _mixer_a_kernel(x_ref, wu_ref, wv_ref, bu_ref, bv_ref, lg_ref, lb_ref, ws_ref, bst_ref,
                    wo_ref, bo_ref, g1_ref, b1_ref, wr_ref, br_ref,
                    x1_ref, ri_ref, rg_ref, o_scr):
    x = x_ref[...]
    xb = x.astype(BF16)
    u = _gelu(jnp.dot(xb, wu_ref[...], preferred_element_type=F32) + bu_ref[...])
    v = _gelu(jnp.dot(xb, wv_ref[...], preferred_element_type=F32) + bv_ref[...])
    vb = _ln(v, lg_ref[...], lb_ref[...]).astype(BF16)
    tm, dff = u.shape
    gd = dff // A_GROUPS
    pi = lax.broadcasted_iota(jnp.int32, (A_BLOCK, A_BLOCK), 0) // CHUNK
    pj = lax.broadcasted_iota(jnp.int32, (A_BLOCK, A_BLOCK), 1) // CHUNK
    mask = pj <= pi
    bst = bst_ref[...]
    for g in range(A_GROUPS):
        wm = jnp.where(mask, ws_ref[g], 0.0).astype(BF16)
        for n in range(tm // A_BLOCK):
            rows = slice(n * A_BLOCK, (n + 1) * A_BLOCK)
            cols = slice(g * gd, (g + 1) * gd)
            sv = jnp.dot(wm, vb[rows, cols], preferred_element_type=F32) + bst[:, g:g + 1]
            o_scr[rows, cols] = (u[rows, cols] * sv).astype(BF16)
    m = jnp.dot(o_scr[...], wo_ref[...], preferred_element_type=F32) + bo_ref[...]
    x1 = _ln(DN_ALPHA * x + m, g1_ref[...], b1_ref[...])
    x1_ref[...] = x1
    _route(x1, wr_ref, br_ref, ri_ref, rg_ref)


def _const_spec(shape):
    nd = len(shape)
    return pl.BlockSpec(shape, lambda *_: (0,) * nd)


def mixer_a_block(xt, w_in, b_in, ln_g, ln_b, w_s, b_s, w_out, b_out, g1, b1, w_r, b_r):
    t, d = xt.shape
    dff = w_out.shape[0]
    tm = TOKEN_TILE
    wu = w_in[:, :dff].astype(BF16)
    wv = w_in[:, dff:].astype(BF16)
    bu = b_in[:dff].reshape(1, dff)
    bv = b_in[dff:].reshape(1, dff)
    args = (xt, wu, wv, bu, bv, ln_g.reshape(1, dff), ln_b.reshape(1, dff), w_s, b_s.T,
            w_out.astype(BF16), b_out.reshape(1, d), g1.reshape(1, d), b1.reshape(1, d), w_r, b_r)
    in_specs = [pl.BlockSpec((tm, d), lambda i: (i, 0))] + [_const_spec(a.shape) for a in args[1:]]
    out_shape = (jax.ShapeDtypeStruct((t, d), F32),
                 jax.ShapeDtypeStruct((t, LANES), jnp.int32),
                 jax.ShapeDtypeStruct((t, LANES), F32))
    out_specs = (pl.BlockSpec((tm, d), lambda i: (i, 0)),
                 pl.BlockSpec((tm, LANES), lambda i: (i, 0)),
                 pl.BlockSpec((tm, LANES), lambda i: (i, 0)))
    return pl.pallas_call(
        _mixer_a_kernel,
        out_shape=out_shape,
        grid=(t // tm,),
        in_specs=in_specs,
        out_specs=out_specs,
        scratch_shapes=[pltpu.VMEM((tm, dff), BF16)],
        compiler_params=pltpu.CompilerParams(
            dimension_semantics=("arbitrary",), vmem_limit_bytes=VMEM_LIMIT),
        name="mixer_a",
    )(*args)


def _mamba_in_kernel(x_ref, wz_ref, wx_ref, wdt_ref, cw_ref, cb_ref, dtb_ref,
                     z_ref, xs_ref, bm_ref, cm_ref, dt_ref, ext_scr):
    s = pl.program_id(1)
    tm = x_ref.shape[0]
    d_inner = xs_ref.shape[1]
    gn = bm_ref.shape[1]
    xb = x_ref[...].astype(BF16)
    z_ref[...] = jnp.dot(xb, wz_ref[...], preferred_element_type=F32).astype(BF16)
    xbc = jnp.dot(xb, wx_ref[...], preferred_element_type=F32)
    dt = jnp.dot(xb, wdt_ref[...], preferred_element_type=F32) + dtb_ref[...]
    dt_ref[...] = jnp.maximum(dt, 0.0) + jnp.log1p(jnp.exp(-jnp.abs(dt)))

    @pl.when(s == 0)
    def _():
        ext_scr[0:8, :] = jnp.zeros((8, ext_scr.shape[1]), F32)

    @pl.when(s > 0)
    def _():
        ext_scr[0:8, :] = ext_scr[tm:tm + 8, :]

    ext_scr[8:8 + tm, :] = xbc
    cw = cw_ref[...]
    acc = cb_ref[...] + cw[0:1, :] * ext_scr[5:5 + tm, :]
    for j in range(1, SSM_CONV):
        acc = acc + cw[j:j + 1, :] * ext_scr[5 + j:5 + j + tm, :]
    act = acc * jax.nn.sigmoid(acc)
    xs_ref[...] = act[:, :d_inner].astype(BF16)
    bm_ref[...] = act[:, d_inner:d_inner + gn].astype(BF16)
    cm_ref[...] = act[:, d_inner + gn:].astype(BF16)


def mamba_in_block(x3, w_in, conv_w, conv_b, dt_bias):
    bsz, s_len, d = x3.shape
    d_inner = SSM_HEADS * SSM_HEAD_DIM
    gn = SSM_GROUPS * SSM_STATE
    conv_dim = d_inner + 2 * gn
    tm = TOKEN_TILE
    wz = w_in[:, :d_inner].astype(BF16)
    wx = w_in[:, d_inner:d_inner + conv_dim].astype(BF16)
    wdt = jnp.pad(w_in[:, d_inner + conv_dim:], ((0, 0), (0, LANES - SSM_HEADS))).astype(BF16)
    dtb = jnp.pad(dt_bias, (0, LANES - SSM_HEADS)).reshape(1, LANES)
    args = (x3, wz, wx, wdt, conv_w, conv_b.reshape(1, conv_dim), dtb)
    tok = lambda w: pl.BlockSpec((None, tm, w), lambda b, s: (b, s, 0))
    in_specs = [tok(d)] + [_const_spec(a.shape) for a in args[1:]]
    out_shape = (jax.ShapeDtypeStruct((bsz, s_len, d_inner), BF16),
                 jax.ShapeDtypeStruct((bsz, s_len, d_inner), BF16),
                 jax.ShapeDtypeStruct((bsz, s_len, gn), BF16),
                 jax.ShapeDtypeStruct((bsz, s_len, gn), BF16),
                 jax.ShapeDtypeStruct((bsz, s_len, LANES), F32))
    out_specs = (tok(d_inner), tok(d_inner), tok(gn), tok(gn), tok(LANES))
    return pl.pallas_call(
        _mamba_in_kernel,
        out_shape=out_shape,
        grid=(bsz, s_len // tm),
        in_specs=in_specs,
        out_specs=out_specs,
        scratch_shapes=[pltpu.VMEM((tm + 8, conv_dim), F32)],
        compiler_params=pltpu.CompilerParams(
            dimension_semantics=("arbitrary", "arbitrary"), vmem_limit_bytes=VMEM_LIMIT),
        name="mamba_in",
    )(*args)


def _mamba_ssd_kernel(x_ref, z_ref, xs_ref, bm_ref, cm_ref, dt_ref, alog_ref, dfull_ref, nw_ref,
                      wo_ref, g1_ref, b1_ref, wr_ref, br_ref,
                      x1_ref, ri_ref, rg_ref, state_scr, y_scr):
    s = pl.program_id(1)
    tm = x_ref.shape[0]
    L, P, N = CHUNK, SSM_HEAD_DIM, SSM_STATE
    R = SSM_HEADS // SSM_GROUPS
    gw = R * P

    @pl.when(s == 0)
    def _():
        state_scr[...] = jnp.zeros(state_scr.shape, F32)

    a_row = -jnp.exp(alog_ref[...])
    li = lax.broadcasted_iota(jnp.int32, (L, L), 0)
    si = lax.broadcasted_iota(jnp.int32, (L, L), 1)
    tril = li >= si
    tril_f = tril.astype(F32)

    def chunk_body(c, carry):
        r0 = pl.multiple_of(c * L, L)
        rows = pl.ds(r0, L)
        dt_c = dt_ref[rows, :]
        adt = dt_c * a_row
        acum = jnp.dot(tril_f, adt, preferred_element_type=F32,
                       precision=lax.Precision.HIGHEST)
        acum_t = acum.T
        eacum = jnp.exp(acum)
        dtw = dt_c * jnp.exp(acum[L - 1:L, :] - acum)
        elast = jnp.exp(acum[L - 1:L, :])
        for g in range(SSM_GROUPS):
            c_g = cm_ref[rows, g * N:(g + 1) * N]
            b_g = bm_ref[rows, g * N:(g + 1) * N]
            cb = lax.dot_general(c_g, b_g, (((1,), (1,)), ((), ())), preferred_element_type=F32)
            st = state_scr[g]
            y_off = jnp.dot(c_g, st.astype(BF16), preferred_element_type=F32)
            xw_parts = []
            for r in range(R):
                h = g * R + r
                hs = slice(h * P, (h + 1) * P)
                seg = acum[:, h:h + 1] - acum_t[h:h + 1, :]
                decay = jnp.exp(jnp.where(tril, seg, -jnp.inf))
                m = (cb * decay).astype(BF16)
                x_h = xs_ref[rows, hs].astype(F32)
                xdt = (x_h * dt_c[:, h:h + 1]).astype(BF16)
                y_d = jnp.dot(m, xdt, preferred_element_type=F32)
                y_scr[rows, hs] = y_d + y_off[:, r * P:(r + 1) * P] * eacum[:, h:h + 1]
                xw_parts.append((x_h * dtw[:, h:h + 1]).astype(BF16))
            xw = jnp.concatenate(xw_parts, axis=1)
            b_t = b_g.astype(F32).T.astype(BF16)
            contrib = jnp.dot(b_t, xw, preferred_element_type=F32)
            dec_parts = [jnp.broadcast_to(elast[:, g * R + r:g * R + r + 1], (1, P)) for r in range(R)]
            dec = jnp.concatenate(dec_parts, axis=1)
            state_scr[g] = st * dec + contrib
        return carry

    lax.fori_loop(0, tm // L, chunk_body, 0)

    y = y_scr[...] + dfull_ref[...] * xs_ref[...].astype(F32)
    zf = z_ref[...].astype(F32)
    y = y * (zf * jax.nn.sigmoid(zf))
    nw = nw_ref[...]
    parts = []
    for g in range(SSM_GROUPS):
        yg = y[:, g * gw:(g + 1) * gw]
        yg = yg * lax.rsqrt(jnp.mean(yg * yg, axis=-1, keepdims=True) + RMS_EPS)
        parts.append((yg * nw[:, g * gw:(g + 1) * gw]).astype(BF16))
    yb = jnp.concatenate(parts, axis=1)
    m = jnp.dot(yb, wo_ref[...], preferred_element_type=F32)
    x1 = _ln(DN_ALPHA * x_ref[...] + m, g1_ref[...], b1_ref[...])
    x1_ref[...] = x1
    _route(x1, wr_ref, br_ref, ri_ref, rg_ref)


def mamba_ssd_block(x3, z, xs, bm, cm, dt, a_log, d_skip, norm_w, w_out, g1, b1, w_r, b_r):
    bsz, s_len, d = x3.shape
    d_inner = xs.shape[-1]
    gn = bm.shape[-1]
    tm = TOKEN_TILE
    alog = jnp.pad(a_log, (0, LANES - SSM_HEADS)).reshape(1, LANES)
    dfull = jnp.repeat(d_skip, SSM_HEAD_DIM).reshape(1, d_inner)
    args = (x3, z, xs, bm, cm, dt, alog, dfull, norm_w.reshape(1, d_inner), w_out.astype(BF16),
            g1.reshape(1, d), b1.reshape(1, d), w_r, b_r)
    tok = lambda w: pl.BlockSpec((None, tm, w), lambda b, s: (b, s, 0))
    in_specs = [tok(d), tok(d_inner), tok(d_inner), tok(gn), tok(gn), tok(LANES)] + \
               [_const_spec(a.shape) for a in args[6:]]
    out_shape = (jax.ShapeDtypeStruct((bsz, s_len, d), F32),
                 jax.ShapeDtypeStruct((bsz, s_len, LANES), jnp.int32),
                 jax.ShapeDtypeStruct((bsz, s_len, LANES), F32))
    out_specs = (tok(d), tok(LANES), tok(LANES))
    return pl.pallas_call(
        _mamba_ssd_kernel,
        out_shape=out_shape,
        grid=(bsz, s_len // tm),
        in_specs=in_specs,
        out_specs=out_specs,
        scratch_shapes=[pltpu.VMEM((SSM_GROUPS, SSM_STATE, d_inner // SSM_GROUPS), F32),
                        pltpu.VMEM((tm, d_inner), F32)],
        compiler_params=pltpu.CompilerParams(
            dimension_semantics=("arbitrary", "arbitrary"), vmem_limit_bytes=VMEM_LIMIT),
        name="mamba_ssd",
    )(*args)


def sc_gather_rows(table, idx):
    n = idx.shape[0]
    d = table.shape[1]
    info = plsc.get_sparse_core_info()
    nc, ns = info.num_cores, info.num_subcores
    per_w, rem = divmod(n, nc * ns)
    assert rem == 0 and per_w % SC_CHUNK == 0
    mesh = plsc.VectorSubcoreMesh(core_axis_name="core", subcore_axis_name="subcore")

    @functools.partial(
        pl.kernel,
        out_type=jax.ShapeDtypeStruct((n, d), table.dtype),
        mesh=mesh,
        scratch_types=[
            pltpu.VMEM((SC_CHUNK,), jnp.int32),
            pltpu.VMEM((SC_CHUNK, d), table.dtype),
            pltpu.SemaphoreType.DMA,
        ],
    )
    def gather(x_hbm, i_hbm, o_hbm, idx_v, rows_v, sem):
        wid = lax.axis_index("subcore") * nc + lax.axis_index("core")
        base = wid * per_w

        @pl.loop(0, per_w // SC_CHUNK)
        def _(c):
            off = base + c * SC_CHUNK
            pltpu.sync_copy(i_hbm.at[pl.ds(off, SC_CHUNK)], idx_v)
            pltpu.async_copy(x_hbm.at[idx_v], rows_v, sem).wait()
            pltpu.sync_copy(rows_v, o_hbm.at[pl.ds(off, SC_CHUNK)])

    return gather(table, idx)


def _moe_mlp_kernel(be_ref, nu_ref, x_ref, w1_ref, b1_ref, w2_ref, b2_ref, o_ref, w1b, w2b):
    i = pl.program_id(0)
    e = be_ref[i]
    prev = be_ref[jnp.maximum(i - 1, 0)]

    @pl.when((i == 0) | (e != prev))
    def _():
        w1b[...] = w1_ref[...].astype(BF16)
        w2b[...] = w2_ref[...].astype(BF16)

    @pl.when(i < nu_ref[0])
    def _():
        dff = w2b.shape[0]
        h = jnp.dot(x_ref[...].astype(BF16), w1b[...], preferred_element_type=F32) + b1_ref[...]
        g = jnp.minimum(h[:, :dff], SWIGLU_LIMIT)
        lin = jnp.clip(h[:, dff:], -SWIGLU_LIMIT, SWIGLU_LIMIT)
        act = (g * jax.nn.sigmoid(SWIGLU_ALPHA * g) * (lin + 1.0)).astype(BF16)
        o_ref[...] = jnp.dot(act, w2b[...], preferred_element_type=F32) + b2_ref[...]


def moe_mlp(xb, block_expert, n_used, w1, b1, w2, b2):
    n_rows, d = xb.shape
    n_e, _, two_dff = w1.shape
    dff = two_dff // 2
    n_blocks = n_rows // MOE_BLOCK
    row = lambda i, be, nu: (jnp.minimum(i, nu[0] - 1), 0)
    grid_spec = pltpu.PrefetchScalarGridSpec(
        num_scalar_prefetch=2,
        grid=(n_blocks,),
        in_specs=[
            pl.BlockSpec((MOE_BLOCK, d), row),
            pl.BlockSpec((None, d, two_dff), lambda i, be, nu: (be[i], 0, 0)),
            pl.BlockSpec((None, 1, two_dff), lambda i, be, nu: (be[i], 0, 0)),
            pl.BlockSpec((None, dff, d), lambda i, be, nu: (be[i], 0, 0)),
            pl.BlockSpec((None, 1, d), lambda i, be, nu: (be[i], 0, 0)),
        ],
        out_specs=pl.BlockSpec((MOE_BLOCK, d), row),
        scratch_shapes=[pltpu.VMEM((d, two_dff), BF16), pltpu.VMEM((dff, d), BF16)],
    )
    return pl.pallas_call(
        _moe_mlp_kernel,
        out_shape=jax.ShapeDtypeStruct((n_rows, d), F32),
        grid_spec=grid_spec,
        compiler_params=pltpu.CompilerParams(
            dimension_semantics=("arbitrary",), vmem_limit_bytes=VMEM_LIMIT),
        name="moe_mlp",
    )(block_expert, n_used, xb, w1, b1.reshape(n_e, 1, two_dff), w2, b2.reshape(n_e, 1, d))


def _combine_kernel(x1_ref, y4_ref, rg_ref, g2_ref, b2_ref, o_ref):
    d = x1_ref.shape[1]
    rg = rg_ref[...]
    f = rg[:, 0:1] * y4_ref[:, 0:d]
    for k in range(1, TOP_K):
        f = f + rg[:, k:k + 1] * y4_ref[:, k * d:(k + 1) * d]
    o_ref[...] = _ln(DN_ALPHA * x1_ref[...] + f, g2_ref[...], b2_ref[...])


def combine_block(x1, y4, rg, g2, b2):
    t, d = x1.shape
    tm = TOKEN_TILE
    return pl.pallas_call(
        _combine_kernel,
        out_shape=jax.ShapeDtypeStruct((t, d), F32),
        grid=(t // tm,),
        in_specs=[pl.BlockSpec((tm, d), lambda i: (i, 0)),
                  pl.BlockSpec((tm, TOP_K * d), lambda i: (i, 0)),
                  pl.BlockSpec((tm, LANES), lambda i: (i, 0)),
                  _const_spec((1, d)), _const_spec((1, d))],
        out_specs=pl.BlockSpec((tm, d), lambda i: (i, 0)),
        compiler_params=pltpu.CompilerParams(
            dimension_semantics=("arbitrary",), vmem_limit_bytes=VMEM_LIMIT),
        name="moe_combine",
    )(x1, y4, rg, g2.reshape(1, d), b2.reshape(1, d))


def moe_block(x1, ri, rg, w1, b1, w2, b2, g2, b2n):
    t, d = x1.shape
    n_assign = t * TOP_K
    n_blocks = -(-n_assign // MOE_BLOCK) + N_EXPERTS
    n_rows = n_blocks * MOE_BLOCK
    top_i = ri[:, :TOP_K]
    cnt = jnp.sum((top_i[:, :, None] == jnp.arange(N_EXPERTS, dtype=jnp.int32)).astype(jnp.int32), axis=1)
    csum = jnp.cumsum(cnt, axis=0)
    counts = csum[-1]
    padded = (counts + MOE_BLOCK - 1) // MOE_BLOCK * MOE_BLOCK
    pend = jnp.cumsum(padded)
    pstart = pend - padded
    rank = jnp.take_along_axis(csum - cnt, top_i, axis=1)
    dest = (pstart[top_i] + rank).reshape(-1).astype(jnp.int32)
    row_tok = jnp.zeros((n_rows,), jnp.int32).at[dest].set(
        jnp.arange(n_assign, dtype=jnp.int32) // TOP_K, unique_indices=True)
    n_used = (pend[-1] // MOE_BLOCK).astype(jnp.int32).reshape(1)
    block_start = jnp.arange(n_blocks, dtype=jnp.int32) * MOE_BLOCK
    block_expert = jnp.minimum(jnp.searchsorted(pend, block_start, side='right'), N_EXPERTS - 1)
    block_expert = block_expert[jnp.minimum(jnp.arange(n_blocks), n_used[0] - 1)].astype(jnp.int32)

    xb = sc_gather_rows(x1, row_tok)
    yb = moe_mlp(xb, block_expert, n_used, w1, b1, w2, b2)
    y4 = sc_gather_rows(yb, dest).reshape(t, TOP_K * d)
    return combine_block(x1, y4, rg, g2, b2n)


def kernel(x, a_w_in, a_b_in, a_ln_g, a_ln_b, a_w_s, a_b_s, a_w_out, a_b_out, b_w_in, b_conv_w, b_conv_b, b_dt_bias, b_a_log, b_d, b_norm_w, b_w_out, moe_w_router, moe_b_router, moe_w1, moe_b1, moe_w2, moe_b2, ln1_g, ln1_b, ln2_g, ln2_b):
    bsz, s_len, d = x.shape
    t = bsz * s_len
    xt = x.reshape(t, d)
    for i in range(DEPTH):
        j = i // 2
        w_r = jnp.pad(moe_w_router[i], ((0, 0), (0, LANES - N_EXPERTS)))
        b_r = jnp.pad(moe_b_router[i], (0, LANES - N_EXPERTS), constant_values=NEG_BIG).reshape(1, LANES)
        if i % 2 == 0:
            x1, ri, rg = mixer_a_block(xt, a_w_in[j], a_b_in[j], a_ln_g[j], a_ln_b[j], a_w_s[j], a_b_s[j],
                                       a_w_out[j], a_b_out[j], ln1_g[i], ln1_b[i], w_r, b_r)
        else:
            x3 = xt.reshape(bsz, s_len, d)
            z, xs, bm, cm, dt = mamba_in_block(x3, b_w_in[j], b_conv_w[j], b_conv_b[j], b_dt_bias[j])
            x1, ri, rg = mamba_ssd_block(x3, z, xs, bm, cm, dt, b_a_log[j], b_d[j], b_norm_w[j],
                                         b_w_out[j], ln1_g[i], ln1_b[i], w_r, b_r)
            x1 = x1.reshape(t, d)
            ri = ri.reshape(t, LANES)
            rg = rg.reshape(t, LANES)
        xt = moe_block(x1, ri, rg, moe_w1[i], moe_b1[i], moe_w2[i], moe_b2[i], ln2_g[i], ln2_b[i])
    return xt.reshape(bsz, s_len, d)
```

```python
import functools
import math

import jax
import jax.numpy as jnp
from jax import lax
from jax.experimental import pallas as pl
from jax.experimental.pallas import tpu as pltpu
from jax.experimental.pallas import tpu_sc as plsc

F32 = jnp.float32
BF16 = jnp.bfloat16
U32 = jnp.uint32

DEPTH = 4
N_EXPERTS = 32
TOP_K = 4
MOE_BLOCK = 256
SWIGLU_LIMIT = 7.0
SWIGLU_ALPHA = 1.702
A_BLOCK = 128
A_GROUPS = 8
CHUNK = 64
SSM_HEADS = 32
SSM_HEAD_DIM = 64
SSM_GROUPS = 4
SSM_STATE = 128
SSM_CONV = 4
DN_ALPHA = (2 * DEPTH) ** 0.25
LN_EPS = 1e-5
RMS_EPS = 1e-5

LANES = 128
SUBLANES = 8
TOKEN_TILE = 256
VMEM_LIMIT = 56 * 1024 * 1024
SC_CHUNK = 64
NEG_BIG = -1e30


def _ln(x, g, b):
    mu = jnp.mean(x, axis=-1, keepdims=True)
    xc = x - mu
    var = jnp.mean(xc * xc, axis=-1, keepdims=True)
    return xc * lax.rsqrt(var + LN_EPS) * g + b


def _gelu(x):
    return 0.5 * x * (1.0 + lax.erf(x * (1.0 / math.sqrt(2.0))))


def _pack_bf16_pairs(y):
    h = y.shape[1] // 2
    lo = lax.bitcast_convert_type(y[:, :h].astype(BF16).astype(F32), U32) >> 16
    hi = lax.bitcast_convert_type(y[:, h:].astype(BF16).astype(F32), U32) & jnp.uint32(0xFFFF0000)
    return hi | lo


def _unpack_bf16_pairs(w):
    lo = lax.bitcast_convert_type(w << 16, F32)
    hi = lax.bitcast_convert_type(w & jnp.uint32(0xFFFF0000), F32)
    return jnp.concatenate([lo, hi], axis=1)


def _route(x1, first, wr_ref, br_ref, ri_ref, rg_ref, cnt_ref, cnt_scr):
    @pl.when(first)
    def _():
        cnt_scr[...] = jnp.zeros(cnt_scr.shape, F32)

    logits = jnp.dot(x1, wr_ref[...], preferred_element_type=F32,
                     precision=lax.Precision.HIGHEST) + br_ref[...]
    tm = logits.shape[0]
    lane = lax.broadcasted_iota(jnp.int32, logits.shape, 1)
    lane_f = lane.astype(F32)
    vals, idxs, hots = [], [], []
    l = logits
    for _ in range(TOP_K):
        m = jnp.max(l, axis=-1, keepdims=True)
        i = jnp.min(jnp.where(l == m, lane_f, float(LANES)), axis=-1, keepdims=True)
        hot = lane_f == i
        vals.append(m)
        idxs.append(i)
        hots.append(hot.astype(F32))
        l = jnp.where(hot, -jnp.inf, l)
    es = [jnp.exp(v - vals[0]) for v in vals]
    tot = es[0] + es[1] + es[2] + es[3]
    cnt = hots[0] + hots[1] + hots[2] + hots[3]
    r_i = lax.broadcasted_iota(jnp.int32, (tm, tm), 0)
    c_i = lax.broadcasted_iota(jnp.int32, (tm, tm), 1)
    before = jnp.dot((c_i < r_i).astype(BF16), cnt.astype(BF16), preferred_element_type=F32)
    before = before + cnt_scr[0:1, :]
    ri = jnp.zeros(logits.shape, F32)
    rg = jnp.zeros(logits.shape, F32)
    for k in range(TOP_K):
        rank = jnp.sum(before * hots[k], axis=-1, keepdims=True)
        ri = jnp.where(lane == k, idxs[k], ri)
        ri = jnp.where(lane == TOP_K + k, rank, ri)
        rg = jnp.where(lane == k, es[k] / tot, rg)
    ri_ref[...] = ri.astype(jnp.int32)
    rg_ref[...] = rg
    total = cnt_scr[0:1, :] + jnp.sum(cnt, axis=0, keepdims=True)
    cnt_scr[0:1, :] = total
    cnt_ref[...] = jnp.broadcast_to(total, cnt_ref.shape)


def _const_spec(shape):
    nd = len(shape)
    return pl.BlockSpec(shape, lambda *_: (0,) * nd)


def _mixer_a_kernel(x_ref, wu_ref, wv_ref, bu_ref, bv_ref, lg_ref, lb_ref, ws_ref, bst_ref,
                    wo_ref, bo_ref, g1_ref, b1_ref, wr_ref, br_ref,
                    x1_ref, x1p_ref, ri_ref, rg_ref, cnt_ref, o_scr, cnt_scr):
    x = x_ref[...]
    xb = x.astype(BF16)
    u = _gelu(jnp.dot(xb, wu_ref[...], preferred_element_type=F32) + bu_ref[...])
    v = _gelu(jnp.dot(xb, wv_ref[...], preferred_element_type=F32) + bv_ref[...])
    vb = _ln(v, lg_ref[...], lb_ref[...]).astype(BF16)
    tm, dff = u.shape
    gd = dff // A_GROUPS
    pi = lax.broadcasted_iota(jnp.int32, (A_BLOCK, A_BLOCK), 0) // CHUNK
    pj = lax.broadcasted_iota(jnp.int32, (A_BLOCK, A_BLOCK), 1) // CHUNK
    mask = pj <= pi
    bst = bst_ref[...]
    for g in range(A_GROUPS):
        wm = jnp.where(mask, ws_ref[g], 0.0).astype(BF16)
        for n in range(tm // A_BLOCK):
            rows = slice(n * A_BLOCK, (n + 1) * A_BLOCK)
            cols = slice(g * gd, (g + 1) * gd)
            sv = jnp.dot(wm, vb[rows, cols], preferred_element_type=F32) + bst[:, g:g + 1]
            o_scr[rows, cols] = (u[rows, cols] * sv).astype(BF16)
    m = jnp.dot(o_scr[...], wo_ref[...], preferred_element_type=F32) + bo_ref[...]
    x1 = _ln(DN_ALPHA * x + m, g1_ref[...], b1_ref[...])
    x1_ref[...] = x1
    x1p_ref[...] = _pack_bf16_pairs(x1)
    _route(x1, pl.program_id(0) == 0, wr_ref, br_ref, ri_ref, rg_ref, cnt_ref, cnt_scr)


def mixer_a_block(xt, w_in, b_in, ln_g, ln_b, w_s, b_s, w_out, b_out, g1, b1, w_r, b_r):
    t, d = xt.shape
    dff = w_out.shape[0]
    tm = TOKEN_TILE
    wu = w_in[:, :dff].astype(BF16)
    wv = w_in[:, dff:].astype(BF16)
    bu = b_in[:dff].reshape(1, dff)
    bv = b_in[dff:].reshape(1, dff)
    args = (xt, wu, wv, bu, bv, ln_g.reshape(1, dff), ln_b.reshape(1, dff), w_s, b_s.T,
            w_out.astype(BF16), b_out.reshape(1, d), g1.reshape(1, d), b1.reshape(1, d), w_r, b_r)
    tok = lambda w: pl.BlockSpec((tm, w), lambda i: (i, 0))
    in_specs = [tok(d)] + [_const_spec(a.shape) for a in args[1:]]
    out_shape = (jax.ShapeDtypeStruct((t, d), F32),
                 jax.ShapeDtypeStruct((t, d // 2), U32),
                 jax.ShapeDtypeStruct((t, LANES), jnp.int32),
                 jax.ShapeDtypeStruct((t, LANES), F32),
                 jax.ShapeDtypeStruct((SUBLANES, LANES), F32))
    out_specs = (tok(d), tok(d // 2), tok(LANES), tok(LANES), _const_spec((SUBLANES, LANES)))
    return pl.pallas_call(
        _mixer_a_kernel,
        out_shape=out_shape,
        grid=(t // tm,),
        in_specs=in_specs,
        out_specs=out_specs,
        scratch_shapes=[pltpu.VMEM((tm, dff), BF16), pltpu.VMEM((SUBLANES, LANES), F32)],
        compiler_params=pltpu.CompilerParams(
            dimension_semantics=("arbitrary",), vmem_limit_bytes=VMEM_LIMIT),
        name="mixer_a",
    )(*args)


def _mamba_in_kernel(x_ref, wz_ref, wx_ref, wdt_ref, cw_ref, cb_ref, dtb_ref,
                     z_ref, xs_ref, bm_ref, cm_ref, dt_ref, ext_scr):
    s = pl.program_id(1)
    tm = x_ref.shape[0]
    d_inner = xs_ref.shape[1]
    gn = bm_ref.shape[1]
    xb = x_ref[...].astype(BF16)
    z_ref[...] = jnp.dot(xb, wz_ref[...], preferred_element_type=F32).astype(BF16)
    xbc = jnp.dot(xb, wx_ref[...], preferred_element_type=F32)
    dt = jnp.dot(xb, wdt_ref[...], preferred_element_type=F32) + dtb_ref[...]
    dt_ref[...] = jnp.maximum(dt, 0.0) + jnp.log1p(jnp.exp(-jnp.abs(dt)))

    @pl.when(s == 0)
    def _():
        ext_scr[0:8, :] = jnp.zeros((8, ext_scr.shape[1]), F32)

    @pl.when(s > 0)
    def _():
        ext_scr[0:8, :] = ext_scr[tm:tm + 8, :]

    ext_scr[8:8 + tm, :] = xbc
    cw = cw_ref[...]
    acc = cb_ref[...] + cw[0:1, :] * ext_scr[5:5 + tm, :]
    for j in range(1, SSM_CONV):
        acc = acc + cw[j:j + 1, :] * ext_scr[5 + j:5 + j + tm, :]
    act = acc * jax.nn.sigmoid(acc)
    xs_ref[...] = act[:, :d_inner].astype(BF16)
    bm_ref[...] = act[:, d_inner:d_inner + gn].astype(BF16)
    cm_ref[...] = act[:, d_inner + gn:].astype(BF16)


def mamba_in_block(x3, w_in, conv_w, conv_b, dt_bias):
    bsz, s_len, d = x3.shape
    d_inner = SSM_HEADS * SSM_HEAD_DIM
    gn = SSM_GROUPS * SSM_STATE
    conv_dim = d_inner + 2 * gn
    tm = TOKEN_TILE
    wz = w_in[:, :d_inner].astype(BF16)
    wx = w_in[:, d_inner:d_inner + conv_dim].astype(BF16)
    wdt = jnp.pad(w_in[:, d_inner + conv_dim:], ((0, 0), (0, LANES - SSM_HEADS))).astype(BF16)
    dtb = jnp.pad(dt_bias, (0, LANES - SSM_HEADS)).reshape(1, LANES)
    args = (x3, wz, wx, wdt, conv_w, conv_b.reshape(1, conv_dim), dtb)
    tok = lambda w: pl.BlockSpec((None, tm, w), lambda b, s: (b, s, 0))
    in_specs = [tok(d)] + [_const_spec(a.shape) for a in args[1:]]
    out_shape = (jax.ShapeDtypeStruct((bsz, s_len, d_inner), BF16),
                 jax.ShapeDtypeStruct((bsz, s_len, d_inner), BF16),
                 jax.ShapeDtypeStruct((bsz, s_len, gn), BF16),
                 jax.ShapeDtypeStruct((bsz, s_len, gn), BF16),
                 jax.ShapeDtypeStruct((bsz, s_len, LANES), F32))
    out_specs = (tok(d_inner), tok(d_inner), tok(gn), tok(gn), tok(LANES))
    return pl.pallas_call(
        _mamba_in_kernel,
        out_shape=out_shape,
        grid=(bsz, s_len // tm),
        in_specs=in_specs,
        out_specs=out_specs,
        scratch_shapes=[pltpu.VMEM((tm + 8, conv_dim), F32)],
        compiler_params=pltpu.CompilerParams(
            dimension_semantics=("arbitrary", "arbitrary"), vmem_limit_bytes=VMEM_LIMIT),
        name="mamba_in",
    )(*args)


def _mamba_ssd_kernel(x_ref, z_ref, xs_ref, bm_ref, cm_ref, dt_ref, alog_ref, dfull_ref, nw_ref,
                      wo_ref, g1_ref, b1_ref, wr_ref, br_ref,
                      x1_ref, x1p_ref, ri_ref, rg_ref, cnt_ref, state_scr, y_scr, cnt_scr):
    s = pl.program_id(1)
    tm = x_ref.shape[0]
    L, P, N = CHUNK, SSM_HEAD_DIM, SSM_STATE
    R = SSM_HEADS // SSM_GROUPS
    gw = R * P

    @pl.when(s == 0)
    def _():
        state_scr[...] = jnp.zeros(state_scr.shape, F32)

    a_row = -jnp.exp(alog_ref[...])
    li = lax.broadcasted_iota(jnp.int32, (L, L), 0)
    si = lax.broadcasted_iota(jnp.int32, (L, L), 1)
    tril = li >= si
    tril_f = tril.astype(F32)

    def chunk_body(c, carry):
        r0 = pl.multiple_of(c * L, L)
        rows = pl.ds(r0, L)
        dt_c = dt_ref[rows, :]
        adt = dt_c * a_row
        acum = jnp.dot(tril_f, adt, preferred_element_type=F32,
                       precision=lax.Precision.HIGHEST)
        acum_t = acum.T
        eacum = jnp.exp(acum)
        dtw = dt_c * jnp.exp(acum[L - 1:L, :] - acum)
        elast = jnp.exp(acum[L - 1:L, :])
        for g in range(SSM_GROUPS):
            c_g = cm_ref[rows, g * N:(g + 1) * N]
            b_g = bm_ref[rows, g * N:(g + 1) * N]
            cb = lax.dot_general(c_g, b_g, (((1,), (1,)), ((), ())), preferred_element_type=F32)
            st = state_scr[g]
            y_off = jnp.dot(c_g, st.astype(BF16), preferred_element_type=F32)
            xw_parts = []
            for r in range(R):
                h = g * R + r
                hs = slice(h * P, (h + 1) * P)
                seg = acum[:, h:h + 1] - acum_t[h:h + 1, :]
                decay = jnp.exp(jnp.where(tril, seg, -jnp.inf))
                m = (cb * decay).astype(BF16)
                x_h = xs_ref[rows, hs].astype(F32)
                xdt = (x_h * dt_c[:, h:h + 1]).astype(BF16)
                y_d = jnp.dot(m, xdt, preferred_element_type=F32)
                y_scr[rows, hs] = y_d + y_off[:, r * P:(r + 1) * P] * eacum[:, h:h + 1]
                xw_parts.append((x_h * dtw[:, h:h + 1]).astype(BF16))
            xw = jnp.concatenate(xw_parts, axis=1)
            b_t = b_g.astype(F32).T.astype(BF16)
            contrib = jnp.dot(b_t, xw, preferred_element_type=F32)
            dec_parts = [jnp.broadcast_to(elast[:, g * R + r:g * R + r + 1], (1, P)) for r in range(R)]
            dec = jnp.concatenate(dec_parts, axis=1)
            state_scr[g] = st * dec + contrib
        return carry

    lax.fori_loop(0, tm // L, chunk_body, 0)

    y = y_scr[...] + dfull_ref[...] * xs_ref[...].astype(F32)
    zf = z_ref[...].astype(F32)
    y = y * (zf * jax.nn.sigmoid(zf))
    nw = nw_ref[...]
    parts = []
    for g in range(SSM_GROUPS):
        yg = y[:, g * gw:(g + 1) * gw]
        yg = yg * lax.rsqrt(jnp.mean(yg * yg, axis=-1, keepdims=True) + RMS_EPS)
        parts.append((yg * nw[:, g * gw:(g + 1) * gw]).astype(BF16))
    yb = jnp.concatenate(parts, axis=1)
    m = jnp.dot(yb, wo_ref[...], preferred_element_type=F32)
    x1 = _ln(DN_ALPHA * x_ref[...] + m, g1_ref[...], b1_ref[...])
    x1_ref[...] = x1
    x1p_ref[...] = _pack_bf16_pairs(x1)
    first = (pl.program_id(0) == 0) & (s == 0)
    _route(x1, first, wr_ref, br_ref, ri_ref, rg_ref, cnt_ref, cnt_scr)


def mamba_ssd_block(x3, z, xs, bm, cm, dt, a_log, d_skip, norm_w, w_out, g1, b1, w_r, b_r):
    bsz, s_len, d = x3.shape
    d_inner = xs.shape[-1]
    gn = bm.shape[-1]
    tm = TOKEN_TILE
    alog = jnp.pad(a_log, (0, LANES - SSM_HEADS)).reshape(1, LANES)
    dfull = jnp.repeat(d_skip, SSM_HEAD_DIM).reshape(1, d_inner)
    args = (x3, z, xs, bm, cm, dt, alog, dfull, norm_w.reshape(1, d_inner), w_out.astype(BF16),
            g1.reshape(1, d), b1.reshape(1, d), w_r, b_r)
    tok = lambda w: pl.BlockSpec((None, tm, w), lambda b, s: (b, s, 0))
    in_specs = [tok(d), tok(d_inner), tok(d_inner), tok(gn), tok(gn), tok(LANES)] + \
               [_const_spec(a.shape) for a in args[6:]]
    out_shape = (jax.ShapeDtypeStruct((bsz, s_len, d), F32),
                 jax.ShapeDtypeStruct((bsz, s_len, d // 2), U32),
                 jax.ShapeDtypeStruct((bsz, s_len, LANES), jnp.int32),
                 jax.ShapeDtypeStruct((bsz, s_len, LANES), F32),
                 jax.ShapeDtypeStruct((SUBLANES, LANES), F32))
    out_specs = (tok(d), tok(d // 2), tok(LANES), tok(LANES), _const_spec((SUBLANES, LANES)))
    return pl.pallas_call(
        _mamba_ssd_kernel,
        out_shape=out_shape,
        grid=(bsz, s_len // tm),
        in_specs=in_specs,
        out_specs=out_specs,
        scratch_shapes=[pltpu.VMEM((SSM_GROUPS, SSM_STATE, d_inner // SSM_GROUPS), F32),
                        pltpu.VMEM((tm, d_inner), F32),
                        pltpu.VMEM((SUBLANES, LANES), F32)],
        compiler_params=pltpu.CompilerParams(
            dimension_semantics=("arbitrary", "arbitrary"), vmem_limit_bytes=VMEM_LIMIT),
        name="mamba_ssd",
    )(*args)


def _sc_workers():
    info = plsc.get_sparse_core_info()
    return info.num_cores, info.num_subcores


def sc_scatter_rows(x, dest_km, n_rows):
    t, d = x.shape
    nc, ns = _sc_workers()
    per_w, rem = divmod(t, nc * ns)
    assert rem == 0 and per_w % SC_CHUNK == 0
    mesh = plsc.VectorSubcoreMesh(core_axis_name="core", subcore_axis_name="subcore")

    @functools.partial(
        pl.kernel,
        out_type=jax.ShapeDtypeStruct((n_rows, d), x.dtype),
        mesh=mesh,
        scratch_types=[pltpu.VMEM((SC_CHUNK,), jnp.int32) for _ in range(TOP_K)] +
                      [pltpu.VMEM((SC_CHUNK, d), x.dtype), pltpu.SemaphoreType.DMA],
    )
    def scatter(x_hbm, i_hbm, o_hbm, i0, i1, i2, i3, rows_v, sem):
        wid = lax.axis_index("subcore") * nc + lax.axis_index("core")
        base = wid * per_w
        idx = (i0, i1, i2, i3)

        @pl.loop(0, per_w // SC_CHUNK)
        def _(c):
            off = base + c * SC_CHUNK
            pltpu.sync_copy(x_hbm.at[pl.ds(off, SC_CHUNK)], rows_v)
            for k in range(TOP_K):
                pltpu.sync_copy(i_hbm.at[pl.ds(k * t + off, SC_CHUNK)], idx[k])
            copies = [pltpu.make_async_copy(rows_v, o_hbm.at[idx[k]], sem) for k in range(TOP_K)]
            for cp in copies:
                cp.start()
            for cp in copies:
                cp.wait()

    return scatter(x, dest_km)


def sc_gather_rows(table, idx):
    n = idx.shape[0]
    d = table.shape[1]
    nc, ns = _sc_workers()
    per_w, rem = divmod(n, nc * ns)
    n_chunks = per_w // SC_CHUNK
    assert rem == 0 and per_w % (2 * SC_CHUNK) == 0
    mesh = plsc.VectorSubcoreMesh(core_axis_name="core", subcore_axis_name="subcore")

    @functools.partial(
        pl.kernel,
        out_type=jax.ShapeDtypeStruct((n, d), table.dtype),
        mesh=mesh,
        scratch_types=[
            pltpu.VMEM((SC_CHUNK,), jnp.int32), pltpu.VMEM((SC_CHUNK,), jnp.int32),
            pltpu.VMEM((SC_CHUNK, d), table.dtype), pltpu.VMEM((SC_CHUNK, d), table.dtype),
            pltpu.SemaphoreType.DMA, pltpu.SemaphoreType.DMA,
            pltpu.SemaphoreType.DMA, pltpu.SemaphoreType.DMA,
        ],
    )
    def gather(x_hbm, i_hbm, o_hbm, idx0, idx1, rows0, rows1, sg0, sg1, sw0, sw1):
        wid = lax.axis_index("subcore") * nc + lax.axis_index("core")
        base = wid * per_w
        idx = (idx0, idx1)
        rows = (rows0, rows1)
        sg = (sg0, sg1)
        sw = (sw0, sw1)

        def gather_copy(slot):
            return pltpu.make_async_copy(x_hbm.at[idx[slot]], rows[slot], sg[slot])

        def write_copy(c, slot):
            return pltpu.make_async_copy(rows[slot], o_hbm.at[pl.ds(base + c * SC_CHUNK, SC_CHUNK)], sw[slot])

        pltpu.sync_copy(i_hbm.at[pl.ds(base, SC_CHUNK)], idx0)
        gather_copy(0).start()

        @pl.loop(0, n_chunks, step=2)
        def _(c0):
            for slot in range(2):
                c = c0 + slot
                nxt = 1 - slot

                @pl.when(c + 1 < n_chunks)
                def _():
                    pltpu.sync_copy(i_hbm.at[pl.ds(base + (c + 1) * SC_CHUNK, SC_CHUNK)], idx[nxt])

                    @pl.when(c >= 1)
                    def _():
                        write_copy(c - 1, nxt).wait()

                    gather_copy(nxt).start()

                gather_copy(slot).wait()
                write_copy(c, slot).start()

        write_copy(n_chunks - 2, 0).wait()
        write_copy(n_chunks - 1, 1).wait()

    return gather(table, idx)


def _moe_mlp_kernel(be_ref, nv_ref, nu_ref, x_ref, w1_ref, b1_ref, w2_ref, b2_ref, o_ref, w1b, w2b):
    i = pl.program_id(0)
    e = be_ref[i]
    prev = be_ref[jnp.maximum(i - 1, 0)]

    @pl.when((i == 0) | (e != prev))
    def _():
        w1b[...] = w1_ref[...].astype(BF16)
        w2b[...] = w2_ref[...].astype(BF16)

    @pl.when(i < nu_ref[0])
    def _():
        dff = w2b.shape[0]
        row = lax.broadcasted_iota(jnp.int32, x_ref.shape, 0)
        xw = jnp.where(row < nv_ref[i], x_ref[...], jnp.uint32(0))
        xb = _unpack_bf16_pairs(xw).astype(BF16)
        h = jnp.dot(xb, w1b[...], preferred_element_type=F32) + b1_ref[...]
        g = jnp.minimum(h[:, :dff], SWIGLU_LIMIT)
        lin = jnp.clip(h[:, dff:], -SWIGLU_LIMIT, SWIGLU_LIMIT)
        act = (g * jax.nn.sigmoid(SWIGLU_ALPHA * g) * (lin + 1.0)).astype(BF16)
        y = jnp.dot(act, w2b[...], preferred_element_type=F32) + b2_ref[...]
        o_ref[...] = _pack_bf16_pairs(y)


def moe_mlp(xp, block_expert, n_valid, n_used, layer, w1, b1, w2, b2):
    n_rows, dh = xp.shape
    _, n_e, d, two_dff = w1.shape
    dff = two_dff // 2
    n_blocks = n_rows // MOE_BLOCK
    row = lambda i, be, nv, nu: (jnp.minimum(i, nu[0] - 1), 0)
    wsel = lambda i, be, nv, nu: (layer, be[i], 0, 0)
    grid_spec = pltpu.PrefetchScalarGridSpec(
        num_scalar_prefetch=3,
        grid=(n_blocks,),
        in_specs=[
            pl.BlockSpec((MOE_BLOCK, dh), row),
            pl.BlockSpec((None, None, d, two_dff), wsel),
            pl.BlockSpec((None, None, 1, two_dff), wsel),
            pl.BlockSpec((None, None, dff, d), wsel),
            pl.BlockSpec((None, None, 1, d), wsel),
        ],
        out_specs=pl.BlockSpec((MOE_BLOCK, dh), row),
        scratch_shapes=[pltpu.VMEM((d, two_dff), BF16), pltpu.VMEM((dff, d), BF16)],
    )
    return pl.pallas_call(
        _moe_mlp_kernel,
        out_shape=jax.ShapeDtypeStruct((n_rows, dh), U32),
        grid_spec=grid_spec,
        compiler_params=pltpu.CompilerParams(
            dimension_semantics=("arbitrary",), vmem_limit_bytes=VMEM_LIMIT),
        name="moe_mlp",
    )(block_expert, n_valid, n_used, xp, w1, b1.reshape(b1.shape[0], n_e, 1, two_dff),
      w2, b2.reshape(b2.shape[0], n_e, 1, d))


def _combine_kernel(x1_ref, y0_ref, y1_ref, y2_ref, y3_ref, rg_ref, g2_ref, b2_ref, o_ref):
    rg = rg_ref[...]
    f = rg[:, 0:1] * _unpack_bf16_pairs(y0_ref[...])
    for k, y_ref in ((1, y1_ref), (2, y2_ref), (3, y3_ref)):
        f = f + rg[:, k:k + 1] * _unpack_bf16_pairs(y_ref[...])
    o_ref[...] = _ln(DN_ALPHA * x1_ref[...] + f, g2_ref[...], b2_ref[...])


def combine_block(x1, y4p, rg, g2, b2):
    t, d = x1.shape
    tm = TOKEN_TILE
    nt = t // tm
    y_specs = [pl.BlockSpec((tm, d // 2), functools.partial(lambda i, k: (k * nt + i, 0), k=k))
               for k in range(TOP_K)]
    return pl.pallas_call(
        _combine_kernel,
        out_shape=jax.ShapeDtypeStruct((t, d), F32),
        grid=(nt,),
        in_specs=[pl.BlockSpec((tm, d), lambda i: (i, 0))] + y_specs +
                 [pl.BlockSpec((tm, LANES), lambda i: (i, 0)), _const_spec((1, d)), _const_spec((1, d))],
        out_specs=pl.BlockSpec((tm, d), lambda i: (i, 0)),
        compiler_params=pltpu.CompilerParams(
            dimension_semantics=("arbitrary",), vmem_limit_bytes=VMEM_LIMIT),
        name="moe_combine",
    )(x1, y4p, y4p, y4p, y4p, rg, g2.reshape(1, d), b2.reshape(1, d))


def moe_block(x1, x1p, ri, rg, cnt, layer, w1, b1, w2, b2, g2, b2n):
    t, d = x1.shape
    n_assign = t * TOP_K
    n_blocks = -(-n_assign // MOE_BLOCK) + N_EXPERTS
    n_rows = n_blocks * MOE_BLOCK
    ar = jnp.arange(N_EXPERTS, dtype=jnp.int32)
    counts = cnt[0, :N_EXPERTS].astype(jnp.int32)
    padded = (counts + MOE_BLOCK - 1) // MOE_BLOCK * MOE_BLOCK
    pend = jnp.sum(jnp.where(ar[None, :] <= ar[:, None], padded[None, :], 0), axis=1)
    pstart = pend - padded
    top_i = ri[:, :TOP_K]
    dest = jnp.sum(jnp.where(top_i[:, :, None] == ar, pstart, 0), axis=-1) + ri[:, TOP_K:2 * TOP_K]
    dest_km = dest.T.reshape(-1)
    n_used = (pend[-1] // MOE_BLOCK).reshape(1)
    block_start = jnp.arange(n_blocks, dtype=jnp.int32) * MOE_BLOCK
    block_expert = jnp.minimum(
        jnp.sum((pend[None, :] <= block_start[:, None]).astype(jnp.int32), axis=1), N_EXPERTS - 1)
    last = jnp.sum(jnp.where(jnp.arange(n_blocks) == n_used[0] - 1, block_expert, 0))
    block_expert = jnp.where(jnp.arange(n_blocks) < n_used[0], block_expert, last)
    vend = jnp.sum(jnp.where(block_expert[:, None] == ar, (pstart + counts)[None, :], 0), axis=1)
    n_valid = jnp.clip(vend - block_start, 0, MOE_BLOCK)

    xp = sc_scatter_rows(x1p, dest_km, n_rows)
    yp = moe_mlp(xp, block_expert, n_valid, n_used, layer, w1, b1, w2, b2)
    y4p = sc_gather_rows(yp, dest_km)
    return combine_block(x1, y4p, rg, g2, b2n)


def kernel(x, a_w_in, a_b_in, a_ln_g, a_ln_b, a_w_s, a_b_s, a_w_out, a_b_out, b_w_in, b_conv_w, b_conv_b, b_dt_bias, b_a_log, b_d, b_norm_w, b_w_out, moe_w_router, moe_b_router, moe_w1, moe_b1, moe_w2, moe_b2, ln1_g, ln1_b, ln2_g, ln2_b):
    bsz, s_len, d = x.shape
    t = bsz * s_len
    xt = x.reshape(t, d)
    for i in range(DEPTH):
        j = i // 2
        w_r = jnp.pad(moe_w_router[i], ((0, 0), (0, LANES - N_EXPERTS)))
        b_r = jnp.pad(moe_b_router[i], (0, LANES - N_EXPERTS), constant_values=NEG_BIG).reshape(1, LANES)
        if i % 2 == 0:
            x1, x1p, ri, rg, cnt = mixer_a_block(
                xt, a_w_in[j], a_b_in[j], a_ln_g[j], a_ln_b[j], a_w_s[j], a_b_s[j],
                a_w_out[j], a_b_out[j], ln1_g[i], ln1_b[i], w_r, b_r)
        else:
            x3 = xt.reshape(bsz, s_len, d)
            z, xs, bm, cm, dt = mamba_in_block(x3, b_w_in[j], b_conv_w[j], b_conv_b[j], b_dt_bias[j])
            x1, x1p, ri, rg, cnt = mamba_ssd_block(
                x3, z, xs, bm, cm, dt, b_a_log[j], b_d[j], b_norm_w[j],
                b_w_out[j], ln1_g[i], ln1_b[i], w_r, b_r)
            x1 = x1.reshape(t, d)
            x1p = x1p.reshape(t, d // 2)
            ri = ri.reshape(t, LANES)
            rg = rg.reshape(t, LANES)
        xt = moe_block(x1, x1p, ri, rg, cnt, i, moe_w1, moe_b1, moe_w2, moe_b2, ln2_g[i], ln2_b[i])
    return xt.reshape(bsz, s_len, d)
```

```python
import functools
import math

import jax
import jax.numpy as jnp
from jax import lax
from jax.experimental import pallas as pl
from jax.experimental.pallas import tpu as pltpu
from jax.experimental.pallas import tpu_sc as plsc

F32 = jnp.float32
BF16 = jnp.bfloat16
U32 = jnp.uint32

DEPTH = 4
N_EXPERTS = 32
TOP_K = 4
MOE_BLOCK = 256
SWIGLU_LIMIT = 7.0
SWIGLU_ALPHA = 1.702
A_BLOCK = 128
A_GROUPS = 8
CHUNK = 64
SSD_BLOCK = 128
SSM_HEADS = 32
SSM_HEAD_DIM = 64
SSM_GROUPS = 4
SSM_STATE = 128
SSM_CONV = 4
DN_ALPHA = (2 * DEPTH) ** 0.25
LN_EPS = 1e-5
RMS_EPS = 1e-5

LANES = 128
SUBLANES = 8
TOKEN_TILE = 256
VMEM_LIMIT = 56 * 1024 * 1024
SC_CHUNK = 64
NEG_BIG = -1e30


def _ln(x, g, b):
    mu = jnp.mean(x, axis=-1, keepdims=True)
    xc = x - mu
    var = jnp.mean(xc * xc, axis=-1, keepdims=True)
    return xc * lax.rsqrt(var + LN_EPS) * g + b


def _gelu(x):
    return 0.5 * x * (1.0 + lax.erf(x * (1.0 / math.sqrt(2.0))))


def _pack_bf16_pairs(y):
    h = y.shape[1] // 2
    lo = lax.bitcast_convert_type(y[:, :h].astype(BF16).astype(F32), U32) >> 16
    hi = lax.bitcast_convert_type(y[:, h:].astype(BF16).astype(F32), U32) & jnp.uint32(0xFFFF0000)
    return hi | lo


def _unpack_bf16_pairs(w):
    lo = lax.bitcast_convert_type(w << 16, F32)
    hi = lax.bitcast_convert_type(w & jnp.uint32(0xFFFF0000), F32)
    return jnp.concatenate([lo, hi], axis=1)


def _route(x1, first, wr_ref, br_ref, ri_ref, rg_ref, cnt_ref, cnt_scr):
    @pl.when(first)
    def _():
        cnt_scr[...] = jnp.zeros(cnt_scr.shape, F32)

    x_hi = x1.astype(BF16)
    x_lo = (x1 - x_hi.astype(F32)).astype(BF16)
    logits = (jnp.dot(x_hi, wr_ref[0], preferred_element_type=F32)
              + (jnp.dot(x_hi, wr_ref[1], preferred_element_type=F32)
                 + jnp.dot(x_lo, wr_ref[0], preferred_element_type=F32))) + br_ref[...]
    tm = logits.shape[0]
    lane = lax.broadcasted_iota(jnp.int32, logits.shape, 1)
    lane_f = lane.astype(F32)
    vals, idxs, hots = [], [], []
    l = logits
    for _ in range(TOP_K):
        m = jnp.max(l, axis=-1, keepdims=True)
        i = jnp.min(jnp.where(l == m, lane_f, float(LANES)), axis=-1, keepdims=True)
        hot = lane_f == i
        vals.append(m)
        idxs.append(i)
        hots.append(hot.astype(F32))
        l = jnp.where(hot, -jnp.inf, l)
    es = [jnp.exp(v - vals[0]) for v in vals]
    tot = es[0] + es[1] + es[2] + es[3]
    cnt = hots[0] + hots[1] + hots[2] + hots[3]
    r_i = lax.broadcasted_iota(jnp.int32, (tm, tm), 0)
    c_i = lax.broadcasted_iota(jnp.int32, (tm, tm), 1)
    before = jnp.dot((c_i < r_i).astype(BF16), cnt.astype(BF16), preferred_element_type=F32)
    before = before + cnt_scr[0:1, :]
    ri = jnp.zeros(logits.shape, F32)
    rg = jnp.zeros(logits.shape, F32)
    for k in range(TOP_K):
        rank = jnp.sum(before * hots[k], axis=-1, keepdims=True)
        ri = jnp.where(lane == k, idxs[k], ri)
        ri = jnp.where(lane == TOP_K + k, rank, ri)
        rg = jnp.where(lane == k, es[k] / tot, rg)
    ri_ref[...] = ri.astype(jnp.int32)
    rg_ref[...] = rg
    total = cnt_scr[0:1, :] + jnp.sum(cnt, axis=0, keepdims=True)
    cnt_scr[0:1, :] = total
    cnt_ref[...] = jnp.broadcast_to(total, cnt_ref.shape)


def _const_spec(shape):
    nd = len(shape)
    return pl.BlockSpec(shape, lambda *_: (0,) * nd)


def _mixer_a_kernel(x_ref, wu_ref, wv_ref, bu_ref, bv_ref, lg_ref, lb_ref, ws_ref, bst_ref,
                    wo_ref, bo_ref, g1_ref, b1_ref, wr_ref, br_ref,
                    x1_ref, x1p_ref, ri_ref, rg_ref, cnt_ref, o_scr, cnt_scr):
    x = x_ref[...]
    xb = x.astype(BF16)
    u = _gelu(jnp.dot(xb, wu_ref[...], preferred_element_type=F32) + bu_ref[...])
    v = _gelu(jnp.dot(xb, wv_ref[...], preferred_element_type=F32) + bv_ref[...])
    vb = _ln(v, lg_ref[...], lb_ref[...]).astype(BF16)
    tm, dff = u.shape
    gd = dff // A_GROUPS
    pi = lax.broadcasted_iota(jnp.int32, (A_BLOCK, A_BLOCK), 0) // CHUNK
    pj = lax.broadcasted_iota(jnp.int32, (A_BLOCK, A_BLOCK), 1) // CHUNK
    mask = pj <= pi
    bst = bst_ref[...]
    for g in range(A_GROUPS):
        wm = jnp.where(mask, ws_ref[g], 0.0).astype(BF16)
        for n in range(tm // A_BLOCK):
            rows = slice(n * A_BLOCK, (n + 1) * A_BLOCK)
            cols = slice(g * gd, (g + 1) * gd)
            sv = jnp.dot(wm, vb[rows, cols], preferred_element_type=F32) + bst[:, g:g + 1]
            o_scr[rows, cols] = (u[rows, cols] * sv).astype(BF16)
    m = jnp.dot(o_scr[...], wo_ref[...], preferred_element_type=F32) + bo_ref[...]
    x1 = _ln(DN_ALPHA * x + m, g1_ref[...], b1_ref[...])
    x1_ref[...] = x1
    x1p_ref[...] = _pack_bf16_pairs(x1)
    _route(x1, pl.program_id(0) == 0, wr_ref, br_ref, ri_ref, rg_ref, cnt_ref, cnt_scr)


def mixer_a_block(xt, w_in, b_in, ln_g, ln_b, w_s, b_s, w_out, b_out, g1, b1, w_r, b_r):
    t, d = xt.shape
    dff = w_out.shape[0]
    tm = TOKEN_TILE
    wu = w_in[:, :dff].astype(BF16)
    wv = w_in[:, dff:].astype(BF16)
    bu = b_in[:dff].reshape(1, dff)
    bv = b_in[dff:].reshape(1, dff)
    args = (xt, wu, wv, bu, bv, ln_g.reshape(1, dff), ln_b.reshape(1, dff), w_s, b_s.T,
            w_out.astype(BF16), b_out.reshape(1, d), g1.reshape(1, d), b1.reshape(1, d), w_r, b_r)
    tok = lambda w: pl.BlockSpec((tm, w), lambda i: (i, 0))
    in_specs = [tok(d)] + [_const_spec(a.shape) for a in args[1:]]
    out_shape = (jax.ShapeDtypeStruct((t, d), F32),
                 jax.ShapeDtypeStruct((t, d // 2), U32),
                 jax.ShapeDtypeStruct((t, LANES), jnp.int32),
                 jax.ShapeDtypeStruct((t, LANES), F32),
                 jax.ShapeDtypeStruct((SUBLANES, LANES), F32))
    out_specs = (tok(d), tok(d // 2), tok(LANES), tok(LANES), _const_spec((SUBLANES, LANES)))
    return pl.pallas_call(
        _mixer_a_kernel,
        out_shape=out_shape,
        grid=(t // tm,),
        in_specs=in_specs,
        out_specs=out_specs,
        scratch_shapes=[pltpu.VMEM((tm, dff), BF16), pltpu.VMEM((SUBLANES, LANES), F32)],
        compiler_params=pltpu.CompilerParams(
            dimension_semantics=("arbitrary",), vmem_limit_bytes=VMEM_LIMIT),
        name="mixer_a",
    )(*args)


def _mamba_in_kernel(x_ref, wz_ref, wx_ref, wdt_ref, cw_ref, cb_ref, dtb_ref,
                     z_ref, xs_ref, bm_ref, cm_ref, dt_ref, ext_scr):
    s = pl.program_id(1)
    tm = x_ref.shape[0]
    d_inner = xs_ref.shape[1]
    gn = bm_ref.shape[1]
    xb = x_ref[...].astype(BF16)
    z_ref[...] = jnp.dot(xb, wz_ref[...], preferred_element_type=F32).astype(BF16)
    xbc = jnp.dot(xb, wx_ref[...], preferred_element_type=F32)
    dt = jnp.dot(xb, wdt_ref[...], preferred_element_type=F32) + dtb_ref[...]
    dt_ref[...] = jnp.maximum(dt, 0.0) + jnp.log1p(jnp.exp(-jnp.abs(dt)))

    @pl.when(s == 0)
    def _():
        ext_scr[0:8, :] = jnp.zeros((8, ext_scr.shape[1]), F32)

    @pl.when(s > 0)
    def _():
        ext_scr[0:8, :] = ext_scr[tm:tm + 8, :]

    ext_scr[8:8 + tm, :] = xbc
    cw = cw_ref[...]
    acc = cb_ref[...] + cw[0:1, :] * ext_scr[5:5 + tm, :]
    for j in range(1, SSM_CONV):
        acc = acc + cw[j:j + 1, :] * ext_scr[5 + j:5 + j + tm, :]
    act = acc * jax.nn.sigmoid(acc)
    xs_ref[...] = act[:, :d_inner].astype(BF16)
    bm_ref[...] = act[:, d_inner:d_inner + gn].astype(BF16)
    cm_ref[...] = act[:, d_inner + gn:].astype(BF16)


def mamba_in_block(x3, w_in, conv_w, conv_b, dt_bias):
    bsz, s_len, d = x3.shape
    d_inner = SSM_HEADS * SSM_HEAD_DIM
    gn = SSM_GROUPS * SSM_STATE
    conv_dim = d_inner + 2 * gn
    tm = TOKEN_TILE
    wz = w_in[:, :d_inner].astype(BF16)
    wx = w_in[:, d_inner:d_inner + conv_dim].astype(BF16)
    wdt = jnp.pad(w_in[:, d_inner + conv_dim:], ((0, 0), (0, LANES - SSM_HEADS))).astype(BF16)
    dtb = jnp.pad(dt_bias, (0, LANES - SSM_HEADS)).reshape(1, LANES)
    args = (x3, wz, wx, wdt, conv_w, conv_b.reshape(1, conv_dim), dtb)
    tok = lambda w: pl.BlockSpec((None, tm, w), lambda b, s: (b, s, 0))
    in_specs = [tok(d)] + [_const_spec(a.shape) for a in args[1:]]
    out_shape = (jax.ShapeDtypeStruct((bsz, s_len, d_inner), BF16),
                 jax.ShapeDtypeStruct((bsz, s_len, d_inner), BF16),
                 jax.ShapeDtypeStruct((bsz, s_len, gn), BF16),
                 jax.ShapeDtypeStruct((bsz, s_len, gn), BF16),
                 jax.ShapeDtypeStruct((bsz, s_len, LANES), F32))
    out_specs = (tok(d_inner), tok(d_inner), tok(gn), tok(gn), tok(LANES))
    return pl.pallas_call(
        _mamba_in_kernel,
        out_shape=out_shape,
        grid=(bsz, s_len // tm),
        in_specs=in_specs,
        out_specs=out_specs,
        scratch_shapes=[pltpu.VMEM((tm + 8, conv_dim), F32)],
        compiler_params=pltpu.CompilerParams(
            dimension_semantics=("arbitrary", "arbitrary"), vmem_limit_bytes=VMEM_LIMIT),
        name="mamba_in",
    )(*args)


def _split3_bf16(q):
    hi = q.astype(BF16)
    r1 = q - hi.astype(F32)
    mid = r1.astype(BF16)
    lo = (r1 - mid.astype(F32)).astype(BF16)
    return hi, mid, lo


def _mamba_ssd_kernel(x_ref, z_ref, xs_ref, bm_ref, cm_ref, dt_ref, alog_ref, dfull_ref, nw_ref, exp_ref,
                      wo_ref, g1_ref, b1_ref, wr_ref, br_ref,
                      x1_ref, x1p_ref, ri_ref, rg_ref, cnt_ref, state_scr, y_scr, dtw_scr, cnt_scr):
    s = pl.program_id(1)
    tm = x_ref.shape[0]
    L, P, N = SSD_BLOCK, SSM_HEAD_DIM, SSM_STATE
    R = SSM_HEADS // SSM_GROUPS
    gw = R * P
    nblk = tm // L

    @pl.when(s == 0)
    def _():
        state_scr[...] = jnp.zeros(state_scr.shape, F32)

    a_row = -jnp.exp(alog_ref[...])
    dt = dt_ref[...]
    adt = dt * a_row
    r_i = lax.broadcasted_iota(jnp.int32, (tm, tm), 0)
    c_i = lax.broadcasted_iota(jnp.int32, (tm, tm), 1)
    btril = ((r_i >= c_i) & (r_i // L == c_i // L)).astype(F32)
    acum = jnp.dot(btril, adt, preferred_element_type=F32,
                   precision=lax.Precision.HIGHEST)
    row_blk = lax.broadcasted_iota(jnp.int32, (tm, LANES), 0) // L
    alast = [acum[(b + 1) * L - 1:(b + 1) * L, :] for b in range(nblk)]
    alast_rows = alast[0]
    for b in range(1, nblk):
        alast_rows = jnp.where(row_blk == b, alast[b], alast_rows)
    dtw = dt * jnp.exp(alast_rows - acum)
    expand = exp_ref[...]
    dtw_scr[...] = jnp.dot(dtw.astype(BF16), expand, preferred_element_type=F32)
    acum_t = acum.T
    dt_t = dt.T
    tril = lax.broadcasted_iota(jnp.int32, (L, L), 0) >= lax.broadcasted_iota(jnp.int32, (L, L), 1)
    lo_half = lax.broadcasted_iota(jnp.int32, (L, 2 * P), 1) < P

    for b in range(nblk):
        rows = slice(b * L, (b + 1) * L)
        acum_b = acum[rows, :]
        elast = jnp.broadcast_to(jnp.exp(alast[b]), (SUBLANES, LANES))
        dec_all = sum(jnp.dot(piece, expand, preferred_element_type=F32)
                      for piece in _split3_bf16(elast))[0:1, :]
        for g in range(SSM_GROUPS):
            c_g = cm_ref[rows, g * N:(g + 1) * N]
            b_g = bm_ref[rows, g * N:(g + 1) * N]
            cb = lax.dot_general(c_g, b_g, (((1,), (1,)), ((), ())), preferred_element_type=F32)
            st = state_scr[g]
            y_off = jnp.dot(c_g, st.astype(BF16), preferred_element_type=F32)
            for q in range(R // 2):
                h0 = g * R + 2 * q
                ms, es = [], []
                for h in (h0, h0 + 1):
                    acol = jnp.broadcast_to(acum_b[:, h:h + 1], (L, L))
                    seg = acol - acum_t[h:h + 1, rows]
                    decay = jnp.exp(jnp.where(tril, seg, -jnp.inf))
                    ms.append((cb * decay * dt_t[h:h + 1, rows]).astype(BF16))
                    es.append(jnp.exp(acol))
                lhs = jnp.concatenate(ms, axis=1)
                cols = slice(h0 * P, (h0 + 2) * P)
                xp = xs_ref[rows, cols]
                zero = jnp.zeros_like(xp)
                rhs = jnp.concatenate([jnp.where(lo_half, xp, zero), jnp.where(lo_half, zero, xp)], axis=0)
                y_d = jnp.dot(lhs, rhs, preferred_element_type=F32)
                scale = jnp.where(lo_half, es[0], es[1])
                y_scr[rows, cols] = y_d + y_off[:, 2 * q * P:(2 * q + 2) * P] * scale
            gcols = slice(g * gw, (g + 1) * gw)
            xw = (xs_ref[rows, gcols].astype(F32) * dtw_scr[rows, gcols]).astype(BF16)
            contrib = lax.dot_general(b_g, xw, (((0,), (0,)), ((), ())), preferred_element_type=F32)
            state_scr[g] = st * dec_all[:, gcols] + contrib

    y = y_scr[...] + dfull_ref[...] * xs_ref[...].astype(F32)
    zf = z_ref[...].astype(F32)
    y = y * (zf * jax.nn.sigmoid(zf))
    nw = nw_ref[...]
    parts = []
    for g in range(SSM_GROUPS):
        yg = y[:, g * gw:(g + 1) * gw]
        yg = yg * lax.rsqrt(jnp.mean(yg * yg, axis=-1, keepdims=True) + RMS_EPS)
        parts.append((yg * nw[:, g * gw:(g + 1) * gw]).astype(BF16))
    yb = jnp.concatenate(parts, axis=1)
    m = jnp.dot(yb, wo_ref[...], preferred_element_type=F32)
    x1 = _ln(DN_ALPHA * x_ref[...] + m, g1_ref[...], b1_ref[...])
    x1_ref[...] = x1
    x1p_ref[...] = _pack_bf16_pairs(x1)
    first = (pl.program_id(0) == 0) & (s == 0)
    _route(x1, first, wr_ref, br_ref, ri_ref, rg_ref, cnt_ref, cnt_scr)


def mamba_ssd_block(x3, z, xs, bm, cm, dt, a_log, d_skip, norm_w, w_out, g1, b1, w_r, b_r):
    bsz, s_len, d = x3.shape
    d_inner = xs.shape[-1]
    gn = bm.shape[-1]
    tm = TOKEN_TILE
    alog = jnp.pad(a_log, (0, LANES - SSM_HEADS)).reshape(1, LANES)
    dfull = jnp.repeat(d_skip, SSM_HEAD_DIM).reshape(1, d_inner)
    head_of_col = jnp.arange(d_inner, dtype=jnp.int32) // SSM_HEAD_DIM
    expand = (jnp.arange(LANES, dtype=jnp.int32)[:, None] == head_of_col[None, :]).astype(BF16)
    args = (x3, z, xs, bm, cm, dt, alog, dfull, norm_w.reshape(1, d_inner), expand, w_out.astype(BF16),
            g1.reshape(1, d), b1.reshape(1, d), w_r, b_r)
    tok = lambda w: pl.BlockSpec((None, tm, w), lambda b, s: (b, s, 0))
    in_specs = [tok(d), tok(d_inner), tok(d_inner), tok(gn), tok(gn), tok(LANES)] + \
               [_const_spec(a.shape) for a in args[6:]]
    out_shape = (jax.ShapeDtypeStruct((bsz, s_len, d), F32),
                 jax.ShapeDtypeStruct((bsz, s_len, d // 2), U32),
                 jax.ShapeDtypeStruct((bsz, s_len, LANES), jnp.int32),
                 jax.ShapeDtypeStruct((bsz, s_len, LANES), F32),
                 jax.ShapeDtypeStruct((SUBLANES, LANES), F32))
    out_specs = (tok(d), tok(d // 2), tok(LANES), tok(LANES), _const_spec((SUBLANES, LANES)))
    return pl.pallas_call(
        _mamba_ssd_kernel,
        out_shape=out_shape,
        grid=(bsz, s_len // tm),
        in_specs=in_specs,
        out_specs=out_specs,
        scratch_shapes=[pltpu.VMEM((SSM_GROUPS, SSM_STATE, d_inner // SSM_GROUPS), F32),
                        pltpu.VMEM((tm, d_inner), F32),
                        pltpu.VMEM((tm, d_inner), F32),
                        pltpu.VMEM((SUBLANES, LANES), F32)],
        compiler_params=pltpu.CompilerParams(
            dimension_semantics=("arbitrary", "arbitrary"), vmem_limit_bytes=VMEM_LIMIT),
        name="mamba_ssd",
    )(*args)


def _sc_workers():
    info = plsc.get_sparse_core_info()
    return info.num_cores, info.num_subcores


def sc_scatter_rows(x, dest_km, n_rows):
    t, d = x.shape
    nc, ns = _sc_workers()
    per_w, rem = divmod(t, nc * ns)
    assert rem == 0 and per_w % SC_CHUNK == 0
    mesh = plsc.VectorSubcoreMesh(core_axis_name="core", subcore_axis_name="subcore")

    @functools.partial(
        pl.kernel,
        out_type=jax.ShapeDtypeStruct((n_rows, d), x.dtype),
        mesh=mesh,
        scratch_types=[pltpu.VMEM((SC_CHUNK,), jnp.int32) for _ in range(TOP_K)] +
                      [pltpu.VMEM((SC_CHUNK, d), x.dtype), pltpu.SemaphoreType.DMA],
    )
    def scatter(x_hbm, i_hbm, o_hbm, i0, i1, i2, i3, rows_v, sem):
        wid = lax.axis_index("subcore") * nc + lax.axis_index("core")
        base = wid * per_w
        idx = (i0, i1, i2, i3)

        @pl.loop(0, per_w // SC_CHUNK)
        def _(c):
            off = base + c * SC_CHUNK
            pltpu.sync_copy(x_hbm.at[pl.ds(off, SC_CHUNK)], rows_v)
            for k in range(TOP_K):
                pltpu.sync_copy(i_hbm.at[pl.ds(k * t + off, SC_CHUNK)], idx[k])
            copies = [pltpu.make_async_copy(rows_v, o_hbm.at[idx[k]], sem) for k in range(TOP_K)]
            for cp in copies:
                cp.start()
            for cp in copies:
                cp.wait()

    return scatter(x, dest_km)


def sc_gather_rows(table, idx):
    n = idx.shape[0]
    d = table.shape[1]
    nc, ns = _sc_workers()
    per_w, rem = divmod(n, nc * ns)
    n_chunks = per_w // SC_CHUNK
    assert rem == 0 and per_w % (2 * SC_CHUNK) == 0
    mesh = plsc.VectorSubcoreMesh(core_axis_name="core", subcore_axis_name="subcore")

    @functools.partial(
        pl.kernel,
        out_type=jax.ShapeDtypeStruct((n, d), table.dtype),
        mesh=mesh,
        scratch_types=[
            pltpu.VMEM((SC_CHUNK,), jnp.int32), pltpu.VMEM((SC_CHUNK,), jnp.int32),
            pltpu.VMEM((SC_CHUNK, d), table.dtype), pltpu.VMEM((SC_CHUNK, d), table.dtype),
            pltpu.SemaphoreType.DMA, pltpu.SemaphoreType.DMA,
            pltpu.SemaphoreType.DMA, pltpu.SemaphoreType.DMA,
        ],
    )
    def gather(x_hbm, i_hbm, o_hbm, idx0, idx1, rows0, rows1, sg0, sg1, sw0, sw1):
        wid = lax.axis_index("subcore") * nc + lax.axis_index("core")
        base = wid * per_w
        idx = (idx0, idx1)
        rows = (rows0, rows1)
        sg = (sg0, sg1)
        sw = (sw0, sw1)

        def gather_copy(slot):
            return pltpu.make_async_copy(x_hbm.at[idx[slot]], rows[slot], sg[slot])

        def write_copy(c, slot):
            return pltpu.make_async_copy(rows[slot], o_hbm.at[pl.ds(base + c * SC_CHUNK, SC_CHUNK)], sw[slot])

        pltpu.sync_copy(i_hbm.at[pl.ds(base, SC_CHUNK)], idx0)
        gather_copy(0).start()

        @pl.loop(0, n_chunks, step=2)
        def _(c0):
            for slot in range(2):
                c = c0 + slot
                nxt = 1 - slot

                @pl.when(c + 1 < n_chunks)
                def _():
                    pltpu.sync_copy(i_hbm.at[pl.ds(base + (c + 1) * SC_CHUNK, SC_CHUNK)], idx[nxt])

                    @pl.when(c >= 1)
                    def _():
                        write_copy(c - 1, nxt).wait()

                    gather_copy(nxt).start()

                gather_copy(slot).wait()
                write_copy(c, slot).start()

        write_copy(n_chunks - 2, 0).wait()
        write_copy(n_chunks - 1, 1).wait()

    return gather(table, idx)


def _moe_mlp_kernel(be_ref, nv_ref, nu_ref, x_ref, w1_ref, b1_ref, w2_ref, b2_ref, o_ref, w1b, w2b):
    i = pl.program_id(0)
    e = be_ref[i]
    prev = be_ref[jnp.maximum(i - 1, 0)]

    @pl.when((i == 0) | (e != prev))
    def _():
        w1b[...] = w1_ref[...].astype(BF16)
        w2b[...] = w2_ref[...].astype(BF16)

    @pl.when(i < nu_ref[0])
    def _():
        dff = w2b.shape[0]
        row = lax.broadcasted_iota(jnp.int32, x_ref.shape, 0)
        xw = jnp.where(row < nv_ref[i], x_ref[...], jnp.uint32(0))
        xb = _unpack_bf16_pairs(xw).astype(BF16)
        h = jnp.dot(xb, w1b[...], preferred_element_type=F32) + b1_ref[...]
        g = jnp.minimum(h[:, :dff], SWIGLU_LIMIT)
        lin = jnp.clip(h[:, dff:], -SWIGLU_LIMIT, SWIGLU_LIMIT)
        act = (g * jax.nn.sigmoid(SWIGLU_ALPHA * g) * (lin + 1.0)).astype(BF16)
        y = jnp.dot(act, w2b[...], preferred_element_type=F32) + b2_ref[...]
        o_ref[...] = _pack_bf16_pairs(y)


def moe_mlp(xp, block_expert, n_valid, n_used, layer, w1, b1, w2, b2):
    n_rows, dh = xp.shape
    _, n_e, d, two_dff = w1.shape
    dff = two_dff // 2
    n_blocks = n_rows // MOE_BLOCK
    row = lambda i, be, nv, nu: (jnp.minimum(i, nu[0] - 1), 0)
    wsel = lambda i, be, nv, nu: (layer, be[i], 0, 0)
    grid_spec = pltpu.PrefetchScalarGridSpec(
        num_scalar_prefetch=3,
        grid=(n_blocks,),
        in_specs=[
            pl.BlockSpec((MOE_BLOCK, dh), row),
            pl.BlockSpec((None, None, d, two_dff), wsel),
            pl.BlockSpec((None, None, 1, two_dff), wsel),
            pl.BlockSpec((None, None, dff, d), wsel),
            pl.BlockSpec((None, None, 1, d), wsel),
        ],
        out_specs=pl.BlockSpec((MOE_BLOCK, dh), row),
        scratch_shapes=[pltpu.VMEM((d, two_dff), BF16), pltpu.VMEM((dff, d), BF16)],
    )
    return pl.pallas_call(
        _moe_mlp_kernel,
        out_shape=jax.ShapeDtypeStruct((n_rows, dh), U32),
        grid_spec=grid_spec,
        compiler_params=pltpu.CompilerParams(
            dimension_semantics=("arbitrary",), vmem_limit_bytes=VMEM_LIMIT),
        name="moe_mlp",
    )(block_expert, n_valid, n_used, xp, w1, b1.reshape(b1.shape[0], n_e, 1, two_dff),
      w2, b2.reshape(b2.shape[0], n_e, 1, d))


def _combine_kernel(x1_ref, y0_ref, y1_ref, y2_ref, y3_ref, rg_ref, g2_ref, b2_ref, o_ref):
    rg = rg_ref[...]
    f = rg[:, 0:1] * _unpack_bf16_pairs(y0_ref[...])
    for k, y_ref in ((1, y1_ref), (2, y2_ref), (3, y3_ref)):
        f = f + rg[:, k:k + 1] * _unpack_bf16_pairs(y_ref[...])
    o_ref[...] = _ln(DN_ALPHA * x1_ref[...] + f, g2_ref[...], b2_ref[...])


def combine_block(x1, y4p, rg, g2, b2):
    t, d = x1.shape
    tm = TOKEN_TILE
    nt = t // tm
    y_specs = [pl.BlockSpec((tm, d // 2), functools.partial(lambda i, k: (k * nt + i, 0), k=k))
               for k in range(TOP_K)]
    return pl.pallas_call(
        _combine_kernel,
        out_shape=jax.ShapeDtypeStruct((t, d), F32),
        grid=(nt,),
        in_specs=[pl.BlockSpec((tm, d), lambda i: (i, 0))] + y_specs +
                 [pl.BlockSpec((tm, LANES), lambda i: (i, 0)), _const_spec((1, d)), _const_spec((1, d))],
        out_specs=pl.BlockSpec((tm, d), lambda i: (i, 0)),
        compiler_params=pltpu.CompilerParams(
            dimension_semantics=("arbitrary",), vmem_limit_bytes=VMEM_LIMIT),
        name="moe_combine",
    )(x1, y4p, y4p, y4p, y4p, rg, g2.reshape(1, d), b2.reshape(1, d))


def moe_block(x1, x1p, ri, rg, cnt, layer, w1, b1, w2, b2, g2, b2n):
    t, d = x1.shape
    n_assign = t * TOP_K
    n_blocks = -(-n_assign // MOE_BLOCK) + N_EXPERTS
    n_rows = n_blocks * MOE_BLOCK
    ar = jnp.arange(N_EXPERTS, dtype=jnp.int32)
    counts = cnt[0, :N_EXPERTS].astype(jnp.int32)
    padded = (counts + MOE_BLOCK - 1) // MOE_BLOCK * MOE_BLOCK
    pend = jnp.sum(jnp.where(ar[None, :] <= ar[:, None], padded[None, :], 0), axis=1)
    pstart = pend - padded
    top_i = ri[:, :TOP_K]
    dest = jnp.sum(jnp.where(top_i[:, :, None] == ar, pstart, 0), axis=-1) + ri[:, TOP_K:2 * TOP_K]
    dest_km = dest.T.reshape(-1)
    n_used = (pend[-1] // MOE_BLOCK).reshape(1)
    block_start = jnp.arange(n_blocks, dtype=jnp.int32) * MOE_BLOCK
    block_expert = jnp.minimum(
        jnp.sum((pend[None, :] <= block_start[:, None]).astype(jnp.int32), axis=1), N_EXPERTS - 1)
    last = jnp.sum(jnp.where(jnp.arange(n_blocks) == n_used[0] - 1, block_expert, 0))
    block_expert = jnp.where(jnp.arange(n_blocks) < n_used[0], block_expert, last)
    vend = jnp.sum(jnp.where(block_expert[:, None] == ar, (pstart + counts)[None, :], 0), axis=1)
    n_valid = jnp.clip(vend - block_start, 0, MOE_BLOCK)

    xp = sc_scatter_rows(x1p, dest_km, n_rows)
    yp = moe_mlp(xp, block_expert, n_valid, n_used, layer, w1, b1, w2, b2)
    y4p = sc_gather_rows(yp, dest_km)
    return combine_block(x1, y4p, rg, g2, b2n)


def kernel(x, a_w_in, a_b_in, a_ln_g, a_ln_b, a_w_s, a_b_s, a_w_out, a_b_out, b_w_in, b_conv_w, b_conv_b, b_dt_bias, b_a_log, b_d, b_norm_w, b_w_out, moe_w_router, moe_b_router, moe_w1, moe_b1, moe_w2, moe_b2, ln1_g, ln1_b, ln2_g, ln2_b):
    bsz, s_len, d = x.shape
    t = bsz * s_len
    xt = x.reshape(t, d)
    for i in range(DEPTH):
        j = i // 2
        w_r32 = jnp.pad(moe_w_router[i], ((0, 0), (0, LANES - N_EXPERTS)))
        w_r_hi = w_r32.astype(BF16)
        w_r = jnp.stack([w_r_hi, (w_r32 - w_r_hi.astype(F32)).astype(BF16)])
        b_r = jnp.pad(moe_b_router[i], (0, LANES - N_EXPERTS), constant_values=NEG_BIG).reshape(1, LANES)
        if i % 2 == 0:
            x1, x1p, ri, rg, cnt = mixer_a_block(
                xt, a_w_in[j], a_b_in[j], a_ln_g[j], a_ln_b[j], a_w_s[j], a_b_s[j],
                a_w_out[j], a_b_out[j], ln1_g[i], ln1_b[i], w_r, b_r)
        else:
            x3 = xt.reshape(bsz, s_len, d)
            z, xs, bm, cm, dt = mamba_in_block(x3, b_w_in[j], b_conv_w[j], b_conv_b[j], b_dt_bias[j])
            x1, x1p, ri, rg, cnt = mamba_ssd_block(
                x3, z, xs, bm, cm, dt, b_a_log[j], b_d[j], b_norm_w[j],
                b_w_out[j], ln1_g[i], ln1_b[i], w_r, b_r)
            x1 = x1.reshape(t, d)
            x1p = x1p.reshape(t, d // 2)
            ri = ri.reshape(t, LANES)
            rg = rg.reshape(t, LANES)
        xt = moe_block(x1, x1p, ri, rg, cnt, i, moe_w1, moe_b1, moe_w2, moe_b2, ln2_g[i], ln2_b[i])
    return xt.reshape(bsz, s_len, d)
```

```python
import functools
import math

import jax
import jax.numpy as jnp
from jax import lax
from jax.experimental import pallas as pl
from jax.experimental.pallas import tpu as pltpu
from jax.experimental.pallas import tpu_sc as plsc

F32 = jnp.float32
BF16 = jnp.bfloat16
U32 = jnp.uint32

DEPTH = 4
N_EXPERTS = 32
TOP_K = 4
MOE_BLOCK = 512
SWIGLU_LIMIT = 7.0
SWIGLU_ALPHA = 1.702
A_BLOCK = 128
A_GROUPS = 8
CHUNK = 64
SSD_BLOCK = 128
SSM_HEADS = 32
SSM_HEAD_DIM = 64
SSM_GROUPS = 4
SSM_STATE = 128
SSM_CONV = 4
DN_ALPHA = (2 * DEPTH) ** 0.25
LN_EPS = 1e-5
RMS_EPS = 1e-5

LANES = 128
SUBLANES = 8
TOKEN_TILE = 256
VMEM_LIMIT = 56 * 1024 * 1024
SC_CHUNK = 64
CONV_COLS = 256


def _ln(x, g, b):
    mu = jnp.mean(x, axis=-1, keepdims=True)
    xc = x - mu
    var = jnp.mean(xc * xc, axis=-1, keepdims=True)
    return xc * lax.rsqrt(var + LN_EPS) * g + b


def _gelu(x):
    return 0.5 * x * (1.0 + lax.erf(x * (1.0 / math.sqrt(2.0))))


def _pack_bf16_pairs(y):
    h = y.shape[1] // 2
    lo = lax.bitcast_convert_type(y[:, :h].astype(BF16).astype(F32), U32) >> 16
    hi = lax.bitcast_convert_type(y[:, h:].astype(BF16).astype(F32), U32) & jnp.uint32(0xFFFF0000)
    return hi | lo


def _unpack_bf16_pairs(w):
    lo = lax.bitcast_convert_type(w << 16, F32)
    hi = lax.bitcast_convert_type(w & jnp.uint32(0xFFFF0000), F32)
    return jnp.concatenate([lo, hi], axis=1)


def _route(x1, first, wr_ref, brc_ref, ri_ref, rg_ref, cnt_ref, cnt_scr):
    @pl.when(first)
    def _():
        cnt_scr[...] = jnp.zeros(cnt_scr.shape, F32)

    x_hi = x1.astype(BF16)
    x_lo = (x1 - x_hi.astype(F32)).astype(BF16)
    logits_tok = (jnp.dot(x_hi, wr_ref[0], preferred_element_type=F32)
                  + (jnp.dot(x_hi, wr_ref[1], preferred_element_type=F32)
                     + jnp.dot(x_lo, wr_ref[0], preferred_element_type=F32)))
    n_e = brc_ref.shape[0]
    logits = logits_tok.T[:n_e, :] + brc_ref[...]
    tm = logits.shape[1]
    sub_f = lax.broadcasted_iota(jnp.int32, logits.shape, 0).astype(F32)
    vals, idxs, hots = [], [], []
    l = logits
    for _ in range(TOP_K):
        m = jnp.max(l, axis=0, keepdims=True)
        i = jnp.min(jnp.where(l == m, sub_f, float(n_e)), axis=0, keepdims=True)
        hot = sub_f == i
        vals.append(m)
        idxs.append(i)
        hots.append(hot.astype(F32))
        l = jnp.where(hot, -jnp.inf, l)
    es = [jnp.exp(v - vals[0]) for v in vals]
    tot = es[0] + es[1] + es[2] + es[3]
    cnt = (hots[0] + hots[1] + hots[2] + hots[3]).astype(BF16)
    r_i = lax.broadcasted_iota(jnp.int32, (tm, tm), 0)
    c_i = lax.broadcasted_iota(jnp.int32, (tm, tm), 1)
    before = jnp.dot(cnt, (r_i < c_i).astype(BF16), preferred_element_type=F32)
    before = before + jnp.concatenate([cnt_scr[...]] * (tm // LANES), axis=1)
    row = lax.broadcasted_iota(jnp.int32, (2 * TOP_K, tm), 0)
    ri = jnp.zeros((2 * TOP_K, tm), F32)
    rg = jnp.zeros((2 * TOP_K, tm), F32)
    for k in range(TOP_K):
        rank = jnp.sum(before * hots[k], axis=0, keepdims=True)
        ri = jnp.where(row == k, idxs[k], ri)
        ri = jnp.where(row == TOP_K + k, rank, ri)
        rg = jnp.where(row == k, es[k] / tot, rg)
    ri_ref[...] = ri.astype(jnp.int32)
    rg_ref[...] = jnp.concatenate([rg, jnp.zeros((LANES - 2 * TOP_K, tm), F32)], axis=0).T
    total = cnt_scr[...] + jnp.dot(cnt, jnp.ones((tm, LANES), BF16), preferred_element_type=F32)
    cnt_scr[...] = total
    cnt_ref[...] = total


def _const_spec(shape):
    nd = len(shape)
    return pl.BlockSpec(shape, lambda *_: (0,) * nd)


def _mixer_a_kernel(x_ref, wu_ref, wv_ref, bu_ref, bv_ref, lg_ref, lb_ref, ws_ref, bst_ref,
                    wo_ref, bo_ref, g1_ref, b1_ref, wr_ref, br_ref,
                    x1_ref, x1p_ref, ri_ref, rg_ref, cnt_ref, v_scr, o_scr, cnt_scr):
    x = x_ref[...]
    xb = x.astype(BF16)
    tm, dff = v_scr.shape
    gd = dff // A_GROUPS
    gcols = [slice(g * gd, (g + 1) * gd) for g in range(A_GROUPS)]
    s1 = jnp.zeros((tm, 1), F32)
    for cols in gcols:
        v_g = _gelu(jnp.dot(xb, wv_ref[:, cols], preferred_element_type=F32) + bv_ref[:, cols])
        v_scr[:, cols] = v_g
        s1 = s1 + jnp.sum(v_g, axis=-1, keepdims=True)
    mu = s1 * (1.0 / dff)
    s2 = jnp.zeros((tm, 1), F32)
    for cols in gcols:
        dv = v_scr[:, cols] - mu
        s2 = s2 + jnp.sum(dv * dv, axis=-1, keepdims=True)
    rstd = lax.rsqrt(s2 * (1.0 / dff) + LN_EPS)
    pi = lax.broadcasted_iota(jnp.int32, (A_BLOCK, A_BLOCK), 0) // CHUNK
    pj = lax.broadcasted_iota(jnp.int32, (A_BLOCK, A_BLOCK), 1) // CHUNK
    mask = pj <= pi
    bst = bst_ref[...]
    for g, cols in enumerate(gcols):
        u_g = _gelu(jnp.dot(xb, wu_ref[:, cols], preferred_element_type=F32) + bu_ref[:, cols])
        vb_g = ((v_scr[:, cols] - mu) * rstd * lg_ref[:, cols] + lb_ref[:, cols]).astype(BF16)
        wm = jnp.where(mask, ws_ref[g], 0.0).astype(BF16)
        for n in range(tm // A_BLOCK):
            rows = slice(n * A_BLOCK, (n + 1) * A_BLOCK)
            sv = jnp.dot(wm, vb_g[rows, :], preferred_element_type=F32) + bst[:, g:g + 1]
            o_scr[rows, cols] = (u_g[rows, :] * sv).astype(BF16)
    m = jnp.dot(o_scr[...], wo_ref[...], preferred_element_type=F32) + bo_ref[...]
    x1 = _ln(DN_ALPHA * x + m, g1_ref[...], b1_ref[...])
    x1_ref[...] = x1
    x1p_ref[...] = _pack_bf16_pairs(x1)
    _route(x1, pl.program_id(0) == 0, wr_ref, br_ref, ri_ref, rg_ref, cnt_ref, cnt_scr)


def mixer_a_block(xt, w_in, b_in, ln_g, ln_b, w_s, b_s, w_out, b_out, g1, b1, w_r, b_r):
    t, d = xt.shape
    dff = w_out.shape[0]
    tm = TOKEN_TILE
    wu = w_in[:, :dff].astype(BF16)
    wv = w_in[:, dff:].astype(BF16)
    bu = b_in[:dff].reshape(1, dff)
    bv = b_in[dff:].reshape(1, dff)
    args = (xt, wu, wv, bu, bv, ln_g.reshape(1, dff), ln_b.reshape(1, dff), w_s, b_s.T,
            w_out.astype(BF16), b_out.reshape(1, d), g1.reshape(1, d), b1.reshape(1, d), w_r, b_r)
    tok = lambda w: pl.BlockSpec((tm, w), lambda i: (i, 0))
    route = pl.BlockSpec((2 * TOP_K, tm), lambda i: (0, i))
    in_specs = [tok(d)] + [_const_spec(a.shape) for a in args[1:]]
    out_shape = (jax.ShapeDtypeStruct((t, d), F32),
                 jax.ShapeDtypeStruct((t, d // 2), U32),
                 jax.ShapeDtypeStruct((2 * TOP_K, t), jnp.int32),
                 jax.ShapeDtypeStruct((t, LANES), F32),
                 jax.ShapeDtypeStruct((N_EXPERTS, LANES), F32))
    out_specs = (tok(d), tok(d // 2), route, tok(LANES), _const_spec((N_EXPERTS, LANES)))
    return pl.pallas_call(
        _mixer_a_kernel,
        out_shape=out_shape,
        grid=(t // tm,),
        in_specs=in_specs,
        out_specs=out_specs,
        scratch_shapes=[pltpu.VMEM((tm, dff), F32), pltpu.VMEM((tm, dff), BF16),
                        pltpu.VMEM((N_EXPERTS, LANES), F32)],
        compiler_params=pltpu.CompilerParams(
            dimension_semantics=("arbitrary",), vmem_limit_bytes=VMEM_LIMIT),
        name="mixer_a",
    )(*args)


def _mamba_in_kernel(x_ref, wz_ref, wx_ref, wdt_ref, cw_ref, cb_ref, dtb_ref,
                     z_ref, xs_ref, bm_ref, cm_ref, dt_ref, *ext_scrs):
    s = pl.program_id(1)
    tm = x_ref.shape[0]
    d_inner = xs_ref.shape[1]
    gn = bm_ref.shape[1]
    @pl.when((pl.program_id(0) == 0) & (s == 0))
    def _():
        for ext in ext_scrs:
            ext[...] = jnp.zeros(ext.shape, F32)

    xb = x_ref[...].astype(BF16)
    dt = jnp.dot(xb, wdt_ref[...], preferred_element_type=F32) + dtb_ref[...]
    dt_ref[...] = jnp.maximum(dt, 0.0) + jnp.log1p(jnp.exp(-jnp.abs(dt)))

    cw = cw_ref[...]
    cbias = cb_ref[...]
    w = CONV_COLS
    for c, ext in enumerate(ext_scrs):
        cols = slice(c * w, (c + 1) * w)
        ext[0:8, :] = jnp.where(s > 0, ext[tm:tm + 8, :], 0.0)
        ext[8:8 + tm, :] = jnp.dot(xb, wx_ref[:, cols], preferred_element_type=F32)
        acc = cbias[:, cols] + cw[0:1, cols] * ext[5:5 + tm, :]
        for j in range(1, SSM_CONV):
            acc = acc + cw[j:j + 1, cols] * ext[5 + j:5 + j + tm, :]
        act = (acc * jax.nn.sigmoid(acc)).astype(BF16)
        lo = c * w
        if lo < d_inner:
            xs_ref[:, lo:lo + w] = act
        elif lo < d_inner + gn:
            bm_ref[:, lo - d_inner:lo - d_inner + w] = act
        else:
            cm_ref[:, lo - d_inner - gn:lo - d_inner - gn + w] = act
        if lo < d_inner:
            z_ref[:, cols] = jnp.dot(xb, wz_ref[:, cols], preferred_element_type=F32).astype(BF16)


def mamba_in_block(x3, w_in, conv_w, conv_b, dt_bias):
    bsz, s_len, d = x3.shape
    d_inner = SSM_HEADS * SSM_HEAD_DIM
    gn = SSM_GROUPS * SSM_STATE
    conv_dim = d_inner + 2 * gn
    tm = TOKEN_TILE
    wz = w_in[:, :d_inner].astype(BF16)
    wx = w_in[:, d_inner:d_inner + conv_dim].astype(BF16)
    wdt = jnp.pad(w_in[:, d_inner + conv_dim:], ((0, 0), (0, LANES - SSM_HEADS))).astype(BF16)
    dtb = jnp.pad(dt_bias, (0, LANES - SSM_HEADS)).reshape(1, LANES)
    args = (x3, wz, wx, wdt, conv_w, conv_b.reshape(1, conv_dim), dtb)
    tok = lambda w: pl.BlockSpec((None, tm, w), lambda b, s: (b, s, 0))
    in_specs = [tok(d)] + [_const_spec(a.shape) for a in args[1:]]
    out_shape = (jax.ShapeDtypeStruct((bsz, s_len, d_inner), BF16),
                 jax.ShapeDtypeStruct((bsz, s_len, d_inner), BF16),
                 jax.ShapeDtypeStruct((bsz, s_len, gn), BF16),
                 jax.ShapeDtypeStruct((bsz, s_len, gn), BF16),
                 jax.ShapeDtypeStruct((bsz, s_len, LANES), F32))
    out_specs = (tok(d_inner), tok(d_inner), tok(gn), tok(gn), tok(LANES))
    return pl.pallas_call(
        _mamba_in_kernel,
        out_shape=out_shape,
        grid=(bsz, s_len // tm),
        in_specs=in_specs,
        out_specs=out_specs,
        scratch_shapes=[pltpu.VMEM((tm + 8, CONV_COLS), F32) for _ in range(conv_dim // CONV_COLS)],
        compiler_params=pltpu.CompilerParams(
            dimension_semantics=("arbitrary", "arbitrary"), vmem_limit_bytes=VMEM_LIMIT),
        name="mamba_in",
    )(*args)


def _split3_bf16(q):
    hi = q.astype(BF16)
    r1 = q - hi.astype(F32)
    mid = r1.astype(BF16)
    lo = (r1 - mid.astype(F32)).astype(BF16)
    return hi, mid, lo


def _mamba_ssd_kernel(x_ref, z_ref, xs_ref, bm_ref, cm_ref, dt_ref, alog_ref, dfull_ref, nw_ref, exp_ref,
                      wo_ref, g1_ref, b1_ref, wr_ref, br_ref,
                      x1_ref, x1p_ref, ri_ref, rg_ref, cnt_ref, state_scr, y_scr, dtw_scr, cnt_scr):
    s = pl.program_id(1)
    tm = x_ref.shape[0]
    L, P, N = SSD_BLOCK, SSM_HEAD_DIM, SSM_STATE
    R = SSM_HEADS // SSM_GROUPS
    gw = R * P
    nblk = tm // L

    @pl.when(s == 0)
    def _():
        state_scr[...] = jnp.zeros(state_scr.shape, F32)

    a_row = -jnp.exp(alog_ref[...])
    dt = dt_ref[...]
    adt = dt * a_row
    r_i = lax.broadcasted_iota(jnp.int32, (tm, tm), 0)
    c_i = lax.broadcasted_iota(jnp.int32, (tm, tm), 1)
    btril = ((r_i >= c_i) & (r_i // L == c_i // L)).astype(F32)
    acum = jnp.dot(btril, adt, preferred_element_type=F32,
                   precision=lax.Precision.HIGHEST)
    row_blk = lax.broadcasted_iota(jnp.int32, (tm, LANES), 0) // L
    alast = [acum[(b + 1) * L - 1:(b + 1) * L, :] for b in range(nblk)]
    alast_rows = alast[0]
    for b in range(1, nblk):
        alast_rows = jnp.where(row_blk == b, alast[b], alast_rows)
    dtw = dt * jnp.exp(alast_rows - acum)
    expand = exp_ref[...]
    dtw_scr[...] = jnp.dot(dtw.astype(BF16), expand, preferred_element_type=F32)
    acum_t = acum.T
    dt_t = dt.T
    tril = lax.broadcasted_iota(jnp.int32, (L, L), 0) >= lax.broadcasted_iota(jnp.int32, (L, L), 1)
    lo_half = lax.broadcasted_iota(jnp.int32, (L, 2 * P), 1) < P

    for b in range(nblk):
        rows = slice(b * L, (b + 1) * L)
        acum_b = acum[rows, :]
        elast = jnp.broadcast_to(jnp.exp(alast[b]), (SUBLANES, LANES))
        dec_all = sum(jnp.dot(piece, expand, preferred_element_type=F32)
                      for piece in _split3_bf16(elast))[0:1, :]
        for g in range(SSM_GROUPS):
            c_g = cm_ref[rows, g * N:(g + 1) * N]
            b_g = bm_ref[rows, g * N:(g + 1) * N]
            cb = lax.dot_general(c_g, b_g, (((1,), (1,)), ((), ())), preferred_element_type=F32)
            st = state_scr[g]
            y_off = jnp.dot(c_g, st.astype(BF16), preferred_element_type=F32)
            for q in range(R // 2):
                h0 = g * R + 2 * q
                ms, es = [], []
                for h in (h0, h0 + 1):
                    acol = jnp.broadcast_to(acum_b[:, h:h + 1], (L, L))
                    seg = acol - acum_t[h:h + 1, rows]
                    decay = jnp.exp(jnp.where(tril, seg, -jnp.inf))
                    ms.append((cb * decay * dt_t[h:h + 1, rows]).astype(BF16))
                    es.append(jnp.exp(acol))
                lhs = jnp.concatenate(ms, axis=1)
                cols = slice(h0 * P, (h0 + 2) * P)
                xp = xs_ref[rows, cols]
                zero = jnp.zeros_like(xp)
                rhs = jnp.concatenate([jnp.where(lo_half, xp, zero), jnp.where(lo_half, zero, xp)], axis=0)
                y_d = jnp.dot(lhs, rhs, preferred_element_type=F32)
                scale = jnp.where(lo_half, es[0], es[1])
                y_scr[rows, cols] = y_d + y_off[:, 2 * q * P:(2 * q + 2) * P] * scale
            gcols = slice(g * gw, (g + 1) * gw)
            xw = (xs_ref[rows, gcols].astype(F32) * dtw_scr[rows, gcols]).astype(BF16)
            contrib = lax.dot_general(b_g, xw, (((0,), (0,)), ((), ())), preferred_element_type=F32)
            state_scr[g] = st * dec_all[:, gcols] + contrib

    y = y_scr[...] + dfull_ref[...] * xs_ref[...].astype(F32)
    zf = z_ref[...].astype(F32)
    y = y * (zf * jax.nn.sigmoid(zf))
    nw = nw_ref[...]
    parts = []
    for g in range(SSM_GROUPS):
        yg = y[:, g * gw:(g + 1) * gw]
        yg = yg * lax.rsqrt(jnp.mean(yg * yg, axis=-1, keepdims=True) + RMS_EPS)
        parts.append((yg * nw[:, g * gw:(g + 1) * gw]).astype(BF16))
    yb = jnp.concatenate(parts, axis=1)
    m = jnp.dot(yb, wo_ref[...], preferred_element_type=F32)
    x1 = _ln(DN_ALPHA * x_ref[...] + m, g1_ref[...], b1_ref[...])
    x1_ref[...] = x1
    x1p_ref[...] = _pack_bf16_pairs(x1)
    first = (pl.program_id(0) == 0) & (s == 0)
    _route(x1, first, wr_ref, br_ref, ri_ref, rg_ref, cnt_ref, cnt_scr)


def mamba_ssd_block(x3, z, xs, bm, cm, dt, a_log, d_skip, norm_w, w_out, g1, b1, w_r, b_r):
    bsz, s_len, d = x3.shape
    d_inner = xs.shape[-1]
    gn = bm.shape[-1]
    tm = TOKEN_TILE
    alog = jnp.pad(a_log, (0, LANES - SSM_HEADS)).reshape(1, LANES)
    dfull = jnp.repeat(d_skip, SSM_HEAD_DIM).reshape(1, d_inner)
    head_of_col = jnp.arange(d_inner, dtype=jnp.int32) // SSM_HEAD_DIM
    expand = (jnp.arange(LANES, dtype=jnp.int32)[:, None] == head_of_col[None, :]).astype(BF16)
    args = (x3, z, xs, bm, cm, dt, alog, dfull, norm_w.reshape(1, d_inner), expand, w_out.astype(BF16),
            g1.reshape(1, d), b1.reshape(1, d), w_r, b_r)
    tok = lambda w: pl.BlockSpec((None, tm, w), lambda b, s: (b, s, 0))
    in_specs = [tok(d), tok(d_inner), tok(d_inner), tok(gn), tok(gn), tok(LANES)] + \
               [_const_spec(a.shape) for a in args[6:]]
    ns = s_len // tm
    route = pl.BlockSpec((2 * TOP_K, tm), lambda b, s: (0, b * ns + s))
    out_shape = (jax.ShapeDtypeStruct((bsz, s_len, d), F32),
                 jax.ShapeDtypeStruct((bsz, s_len, d // 2), U32),
                 jax.ShapeDtypeStruct((2 * TOP_K, bsz * s_len), jnp.int32),
                 jax.ShapeDtypeStruct((bsz, s_len, LANES), F32),
                 jax.ShapeDtypeStruct((N_EXPERTS, LANES), F32))
    out_specs = (tok(d), tok(d // 2), route, tok(LANES), _const_spec((N_EXPERTS, LANES)))
    return pl.pallas_call(
        _mamba_ssd_kernel,
        out_shape=out_shape,
        grid=(bsz, ns),
        in_specs=in_specs,
        out_specs=out_specs,
        scratch_shapes=[pltpu.VMEM((SSM_GROUPS, SSM_STATE, d_inner // SSM_GROUPS), F32),
                        pltpu.VMEM((tm, d_inner), F32),
                        pltpu.VMEM((tm, d_inner), F32),
                        pltpu.VMEM((N_EXPERTS, LANES), F32)],
        compiler_params=pltpu.CompilerParams(
            dimension_semantics=("arbitrary", "arbitrary"), vmem_limit_bytes=VMEM_LIMIT),
        name="mamba_ssd",
    )(*args)


def _sc_workers():
    info = plsc.get_sparse_core_info()
    return info.num_cores, info.num_subcores


def sc_scatter_rows(x, dest_km, n_rows):
    t, d = x.shape
    nc, ns = _sc_workers()
    per_w, rem = divmod(t, nc * ns)
    assert rem == 0 and per_w % SC_CHUNK == 0
    mesh = plsc.VectorSubcoreMesh(core_axis_name="core", subcore_axis_name="subcore")

    @functools.partial(
        pl.kernel,
        out_type=jax.ShapeDtypeStruct((n_rows, d), x.dtype),
        mesh=mesh,
        scratch_types=[pltpu.VMEM((SC_CHUNK,), jnp.int32) for _ in range(TOP_K)] +
                      [pltpu.VMEM((SC_CHUNK, d), x.dtype), pltpu.SemaphoreType.DMA],
    )
    def scatter(x_hbm, i_hbm, o_hbm, i0, i1, i2, i3, rows_v, sem):
        wid = lax.axis_index("subcore") * nc + lax.axis_index("core")
        base = wid * per_w
        idx = (i0, i1, i2, i3)

        @pl.loop(0, per_w // SC_CHUNK)
        def _(c):
            off = base + c * SC_CHUNK
            pltpu.sync_copy(x_hbm.at[pl.ds(off, SC_CHUNK)], rows_v)
            for k in range(TOP_K):
                pltpu.sync_copy(i_hbm.at[pl.ds(k * t + off, SC_CHUNK)], idx[k])
            copies = [pltpu.make_async_copy(rows_v, o_hbm.at[idx[k]], sem) for k in range(TOP_K)]
            for cp in copies:
                cp.start()
            for cp in copies:
                cp.wait()

    return scatter(x, dest_km)


def sc_gather_rows(table, idx):
    n = idx.shape[0]
    d = table.shape[1]
    nc, ns = _sc_workers()
    per_w, rem = divmod(n, nc * ns)
    n_chunks = per_w // SC_CHUNK
    assert rem == 0 and per_w % (2 * SC_CHUNK) == 0
    mesh = plsc.VectorSubcoreMesh(core_axis_name="core", subcore_axis_name="subcore")

    @functools.partial(
        pl.kernel,
        out_type=jax.ShapeDtypeStruct((n, d), table.dtype),
        mesh=mesh,
        scratch_types=[
            pltpu.VMEM((SC_CHUNK,), jnp.int32), pltpu.VMEM((SC_CHUNK,), jnp.int32),
            pltpu.VMEM((SC_CHUNK, d), table.dtype), pltpu.VMEM((SC_CHUNK, d), table.dtype),
            pltpu.SemaphoreType.DMA, pltpu.SemaphoreType.DMA,
            pltpu.SemaphoreType.DMA, pltpu.SemaphoreType.DMA,
        ],
    )
    def gather(x_hbm, i_hbm, o_hbm, idx0, idx1, rows0, rows1, sg0, sg1, sw0, sw1):
        wid = lax.axis_index("subcore") * nc + lax.axis_index("core")
        base = wid * per_w
        idx = (idx0, idx1)
        rows = (rows0, rows1)
        sg = (sg0, sg1)
        sw = (sw0, sw1)

        def gather_copy(slot):
            return pltpu.make_async_copy(x_hbm.at[idx[slot]], rows[slot], sg[slot])

        def write_copy(c, slot):
            return pltpu.make_async_copy(rows[slot], o_hbm.at[pl.ds(base + c * SC_CHUNK, SC_CHUNK)], sw[slot])

        pltpu.sync_copy(i_hbm.at[pl.ds(base, SC_CHUNK)], idx0)
        gather_copy(0).start()

        @pl.loop(0, n_chunks, step=2)
        def _(c0):
            for slot in range(2):
                c = c0 + slot
                nxt = 1 - slot

                @pl.when(c + 1 < n_chunks)
                def _():
                    pltpu.sync_copy(i_hbm.at[pl.ds(base + (c + 1) * SC_CHUNK, SC_CHUNK)], idx[nxt])

                    @pl.when(c >= 1)
                    def _():
                        write_copy(c - 1, nxt).wait()

                    gather_copy(nxt).start()

                gather_copy(slot).wait()
                write_copy(c, slot).start()

        write_copy(n_chunks - 2, 0).wait()
        write_copy(n_chunks - 1, 1).wait()

    return gather(table, idx)


def _moe_mlp_kernel(be_ref, nv_ref, nu_ref, x_ref, w1_ref, b1_ref, w2_ref, b2_ref, o_ref, w1b, w2b):
    i = pl.program_id(0)
    e = be_ref[i]
    prev = be_ref[jnp.maximum(i - 1, 0)]

    @pl.when((i == 0) | (e != prev))
    def _():
        w1b[...] = w1_ref[...].astype(BF16)
        w2b[...] = w2_ref[...].astype(BF16)

    @pl.when(i < nu_ref[0])
    def _():
        dff = w2b.shape[0]
        row = lax.broadcasted_iota(jnp.int32, x_ref.shape, 0)
        xw = jnp.where(row < nv_ref[i], x_ref[...], jnp.uint32(0))
        xb = _unpack_bf16_pairs(xw).astype(BF16)
        h = jnp.dot(xb, w1b[...], preferred_element_type=F32) + b1_ref[...]
        g = jnp.minimum(h[:, :dff], SWIGLU_LIMIT)
        lin = jnp.clip(h[:, dff:], -SWIGLU_LIMIT, SWIGLU_LIMIT)
        act = (g * jax.nn.sigmoid(SWIGLU_ALPHA * g) * (lin + 1.0)).astype(BF16)
        y = jnp.dot(act, w2b[...], preferred_element_type=F32) + b2_ref[...]
        o_ref[...] = _pack_bf16_pairs(y)


def moe_mlp(xp, block_expert, n_valid, n_used, layer, w1, b1, w2, b2):
    n_rows, dh = xp.shape
    _, n_e, d, two_dff = w1.shape
    dff = two_dff // 2
    n_blocks = n_rows // MOE_BLOCK
    row = lambda i, be, nv, nu: (jnp.minimum(i, nu[0] - 1), 0)
    wsel = lambda i, be, nv, nu: (layer, be[i], 0, 0)
    grid_spec = pltpu.PrefetchScalarGridSpec(
        num_scalar_prefetch=3,
        grid=(n_blocks,),
        in_specs=[
            pl.BlockSpec((MOE_BLOCK, dh), row),
            pl.BlockSpec((None, None, d, two_dff), wsel),
            pl.BlockSpec((None, None, 1, two_dff), wsel),
            pl.BlockSpec((None, None, dff, d), wsel),
            pl.BlockSpec((None, None, 1, d), wsel),
        ],
        out_specs=pl.BlockSpec((MOE_BLOCK, dh), row),
        scratch_shapes=[pltpu.VMEM((d, two_dff), BF16), pltpu.VMEM((dff, d), BF16)],
    )
    return pl.pallas_call(
        _moe_mlp_kernel,
        out_shape=jax.ShapeDtypeStruct((n_rows, dh), U32),
        grid_spec=grid_spec,
        compiler_params=pltpu.CompilerParams(
            dimension_semantics=("arbitrary",), vmem_limit_bytes=VMEM_LIMIT),
        name="moe_mlp",
    )(block_expert, n_valid, n_used, xp, w1, b1.reshape(b1.shape[0], n_e, 1, two_dff),
      w2, b2.reshape(b2.shape[0], n_e, 1, d))


def _combine_kernel(x1_ref, y0_ref, y1_ref, y2_ref, y3_ref, rg_ref, g2_ref, b2_ref, o_ref):
    rg = rg_ref[...]
    f = rg[:, 0:1] * _unpack_bf16_pairs(y0_ref[...])
    for k, y_ref in ((1, y1_ref), (2, y2_ref), (3, y3_ref)):
        f = f + rg[:, k:k + 1] * _unpack_bf16_pairs(y_ref[...])
    o_ref[...] = _ln(DN_ALPHA * x1_ref[...] + f, g2_ref[...], b2_ref[...])


def combine_block(x1, y4p, rg, g2, b2):
    t, d = x1.shape
    tm = TOKEN_TILE
    nt = t // tm
    y_specs = [pl.BlockSpec((tm, d // 2), functools.partial(lambda i, k: (k * nt + i, 0), k=k))
               for k in range(TOP_K)]
    return pl.pallas_call(
        _combine_kernel,
        out_shape=jax.ShapeDtypeStruct((t, d), F32),
        grid=(nt,),
        in_specs=[pl.BlockSpec((tm, d), lambda i: (i, 0))] + y_specs +
                 [pl.BlockSpec((tm, LANES), lambda i: (i, 0)), _const_spec((1, d)), _const_spec((1, d))],
        out_specs=pl.BlockSpec((tm, d), lambda i: (i, 0)),
        compiler_params=pltpu.CompilerParams(
            dimension_semantics=("arbitrary",), vmem_limit_bytes=VMEM_LIMIT),
        name="moe_combine",
    )(x1, y4p, y4p, y4p, y4p, rg, g2.reshape(1, d), b2.reshape(1, d))


def moe_block(x1, x1p, ri, rg, cnt, layer, w1, b1, w2, b2, g2, b2n):
    t, d = x1.shape
    n_assign = t * TOP_K
    n_blocks = -(-n_assign // MOE_BLOCK) + N_EXPERTS
    n_rows = n_blocks * MOE_BLOCK
    ar = jnp.arange(N_EXPERTS, dtype=jnp.int32)
    counts = cnt[:, 0].astype(jnp.int32)
    padded = (counts + MOE_BLOCK - 1) // MOE_BLOCK * MOE_BLOCK
    pend = jnp.sum(jnp.where(ar[None, :] <= ar[:, None], padded[None, :], 0), axis=1)
    pstart = pend - padded
    top_i = ri[:TOP_K]
    dest = jnp.sum(jnp.where(top_i[:, :, None] == ar, pstart, 0), axis=-1) + ri[TOP_K:]
    dest_km = dest.reshape(-1)
    n_used = (pend[-1] // MOE_BLOCK).reshape(1)
    block_start = jnp.arange(n_blocks, dtype=jnp.int32) * MOE_BLOCK
    block_expert = jnp.minimum(
        jnp.sum((pend[None, :] <= block_start[:, None]).astype(jnp.int32), axis=1), N_EXPERTS - 1)
    last = jnp.sum(jnp.where(jnp.arange(n_blocks) == n_used[0] - 1, block_expert, 0))
    block_expert = jnp.where(jnp.arange(n_blocks) < n_used[0], block_expert, last)
    vend = jnp.sum(jnp.where(block_expert[:, None] == ar, (pstart + counts)[None, :], 0), axis=1)
    n_valid = jnp.clip(vend - block_start, 0, MOE_BLOCK)

    xp = sc_scatter_rows(x1p, dest_km, n_rows)
    yp = moe_mlp(xp, block_expert, n_valid, n_used, layer, w1, b1, w2, b2)
    y4p = sc_gather_rows(yp, dest_km)
    return combine_block(x1, y4p, rg, g2, b2n)


def kernel(x, a_w_in, a_b_in, a_ln_g, a_ln_b, a_w_s, a_b_s, a_w_out, a_b_out, b_w_in, b_conv_w, b_conv_b, b_dt_bias, b_a_log, b_d, b_norm_w, b_w_out, moe_w_router, moe_b_router, moe_w1, moe_b1, moe_w2, moe_b2, ln1_g, ln1_b, ln2_g, ln2_b):
    bsz, s_len, d = x.shape
    t = bsz * s_len
    xt = x.reshape(t, d)
    for i in range(DEPTH):
        j = i // 2
        w_r32 = jnp.pad(moe_w_router[i], ((0, 0), (0, LANES - N_EXPERTS)))
        w_r_hi = w_r32.astype(BF16)
        w_r = jnp.stack([w_r_hi, (w_r32 - w_r_hi.astype(F32)).astype(BF16)])
        b_r = moe_b_router[i].reshape(N_EXPERTS, 1)
        if i % 2 == 0:
            x1, x1p, ri, rg, cnt = mixer_a_block(
                xt, a_w_in[j], a_b_in[j], a_ln_g[j], a_ln_b[j], a_w_s[j], a_b_s[j],
                a_w_out[j], a_b_out[j], ln1_g[i], ln1_b[i], w_r, b_r)
        else:
            x3 = xt.reshape(bsz, s_len, d)
            z, xs, bm, cm, dt = mamba_in_block(x3, b_w_in[j], b_conv_w[j], b_conv_b[j], b_dt_bias[j])
            x1, x1p, ri, rg, cnt = mamba_ssd_block(
                x3, z, xs, bm, cm, dt, b_a_log[j], b_d[j], b_norm_w[j],
                b_w_out[j], ln1_g[i], ln1_b[i], w_r, b_r)
            x1 = x1.reshape(t, d)
            x1p = x1p.reshape(t, d // 2)
            rg = rg.reshape(t, LANES)
        xt = moe_block(x1, x1p, ri, rg, cnt, i, moe_w1, moe_b1, moe_w2, moe_b2, ln2_g[i], ln2_b[i])
    return xt.reshape(bsz, s_len, d)
```

```python
import functools
import math

import jax
import jax.numpy as jnp
from jax import lax
from jax.experimental import pallas as pl
from jax.experimental.pallas import tpu as pltpu
from jax.experimental.pallas import tpu_sc as plsc

F32 = jnp.float32
BF16 = jnp.bfloat16
U32 = jnp.uint32

DEPTH = 4
N_EXPERTS = 32
TOP_K = 4
MOE_BLOCK = 512
MOE_SUB = 256
N_SPLITS = 2
SWIGLU_LIMIT = 7.0
SWIGLU_ALPHA = 1.702
A_BLOCK = 128
A_GROUPS = 8
CHUNK = 64
SSD_BLOCK = 128
SSM_HEADS = 32
SSM_HEAD_DIM = 64
SSM_GROUPS = 4
SSM_STATE = 128
SSM_CONV = 4
DN_ALPHA = (2 * DEPTH) ** 0.25
LN_EPS = 1e-5
RMS_EPS = 1e-5

LANES = 128
SUBLANES = 8
TOKEN_TILE = 256
VMEM_LIMIT = 56 * 1024 * 1024
SC_CHUNK = 64
CONV_COLS = 256


def _ln(x, g, b):
    mu = jnp.mean(x, axis=-1, keepdims=True)
    xc = x - mu
    var = jnp.mean(xc * xc, axis=-1, keepdims=True)
    return xc * lax.rsqrt(var + LN_EPS) * g + b


def _gelu(x):
    return 0.5 * x * (1.0 + lax.erf(x * (1.0 / math.sqrt(2.0))))


def _pack_bf16_pairs(y):
    h = y.shape[1] // 2
    lo = lax.bitcast_convert_type(y[:, :h].astype(BF16).astype(F32), U32) >> 16
    hi = lax.bitcast_convert_type(y[:, h:].astype(BF16).astype(F32), U32) & jnp.uint32(0xFFFF0000)
    return hi | lo


def _unpack_bf16_pairs(w):
    lo = lax.bitcast_convert_type(w << 16, F32)
    hi = lax.bitcast_convert_type(w & jnp.uint32(0xFFFF0000), F32)
    return jnp.concatenate([lo, hi], axis=1)


def _route(x1, first, wr_ref, brc_ref, ri_ref, rg_ref, cnt_ref, cnt_scr):
    @pl.when(first)
    def _():
        cnt_scr[...] = jnp.zeros(cnt_scr.shape, F32)

    x_hi = x1.astype(BF16)
    x_lo = (x1 - x_hi.astype(F32)).astype(BF16)
    logits_tok = (jnp.dot(x_hi, wr_ref[0], preferred_element_type=F32)
                  + (jnp.dot(x_hi, wr_ref[1], preferred_element_type=F32)
                     + jnp.dot(x_lo, wr_ref[0], preferred_element_type=F32)))
    n_e = brc_ref.shape[0]
    logits = logits_tok.T[:n_e, :] + brc_ref[...]
    tm = logits.shape[1]
    sub_f = lax.broadcasted_iota(jnp.int32, logits.shape, 0).astype(F32)
    vals, idxs, hots = [], [], []
    l = logits
    for _ in range(TOP_K):
        m = jnp.max(l, axis=0, keepdims=True)
        i = jnp.min(jnp.where(l == m, sub_f, float(n_e)), axis=0, keepdims=True)
        hot = sub_f == i
        vals.append(m)
        idxs.append(i)
        hots.append(hot.astype(F32))
        l = jnp.where(hot, -jnp.inf, l)
    es = [jnp.exp(v - vals[0]) for v in vals]
    tot = es[0] + es[1] + es[2] + es[3]
    cnt = (hots[0] + hots[1] + hots[2] + hots[3]).astype(BF16)
    r_i = lax.broadcasted_iota(jnp.int32, (tm, tm), 0)
    c_i = lax.broadcasted_iota(jnp.int32, (tm, tm), 1)
    before = jnp.dot(cnt, (r_i < c_i).astype(BF16), preferred_element_type=F32)
    before = before + jnp.concatenate([cnt_scr[...]] * (tm // LANES), axis=1)
    row = lax.broadcasted_iota(jnp.int32, (2 * TOP_K, tm), 0)
    ri = jnp.zeros((2 * TOP_K, tm), F32)
    rg = jnp.zeros((2 * TOP_K, tm), F32)
    for k in range(TOP_K):
        rank = jnp.sum(before * hots[k], axis=0, keepdims=True)
        ri = jnp.where(row == k, idxs[k], ri)
        ri = jnp.where(row == TOP_K + k, rank, ri)
        rg = jnp.where(row == k, es[k] / tot, rg)
    ri_ref[...] = ri.astype(jnp.int32)
    rg_ref[...] = jnp.concatenate([rg, jnp.zeros((LANES - 2 * TOP_K, tm), F32)], axis=0).T
    total = cnt_scr[...] + jnp.dot(cnt, jnp.ones((tm, LANES), BF16), preferred_element_type=F32)
    cnt_scr[...] = total
    cnt_ref[...] = total


def _const_spec(shape):
    nd = len(shape)
    return pl.BlockSpec(shape, lambda *_: (0,) * nd)


def _mixer_a_kernel(x_ref, wu_ref, wv_ref, bu_ref, bv_ref, lg_ref, lb_ref, ws_ref, bst_ref,
                    wo_ref, bo_ref, g1_ref, b1_ref, wr_ref, br_ref,
                    x1_ref, x1p_ref, ri_ref, rg_ref, cnt_ref, v_scr, o_scr, cnt_scr):
    x = x_ref[...]
    xb = x.astype(BF16)
    tm, dff = v_scr.shape
    gd = dff // A_GROUPS
    gcols = [slice(g * gd, (g + 1) * gd) for g in range(A_GROUPS)]
    s1 = jnp.zeros((tm, 1), F32)
    for cols in gcols:
        v_g = _gelu(jnp.dot(xb, wv_ref[:, cols], preferred_element_type=F32) + bv_ref[:, cols])
        v_scr[:, cols] = v_g
        s1 = s1 + jnp.sum(v_g, axis=-1, keepdims=True)
    mu = s1 * (1.0 / dff)
    s2 = jnp.zeros((tm, 1), F32)
    for cols in gcols:
        dv = v_scr[:, cols] - mu
        s2 = s2 + jnp.sum(dv * dv, axis=-1, keepdims=True)
    rstd = lax.rsqrt(s2 * (1.0 / dff) + LN_EPS)
    pi = lax.broadcasted_iota(jnp.int32, (A_BLOCK, A_BLOCK), 0) // CHUNK
    pj = lax.broadcasted_iota(jnp.int32, (A_BLOCK, A_BLOCK), 1) // CHUNK
    mask = pj <= pi
    bst = bst_ref[...]
    for g, cols in enumerate(gcols):
        u_g = _gelu(jnp.dot(xb, wu_ref[:, cols], preferred_element_type=F32) + bu_ref[:, cols])
        vb_g = ((v_scr[:, cols] - mu) * rstd * lg_ref[:, cols] + lb_ref[:, cols]).astype(BF16)
        wm = jnp.where(mask, ws_ref[g], 0.0).astype(BF16)
        for n in range(tm // A_BLOCK):
            rows = slice(n * A_BLOCK, (n + 1) * A_BLOCK)
            sv = jnp.dot(wm, vb_g[rows, :], preferred_element_type=F32) + bst[:, g:g + 1]
            o_scr[rows, cols] = (u_g[rows, :] * sv).astype(BF16)
    m = jnp.dot(o_scr[...], wo_ref[...], preferred_element_type=F32) + bo_ref[...]
    x1 = _ln(DN_ALPHA * x + m, g1_ref[...], b1_ref[...])
    x1_ref[...] = x1
    x1p_ref[...] = _pack_bf16_pairs(x1)
    _route(x1, pl.program_id(0) == 0, wr_ref, br_ref, ri_ref, rg_ref, cnt_ref, cnt_scr)


def mixer_a_block(xt, w_in, b_in, ln_g, ln_b, w_s, b_s, w_out, b_out, g1, b1, w_r, b_r):
    t, d = xt.shape
    dff = w_out.shape[0]
    tm = TOKEN_TILE
    wu = w_in[:, :dff].astype(BF16)
    wv = w_in[:, dff:].astype(BF16)
    bu = b_in[:dff].reshape(1, dff)
    bv = b_in[dff:].reshape(1, dff)
    args = (xt, wu, wv, bu, bv, ln_g.reshape(1, dff), ln_b.reshape(1, dff), w_s, b_s.T,
            w_out.astype(BF16), b_out.reshape(1, d), g1.reshape(1, d), b1.reshape(1, d), w_r, b_r)
    tok = lambda w: pl.BlockSpec((tm, w), lambda i: (i, 0))
    route = pl.BlockSpec((2 * TOP_K, tm), lambda i: (0, i))
    in_specs = [tok(d)] + [_const_spec(a.shape) for a in args[1:]]
    out_shape = (jax.ShapeDtypeStruct((t, d), F32),
                 jax.ShapeDtypeStruct((t, d // 2), U32),
                 jax.ShapeDtypeStruct((2 * TOP_K, t), jnp.int32),
                 jax.ShapeDtypeStruct((t, LANES), F32),
                 jax.ShapeDtypeStruct((N_EXPERTS, LANES), F32))
    out_specs = (tok(d), tok(d // 2), route, tok(LANES), _const_spec((N_EXPERTS, LANES)))
    return pl.pallas_call(
        _mixer_a_kernel,
        out_shape=out_shape,
        grid=(t // tm,),
        in_specs=in_specs,
        out_specs=out_specs,
        scratch_shapes=[pltpu.VMEM((tm, dff), F32), pltpu.VMEM((tm, dff), BF16),
                        pltpu.VMEM((N_EXPERTS, LANES), F32)],
        compiler_params=pltpu.CompilerParams(
            dimension_semantics=("arbitrary",), vmem_limit_bytes=VMEM_LIMIT),
        name="mixer_a",
    )(*args)


def _mamba_in_kernel(x_ref, wz_ref, wx_ref, wdt_ref, cw_ref, cb_ref, dtb_ref,
                     z_ref, xs_ref, bm_ref, cm_ref, dt_ref, *ext_scrs):
    s = pl.program_id(1)
    tm = x_ref.shape[0]
    d_inner = xs_ref.shape[1]
    gn = bm_ref.shape[1]
    @pl.when((pl.program_id(0) == 0) & (s == 0))
    def _():
        for ext in ext_scrs:
            ext[...] = jnp.zeros(ext.shape, F32)

    xb = x_ref[...].astype(BF16)
    dt = jnp.dot(xb, wdt_ref[...], preferred_element_type=F32) + dtb_ref[...]
    dt_ref[...] = jnp.maximum(dt, 0.0) + jnp.log1p(jnp.exp(-jnp.abs(dt)))

    cw = cw_ref[...]
    cbias = cb_ref[...]
    w = CONV_COLS
    for c, ext in enumerate(ext_scrs):
        cols = slice(c * w, (c + 1) * w)
        ext[0:8, :] = jnp.where(s > 0, ext[tm:tm + 8, :], 0.0)
        ext[8:8 + tm, :] = jnp.dot(xb, wx_ref[:, cols], preferred_element_type=F32)
        acc = cbias[:, cols] + cw[0:1, cols] * ext[5:5 + tm, :]
        for j in range(1, SSM_CONV):
            acc = acc + cw[j:j + 1, cols] * ext[5 + j:5 + j + tm, :]
        act = (acc * jax.nn.sigmoid(acc)).astype(BF16)
        lo = c * w
        if lo < d_inner:
            xs_ref[:, lo:lo + w] = act
        elif lo < d_inner + gn:
            bm_ref[:, lo - d_inner:lo - d_inner + w] = act
        else:
            cm_ref[:, lo - d_inner - gn:lo - d_inner - gn + w] = act
        if lo < d_inner:
            z_ref[:, cols] = jnp.dot(xb, wz_ref[:, cols], preferred_element_type=F32).astype(BF16)


def mamba_in_block(x3, w_in, conv_w, conv_b, dt_bias):
    bsz, s_len, d = x3.shape
    d_inner = SSM_HEADS * SSM_HEAD_DIM
    gn = SSM_GROUPS * SSM_STATE
    conv_dim = d_inner + 2 * gn
    tm = TOKEN_TILE
    wz = w_in[:, :d_inner].astype(BF16)
    wx = w_in[:, d_inner:d_inner + conv_dim].astype(BF16)
    wdt = jnp.pad(w_in[:, d_inner + conv_dim:], ((0, 0), (0, LANES - SSM_HEADS))).astype(BF16)
    dtb = jnp.pad(dt_bias, (0, LANES - SSM_HEADS)).reshape(1, LANES)
    args = (x3, wz, wx, wdt, conv_w, conv_b.reshape(1, conv_dim), dtb)
    tok = lambda w: pl.BlockSpec((None, tm, w), lambda b, s: (b, s, 0))
    in_specs = [tok(d)] + [_const_spec(a.shape) for a in args[1:]]
    out_shape = (jax.ShapeDtypeStruct((bsz, s_len, d_inner), BF16),
                 jax.ShapeDtypeStruct((bsz, s_len, d_inner), BF16),
                 jax.ShapeDtypeStruct((bsz, s_len, gn), BF16),
                 jax.ShapeDtypeStruct((bsz, s_len, gn), BF16),
                 jax.ShapeDtypeStruct((bsz, s_len, LANES), F32))
    out_specs = (tok(d_inner), tok(d_inner), tok(gn), tok(gn), tok(LANES))
    return pl.pallas_call(
        _mamba_in_kernel,
        out_shape=out_shape,
        grid=(bsz, s_len // tm),
        in_specs=in_specs,
        out_specs=out_specs,
        scratch_shapes=[pltpu.VMEM((tm + 8, CONV_COLS), F32) for _ in range(conv_dim // CONV_COLS)],
        compiler_params=pltpu.CompilerParams(
            dimension_semantics=("arbitrary", "arbitrary"), vmem_limit_bytes=VMEM_LIMIT),
        name="mamba_in",
    )(*args)


def _split3_bf16(q):
    hi = q.astype(BF16)
    r1 = q - hi.astype(F32)
    mid = r1.astype(BF16)
    lo = (r1 - mid.astype(F32)).astype(BF16)
    return hi, mid, lo


def _mamba_ssd_kernel(x_ref, z_ref, xs_ref, bm_ref, cm_ref, dt_ref, alog_ref, dfull_ref, nw_ref, exp_ref,
                      wo_ref, g1_ref, b1_ref, wr_ref, br_ref,
                      x1_ref, x1p_ref, ri_ref, rg_ref, cnt_ref, state_scr, y_scr, dtw_scr, cnt_scr):
    s = pl.program_id(1)
    tm = x_ref.shape[0]
    L, P, N = SSD_BLOCK, SSM_HEAD_DIM, SSM_STATE
    R = SSM_HEADS // SSM_GROUPS
    gw = R * P
    nblk = tm // L

    @pl.when(s == 0)
    def _():
        state_scr[...] = jnp.zeros(state_scr.shape, F32)

    a_row = -jnp.exp(alog_ref[...])
    dt = dt_ref[...]
    adt = dt * a_row
    r_i = lax.broadcasted_iota(jnp.int32, (tm, tm), 0)
    c_i = lax.broadcasted_iota(jnp.int32, (tm, tm), 1)
    btril = ((r_i >= c_i) & (r_i // L == c_i // L)).astype(F32)
    acum = jnp.dot(btril, adt, preferred_element_type=F32,
                   precision=lax.Precision.HIGHEST)
    row_blk = lax.broadcasted_iota(jnp.int32, (tm, LANES), 0) // L
    alast = [acum[(b + 1) * L - 1:(b + 1) * L, :] for b in range(nblk)]
    alast_rows = alast[0]
    for b in range(1, nblk):
        alast_rows = jnp.where(row_blk == b, alast[b], alast_rows)
    dtw = dt * jnp.exp(alast_rows - acum)
    expand = exp_ref[...]
    dtw_scr[...] = jnp.dot(dtw.astype(BF16), expand, preferred_element_type=F32)
    acum_t = acum.T
    dt_t = dt.T
    tril = lax.broadcasted_iota(jnp.int32, (L, L), 0) >= lax.broadcasted_iota(jnp.int32, (L, L), 1)
    lo_half = lax.broadcasted_iota(jnp.int32, (L, 2 * P), 1) < P

    for b in range(nblk):
        rows = slice(b * L, (b + 1) * L)
        acum_b = acum[rows, :]
        elast = jnp.broadcast_to(jnp.exp(alast[b]), (SUBLANES, LANES))
        dec_all = sum(jnp.dot(piece, expand, preferred_element_type=F32)
                      for piece in _split3_bf16(elast))[0:1, :]
        for g in range(SSM_GROUPS):
            c_g = cm_ref[rows, g * N:(g + 1) * N]
            b_g = bm_ref[rows, g * N:(g + 1) * N]
            cb = lax.dot_general(c_g, b_g, (((1,), (1,)), ((), ())), preferred_element_type=F32)
            st = state_scr[g]
            y_off = jnp.dot(c_g, st.astype(BF16), preferred_element_type=F32)
            for q in range(R // 2):
                h0 = g * R + 2 * q
                ms, es = [], []
                for h in (h0, h0 + 1):
                    acol = jnp.broadcast_to(acum_b[:, h:h + 1], (L, L))
                    seg = acol - acum_t[h:h + 1, rows]
                    decay = jnp.exp(jnp.where(tril, seg, -jnp.inf))
                    ms.append((cb * decay * dt_t[h:h + 1, rows]).astype(BF16))
                    es.append(jnp.exp(acol))
                lhs = jnp.concatenate(ms, axis=1)
                cols = slice(h0 * P, (h0 + 2) * P)
                xp = xs_ref[rows, cols]
                zero = jnp.zeros_like(xp)
                rhs = jnp.concatenate([jnp.where(lo_half, xp, zero), jnp.where(lo_half, zero, xp)], axis=0)
                y_d = jnp.dot(lhs, rhs, preferred_element_type=F32)
                scale = jnp.where(lo_half, es[0], es[1])
                y_scr[rows, cols] = y_d + y_off[:, 2 * q * P:(2 * q + 2) * P] * scale
            gcols = slice(g * gw, (g + 1) * gw)
            xw = (xs_ref[rows, gcols].astype(F32) * dtw_scr[rows, gcols]).astype(BF16)
            contrib = lax.dot_general(b_g, xw, (((0,), (0,)), ((), ())), preferred_element_type=F32)
            state_scr[g] = st * dec_all[:, gcols] + contrib

    y = y_scr[...] + dfull_ref[...] * xs_ref[...].astype(F32)
    zf = z_ref[...].astype(F32)
    y = y * (zf * jax.nn.sigmoid(zf))
    nw = nw_ref[...]
    parts = []
    for g in range(SSM_GROUPS):
        yg = y[:, g * gw:(g + 1) * gw]
        yg = yg * lax.rsqrt(jnp.mean(yg * yg, axis=-1, keepdims=True) + RMS_EPS)
        parts.append((yg * nw[:, g * gw:(g + 1) * gw]).astype(BF16))
    yb = jnp.concatenate(parts, axis=1)
    m = jnp.dot(yb, wo_ref[...], preferred_element_type=F32)
    x1 = _ln(DN_ALPHA * x_ref[...] + m, g1_ref[...], b1_ref[...])
    x1_ref[...] = x1
    x1p_ref[...] = _pack_bf16_pairs(x1)
    first = (pl.program_id(0) == 0) & (s == 0)
    _route(x1, first, wr_ref, br_ref, ri_ref, rg_ref, cnt_ref, cnt_scr)


def mamba_ssd_block(x3, z, xs, bm, cm, dt, a_log, d_skip, norm_w, w_out, g1, b1, w_r, b_r):
    bsz, s_len, d = x3.shape
    d_inner = xs.shape[-1]
    gn = bm.shape[-1]
    tm = TOKEN_TILE
    alog = jnp.pad(a_log, (0, LANES - SSM_HEADS)).reshape(1, LANES)
    dfull = jnp.repeat(d_skip, SSM_HEAD_DIM).reshape(1, d_inner)
    head_of_col = jnp.arange(d_inner, dtype=jnp.int32) // SSM_HEAD_DIM
    expand = (jnp.arange(LANES, dtype=jnp.int32)[:, None] == head_of_col[None, :]).astype(BF16)
    args = (x3, z, xs, bm, cm, dt, alog, dfull, norm_w.reshape(1, d_inner), expand, w_out.astype(BF16),
            g1.reshape(1, d), b1.reshape(1, d), w_r, b_r)
    tok = lambda w: pl.BlockSpec((None, tm, w), lambda b, s: (b, s, 0))
    in_specs = [tok(d), tok(d_inner), tok(d_inner), tok(gn), tok(gn), tok(LANES)] + \
               [_const_spec(a.shape) for a in args[6:]]
    ns = s_len // tm
    route = pl.BlockSpec((2 * TOP_K, tm), lambda b, s: (0, b * ns + s))
    out_shape = (jax.ShapeDtypeStruct((bsz, s_len, d), F32),
                 jax.ShapeDtypeStruct((bsz, s_len, d // 2), U32),
                 jax.ShapeDtypeStruct((2 * TOP_K, bsz * s_len), jnp.int32),
                 jax.ShapeDtypeStruct((bsz, s_len, LANES), F32),
                 jax.ShapeDtypeStruct((N_EXPERTS, LANES), F32))
    out_specs = (tok(d), tok(d // 2), route, tok(LANES), _const_spec((N_EXPERTS, LANES)))
    return pl.pallas_call(
        _mamba_ssd_kernel,
        out_shape=out_shape,
        grid=(bsz, ns),
        in_specs=in_specs,
        out_specs=out_specs,
        scratch_shapes=[pltpu.VMEM((SSM_GROUPS, SSM_STATE, d_inner // SSM_GROUPS), F32),
                        pltpu.VMEM((tm, d_inner), F32),
                        pltpu.VMEM((tm, d_inner), F32),
                        pltpu.VMEM((N_EXPERTS, LANES), F32)],
        compiler_params=pltpu.CompilerParams(
            dimension_semantics=("arbitrary", "arbitrary"), vmem_limit_bytes=VMEM_LIMIT),
        name="mamba_ssd",
    )(*args)


def _sc_workers():
    info = plsc.get_sparse_core_info()
    return info.num_cores, info.num_subcores


def sc_scatter_rows(x, dest_km, n_rows):
    t, d = x.shape
    nc, ns = _sc_workers()
    per_w, rem = divmod(t, nc * ns)
    assert rem == 0 and per_w % SC_CHUNK == 0
    mesh = plsc.VectorSubcoreMesh(core_axis_name="core", subcore_axis_name="subcore")

    @functools.partial(
        pl.kernel,
        out_type=jax.ShapeDtypeStruct((n_rows, d), x.dtype),
        mesh=mesh,
        scratch_types=[pltpu.VMEM((SC_CHUNK,), jnp.int32) for _ in range(TOP_K)] +
                      [pltpu.VMEM((SC_CHUNK, d), x.dtype), pltpu.SemaphoreType.DMA],
    )
    def scatter(x_hbm, i_hbm, o_hbm, i0, i1, i2, i3, rows_v, sem):
        wid = lax.axis_index("subcore") * nc + lax.axis_index("core")
        base = wid * per_w
        idx = (i0, i1, i2, i3)

        @pl.loop(0, per_w // SC_CHUNK)
        def _(c):
            off = base + c * SC_CHUNK
            pltpu.sync_copy(x_hbm.at[pl.ds(off, SC_CHUNK)], rows_v)
            for k in range(TOP_K):
                pltpu.sync_copy(i_hbm.at[pl.ds(k * t + off, SC_CHUNK)], idx[k])
            copies = [pltpu.make_async_copy(rows_v, o_hbm.at[idx[k]], sem) for k in range(TOP_K)]
            for cp in copies:
                cp.start()
            for cp in copies:
                cp.wait()

    return scatter(x, dest_km)


def sc_gather_rows(table, idx):
    n = idx.shape[0]
    d = table.shape[1]
    nc, ns = _sc_workers()
    per_w, rem = divmod(n, nc * ns)
    n_chunks = per_w // SC_CHUNK
    assert rem == 0 and per_w % (2 * SC_CHUNK) == 0
    mesh = plsc.VectorSubcoreMesh(core_axis_name="core", subcore_axis_name="subcore")

    @functools.partial(
        pl.kernel,
        out_type=jax.ShapeDtypeStruct((n, d), table.dtype),
        mesh=mesh,
        scratch_types=[
            pltpu.VMEM((SC_CHUNK,), jnp.int32), pltpu.VMEM((SC_CHUNK,), jnp.int32),
            pltpu.VMEM((SC_CHUNK, d), table.dtype), pltpu.VMEM((SC_CHUNK, d), table.dtype),
            pltpu.SemaphoreType.DMA, pltpu.SemaphoreType.DMA,
            pltpu.SemaphoreType.DMA, pltpu.SemaphoreType.DMA,
        ],
    )
    def gather(x_hbm, i_hbm, o_hbm, idx0, idx1, rows0, rows1, sg0, sg1, sw0, sw1):
        wid = lax.axis_index("subcore") * nc + lax.axis_index("core")
        base = wid * per_w
        idx = (idx0, idx1)
        rows = (rows0, rows1)
        sg = (sg0, sg1)
        sw = (sw0, sw1)

        def gather_copy(slot):
            return pltpu.make_async_copy(x_hbm.at[idx[slot]], rows[slot], sg[slot])

        def write_copy(c, slot):
            return pltpu.make_async_copy(rows[slot], o_hbm.at[pl.ds(base + c * SC_CHUNK, SC_CHUNK)], sw[slot])

        pltpu.sync_copy(i_hbm.at[pl.ds(base, SC_CHUNK)], idx0)
        gather_copy(0).start()

        @pl.loop(0, n_chunks, step=2)
        def _(c0):
            for slot in range(2):
                c = c0 + slot
                nxt = 1 - slot

                @pl.when(c + 1 < n_chunks)
                def _():
                    pltpu.sync_copy(i_hbm.at[pl.ds(base + (c + 1) * SC_CHUNK, SC_CHUNK)], idx[nxt])

                    @pl.when(c >= 1)
                    def _():
                        write_copy(c - 1, nxt).wait()

                    gather_copy(nxt).start()

                gather_copy(slot).wait()
                write_copy(c, slot).start()

        write_copy(n_chunks - 2, 0).wait()
        write_copy(n_chunks - 1, 1).wait()

    return gather(table, idx)


def _moe_mlp_kernel(be_ref, nv_ref, nu_ref, x_ref, w1_ref, b1_ref, w2_ref, b2_ref, o_ref, w1b, w2b):
    i = pl.program_id(0)
    e = be_ref[i]
    prev = be_ref[jnp.maximum(i - 1, 0)]

    @pl.when((i == 0) | (e != prev))
    def _():
        w1b[...] = w1_ref[...].astype(BF16)
        w2b[...] = w2_ref[...].astype(BF16)

    nv = nv_ref[i]
    dff = w2b.shape[0]
    for sb in range(MOE_BLOCK // MOE_SUB):
        @pl.when(nv > sb * MOE_SUB)
        def _(sb=sb):
            rows = slice(sb * MOE_SUB, (sb + 1) * MOE_SUB)
            row = lax.broadcasted_iota(jnp.int32, (MOE_SUB, x_ref.shape[1]), 0) + sb * MOE_SUB
            xw = jnp.where(row < nv, x_ref[rows, :], jnp.uint32(0))
            xb = _unpack_bf16_pairs(xw).astype(BF16)
            h = jnp.dot(xb, w1b[...], preferred_element_type=F32) + b1_ref[...]
            g = jnp.minimum(h[:, :dff], SWIGLU_LIMIT)
            lin = jnp.clip(h[:, dff:], -SWIGLU_LIMIT, SWIGLU_LIMIT)
            act = (g * jax.nn.sigmoid(SWIGLU_ALPHA * g) * (lin + 1.0)).astype(BF16)
            y = jnp.dot(act, w2b[...], preferred_element_type=F32) + b2_ref[...]
            o_ref[rows, :] = _pack_bf16_pairs(y)


def moe_mlp(xp, block_expert, n_valid, n_used, layer, w1, b1, w2, b2):
    n_rows, dh = xp.shape
    _, n_e, d, two_dff = w1.shape
    dff = two_dff // 2
    n_blocks = n_rows // MOE_BLOCK
    row = lambda i, be, nv, nu: (jnp.minimum(i, nu[0] - 1), 0)
    wsel = lambda i, be, nv, nu: (layer, be[i], 0, 0)
    grid_spec = pltpu.PrefetchScalarGridSpec(
        num_scalar_prefetch=3,
        grid=(n_blocks,),
        in_specs=[
            pl.BlockSpec((MOE_BLOCK, dh), row),
            pl.BlockSpec((None, None, d, two_dff), wsel),
            pl.BlockSpec((None, None, 1, two_dff), wsel),
            pl.BlockSpec((None, None, dff, d), wsel),
            pl.BlockSpec((None, None, 1, d), wsel),
        ],
        out_specs=pl.BlockSpec((MOE_BLOCK, dh), row),
        scratch_shapes=[pltpu.VMEM((d, two_dff), BF16), pltpu.VMEM((dff, d), BF16)],
    )
    return pl.pallas_call(
        _moe_mlp_kernel,
        out_shape=jax.ShapeDtypeStruct((n_rows, dh), U32),
        grid_spec=grid_spec,
        compiler_params=pltpu.CompilerParams(
            dimension_semantics=("arbitrary",), vmem_limit_bytes=VMEM_LIMIT),
        name="moe_mlp",
    )(block_expert, n_valid, n_used, xp, w1, b1.reshape(b1.shape[0], n_e, 1, two_dff),
      w2, b2.reshape(b2.shape[0], n_e, 1, d))


def _combine_kernel(x1_ref, y0_ref, y1_ref, y2_ref, y3_ref, rg_ref, g2_ref, b2_ref, o_ref):
    rg = rg_ref[...]
    f = rg[:, 0:1] * _unpack_bf16_pairs(y0_ref[...])
    for k, y_ref in ((1, y1_ref), (2, y2_ref), (3, y3_ref)):
        f = f + rg[:, k:k + 1] * _unpack_bf16_pairs(y_ref[...])
    o_ref[...] = _ln(DN_ALPHA * x1_ref[...] + f, g2_ref[...], b2_ref[...])


def combine_block(x1, y4p, rg, g2, b2):
    t, d = x1.shape
    tm = TOKEN_TILE
    nt = t // tm
    y_specs = [pl.BlockSpec((tm, d // 2), functools.partial(lambda i, k: (k * nt + i, 0), k=k))
               for k in range(TOP_K)]
    return pl.pallas_call(
        _combine_kernel,
        out_shape=jax.ShapeDtypeStruct((t, d), F32),
        grid=(nt,),
        in_specs=[pl.BlockSpec((tm, d), lambda i: (i, 0))] + y_specs +
                 [pl.BlockSpec((tm, LANES), lambda i: (i, 0)), _const_spec((1, d)), _const_spec((1, d))],
        out_specs=pl.BlockSpec((tm, d), lambda i: (i, 0)),
        compiler_params=pltpu.CompilerParams(
            dimension_semantics=("arbitrary",), vmem_limit_bytes=VMEM_LIMIT),
        name="moe_combine",
    )(x1, y4p, y4p, y4p, y4p, rg, g2.reshape(1, d), b2.reshape(1, d))


def moe_block(x1, x1p, ri, rg, cnt, layer, w1, b1, w2, b2, g2, b2n):
    t, d = x1.shape
    n_assign = t * TOP_K
    n_blocks = -(-n_assign // MOE_BLOCK) + N_EXPERTS
    n_rows = n_blocks * MOE_BLOCK
    ar = jnp.arange(N_EXPERTS, dtype=jnp.int32)
    counts = cnt[:, 0].astype(jnp.int32)
    padded = (counts + MOE_BLOCK - 1) // MOE_BLOCK * MOE_BLOCK
    pend = jnp.sum(jnp.where(ar[None, :] <= ar[:, None], padded[None, :], 0), axis=1)
    pstart = pend - padded
    top_i = ri[:TOP_K]
    dest = jnp.sum(jnp.where(top_i[:, :, None] == ar, pstart, 0), axis=-1) + ri[TOP_K:]
    dest_km = dest.reshape(-1)
    n_used = (pend[-1] // MOE_BLOCK).reshape(1)
    block_start = jnp.arange(n_blocks, dtype=jnp.int32) * MOE_BLOCK
    block_expert = jnp.minimum(
        jnp.sum((pend[None, :] <= block_start[:, None]).astype(jnp.int32), axis=1), N_EXPERTS - 1)
    last = jnp.sum(jnp.where(jnp.arange(n_blocks) == n_used[0] - 1, block_expert, 0))
    block_expert = jnp.where(jnp.arange(n_blocks) < n_used[0], block_expert, last)
    vend = jnp.sum(jnp.where(block_expert[:, None] == ar, (pstart + counts)[None, :], 0), axis=1)
    n_valid = jnp.clip(vend - block_start, 0, MOE_BLOCK)

    xp = sc_scatter_rows(x1p, dest_km, n_rows)
    yp = moe_mlp(xp, block_expert, n_valid, n_used, layer, w1, b1, w2, b2)
    y4p = sc_gather_rows(yp, dest_km)
    return combine_block(x1, y4p, rg, g2, b2n)


def kernel(x, a_w_in, a_b_in, a_ln_g, a_ln_b, a_w_s, a_b_s, a_w_out, a_b_out, b_w_in, b_conv_w, b_conv_b, b_dt_bias, b_a_log, b_d, b_norm_w, b_w_out, moe_w_router, moe_b_router, moe_w1, moe_b1, moe_w2, moe_b2, ln1_g, ln1_b, ln2_g, ln2_b):
    bsz, s_len, d = x.shape
    gb = bsz // N_SPLITS
    t = gb * s_len
    xts = [x[g * gb:(g + 1) * gb].reshape(t, d) for g in range(N_SPLITS)]
    for i in range(DEPTH):
        j = i // 2
        w_r32 = jnp.pad(moe_w_router[i], ((0, 0), (0, LANES - N_EXPERTS)))
        w_r_hi = w_r32.astype(BF16)
        w_r = jnp.stack([w_r_hi, (w_r32 - w_r_hi.astype(F32)).astype(BF16)])
        b_r = moe_b_router[i].reshape(N_EXPERTS, 1)
        mixed = []
        for xt in xts:
            if i % 2 == 0:
                x1, x1p, ri, rg, cnt = mixer_a_block(
                    xt, a_w_in[j], a_b_in[j], a_ln_g[j], a_ln_b[j], a_w_s[j], a_b_s[j],
                    a_w_out[j], a_b_out[j], ln1_g[i], ln1_b[i], w_r, b_r)
            else:
                x3 = xt.reshape(gb, s_len, d)
                z, xs, bm, cm, dt = mamba_in_block(x3, b_w_in[j], b_conv_w[j], b_conv_b[j], b_dt_bias[j])
                x1, x1p, ri, rg, cnt = mamba_ssd_block(
                    x3, z, xs, bm, cm, dt, b_a_log[j], b_d[j], b_norm_w[j],
                    b_w_out[j], ln1_g[i], ln1_b[i], w_r, b_r)
                x1 = x1.reshape(t, d)
                x1p = x1p.reshape(t, d // 2)
                rg = rg.reshape(t, LANES)
            mixed.append((x1, x1p, ri, rg, cnt))
        xts = [moe_block(*m, i, moe_w1, moe_b1, moe_w2, moe_b2, ln2_g[i], ln2_b[i]) for m in mixed]
    return jnp.concatenate(xts, axis=0).reshape(bsz, s_len, d)
```

```python
import functools
import math

import jax
import jax.numpy as jnp
from jax import lax
from jax.experimental import pallas as pl
from jax.experimental.pallas import tpu as pltpu
from jax.experimental.pallas import tpu_sc as plsc

F32 = jnp.float32
BF16 = jnp.bfloat16
U32 = jnp.uint32

DEPTH = 4
N_EXPERTS = 32
TOP_K = 4
MOE_BLOCK = 512
SWIGLU_LIMIT = 7.0
SWIGLU_ALPHA = 1.702
A_BLOCK = 128
A_GROUPS = 8
CHUNK = 64
SSD_BLOCK = 128
SSM_HEADS = 32
SSM_HEAD_DIM = 64
SSM_GROUPS = 4
SSM_STATE = 128
SSM_CONV = 4
DN_ALPHA = (2 * DEPTH) ** 0.25
LN_EPS = 1e-5
RMS_EPS = 1e-5

LANES = 128
SUBLANES = 8
TOKEN_TILE = 256
VMEM_LIMIT = 56 * 1024 * 1024
SC_CHUNK = 64
CONV_COLS = 256
COMBINE_PARTS = 2


def _ln(x, g, b):
    mu = jnp.mean(x, axis=-1, keepdims=True)
    xc = x - mu
    var = jnp.mean(xc * xc, axis=-1, keepdims=True)
    return xc * lax.rsqrt(var + LN_EPS) * g + b


def _gelu(x):
    return 0.5 * x * (1.0 + lax.erf(x * (1.0 / math.sqrt(2.0))))


def _pack_bf16_pairs(y):
    h = y.shape[1] // 2
    lo = lax.bitcast_convert_type(y[:, :h].astype(BF16).astype(F32), U32) >> 16
    hi = lax.bitcast_convert_type(y[:, h:].astype(BF16).astype(F32), U32) & jnp.uint32(0xFFFF0000)
    return hi | lo


def _unpack_bf16_pairs(w):
    lo = lax.bitcast_convert_type(w << 16, F32)
    hi = lax.bitcast_convert_type(w & jnp.uint32(0xFFFF0000), F32)
    return jnp.concatenate([lo, hi], axis=1)


def _route(x1, first, wr_ref, brc_ref, ri_ref, rg_ref, cnt_ref, cnt_scr):
    @pl.when(first)
    def _():
        cnt_scr[...] = jnp.zeros(cnt_scr.shape, F32)

    x_hi = x1.astype(BF16)
    x_lo = (x1 - x_hi.astype(F32)).astype(BF16)
    logits_tok = (jnp.dot(x_hi, wr_ref[0], preferred_element_type=F32)
                  + (jnp.dot(x_hi, wr_ref[1], preferred_element_type=F32)
                     + jnp.dot(x_lo, wr_ref[0], preferred_element_type=F32)))
    n_e = brc_ref.shape[0]
    logits = logits_tok.T[:n_e, :] + brc_ref[...]
    tm = logits.shape[1]
    sub_f = lax.broadcasted_iota(jnp.int32, logits.shape, 0).astype(F32)
    vals, idxs, hots = [], [], []
    l = logits
    for _ in range(TOP_K):
        m = jnp.max(l, axis=0, keepdims=True)
        i = jnp.min(jnp.where(l == m, sub_f, float(n_e)), axis=0, keepdims=True)
        hot = sub_f == i
        vals.append(m)
        idxs.append(i)
        hots.append(hot.astype(F32))
        l = jnp.where(hot, -jnp.inf, l)
    es = [jnp.exp(v - vals[0]) for v in vals]
    tot = es[0] + es[1] + es[2] + es[3]
    cnt = (hots[0] + hots[1] + hots[2] + hots[3]).astype(BF16)
    r_i = lax.broadcasted_iota(jnp.int32, (tm, tm), 0)
    c_i = lax.broadcasted_iota(jnp.int32, (tm, tm), 1)
    before = jnp.dot(cnt, (r_i < c_i).astype(BF16), preferred_element_type=F32)
    before = before + jnp.concatenate([cnt_scr[...]] * (tm // LANES), axis=1)
    row = lax.broadcasted_iota(jnp.int32, (2 * TOP_K, tm), 0)
    ri = jnp.zeros((2 * TOP_K, tm), F32)
    rg = jnp.zeros((2 * TOP_K, tm), F32)
    for k in range(TOP_K):
        rank = jnp.sum(before * hots[k], axis=0, keepdims=True)
        ri = jnp.where(row == k, idxs[k], ri)
        ri = jnp.where(row == TOP_K + k, rank, ri)
        rg = jnp.where(row == k, es[k] / tot, rg)
    ri_ref[...] = ri.astype(jnp.int32)
    rg_ref[...] = jnp.concatenate([rg, jnp.zeros((LANES - 2 * TOP_K, tm), F32)], axis=0).T
    total = cnt_scr[...] + jnp.dot(cnt, jnp.ones((tm, LANES), BF16), preferred_element_type=F32)
    cnt_scr[...] = total
    cnt_ref[...] = total


def _const_spec(shape):
    nd = len(shape)
    return pl.BlockSpec(shape, lambda *_: (0,) * nd)


def _mixer_a_kernel(x_ref, wu_ref, wv_ref, bu_ref, bv_ref, lg_ref, lb_ref, ws_ref, bst_ref,
                    wo_ref, bo_ref, g1_ref, b1_ref, wr_ref, br_ref,
                    x1_ref, x1p_ref, ri_ref, rg_ref, cnt_ref, v_scr, o_scr, cnt_scr):
    x = x_ref[...]
    xb = x.astype(BF16)
    tm, dff = v_scr.shape
    gd = dff // A_GROUPS
    gcols = [slice(g * gd, (g + 1) * gd) for g in range(A_GROUPS)]
    s1 = jnp.zeros((tm, 1), F32)
    for cols in gcols:
        v_g = _gelu(jnp.dot(xb, wv_ref[:, cols], preferred_element_type=F32) + bv_ref[:, cols])
        v_scr[:, cols] = v_g
        s1 = s1 + jnp.sum(v_g, axis=-1, keepdims=True)
    mu = s1 * (1.0 / dff)
    s2 = jnp.zeros((tm, 1), F32)
    for cols in gcols:
        dv = v_scr[:, cols] - mu
        s2 = s2 + jnp.sum(dv * dv, axis=-1, keepdims=True)
    rstd = lax.rsqrt(s2 * (1.0 / dff) + LN_EPS)
    pi = lax.broadcasted_iota(jnp.int32, (A_BLOCK, A_BLOCK), 0) // CHUNK
    pj = lax.broadcasted_iota(jnp.int32, (A_BLOCK, A_BLOCK), 1) // CHUNK
    mask = pj <= pi
    bst = bst_ref[...]
    for g, cols in enumerate(gcols):
        u_g = _gelu(jnp.dot(xb, wu_ref[:, cols], preferred_element_type=F32) + bu_ref[:, cols])
        vb_g = ((v_scr[:, cols] - mu) * rstd * lg_ref[:, cols] + lb_ref[:, cols]).astype(BF16)
        wm = jnp.where(mask, ws_ref[g], 0.0).astype(BF16)
        for n in range(tm // A_BLOCK):
            rows = slice(n * A_BLOCK, (n + 1) * A_BLOCK)
            sv = jnp.dot(wm, vb_g[rows, :], preferred_element_type=F32) + bst[:, g:g + 1]
            o_scr[rows, cols] = (u_g[rows, :] * sv).astype(BF16)
    m = jnp.dot(o_scr[...], wo_ref[...], preferred_element_type=F32) + bo_ref[...]
    x1 = _ln(DN_ALPHA * x + m, g1_ref[...], b1_ref[...])
    x1_ref[...] = x1
    x1p_ref[...] = _pack_bf16_pairs(x1)
    _route(x1, pl.program_id(0) == 0, wr_ref, br_ref, ri_ref, rg_ref, cnt_ref, cnt_scr)


def mixer_a_block(xt, w_in, b_in, ln_g, ln_b, w_s, b_s, w_out, b_out, g1, b1, w_r, b_r):
    t, d = xt.shape
    dff = w_out.shape[0]
    tm = TOKEN_TILE
    wu = w_in[:, :dff].astype(BF16)
    wv = w_in[:, dff:].astype(BF16)
    bu = b_in[:dff].reshape(1, dff)
    bv = b_in[dff:].reshape(1, dff)
    args = (xt, wu, wv, bu, bv, ln_g.reshape(1, dff), ln_b.reshape(1, dff), w_s, b_s.T,
            w_out.astype(BF16), b_out.reshape(1, d), g1.reshape(1, d), b1.reshape(1, d), w_r, b_r)
    tok = lambda w: pl.BlockSpec((tm, w), lambda i: (i, 0))
    route = pl.BlockSpec((2 * TOP_K, tm), lambda i: (0, i))
    in_specs = [tok(d)] + [_const_spec(a.shape) for a in args[1:]]
    out_shape = (jax.ShapeDtypeStruct((t, d), F32),
                 jax.ShapeDtypeStruct((t, d // 2), U32),
                 jax.ShapeDtypeStruct((2 * TOP_K, t), jnp.int32),
                 jax.ShapeDtypeStruct((t, LANES), F32),
                 jax.ShapeDtypeStruct((N_EXPERTS, LANES), F32))
    out_specs = (tok(d), tok(d // 2), route, tok(LANES), _const_spec((N_EXPERTS, LANES)))
    return pl.pallas_call(
        _mixer_a_kernel,
        out_shape=out_shape,
        grid=(t // tm,),
        in_specs=in_specs,
        out_specs=out_specs,
        scratch_shapes=[pltpu.VMEM((tm, dff), F32), pltpu.VMEM((tm, dff), BF16),
                        pltpu.VMEM((N_EXPERTS, LANES), F32)],
        compiler_params=pltpu.CompilerParams(
            dimension_semantics=("arbitrary",), vmem_limit_bytes=VMEM_LIMIT),
        name="mixer_a",
    )(*args)


def _mamba_in_kernel(x_ref, wz_ref, wx_ref, wdt_ref, cw_ref, cb_ref, dtb_ref,
                     z_ref, xs_ref, bm_ref, cm_ref, dt_ref, *ext_scrs):
    s = pl.program_id(1)
    tm = x_ref.shape[0]
    d_inner = xs_ref.shape[1]
    gn = bm_ref.shape[1]

    @pl.when((pl.program_id(0) == 0) & (s == 0))
    def _():
        for ext in ext_scrs:
            ext[...] = jnp.zeros(ext.shape, F32)

    xb = x_ref[...].astype(BF16)
    dt = jnp.dot(xb, wdt_ref[...], preferred_element_type=F32) + dtb_ref[...]
    dt_ref[...] = jnp.maximum(dt, 0.0) + jnp.log1p(jnp.exp(-jnp.abs(dt)))

    cw = cw_ref[...]
    cbias = cb_ref[...]
    w = CONV_COLS
    for c, ext in enumerate(ext_scrs):
        cols = slice(c * w, (c + 1) * w)
        ext[0:8, :] = jnp.where(s > 0, ext[tm:tm + 8, :], 0.0)
        ext[8:8 + tm, :] = jnp.dot(xb, wx_ref[:, cols], preferred_element_type=F32)
        acc = cbias[:, cols] + cw[0:1, cols] * ext[5:5 + tm, :]
        for j in range(1, SSM_CONV):
            acc = acc + cw[j:j + 1, cols] * ext[5 + j:5 + j + tm, :]
        act = (acc * jax.nn.sigmoid(acc)).astype(BF16)
        lo = c * w
        if lo < d_inner:
            xs_ref[:, lo:lo + w] = act
        elif lo < d_inner + gn:
            bm_ref[:, lo - d_inner:lo - d_inner + w] = act
        else:
            cm_ref[:, lo - d_inner - gn:lo - d_inner - gn + w] = act
        if lo < d_inner:
            z_ref[:, cols] = jnp.dot(xb, wz_ref[:, cols], preferred_element_type=F32).astype(BF16)


def mamba_in_block(x3, w_in, conv_w, conv_b, dt_bias):
    bsz, s_len, d = x3.shape
    d_inner = SSM_HEADS * SSM_HEAD_DIM
    gn = SSM_GROUPS * SSM_STATE
    conv_dim = d_inner + 2 * gn
    tm = TOKEN_TILE
    wz = w_in[:, :d_inner].astype(BF16)
    wx = w_in[:, d_inner:d_inner + conv_dim].astype(BF16)
    wdt = jnp.pad(w_in[:, d_inner + conv_dim:], ((0, 0), (0, LANES - SSM_HEADS))).astype(BF16)
    dtb = jnp.pad(dt_bias, (0, LANES - SSM_HEADS)).reshape(1, LANES)
    args = (x3, wz, wx, wdt, conv_w, conv_b.reshape(1, conv_dim), dtb)
    tok = lambda w: pl.BlockSpec((None, tm, w), lambda b, s: (b, s, 0))
    in_specs = [tok(d)] + [_const_spec(a.shape) for a in args[1:]]
    out_shape = (jax.ShapeDtypeStruct((bsz, s_len, d_inner), BF16),
                 jax.ShapeDtypeStruct((bsz, s_len, d_inner), BF16),
                 jax.ShapeDtypeStruct((bsz, s_len, gn), BF16),
                 jax.ShapeDtypeStruct((bsz, s_len, gn), BF16),
                 jax.ShapeDtypeStruct((bsz, s_len, LANES), F32))
    out_specs = (tok(d_inner), tok(d_inner), tok(gn), tok(gn), tok(LANES))
    return pl.pallas_call(
        _mamba_in_kernel,
        out_shape=out_shape,
        grid=(bsz, s_len // tm),
        in_specs=in_specs,
        out_specs=out_specs,
        scratch_shapes=[pltpu.VMEM((tm + 8, CONV_COLS), F32) for _ in range(conv_dim // CONV_COLS)],
        compiler_params=pltpu.CompilerParams(
            dimension_semantics=("arbitrary", "arbitrary"), vmem_limit_bytes=VMEM_LIMIT),
        name="mamba_in",
    )(*args)


def _split3_bf16(q):
    hi = q.astype(BF16)
    r1 = q - hi.astype(F32)
    mid = r1.astype(BF16)
    lo = (r1 - mid.astype(F32)).astype(BF16)
    return hi, mid, lo


def _mamba_ssd_kernel(x_ref, z_ref, xs_ref, bm_ref, cm_ref, dt_ref, alog_ref, dfull_ref, nw_ref, exp_ref,
                      wo_ref, g1_ref, b1_ref, wr_ref, br_ref,
                      x1_ref, x1p_ref, ri_ref, rg_ref, cnt_ref, state_scr, y_scr, dtw_scr, cnt_scr):
    s = pl.program_id(1)
    tm = x_ref.shape[0]
    L, P, N = SSD_BLOCK, SSM_HEAD_DIM, SSM_STATE
    R = SSM_HEADS // SSM_GROUPS
    gw = R * P
    nblk = tm // L

    @pl.when(s == 0)
    def _():
        state_scr[...] = jnp.zeros(state_scr.shape, F32)

    a_row = -jnp.exp(alog_ref[...])
    dt = dt_ref[...]
    adt = dt * a_row
    r_i = lax.broadcasted_iota(jnp.int32, (tm, tm), 0)
    c_i = lax.broadcasted_iota(jnp.int32, (tm, tm), 1)
    btril = ((r_i >= c_i) & (r_i // L == c_i // L)).astype(F32)
    acum = jnp.dot(btril, adt, preferred_element_type=F32,
                   precision=lax.Precision.HIGHEST)
    row_blk = lax.broadcasted_iota(jnp.int32, (tm, LANES), 0) // L
    alast = [acum[(b + 1) * L - 1:(b + 1) * L, :] for b in range(nblk)]
    alast_rows = alast[0]
    for b in range(1, nblk):
        alast_rows = jnp.where(row_blk == b, alast[b], alast_rows)
    dtw = dt * jnp.exp(alast_rows - acum)
    expand = exp_ref[...]
    dtw_scr[...] = jnp.dot(dtw.astype(BF16), expand, preferred_element_type=F32)
    acum_t = acum.T
    dt_t = dt.T
    tril = lax.broadcasted_iota(jnp.int32, (L, L), 0) >= lax.broadcasted_iota(jnp.int32, (L, L), 1)
    lo_half = lax.broadcasted_iota(jnp.int32, (L, 2 * P), 1) < P

    for b in range(nblk):
        rows = slice(b * L, (b + 1) * L)
        acum_b = acum[rows, :]
        elast = jnp.broadcast_to(jnp.exp(alast[b]), (SUBLANES, LANES))
        dec_all = sum(jnp.dot(piece, expand, preferred_element_type=F32)
                      for piece in _split3_bf16(elast))[0:1, :]
        for g in range(SSM_GROUPS):
            c_g = cm_ref[rows, g * N:(g + 1) * N]
            b_g = bm_ref[rows, g * N:(g + 1) * N]
            cb = lax.dot_general(c_g, b_g, (((1,), (1,)), ((), ())), preferred_element_type=F32)
            st = state_scr[g]
            y_off = jnp.dot(c_g, st.astype(BF16), preferred_element_type=F32)
            for q in range(R // 2):
                h0 = g * R + 2 * q
                ms, es = [], []
                for h in (h0, h0 + 1):
                    acol = jnp.broadcast_to(acum_b[:, h:h + 1], (L, L))
                    seg = acol - acum_t[h:h + 1, rows]
                    decay = jnp.exp(jnp.where(tril, seg, -jnp.inf))
                    ms.append((cb * decay * dt_t[h:h + 1, rows]).astype(BF16))
                    es.append(jnp.exp(acol))
                lhs = jnp.concatenate(ms, axis=1)
                cols = slice(h0 * P, (h0 + 2) * P)
                xp = xs_ref[rows, cols]
                zero = jnp.zeros_like(xp)
                rhs = jnp.concatenate([jnp.where(lo_half, xp, zero), jnp.where(lo_half, zero, xp)], axis=0)
                y_d = jnp.dot(lhs, rhs, preferred_element_type=F32)
                scale = jnp.where(lo_half, es[0], es[1])
                y_scr[rows, cols] = y_d + y_off[:, 2 * q * P:(2 * q + 2) * P] * scale
            gcols = slice(g * gw, (g + 1) * gw)
            xw = (xs_ref[rows, gcols].astype(F32) * dtw_scr[rows, gcols]).astype(BF16)
            contrib = lax.dot_general(b_g, xw, (((0,), (0,)), ((), ())), preferred_element_type=F32)
            state_scr[g] = st * dec_all[:, gcols] + contrib

    y = y_scr[...] + dfull_ref[...] * xs_ref[...].astype(F32)
    zf = z_ref[...].astype(F32)
    y = y * (zf * jax.nn.sigmoid(zf))
    nw = nw_ref[...]
    parts = []
    for g in range(SSM_GROUPS):
        yg = y[:, g * gw:(g + 1) * gw]
        yg = yg * lax.rsqrt(jnp.mean(yg * yg, axis=-1, keepdims=True) + RMS_EPS)
        parts.append((yg * nw[:, g * gw:(g + 1) * gw]).astype(BF16))
    yb = jnp.concatenate(parts, axis=1)
    m = jnp.dot(yb, wo_ref[...], preferred_element_type=F32)
    x1 = _ln(DN_ALPHA * x_ref[...] + m, g1_ref[...], b1_ref[...])
    x1_ref[...] = x1
    x1p_ref[...] = _pack_bf16_pairs(x1)
    first = (pl.program_id(0) == 0) & (s == 0)
    _route(x1, first, wr_ref, br_ref, ri_ref, rg_ref, cnt_ref, cnt_scr)


def mamba_ssd_block(x3, z, xs, bm, cm, dt, a_log, d_skip, norm_w, w_out, g1, b1, w_r, b_r):
    bsz, s_len, d = x3.shape
    d_inner = xs.shape[-1]
    gn = bm.shape[-1]
    tm = TOKEN_TILE
    alog = jnp.pad(a_log, (0, LANES - SSM_HEADS)).reshape(1, LANES)
    dfull = jnp.repeat(d_skip, SSM_HEAD_DIM).reshape(1, d_inner)
    head_of_col = jnp.arange(d_inner, dtype=jnp.int32) // SSM_HEAD_DIM
    expand = (jnp.arange(LANES, dtype=jnp.int32)[:, None] == head_of_col[None, :]).astype(BF16)
    args = (x3, z, xs, bm, cm, dt, alog, dfull, norm_w.reshape(1, d_inner), expand, w_out.astype(BF16),
            g1.reshape(1, d), b1.reshape(1, d), w_r, b_r)
    tok = lambda w: pl.BlockSpec((None, tm, w), lambda b, s: (b, s, 0))
    in_specs = [tok(d), tok(d_inner), tok(d_inner), tok(gn), tok(gn), tok(LANES)] + \
               [_const_spec(a.shape) for a in args[6:]]
    ns = s_len // tm
    route = pl.BlockSpec((2 * TOP_K, tm), lambda b, s: (0, b * ns + s))
    out_shape = (jax.ShapeDtypeStruct((bsz, s_len, d), F32),
                 jax.ShapeDtypeStruct((bsz, s_len, d // 2), U32),
                 jax.ShapeDtypeStruct((2 * TOP_K, bsz * s_len), jnp.int32),
                 jax.ShapeDtypeStruct((bsz, s_len, LANES), F32),
                 jax.ShapeDtypeStruct((N_EXPERTS, LANES), F32))
    out_specs = (tok(d), tok(d // 2), route, tok(LANES), _const_spec((N_EXPERTS, LANES)))
    return pl.pallas_call(
        _mamba_ssd_kernel,
        out_shape=out_shape,
        grid=(bsz, ns),
        in_specs=in_specs,
        out_specs=out_specs,
        scratch_shapes=[pltpu.VMEM((SSM_GROUPS, SSM_STATE, d_inner // SSM_GROUPS), F32),
                        pltpu.VMEM((tm, d_inner), F32),
                        pltpu.VMEM((tm, d_inner), F32),
                        pltpu.VMEM((N_EXPERTS, LANES), F32)],
        compiler_params=pltpu.CompilerParams(
            dimension_semantics=("arbitrary", "arbitrary"), vmem_limit_bytes=VMEM_LIMIT),
        name="mamba_ssd",
    )(*args)


def _sc_workers():
    info = plsc.get_sparse_core_info()
    return info.num_cores, info.num_subcores


def sc_scatter_rows(x, dest_km, n_rows):
    t, d = x.shape
    nc, ns = _sc_workers()
    per_w, rem = divmod(t, nc * ns)
    assert rem == 0 and per_w % SC_CHUNK == 0
    mesh = plsc.VectorSubcoreMesh(core_axis_name="core", subcore_axis_name="subcore")

    @functools.partial(
        pl.kernel,
        out_type=jax.ShapeDtypeStruct((n_rows, d), x.dtype),
        mesh=mesh,
        scratch_types=[pltpu.VMEM((SC_CHUNK,), jnp.int32) for _ in range(TOP_K)] +
                      [pltpu.VMEM((SC_CHUNK, d), x.dtype), pltpu.SemaphoreType.DMA],
    )
    def scatter(x_hbm, i_hbm, o_hbm, i0, i1, i2, i3, rows_v, sem):
        wid = lax.axis_index("subcore") * nc + lax.axis_index("core")
        base = wid * per_w
        idx = (i0, i1, i2, i3)

        @pl.loop(0, per_w // SC_CHUNK)
        def _(c):
            off = base + c * SC_CHUNK
            pltpu.sync_copy(x_hbm.at[pl.ds(off, SC_CHUNK)], rows_v)
            for k in range(TOP_K):
                pltpu.sync_copy(i_hbm.at[pl.ds(k * t + off, SC_CHUNK)], idx[k])
            copies = [pltpu.make_async_copy(rows_v, o_hbm.at[idx[k]], sem) for k in range(TOP_K)]
            for cp in copies:
                cp.start()
            for cp in copies:
                cp.wait()

    return scatter(x, dest_km)


def sc_gather_rows(table, idx):
    n = idx.shape[0]
    d = table.shape[1]
    nc, ns = _sc_workers()
    per_w, rem = divmod(n, nc * ns)
    n_chunks = per_w // SC_CHUNK
    assert rem == 0 and per_w % (2 * SC_CHUNK) == 0
    mesh = plsc.VectorSubcoreMesh(core_axis_name="core", subcore_axis_name="subcore")

    @functools.partial(
        pl.kernel,
        out_type=jax.ShapeDtypeStruct((n, d), table.dtype),
        mesh=mesh,
        scratch_types=[
            pltpu.VMEM((SC_CHUNK,), jnp.int32), pltpu.VMEM((SC_CHUNK,), jnp.int32),
            pltpu.VMEM((SC_CHUNK, d), table.dtype), pltpu.VMEM((SC_CHUNK, d), table.dtype),
            pltpu.SemaphoreType.DMA, pltpu.SemaphoreType.DMA,
            pltpu.SemaphoreType.DMA, pltpu.SemaphoreType.DMA,
        ],
    )
    def gather(x_hbm, i_hbm, o_hbm, idx0, idx1, rows0, rows1, sg0, sg1, sw0, sw1):
        wid = lax.axis_index("subcore") * nc + lax.axis_index("core")
        base = wid * per_w
        idx = (idx0, idx1)
        rows = (rows0, rows1)
        sg = (sg0, sg1)
        sw = (sw0, sw1)

        def gather_copy(slot):
            return pltpu.make_async_copy(x_hbm.at[idx[slot]], rows[slot], sg[slot])

        def write_copy(c, slot):
            return pltpu.make_async_copy(rows[slot], o_hbm.at[pl.ds(base + c * SC_CHUNK, SC_CHUNK)], sw[slot])

        pltpu.sync_copy(i_hbm.at[pl.ds(base, SC_CHUNK)], idx0)
        gather_copy(0).start()

        @pl.loop(0, n_chunks, step=2)
        def _(c0):
            for slot in range(2):
                c = c0 + slot
                nxt = 1 - slot

                @pl.when(c + 1 < n_chunks)
                def _():
                    pltpu.sync_copy(i_hbm.at[pl.ds(base + (c + 1) * SC_CHUNK, SC_CHUNK)], idx[nxt])

                    @pl.when(c >= 1)
                    def _():
                        write_copy(c - 1, nxt).wait()

                    gather_copy(nxt).start()

                gather_copy(slot).wait()
                write_copy(c, slot).start()

        write_copy(n_chunks - 2, 0).wait()
        write_copy(n_chunks - 1, 1).wait()

    return gather(table, idx)


def _moe_mlp_kernel(be_ref, nv_ref, nu_ref, x_ref, w1_ref, b1_ref, w2_ref, b2_ref, o_ref, w1b, w2b):
    i = pl.program_id(0)
    e = be_ref[i]
    prev = be_ref[jnp.maximum(i - 1, 0)]

    @pl.when((i == 0) | (e != prev))
    def _():
        w1b[...] = w1_ref[...].astype(BF16)
        w2b[...] = w2_ref[...].astype(BF16)

    @pl.when(i < nu_ref[0])
    def _():
        dff = w2b.shape[0]
        row = lax.broadcasted_iota(jnp.int32, x_ref.shape, 0)
        xw = jnp.where(row < nv_ref[i], x_ref[...], jnp.uint32(0))
        xb = _unpack_bf16_pairs(xw).astype(BF16)
        h = jnp.dot(xb, w1b[...], preferred_element_type=F32) + b1_ref[...]
        g = jnp.minimum(h[:, :dff], SWIGLU_LIMIT)
        lin = jnp.clip(h[:, dff:], -SWIGLU_LIMIT, SWIGLU_LIMIT)
        act = (g * jax.nn.sigmoid(SWIGLU_ALPHA * g) * (lin + 1.0)).astype(BF16)
        y = jnp.dot(act, w2b[...], preferred_element_type=F32) + b2_ref[...]
        o_ref[...] = _pack_bf16_pairs(y)


def moe_mlp(xp, block_expert, n_valid, n_used, layer, w1, b1, w2, b2):
    n_rows, dh = xp.shape
    _, n_e, d, two_dff = w1.shape
    dff = two_dff // 2
    n_blocks = n_rows // MOE_BLOCK
    row = lambda i, be, nv, nu: (jnp.minimum(i, nu[0] - 1), 0)
    wsel = lambda i, be, nv, nu: (layer, be[i], 0, 0)
    grid_spec = pltpu.PrefetchScalarGridSpec(
        num_scalar_prefetch=3,
        grid=(n_blocks,),
        in_specs=[
            pl.BlockSpec((MOE_BLOCK, dh), row),
            pl.BlockSpec((None, None, d, two_dff), wsel),
            pl.BlockSpec((None, None, 1, two_dff), wsel),
            pl.BlockSpec((None, None, dff, d), wsel),
            pl.BlockSpec((None, None, 1, d), wsel),
        ],
        out_specs=pl.BlockSpec((MOE_BLOCK, dh), row),
        scratch_shapes=[pltpu.VMEM((d, two_dff), BF16), pltpu.VMEM((dff, d), BF16)],
    )
    return pl.pallas_call(
        _moe_mlp_kernel,
        out_shape=jax.ShapeDtypeStruct((n_rows, dh), U32),
        grid_spec=grid_spec,
        compiler_params=pltpu.CompilerParams(
            dimension_semantics=("arbitrary",), vmem_limit_bytes=VMEM_LIMIT),
        name="moe_mlp",
    )(block_expert, n_valid, n_used, xp, w1, b1.reshape(b1.shape[0], n_e, 1, two_dff),
      w2, b2.reshape(b2.shape[0], n_e, 1, d))


def _combine_kernel(x1_ref, y0_ref, y1_ref, y2_ref, y3_ref, rg_ref, g2_ref, b2_ref, *rest):
    o_ref = rest[-1]
    rg = rg_ref[...]
    f = rg[:, 0:1] * _unpack_bf16_pairs(y0_ref[...])
    for k, y_ref in ((1, y1_ref), (2, y2_ref), (3, y3_ref)):
        f = f + rg[:, k:k + 1] * _unpack_bf16_pairs(y_ref[...])
    o_ref[...] = _ln(DN_ALPHA * x1_ref[...] + f, g2_ref[...], b2_ref[...])


def combine_block(x1, y4p, rg, g2, b2, part, partial_out):
    t, d = x1.shape
    tm = TOKEN_TILE
    ntp = t // tm // COMBINE_PARTS
    base = part * ntp
    tok = lambda w: pl.BlockSpec((tm, w), lambda i: (base + i, 0))
    y_specs = [pl.BlockSpec((tm, d // 2), functools.partial(lambda i, k: (k * ntp + i, 0), k=k))
               for k in range(TOP_K)]
    in_specs = [tok(d)] + y_specs + [tok(LANES), _const_spec((1, d)), _const_spec((1, d))]
    args = [x1, y4p, y4p, y4p, y4p, rg, g2.reshape(1, d), b2.reshape(1, d)]
    aliases = {}
    if partial_out is not None:
        in_specs.append(pl.BlockSpec(memory_space=pl.ANY))
        args.append(partial_out)
        aliases = {len(args) - 1: 0}
    return pl.pallas_call(
        _combine_kernel,
        out_shape=jax.ShapeDtypeStruct((t, d), F32),
        grid=(ntp,),
        in_specs=in_specs,
        out_specs=tok(d),
        input_output_aliases=aliases,
        compiler_params=pltpu.CompilerParams(
            dimension_semantics=("arbitrary",), vmem_limit_bytes=VMEM_LIMIT),
        name="moe_combine",
    )(*args)


def moe_block(x1, x1p, ri, rg, cnt, layer, w1, b1, w2, b2, g2, b2n):
    t, d = x1.shape
    n_assign = t * TOP_K
    n_blocks = -(-n_assign // MOE_BLOCK) + N_EXPERTS
    n_rows = n_blocks * MOE_BLOCK
    ar = jnp.arange(N_EXPERTS, dtype=jnp.int32)
    counts = cnt[:, 0].astype(jnp.int32)
    padded = (counts + MOE_BLOCK - 1) // MOE_BLOCK * MOE_BLOCK
    pend = jnp.sum(jnp.where(ar[None, :] <= ar[:, None], padded[None, :], 0), axis=1)
    pstart = pend - padded
    top_i = ri[:TOP_K]
    dest = jnp.sum(jnp.where(top_i[:, :, None] == ar, pstart, 0), axis=-1) + ri[TOP_K:]
    dest_km = dest.reshape(-1)
    n_used = (pend[-1] // MOE_BLOCK).reshape(1)
    block_start = jnp.arange(n_blocks, dtype=jnp.int32) * MOE_BLOCK
    block_expert = jnp.minimum(
        jnp.sum((pend[None, :] <= block_start[:, None]).astype(jnp.int32), axis=1), N_EXPERTS - 1)
    last = jnp.sum(jnp.where(jnp.arange(n_blocks) == n_used[0] - 1, block_expert, 0))
    block_expert = jnp.where(jnp.arange(n_blocks) < n_used[0], block_expert, last)
    vend = jnp.sum(jnp.where(block_expert[:, None] == ar, (pstart + counts)[None, :], 0), axis=1)
    n_valid = jnp.clip(vend - block_start, 0, MOE_BLOCK)

    xp = sc_scatter_rows(x1p, dest_km, n_rows)
    yp = moe_mlp(xp, block_expert, n_valid, n_used, layer, w1, b1, w2, b2)
    tp = t // COMBINE_PARTS
    out = None
    for part in range(COMBINE_PARTS):
        y4p = sc_gather_rows(yp, dest[:, part * tp:(part + 1) * tp].reshape(-1))
        out = combine_block(x1, y4p, rg, g2, b2n, part, out)
    return out


def kernel(x, a_w_in, a_b_in, a_ln_g, a_ln_b, a_w_s, a_b_s, a_w_out, a_b_out, b_w_in, b_conv_w, b_conv_b, b_dt_bias, b_a_log, b_d, b_norm_w, b_w_out, moe_w_router, moe_b_router, moe_w1, moe_b1, moe_w2, moe_b2, ln1_g, ln1_b, ln2_g, ln2_b):
    bsz, s_len, d = x.shape
    t = bsz * s_len
    xt = x.reshape(t, d)
    for i in range(DEPTH):
        j = i // 2
        w_r32 = jnp.pad(moe_w_router[i], ((0, 0), (0, LANES - N_EXPERTS)))
        w_r_hi = w_r32.astype(BF16)
        w_r = jnp.stack([w_r_hi, (w_r32 - w_r_hi.astype(F32)).astype(BF16)])
        b_r = moe_b_router[i].reshape(N_EXPERTS, 1)
        if i % 2 == 0:
            x1, x1p, ri, rg, cnt = mixer_a_block(
                xt, a_w_in[j], a_b_in[j], a_ln_g[j], a_ln_b[j], a_w_s[j], a_b_s[j],
                a_w_out[j], a_b_out[j], ln1_g[i], ln1_b[i], w_r, b_r)
        else:
            x3 = xt.reshape(bsz, s_len, d)
            z, xs, bm, cm, dt = mamba_in_block(x3, b_w_in[j], b_conv_w[j], b_conv_b[j], b_dt_bias[j])
            x1, x1p, ri, rg, cnt = mamba_ssd_block(
                x3, z, xs, bm, cm, dt, b_a_log[j], b_d[j], b_norm_w[j],
                b_w_out[j], ln1_g[i], ln1_b[i], w_r, b_r)
            x1 = x1.reshape(t, d)
            x1p = x1p.reshape(t, d // 2)
            rg = rg.reshape(t, LANES)
        xt = moe_block(x1, x1p, ri, rg, cnt, i, moe_w1, moe_b1, moe_w2, moe_b2, ln2_g[i], ln2_b[i])
    return xt.reshape(bsz, s_len, d)
```

```python
import functools
import math

import jax
import jax.numpy as jnp
from jax import lax
from jax.experimental import pallas as pl
from jax.experimental.pallas import tpu as pltpu
from jax.experimental.pallas import tpu_sc as plsc

F32 = jnp.float32
BF16 = jnp.bfloat16
U32 = jnp.uint32

DEPTH = 4
N_EXPERTS = 32
TOP_K = 4
MOE_BLOCK = 512
SWIGLU_LIMIT = 7.0
SWIGLU_ALPHA = 1.702
A_BLOCK = 128
A_GROUPS = 8
CHUNK = 64
SSD_BLOCK = 128
SSM_HEADS = 32
SSM_HEAD_DIM = 64
SSM_GROUPS = 4
SSM_STATE = 128
SSM_CONV = 4
DN_ALPHA = (2 * DEPTH) ** 0.25
LN_EPS = 1e-5
LOG2E = 1.0 / math.log(2.0)
RMS_EPS = 1e-5

LANES = 128
SUBLANES = 8
TOKEN_TILE = 256
VMEM_LIMIT = 56 * 1024 * 1024
SC_CHUNK = 64
CONV_COLS = 256
COMBINE_PARTS = 2


def _ln(x, g, b):
    mu = jnp.mean(x, axis=-1, keepdims=True)
    xc = x - mu
    var = jnp.mean(xc * xc, axis=-1, keepdims=True)
    return xc * lax.rsqrt(var + LN_EPS) * g + b


def _gelu(x):
    return 0.5 * x * (1.0 + lax.erf(x * (1.0 / math.sqrt(2.0))))


def _pack_bf16_pairs(y):
    h = y.shape[1] // 2
    lo = lax.bitcast_convert_type(y[:, :h].astype(BF16).astype(F32), U32) >> 16
    hi = lax.bitcast_convert_type(y[:, h:].astype(BF16).astype(F32), U32) & jnp.uint32(0xFFFF0000)
    return hi | lo


def _unpack_bf16_pairs(w):
    lo = lax.bitcast_convert_type(w << 16, F32)
    hi = lax.bitcast_convert_type(w & jnp.uint32(0xFFFF0000), F32)
    return jnp.concatenate([lo, hi], axis=1)


def _route(x1, first, wr_ref, brc_ref, ri_ref, rg_ref, cnt_ref, cnt_scr):
    @pl.when(first)
    def _():
        cnt_scr[...] = jnp.zeros(cnt_scr.shape, F32)

    x_hi = x1.astype(BF16)
    x_lo = (x1 - x_hi.astype(F32)).astype(BF16)
    logits_tok = (jnp.dot(x_hi, wr_ref[0], preferred_element_type=F32)
                  + (jnp.dot(x_hi, wr_ref[1], preferred_element_type=F32)
                     + jnp.dot(x_lo, wr_ref[0], preferred_element_type=F32)))
    n_e = brc_ref.shape[0]
    logits = logits_tok.T[:n_e, :] + brc_ref[...]
    tm = logits.shape[1]
    sub_f = lax.broadcasted_iota(jnp.int32, logits.shape, 0).astype(F32)
    vals, idxs, hots = [], [], []
    l = logits
    for _ in range(TOP_K):
        m = jnp.max(l, axis=0, keepdims=True)
        i = jnp.min(jnp.where(l == m, sub_f, float(n_e)), axis=0, keepdims=True)
        hot = sub_f == i
        vals.append(m)
        idxs.append(i)
        hots.append(hot.astype(F32))
        l = jnp.where(hot, -jnp.inf, l)
    es = [jnp.exp(v - vals[0]) for v in vals]
    tot = es[0] + es[1] + es[2] + es[3]
    cnt = (hots[0] + hots[1] + hots[2] + hots[3]).astype(BF16)
    r_i = lax.broadcasted_iota(jnp.int32, (tm, tm), 0)
    c_i = lax.broadcasted_iota(jnp.int32, (tm, tm), 1)
    before = jnp.dot(cnt, (r_i < c_i).astype(BF16), preferred_element_type=F32)
    before = before + jnp.concatenate([cnt_scr[...]] * (tm // LANES), axis=1)
    row = lax.broadcasted_iota(jnp.int32, (2 * TOP_K, tm), 0)
    ri = jnp.zeros((2 * TOP_K, tm), F32)
    rg = jnp.zeros((2 * TOP_K, tm), F32)
    for k in range(TOP_K):
        rank = jnp.sum(before * hots[k], axis=0, keepdims=True)
        ri = jnp.where(row == k, idxs[k], ri)
        ri = jnp.where(row == TOP_K + k, rank, ri)
        rg = jnp.where(row == k, es[k] / tot, rg)
    ri_ref[...] = ri.astype(jnp.int32)
    rg_ref[...] = jnp.concatenate([rg, jnp.zeros((LANES - 2 * TOP_K, tm), F32)], axis=0).T
    total = cnt_scr[...] + jnp.dot(cnt, jnp.ones((tm, LANES), BF16), preferred_element_type=F32)
    cnt_scr[...] = total
    cnt_ref[...] = total


def _const_spec(shape):
    nd = len(shape)
    return pl.BlockSpec(shape, lambda *_: (0,) * nd)


def _mixer_a_kernel(x_ref, wu_ref, wv_ref, bu_ref, bv_ref, lg_ref, lb_ref, ws_ref, bst_ref,
                    wo_ref, bo_ref, g1_ref, b1_ref, wr_ref, br_ref,
                    x1_ref, x1p_ref, ri_ref, rg_ref, cnt_ref, v_scr, o_scr, cnt_scr):
    x = x_ref[...]
    xb = x.astype(BF16)
    tm, dff = v_scr.shape
    gd = dff // A_GROUPS
    gcols = [slice(g * gd, (g + 1) * gd) for g in range(A_GROUPS)]
    s1 = jnp.zeros((tm, 1), F32)
    s2 = jnp.zeros((tm, 1), F32)
    shift = None
    for cols in gcols:
        v_g = _gelu(jnp.dot(xb, wv_ref[:, cols], preferred_element_type=F32) + bv_ref[:, cols])
        v_scr[:, cols] = v_g
        if shift is None:
            shift = jnp.mean(v_g, axis=-1, keepdims=True)
        dv = v_g - shift
        s1 = s1 + jnp.sum(dv, axis=-1, keepdims=True)
        s2 = s2 + jnp.sum(dv * dv, axis=-1, keepdims=True)
    dmu = s1 * (1.0 / dff)
    mu = shift + dmu
    rstd = lax.rsqrt(s2 * (1.0 / dff) - dmu * dmu + LN_EPS)
    pi = lax.broadcasted_iota(jnp.int32, (A_BLOCK, A_BLOCK), 0) // CHUNK
    pj = lax.broadcasted_iota(jnp.int32, (A_BLOCK, A_BLOCK), 1) // CHUNK
    mask = pj <= pi
    bst = bst_ref[...]
    for g, cols in enumerate(gcols):
        u_g = _gelu(jnp.dot(xb, wu_ref[:, cols], preferred_element_type=F32) + bu_ref[:, cols])
        vb_g = ((v_scr[:, cols] - mu) * rstd * lg_ref[:, cols] + lb_ref[:, cols]).astype(BF16)
        wm = jnp.where(mask, ws_ref[g], 0.0).astype(BF16)
        for n in range(tm // A_BLOCK):
            rows = slice(n * A_BLOCK, (n + 1) * A_BLOCK)
            sv = jnp.dot(wm, vb_g[rows, :], preferred_element_type=F32) + bst[:, g:g + 1]
            o_scr[rows, cols] = (u_g[rows, :] * sv).astype(BF16)
    m = jnp.dot(o_scr[...], wo_ref[...], preferred_element_type=F32) + bo_ref[...]
    x1 = _ln(DN_ALPHA * x + m, g1_ref[...], b1_ref[...])
    x1_ref[...] = x1
    x1p_ref[...] = _pack_bf16_pairs(x1)
    _route(x1, pl.program_id(0) == 0, wr_ref, br_ref, ri_ref, rg_ref, cnt_ref, cnt_scr)


def mixer_a_block(xt, w_in, b_in, ln_g, ln_b, w_s, b_s, w_out, b_out, g1, b1, w_r, b_r):
    t, d = xt.shape
    dff = w_out.shape[0]
    tm = TOKEN_TILE
    wu = w_in[:, :dff].astype(BF16)
    wv = w_in[:, dff:].astype(BF16)
    bu = b_in[:dff].reshape(1, dff)
    bv = b_in[dff:].reshape(1, dff)
    args = (xt, wu, wv, bu, bv, ln_g.reshape(1, dff), ln_b.reshape(1, dff), w_s, b_s.T,
            w_out.astype(BF16), b_out.reshape(1, d), g1.reshape(1, d), b1.reshape(1, d), w_r, b_r)
    tok = lambda w: pl.BlockSpec((tm, w), lambda i: (i, 0))
    route = pl.BlockSpec((2 * TOP_K, tm), lambda i: (0, i))
    in_specs = [tok(d)] + [_const_spec(a.shape) for a in args[1:]]
    out_shape = (jax.ShapeDtypeStruct((t, d), F32),
                 jax.ShapeDtypeStruct((t, d // 2), U32),
                 jax.ShapeDtypeStruct((2 * TOP_K, t), jnp.int32),
                 jax.ShapeDtypeStruct((t, LANES), F32),
                 jax.ShapeDtypeStruct((N_EXPERTS, LANES), F32))
    out_specs = (tok(d), tok(d // 2), route, tok(LANES), _const_spec((N_EXPERTS, LANES)))
    return pl.pallas_call(
        _mixer_a_kernel,
        out_shape=out_shape,
        grid=(t // tm,),
        in_specs=in_specs,
        out_specs=out_specs,
        scratch_shapes=[pltpu.VMEM((tm, dff), F32), pltpu.VMEM((tm, dff), BF16),
                        pltpu.VMEM((N_EXPERTS, LANES), F32)],
        compiler_params=pltpu.CompilerParams(
            dimension_semantics=("arbitrary",), vmem_limit_bytes=VMEM_LIMIT),
        name="mixer_a",
    )(*args)


def _split3_bf16(q):
    hi = q.astype(BF16)
    r1 = q - hi.astype(F32)
    mid = r1.astype(BF16)
    lo = (r1 - mid.astype(F32)).astype(BF16)
    return hi, mid, lo


def _mamba_kernel(x_ref, wz_ref, wx_ref, wdt_ref, cw_ref, cb_ref, dtb_ref,
                  alog_ref, dfull_ref, nw_ref, exp_ref, wo_ref, g1_ref, b1_ref, wr_ref, br_ref,
                  x1_ref, x1p_ref, ri_ref, rg_ref, cnt_ref,
                  xs_scr, bm_scr, cm_scr, z_scr, state_scr, y_scr, dtw_scr, cnt_scr, *ext_scrs):
    s = pl.program_id(1)
    tm = x_ref.shape[0]
    d_inner = xs_scr.shape[1]
    gn = bm_scr.shape[1]
    L, P, N = SSD_BLOCK, SSM_HEAD_DIM, SSM_STATE
    R = SSM_HEADS // SSM_GROUPS
    gw = R * P
    nblk = tm // L

    @pl.when(s == 0)
    def _():
        state_scr[...] = jnp.zeros(state_scr.shape, F32)
        for ext in ext_scrs:
            ext[...] = jnp.zeros(ext.shape, F32)

    x = x_ref[...]
    xb = x.astype(BF16)
    dt = jnp.dot(xb, wdt_ref[...], preferred_element_type=F32) + dtb_ref[...]
    dt = jnp.maximum(dt, 0.0) + jnp.log1p(jnp.exp(-jnp.abs(dt)))

    cw = cw_ref[...]
    cbias = cb_ref[...]
    w = CONV_COLS
    for c, ext in enumerate(ext_scrs):
        cols = slice(c * w, (c + 1) * w)
        ext[0:8, :] = jnp.where(s > 0, ext[tm:tm + 8, :], 0.0)
        ext[8:8 + tm, :] = jnp.dot(xb, wx_ref[:, cols], preferred_element_type=F32)
        acc = cbias[:, cols] + cw[0:1, cols] * ext[5:5 + tm, :]
        for j in range(1, SSM_CONV):
            acc = acc + cw[j:j + 1, cols] * ext[5 + j:5 + j + tm, :]
        act = (acc * jax.nn.sigmoid(acc)).astype(BF16)
        lo = c * w
        if lo < d_inner:
            xs_scr[:, lo:lo + w] = act
        elif lo < d_inner + gn:
            bm_scr[:, lo - d_inner:lo - d_inner + w] = act
        else:
            cm_scr[:, lo - d_inner - gn:lo - d_inner - gn + w] = act

    def z_task(c):
        cols = slice(c * w, (c + 1) * w)
        z_scr[:, cols] = jnp.dot(xb, wz_ref[:, cols], preferred_element_type=F32).astype(BF16)

    z_tasks = [functools.partial(z_task, c) for c in range(d_inner // w)]

    a_row = -jnp.exp(alog_ref[...])
    adt = dt * a_row
    r_i = lax.broadcasted_iota(jnp.int32, (tm, tm), 0)
    c_i = lax.broadcasted_iota(jnp.int32, (tm, tm), 1)
    btril = ((r_i >= c_i) & (r_i // L == c_i // L)).astype(F32)
    acum = jnp.dot(btril, adt, preferred_element_type=F32,
                   precision=lax.Precision.HIGHEST)
    row_blk = lax.broadcasted_iota(jnp.int32, (tm, LANES), 0) // L
    alast = [acum[(b + 1) * L - 1:(b + 1) * L, :] for b in range(nblk)]
    alast_rows = alast[0]
    for b in range(1, nblk):
        alast_rows = jnp.where(row_blk == b, alast[b], alast_rows)
    dtw = dt * jnp.exp(alast_rows - acum)
    expand = exp_ref[...]
    dtw_scr[...] = jnp.dot(dtw.astype(BF16), expand, preferred_element_type=F32)
    acum2 = acum * LOG2E
    arow_t = acum2.T - jnp.log(dt.T) * LOG2E
    tril = lax.broadcasted_iota(jnp.int32, (L, L), 0) >= lax.broadcasted_iota(jnp.int32, (L, L), 1)
    lo_half = lax.broadcasted_iota(jnp.int32, (L, 2 * P), 1) < P

    for b in range(nblk):
        rows = slice(b * L, (b + 1) * L)
        acum_b = acum2[rows, :]
        elast = jnp.broadcast_to(jnp.exp(alast[b]), (SUBLANES, LANES))
        dec_all = sum(jnp.dot(piece, expand, preferred_element_type=F32)
                      for piece in _split3_bf16(elast))[0:1, :]
        for g in range(SSM_GROUPS):
            c_g = cm_scr[rows, g * N:(g + 1) * N]
            b_g = bm_scr[rows, g * N:(g + 1) * N]
            cb = lax.dot_general(c_g, b_g, (((1,), (1,)), ((), ())), preferred_element_type=F32)
            st = state_scr[g]
            y_off = jnp.dot(c_g, st.astype(BF16), preferred_element_type=F32)
            for q in range(R // 2):
                h0 = g * R + 2 * q
                ms, es = [], []
                for h in (h0, h0 + 1):
                    acol = jnp.broadcast_to(acum_b[:, h:h + 1], (L, L))
                    seg = acol - arow_t[h:h + 1, rows]
                    ms.append((cb * jnp.exp2(jnp.where(tril, seg, -jnp.inf))).astype(BF16))
                    es.append(jnp.exp2(acol))
                lhs = jnp.concatenate(ms, axis=1)
                cols = slice(h0 * P, (h0 + 2) * P)
                xp = xs_scr[rows, cols]
                zero = jnp.zeros_like(xp)
                rhs = jnp.concatenate([jnp.where(lo_half, xp, zero), jnp.where(lo_half, zero, xp)], axis=0)
                y_d = jnp.dot(lhs, rhs, preferred_element_type=F32)
                scale = jnp.where(lo_half, es[0], es[1])
                y_scr[rows, cols] = y_d + y_off[:, 2 * q * P:(2 * q + 2) * P] * scale
            gcols = slice(g * gw, (g + 1) * gw)
            xw = (xs_scr[rows, gcols].astype(F32) * dtw_scr[rows, gcols]).astype(BF16)
            contrib = lax.dot_general(b_g, xw, (((0,), (0,)), ((), ())), preferred_element_type=F32)
            state_scr[g] = st * dec_all[:, gcols] + contrib
            if z_tasks:
                z_tasks.pop(0)()
    for task in z_tasks:
        task()

    y = y_scr[...] + dfull_ref[...] * xs_scr[...].astype(F32)
    zf = z_scr[...].astype(F32)
    y = y * (zf * jax.nn.sigmoid(zf))
    nw = nw_ref[...]
    parts = []
    for g in range(SSM_GROUPS):
        yg = y[:, g * gw:(g + 1) * gw]
        yg = yg * lax.rsqrt(jnp.mean(yg * yg, axis=-1, keepdims=True) + RMS_EPS)
        parts.append((yg * nw[:, g * gw:(g + 1) * gw]).astype(BF16))
    yb = jnp.concatenate(parts, axis=1)
    m = jnp.dot(yb, wo_ref[...], preferred_element_type=F32)
    x1 = _ln(DN_ALPHA * x + m, g1_ref[...], b1_ref[...])
    x1_ref[...] = x1
    x1p_ref[...] = _pack_bf16_pairs(x1)
    first = (pl.program_id(0) == 0) & (s == 0)
    _route(x1, first, wr_ref, br_ref, ri_ref, rg_ref, cnt_ref, cnt_scr)


def mamba_block(x3, w_in, conv_w, conv_b, dt_bias, a_log, d_skip, norm_w, w_out, g1, b1, w_r, b_r):
    bsz, s_len, d = x3.shape
    d_inner = SSM_HEADS * SSM_HEAD_DIM
    gn = SSM_GROUPS * SSM_STATE
    conv_dim = d_inner + 2 * gn
    tm = TOKEN_TILE
    wz = w_in[:, :d_inner].astype(BF16)
    wx = w_in[:, d_inner:d_inner + conv_dim].astype(BF16)
    wdt = jnp.pad(w_in[:, d_inner + conv_dim:], ((0, 0), (0, LANES - SSM_HEADS))).astype(BF16)
    dtb = jnp.pad(dt_bias, (0, LANES - SSM_HEADS)).reshape(1, LANES)
    alog = jnp.pad(a_log, (0, LANES - SSM_HEADS)).reshape(1, LANES)
    dfull = jnp.repeat(d_skip, SSM_HEAD_DIM).reshape(1, d_inner)
    head_of_col = jnp.arange(d_inner, dtype=jnp.int32) // SSM_HEAD_DIM
    expand = (jnp.arange(LANES, dtype=jnp.int32)[:, None] == head_of_col[None, :]).astype(BF16)
    args = (x3, wz, wx, wdt, conv_w, conv_b.reshape(1, conv_dim), dtb, alog, dfull,
            norm_w.reshape(1, d_inner), expand, w_out.astype(BF16), g1.reshape(1, d), b1.reshape(1, d), w_r, b_r)
    tok = lambda w: pl.BlockSpec((None, tm, w), lambda b, s: (b, s, 0))
    in_specs = [tok(d)] + [_const_spec(a.shape) for a in args[1:]]
    ns = s_len // tm
    route = pl.BlockSpec((2 * TOP_K, tm), lambda b, s: (0, b * ns + s))
    out_shape = (jax.ShapeDtypeStruct((bsz, s_len, d), F32),
                 jax.ShapeDtypeStruct((bsz, s_len, d // 2), U32),
                 jax.ShapeDtypeStruct((2 * TOP_K, bsz * s_len), jnp.int32),
                 jax.ShapeDtypeStruct((bsz, s_len, LANES), F32),
                 jax.ShapeDtypeStruct((N_EXPERTS, LANES), F32))
    out_specs = (tok(d), tok(d // 2), route, tok(LANES), _const_spec((N_EXPERTS, LANES)))
    return pl.pallas_call(
        _mamba_kernel,
        out_shape=out_shape,
        grid=(bsz, ns),
        in_specs=in_specs,
        out_specs=out_specs,
        scratch_shapes=[pltpu.VMEM((tm, d_inner), BF16), pltpu.VMEM((tm, gn), BF16), pltpu.VMEM((tm, gn), BF16),
                        pltpu.VMEM((tm, d_inner), BF16),
                        pltpu.VMEM((SSM_GROUPS, SSM_STATE, d_inner // SSM_GROUPS), F32),
                        pltpu.VMEM((tm, d_inner), F32),
                        pltpu.VMEM((tm, d_inner), F32),
                        pltpu.VMEM((N_EXPERTS, LANES), F32)] +
                       [pltpu.VMEM((tm + 8, CONV_COLS), F32) for _ in range(conv_dim // CONV_COLS)],
        compiler_params=pltpu.CompilerParams(
            dimension_semantics=("arbitrary", "arbitrary"), vmem_limit_bytes=VMEM_LIMIT),
        name="mamba",
    )(*args)


def _sc_workers():
    info = plsc.get_sparse_core_info()
    return info.num_cores, info.num_subcores


def sc_scatter_rows(x, dest_km, n_rows):
    t, d = x.shape
    nc, ns = _sc_workers()
    per_w, rem = divmod(t, nc * ns)
    assert rem == 0 and per_w % SC_CHUNK == 0
    mesh = plsc.VectorSubcoreMesh(core_axis_name="core", subcore_axis_name="subcore")

    @functools.partial(
        pl.kernel,
        out_type=jax.ShapeDtypeStruct((n_rows, d), x.dtype),
        mesh=mesh,
        scratch_types=[pltpu.VMEM((SC_CHUNK,), jnp.int32) for _ in range(TOP_K)] +
                      [pltpu.VMEM((SC_CHUNK, d), x.dtype), pltpu.SemaphoreType.DMA],
    )
    def scatter(x_hbm, i_hbm, o_hbm, i0, i1, i2, i3, rows_v, sem):
        wid = lax.axis_index("subcore") * nc + lax.axis_index("core")
        base = wid * per_w
        idx = (i0, i1, i2, i3)

        @pl.loop(0, per_w // SC_CHUNK)
        def _(c):
            off = base + c * SC_CHUNK
            pltpu.sync_copy(x_hbm.at[pl.ds(off, SC_CHUNK)], rows_v)
            for k in range(TOP_K):
                pltpu.sync_copy(i_hbm.at[pl.ds(k * t + off, SC_CHUNK)], idx[k])
            copies = [pltpu.make_async_copy(rows_v, o_hbm.at[idx[k]], sem) for k in range(TOP_K)]
            for cp in copies:
                cp.start()
            for cp in copies:
                cp.wait()

    return scatter(x, dest_km)


def sc_gather_rows(table, idx):
    n = idx.shape[0]
    d = table.shape[1]
    nc, ns = _sc_workers()
    per_w, rem = divmod(n, nc * ns)
    n_chunks = per_w // SC_CHUNK
    assert rem == 0 and per_w % (2 * SC_CHUNK) == 0
    mesh = plsc.VectorSubcoreMesh(core_axis_name="core", subcore_axis_name="subcore")

    @functools.partial(
        pl.kernel,
        out_type=jax.ShapeDtypeStruct((n, d), table.dtype),
        mesh=mesh,
        scratch_types=[
            pltpu.VMEM((SC_CHUNK,), jnp.int32), pltpu.VMEM((SC_CHUNK,), jnp.int32),
            pltpu.VMEM((SC_CHUNK, d), table.dtype), pltpu.VMEM((SC_CHUNK, d), table.dtype),
            pltpu.SemaphoreType.DMA, pltpu.SemaphoreType.DMA,
            pltpu.SemaphoreType.DMA, pltpu.SemaphoreType.DMA,
        ],
    )
    def gather(x_hbm, i_hbm, o_hbm, idx0, idx1, rows0, rows1, sg0, sg1, sw0, sw1):
        wid = lax.axis_index("subcore") * nc + lax.axis_index("core")
        base = wid * per_w
        idx = (idx0, idx1)
        rows = (rows0, rows1)
        sg = (sg0, sg1)
        sw = (sw0, sw1)

        def gather_copy(slot):
            return pltpu.make_async_copy(x_hbm.at[idx[slot]], rows[slot], sg[slot])

        def write_copy(c, slot):
            return pltpu.make_async_copy(rows[slot], o_hbm.at[pl.ds(base + c * SC_CHUNK, SC_CHUNK)], sw[slot])

        pltpu.sync_copy(i_hbm.at[pl.ds(base, SC_CHUNK)], idx0)
        gather_copy(0).start()

        @pl.loop(0, n_chunks, step=2)
        def _(c0):
            for slot in range(2):
                c = c0 + slot
                nxt = 1 - slot

                @pl.when(c + 1 < n_chunks)
                def _():
                    pltpu.sync_copy(i_hbm.at[pl.ds(base + (c + 1) * SC_CHUNK, SC_CHUNK)], idx[nxt])

                    @pl.when(c >= 1)
                    def _():
                        write_copy(c - 1, nxt).wait()

                    gather_copy(nxt).start()

                gather_copy(slot).wait()
                write_copy(c, slot).start()

        write_copy(n_chunks - 2, 0).wait()
        write_copy(n_chunks - 1, 1).wait()

    return gather(table, idx)


def _moe_mlp_kernel(be_ref, nv_ref, nu_ref, x_ref, w1_ref, b1_ref, w2_ref, b2_ref, o_ref, w1b, w2b):
    i = pl.program_id(0)
    e = be_ref[i]
    prev = be_ref[jnp.maximum(i - 1, 0)]

    @pl.when((i == 0) | (e != prev))
    def _():
        w1b[...] = w1_ref[...].astype(BF16)
        w2b[...] = w2_ref[...].astype(BF16)

    @pl.when(i < nu_ref[0])
    def _():
        dff = w2b.shape[0]
        row = lax.broadcasted_iota(jnp.int32, x_ref.shape, 0)
        xw = jnp.where(row < nv_ref[i], x_ref[...], jnp.uint32(0))
        xb = _unpack_bf16_pairs(xw).astype(BF16)
        h = jnp.dot(xb, w1b[...], preferred_element_type=F32) + b1_ref[...]
        g = jnp.minimum(h[:, :dff], SWIGLU_LIMIT)
        lin = jnp.clip(h[:, dff:], -SWIGLU_LIMIT, SWIGLU_LIMIT)
        act = (g * jax.nn.sigmoid(SWIGLU_ALPHA * g) * (lin + 1.0)).astype(BF16)
        y = jnp.dot(act, w2b[...], preferred_element_type=F32) + b2_ref[...]
        o_ref[...] = _pack_bf16_pairs(y)


def moe_mlp(xp, block_expert, n_valid, n_used, layer, w1, b1, w2, b2):
    n_rows, dh = xp.shape
    _, n_e, d, two_dff = w1.shape
    dff = two_dff // 2
    n_blocks = n_rows // MOE_BLOCK
    row = lambda i, be, nv, nu: (jnp.minimum(i, nu[0] - 1), 0)
    wsel = lambda i, be, nv, nu: (layer, be[i], 0, 0)
    grid_spec = pltpu.PrefetchScalarGridSpec(
        num_scalar_prefetch=3,
        grid=(n_blocks,),
        in_specs=[
            pl.BlockSpec((MOE_BLOCK, dh), row),
            pl.BlockSpec((None, None, d, two_dff), wsel),
            pl.BlockSpec((None, None, 1, two_dff), wsel),
            pl.BlockSpec((None, None, dff, d), wsel),
            pl.BlockSpec((None, None, 1, d), wsel),
        ],
        out_specs=pl.BlockSpec((MOE_BLOCK, dh), row),
        scratch_shapes=[pltpu.VMEM((d, two_dff), BF16), pltpu.VMEM((dff, d), BF16)],
    )
    return pl.pallas_call(
        _moe_mlp_kernel,
        out_shape=jax.ShapeDtypeStruct((n_rows, dh), U32),
        grid_spec=grid_spec,
        compiler_params=pltpu.CompilerParams(
            dimension_semantics=("arbitrary",), vmem_limit_bytes=VMEM_LIMIT),
        name="moe_mlp",
    )(block_expert, n_valid, n_used, xp, w1, b1.reshape(b1.shape[0], n_e, 1, two_dff),
      w2, b2.reshape(b2.shape[0], n_e, 1, d))


def _combine_kernel(x1_ref, y0_ref, y1_ref, y2_ref, y3_ref, rg_ref, g2_ref, b2_ref, *rest):
    o_ref = rest[-1]
    rg = rg_ref[...]
    f = rg[:, 0:1] * _unpack_bf16_pairs(y0_ref[...])
    for k, y_ref in ((1, y1_ref), (2, y2_ref), (3, y3_ref)):
        f = f + rg[:, k:k + 1] * _unpack_bf16_pairs(y_ref[...])
    o_ref[...] = _ln(DN_ALPHA * x1_ref[...] + f, g2_ref[...], b2_ref[...])


def combine_block(x1, y4p, rg, g2, b2, part, partial_out):
    t, d = x1.shape
    tm = TOKEN_TILE
    ntp = t // tm // COMBINE_PARTS
    base = part * ntp
    tok = lambda w: pl.BlockSpec((tm, w), lambda i: (base + i, 0))
    y_specs = [pl.BlockSpec((tm, d // 2), functools.partial(lambda i, k: (k * ntp + i, 0), k=k))
               for k in range(TOP_K)]
    in_specs = [tok(d)] + y_specs + [tok(LANES), _const_spec((1, d)), _const_spec((1, d))]
    args = [x1, y4p, y4p, y4p, y4p, rg, g2.reshape(1, d), b2.reshape(1, d)]
    aliases = {}
    if partial_out is not None:
        in_specs.append(pl.BlockSpec(memory_space=pl.ANY))
        args.append(partial_out)
        aliases = {len(args) - 1: 0}
    return pl.pallas_call(
        _combine_kernel,
        out_shape=jax.ShapeDtypeStruct((t, d), F32),
        grid=(ntp,),
        in_specs=in_specs,
        out_specs=tok(d),
        input_output_aliases=aliases,
        compiler_params=pltpu.CompilerParams(
            dimension_semantics=("arbitrary",), vmem_limit_bytes=VMEM_LIMIT),
        name="moe_combine",
    )(*args)


def moe_block(x1, x1p, ri, rg, cnt, layer, w1, b1, w2, b2, g2, b2n):
    t, d = x1.shape
    n_assign = t * TOP_K
    n_blocks = -(-n_assign // MOE_BLOCK) + N_EXPERTS
    n_rows = n_blocks * MOE_BLOCK
    ar = jnp.arange(N_EXPERTS, dtype=jnp.int32)
    counts = cnt[:, 0].astype(jnp.int32)
    padded = (counts + MOE_BLOCK - 1) // MOE_BLOCK * MOE_BLOCK
    pend = jnp.sum(jnp.where(ar[None, :] <= ar[:, None], padded[None, :], 0), axis=1)
    pstart = pend - padded
    top_i = ri[:TOP_K]
    dest = jnp.sum(jnp.where(top_i[:, :, None] == ar, pstart, 0), axis=-1) + ri[TOP_K:]
    dest_km = dest.reshape(-1)
    n_used = (pend[-1] // MOE_BLOCK).reshape(1)
    block_start = jnp.arange(n_blocks, dtype=jnp.int32) * MOE_BLOCK
    block_expert = jnp.minimum(
        jnp.sum((pend[None, :] <= block_start[:, None]).astype(jnp.int32), axis=1), N_EXPERTS - 1)
    last = jnp.sum(jnp.where(jnp.arange(n_blocks) == n_used[0] - 1, block_expert, 0))
    block_expert = jnp.where(jnp.arange(n_blocks) < n_used[0], block_expert, last)
    vend = jnp.sum(jnp.where(block_expert[:, None] == ar, (pstart + counts)[None, :], 0), axis=1)
    n_valid = jnp.clip(vend - block_start, 0, MOE_BLOCK)

    xp = sc_scatter_rows(x1p, dest_km, n_rows)
    yp = moe_mlp(xp, block_expert, n_valid, n_used, layer, w1, b1, w2, b2)
    tp = t // COMBINE_PARTS
    out = None
    for part in range(COMBINE_PARTS):
        y4p = sc_gather_rows(yp, dest[:, part * tp:(part + 1) * tp].reshape(-1))
        out = combine_block(x1, y4p, rg, g2, b2n, part, out)
    return out


def kernel(x, a_w_in, a_b_in, a_ln_g, a_ln_b, a_w_s, a_b_s, a_w_out, a_b_out, b_w_in, b_conv_w, b_conv_b, b_dt_bias, b_a_log, b_d, b_norm_w, b_w_out, moe_w_router, moe_b_router, moe_w1, moe_b1, moe_w2, moe_b2, ln1_g, ln1_b, ln2_g, ln2_b):
    bsz, s_len, d = x.shape
    t = bsz * s_len
    xt = x.reshape(t, d)
    for i in range(DEPTH):
        j = i // 2
        w_r32 = jnp.pad(moe_w_router[i], ((0, 0), (0, LANES - N_EXPERTS)))
        w_r_hi = w_r32.astype(BF16)
        w_r = jnp.stack([w_r_hi, (w_r32 - w_r_hi.astype(F32)).astype(BF16)])
        b_r = moe_b_router[i].reshape(N_EXPERTS, 1)
        if i % 2 == 0:
            x1, x1p, ri, rg, cnt = mixer_a_block(
                xt, a_w_in[j], a_b_in[j], a_ln_g[j], a_ln_b[j], a_w_s[j], a_b_s[j],
                a_w_out[j], a_b_out[j], ln1_g[i], ln1_b[i], w_r, b_r)
        else:
            x3 = xt.reshape(bsz, s_len, d)
            x1, x1p, ri, rg, cnt = mamba_block(
                x3, b_w_in[j], b_conv_w[j], b_conv_b[j], b_dt_bias[j], b_a_log[j], b_d[j], b_norm_w[j],
                b_w_out[j], ln1_g[i], ln1_b[i], w_r, b_r)
            x1 = x1.reshape(t, d)
            x1p = x1p.reshape(t, d // 2)
            rg = rg.reshape(t, LANES)
        xt = moe_block(x1, x1p, ri, rg, cnt, i, moe_w1, moe_b1, moe_w2, moe_b2, ln2_g[i], ln2_b[i])
    return xt.reshape(bsz, s_len, d)
```

```python
import functools
import math

import jax
import jax.numpy as jnp
from jax import lax
from jax.experimental import pallas as pl
from jax.experimental.pallas import tpu as pltpu
from jax.experimental.pallas import tpu_sc as plsc

F32 = jnp.float32
BF16 = jnp.bfloat16
U32 = jnp.uint32

DEPTH = 4
N_EXPERTS = 32
TOP_K = 4
MOE_BLOCK = 512
SWIGLU_LIMIT = 7.0
SWIGLU_ALPHA = 1.702
A_BLOCK = 128
A_GROUPS = 8
CHUNK = 64
SSD_BLOCK = 128
SSM_HEADS = 32
SSM_HEAD_DIM = 64
SSM_GROUPS = 4
SSM_STATE = 128
SSM_CONV = 4
DN_ALPHA = (2 * DEPTH) ** 0.25
LN_EPS = 1e-5
LOG2E = 1.0 / math.log(2.0)
RMS_EPS = 1e-5

LANES = 128
SUBLANES = 8
TOKEN_TILE = 256
VMEM_LIMIT = 56 * 1024 * 1024
SC_CHUNK = 64
CONV_COLS = 256
COMBINE_PARTS = 2


def _ln(x, g, b):
    mu = jnp.mean(x, axis=-1, keepdims=True)
    xc = x - mu
    var = jnp.mean(xc * xc, axis=-1, keepdims=True)
    return xc * lax.rsqrt(var + LN_EPS) * g + b


def _gelu(x):
    return 0.5 * x * (1.0 + lax.erf(x * (1.0 / math.sqrt(2.0))))


def _silu(x):
    h = 0.5 * x
    return h + h * jnp.tanh(h)


def _pack_bf16_pairs(y):
    h = y.shape[1] // 2
    lo = lax.bitcast_convert_type(y[:, :h].astype(BF16).astype(F32), U32) >> 16
    hi = lax.bitcast_convert_type(y[:, h:].astype(BF16).astype(F32), U32) & jnp.uint32(0xFFFF0000)
    return hi | lo


def _unpack_bf16_pairs(w):
    lo = lax.bitcast_convert_type(w << 16, F32)
    hi = lax.bitcast_convert_type(w & jnp.uint32(0xFFFF0000), F32)
    return jnp.concatenate([lo, hi], axis=1)


def _route(x1, first, wr_ref, brc_ref, ri_ref, rg_ref, cnt_ref, cnt_scr):
    @pl.when(first)
    def _():
        cnt_scr[...] = jnp.zeros(cnt_scr.shape, F32)

    x_hi = x1.astype(BF16)
    x_lo = (x1 - x_hi.astype(F32)).astype(BF16)
    logits_tok = (jnp.dot(x_hi, wr_ref[0], preferred_element_type=F32)
                  + (jnp.dot(x_hi, wr_ref[1], preferred_element_type=F32)
                     + jnp.dot(x_lo, wr_ref[0], preferred_element_type=F32)))
    n_e = brc_ref.shape[0]
    logits = logits_tok.T[:n_e, :] + brc_ref[...]
    tm = logits.shape[1]
    sub_f = lax.broadcasted_iota(jnp.int32, logits.shape, 0).astype(F32)
    vals, idxs, hots = [], [], []
    l = logits
    for _ in range(TOP_K):
        m = jnp.max(l, axis=0, keepdims=True)
        i = jnp.min(jnp.where(l == m, sub_f, float(n_e)), axis=0, keepdims=True)
        hot = sub_f == i
        vals.append(m)
        idxs.append(i)
        hots.append(hot.astype(F32))
        l = jnp.where(hot, -jnp.inf, l)
    es = [jnp.exp(v - vals[0]) for v in vals]
    tot = es[0] + es[1] + es[2] + es[3]
    cnt = (hots[0] + hots[1] + hots[2] + hots[3]).astype(BF16)
    r_i = lax.broadcasted_iota(jnp.int32, (tm, tm), 0)
    c_i = lax.broadcasted_iota(jnp.int32, (tm, tm), 1)
    before = jnp.dot(cnt, (r_i < c_i).astype(BF16), preferred_element_type=F32)
    before = before + jnp.concatenate([cnt_scr[...]] * (tm // LANES), axis=1)
    row = lax.broadcasted_iota(jnp.int32, (2 * TOP_K, tm), 0)
    ri = jnp.zeros((2 * TOP_K, tm), F32)
    rg = jnp.zeros((2 * TOP_K, tm), F32)
    for k in range(TOP_K):
        rank = jnp.sum(before * hots[k], axis=0, keepdims=True)
        ri = jnp.where(row == k, idxs[k], ri)
        ri = jnp.where(row == TOP_K + k, rank, ri)
        rg = jnp.where(row == k, es[k] / tot, rg)
    ri_ref[...] = ri.astype(jnp.int32)
    rg_ref[...] = jnp.concatenate([rg, jnp.zeros((LANES - 2 * TOP_K, tm), F32)], axis=0).T
    total = cnt_scr[...] + jnp.dot(cnt, jnp.ones((tm, LANES), BF16), preferred_element_type=F32)
    cnt_scr[...] = total
    cnt_ref[...] = total


def _const_spec(shape):
    nd = len(shape)
    return pl.BlockSpec(shape, lambda *_: (0,) * nd)


def _mixer_a_kernel(x_ref, wu_ref, wv_ref, bu_ref, bv_ref, lg_ref, lb_ref, ws_ref, bst_ref,
                    wo_ref, bo_ref, g1_ref, b1_ref, wr_ref, br_ref,
                    x1_ref, x1p_ref, ri_ref, rg_ref, cnt_ref, v_scr, o_scr, cnt_scr):
    x = x_ref[...]
    xb = x.astype(BF16)
    tm, dff = v_scr.shape
    gd = dff // A_GROUPS
    gcols = [slice(g * gd, (g + 1) * gd) for g in range(A_GROUPS)]
    s1 = jnp.zeros((tm, 1), F32)
    s2 = jnp.zeros((tm, 1), F32)
    shift = None
    for cols in gcols:
        v_g = _gelu(jnp.dot(xb, wv_ref[:, cols], preferred_element_type=F32) + bv_ref[:, cols])
        v_scr[:, cols] = v_g
        if shift is None:
            shift = jnp.mean(v_g, axis=-1, keepdims=True)
        dv = v_g - shift
        s1 = s1 + jnp.sum(dv, axis=-1, keepdims=True)
        s2 = s2 + jnp.sum(dv * dv, axis=-1, keepdims=True)
    dmu = s1 * (1.0 / dff)
    mu = shift + dmu
    rstd = lax.rsqrt(s2 * (1.0 / dff) - dmu * dmu + LN_EPS)
    pi = lax.broadcasted_iota(jnp.int32, (A_BLOCK, A_BLOCK), 0) // CHUNK
    pj = lax.broadcasted_iota(jnp.int32, (A_BLOCK, A_BLOCK), 1) // CHUNK
    mask = pj <= pi
    bst = bst_ref[...]
    for g, cols in enumerate(gcols):
        u_g = _gelu(jnp.dot(xb, wu_ref[:, cols], preferred_element_type=F32) + bu_ref[:, cols])
        vb_g = ((v_scr[:, cols] - mu) * rstd * lg_ref[:, cols] + lb_ref[:, cols]).astype(BF16)
        wm = jnp.where(mask, ws_ref[g], 0.0).astype(BF16)
        for n in range(tm // A_BLOCK):
            rows = slice(n * A_BLOCK, (n + 1) * A_BLOCK)
            sv = jnp.dot(wm, vb_g[rows, :], preferred_element_type=F32) + bst[:, g:g + 1]
            o_scr[rows, cols] = (u_g[rows, :] * sv).astype(BF16)
    m = jnp.dot(o_scr[...], wo_ref[...], preferred_element_type=F32) + bo_ref[...]
    x1 = _ln(DN_ALPHA * x + m, g1_ref[...], b1_ref[...])
    x1_ref[...] = x1
    x1p_ref[...] = _pack_bf16_pairs(x1)
    _route(x1, pl.program_id(0) == 0, wr_ref, br_ref, ri_ref, rg_ref, cnt_ref, cnt_scr)


def mixer_a_block(xt, w_in, b_in, ln_g, ln_b, w_s, b_s, w_out, b_out, g1, b1, w_r, b_r):
    t, d = xt.shape
    dff = w_out.shape[0]
    tm = TOKEN_TILE
    wu = w_in[:, :dff].astype(BF16)
    wv = w_in[:, dff:].astype(BF16)
    bu = b_in[:dff].reshape(1, dff)
    bv = b_in[dff:].reshape(1, dff)
    args = (xt, wu, wv, bu, bv, ln_g.reshape(1, dff), ln_b.reshape(1, dff), w_s, b_s.T,
            w_out.astype(BF16), b_out.reshape(1, d), g1.reshape(1, d), b1.reshape(1, d), w_r, b_r)
    tok = lambda w: pl.BlockSpec((tm, w), lambda i: (i, 0))
    route = pl.BlockSpec((2 * TOP_K, tm), lambda i: (0, i))
    in_specs = [tok(d)] + [_const_spec(a.shape) for a in args[1:]]
    out_shape = (jax.ShapeDtypeStruct((t, d), F32),
                 jax.ShapeDtypeStruct((t, d // 2), U32),
                 jax.ShapeDtypeStruct((2 * TOP_K, t), jnp.int32),
                 jax.ShapeDtypeStruct((t, LANES), F32),
                 jax.ShapeDtypeStruct((N_EXPERTS, LANES), F32))
    out_specs = (tok(d), tok(d // 2), route, tok(LANES), _const_spec((N_EXPERTS, LANES)))
    return pl.pallas_call(
        _mixer_a_kernel,
        out_shape=out_shape,
        grid=(t // tm,),
        in_specs=in_specs,
        out_specs=out_specs,
        scratch_shapes=[pltpu.VMEM((tm, dff), F32), pltpu.VMEM((tm, dff), BF16),
                        pltpu.VMEM((N_EXPERTS, LANES), F32)],
        compiler_params=pltpu.CompilerParams(
            dimension_semantics=("arbitrary",), vmem_limit_bytes=VMEM_LIMIT),
        name="mixer_a",
    )(*args)


def _split3_bf16(q):
    hi = q.astype(BF16)
    r1 = q - hi.astype(F32)
    mid = r1.astype(BF16)
    lo = (r1 - mid.astype(F32)).astype(BF16)
    return hi, mid, lo


def _mamba_kernel(x_ref, wz_ref, wx_ref, wdt_ref, cw_ref, cb_ref, dtb_ref,
                  alog_ref, dfull_ref, nw_ref, exp_ref, wo_ref, g1_ref, b1_ref, wr_ref, br_ref,
                  x1_ref, x1p_ref, ri_ref, rg_ref, cnt_ref,
                  xs_scr, bm_scr, cm_scr, z_scr, state_scr, y_scr, dtw_scr, cnt_scr, *ext_scrs):
    s = pl.program_id(1)
    tm = x_ref.shape[0]
    d_inner = xs_scr.shape[1]
    gn = bm_scr.shape[1]
    L, P, N = SSD_BLOCK, SSM_HEAD_DIM, SSM_STATE
    R = SSM_HEADS // SSM_GROUPS
    gw = R * P
    nblk = tm // L

    @pl.when(s == 0)
    def _():
        state_scr[...] = jnp.zeros(state_scr.shape, F32)
        for ext in ext_scrs:
            ext[...] = jnp.zeros(ext.shape, F32)

    x = x_ref[...]
    xb = x.astype(BF16)
    dt = jnp.dot(xb, wdt_ref[...], preferred_element_type=F32) + dtb_ref[...]
    dt = jnp.maximum(dt, 0.0) + jnp.log1p(jnp.exp(-jnp.abs(dt)))

    cw = cw_ref[...]
    cbias = cb_ref[...]
    w = CONV_COLS
    for c, ext in enumerate(ext_scrs):
        cols = slice(c * w, (c + 1) * w)
        ext[0:8, :] = jnp.where(s > 0, ext[tm:tm + 8, :], 0.0)
        ext[8:8 + tm, :] = jnp.dot(xb, wx_ref[:, cols], preferred_element_type=F32)
        acc = cbias[:, cols] + cw[0:1, cols] * ext[5:5 + tm, :]
        for j in range(1, SSM_CONV):
            acc = acc + cw[j:j + 1, cols] * ext[5 + j:5 + j + tm, :]
        act = _silu(acc).astype(BF16)
        lo = c * w
        if lo < d_inner:
            xs_scr[:, lo:lo + w] = act
        elif lo < d_inner + gn:
            bm_scr[:, lo - d_inner:lo - d_inner + w] = act
        else:
            cm_scr[:, lo - d_inner - gn:lo - d_inner - gn + w] = act

    def z_task(c):
        cols = slice(c * w, (c + 1) * w)
        z_scr[:, cols] = jnp.dot(xb, wz_ref[:, cols], preferred_element_type=F32).astype(BF16)

    z_tasks = [functools.partial(z_task, c) for c in range(d_inner // w)]

    a_row = -jnp.exp(alog_ref[...])
    adt = dt * a_row
    r_i = lax.broadcasted_iota(jnp.int32, (tm, tm), 0)
    c_i = lax.broadcasted_iota(jnp.int32, (tm, tm), 1)
    btril = ((r_i >= c_i) & (r_i // L == c_i // L)).astype(F32)
    acum = jnp.dot(btril, adt, preferred_element_type=F32,
                   precision=lax.Precision.HIGHEST)
    row_blk = lax.broadcasted_iota(jnp.int32, (tm, LANES), 0) // L
    alast = [acum[(b + 1) * L - 1:(b + 1) * L, :] for b in range(nblk)]
    alast_rows = alast[0]
    for b in range(1, nblk):
        alast_rows = jnp.where(row_blk == b, alast[b], alast_rows)
    dtw = dt * jnp.exp(alast_rows - acum)
    expand = exp_ref[...]
    dtw_scr[...] = jnp.dot(dtw.astype(BF16), expand, preferred_element_type=F32)
    acum2 = acum * LOG2E
    arow_t = acum2.T - jnp.log(dt.T) * LOG2E
    tril = lax.broadcasted_iota(jnp.int32, (L, L), 0) >= lax.broadcasted_iota(jnp.int32, (L, L), 1)
    lo_half = lax.broadcasted_iota(jnp.int32, (L, 2 * P), 1) < P

    for b in range(nblk):
        rows = slice(b * L, (b + 1) * L)
        acum_b = acum2[rows, :]
        elast = jnp.broadcast_to(jnp.exp(alast[b]), (SUBLANES, LANES))
        dec_all = sum(jnp.dot(piece, expand, preferred_element_type=F32)
                      for piece in _split3_bf16(elast))[0:1, :]
        for g in range(SSM_GROUPS):
            c_g = cm_scr[rows, g * N:(g + 1) * N]
            b_g = bm_scr[rows, g * N:(g + 1) * N]
            cb = lax.dot_general(c_g, b_g, (((1,), (1,)), ((), ())), preferred_element_type=F32)
            st = state_scr[g]
            y_off = jnp.dot(c_g, st.astype(BF16), preferred_element_type=F32)
            for q in range(R // 2):
                h0 = g * R + 2 * q
                ms, es = [], []
                for h in (h0, h0 + 1):
                    acol = jnp.broadcast_to(acum_b[:, h:h + 1], (L, L))
                    seg = acol - arow_t[h:h + 1, rows]
                    ms.append((cb * jnp.exp2(jnp.where(tril, seg, -jnp.inf))).astype(BF16))
                    es.append(jnp.exp2(acol))
                lhs = jnp.concatenate(ms, axis=1)
                cols = slice(h0 * P, (h0 + 2) * P)
                xp = xs_scr[rows, cols]
                zero = jnp.zeros_like(xp)
                rhs = jnp.concatenate([jnp.where(lo_half, xp, zero), jnp.where(lo_half, zero, xp)], axis=0)
                y_d = jnp.dot(lhs, rhs, preferred_element_type=F32)
                scale = jnp.where(lo_half, es[0], es[1])
                y_scr[rows, cols] = y_d + y_off[:, 2 * q * P:(2 * q + 2) * P] * scale
            gcols = slice(g * gw, (g + 1) * gw)
            xw = (xs_scr[rows, gcols].astype(F32) * dtw_scr[rows, gcols]).astype(BF16)
            contrib = lax.dot_general(b_g, xw, (((0,), (0,)), ((), ())), preferred_element_type=F32)
            state_scr[g] = st * dec_all[:, gcols] + contrib
            if z_tasks:
                z_tasks.pop(0)()
    for task in z_tasks:
        task()

    y = y_scr[...] + dfull_ref[...] * xs_scr[...].astype(F32)
    y = y * _silu(z_scr[...].astype(F32))
    nw = nw_ref[...]
    parts = []
    for g in range(SSM_GROUPS):
        yg = y[:, g * gw:(g + 1) * gw]
        yg = yg * lax.rsqrt(jnp.mean(yg * yg, axis=-1, keepdims=True) + RMS_EPS)
        parts.append((yg * nw[:, g * gw:(g + 1) * gw]).astype(BF16))
    yb = jnp.concatenate(parts, axis=1)
    m = jnp.dot(yb, wo_ref[...], preferred_element_type=F32)
    x1 = _ln(DN_ALPHA * x + m, g1_ref[...], b1_ref[...])
    x1_ref[...] = x1
    x1p_ref[...] = _pack_bf16_pairs(x1)
    first = (pl.program_id(0) == 0) & (s == 0)
    _route(x1, first, wr_ref, br_ref, ri_ref, rg_ref, cnt_ref, cnt_scr)


def mamba_block(x3, w_in, conv_w, conv_b, dt_bias, a_log, d_skip, norm_w, w_out, g1, b1, w_r, b_r):
    bsz, s_len, d = x3.shape
    d_inner = SSM_HEADS * SSM_HEAD_DIM
    gn = SSM_GROUPS * SSM_STATE
    conv_dim = d_inner + 2 * gn
    tm = TOKEN_TILE
    wz = w_in[:, :d_inner].astype(BF16)
    wx = w_in[:, d_inner:d_inner + conv_dim].astype(BF16)
    wdt = jnp.pad(w_in[:, d_inner + conv_dim:], ((0, 0), (0, LANES - SSM_HEADS))).astype(BF16)
    dtb = jnp.pad(dt_bias, (0, LANES - SSM_HEADS)).reshape(1, LANES)
    alog = jnp.pad(a_log, (0, LANES - SSM_HEADS)).reshape(1, LANES)
    dfull = jnp.repeat(d_skip, SSM_HEAD_DIM).reshape(1, d_inner)
    head_of_col = jnp.arange(d_inner, dtype=jnp.int32) // SSM_HEAD_DIM
    expand = (jnp.arange(LANES, dtype=jnp.int32)[:, None] == head_of_col[None, :]).astype(BF16)
    args = (x3, wz, wx, wdt, conv_w, conv_b.reshape(1, conv_dim), dtb, alog, dfull,
            norm_w.reshape(1, d_inner), expand, w_out.astype(BF16), g1.reshape(1, d), b1.reshape(1, d), w_r, b_r)
    tok = lambda w: pl.BlockSpec((None, tm, w), lambda b, s: (b, s, 0))
    in_specs = [tok(d)] + [_const_spec(a.shape) for a in args[1:]]
    ns = s_len // tm
    route = pl.BlockSpec((2 * TOP_K, tm), lambda b, s: (0, b * ns + s))
    out_shape = (jax.ShapeDtypeStruct((bsz, s_len, d), F32),
                 jax.ShapeDtypeStruct((bsz, s_len, d // 2), U32),
                 jax.ShapeDtypeStruct((2 * TOP_K, bsz * s_len), jnp.int32),
                 jax.ShapeDtypeStruct((bsz, s_len, LANES), F32),
                 jax.ShapeDtypeStruct((N_EXPERTS, LANES), F32))
    out_specs = (tok(d), tok(d // 2), route, tok(LANES), _const_spec((N_EXPERTS, LANES)))
    return pl.pallas_call(
        _mamba_kernel,
        out_shape=out_shape,
        grid=(bsz, ns),
        in_specs=in_specs,
        out_specs=out_specs,
        scratch_shapes=[pltpu.VMEM((tm, d_inner), BF16), pltpu.VMEM((tm, gn), BF16), pltpu.VMEM((tm, gn), BF16),
                        pltpu.VMEM((tm, d_inner), BF16),
                        pltpu.VMEM((SSM_GROUPS, SSM_STATE, d_inner // SSM_GROUPS), F32),
                        pltpu.VMEM((tm, d_inner), F32),
                        pltpu.VMEM((tm, d_inner), F32),
                        pltpu.VMEM((N_EXPERTS, LANES), F32)] +
                       [pltpu.VMEM((tm + 8, CONV_COLS), F32) for _ in range(conv_dim // CONV_COLS)],
        compiler_params=pltpu.CompilerParams(
            dimension_semantics=("arbitrary", "arbitrary"), vmem_limit_bytes=VMEM_LIMIT),
        name="mamba",
    )(*args)


def _sc_workers():
    info = plsc.get_sparse_core_info()
    return info.num_cores, info.num_subcores


def sc_scatter_rows(x, dest_km, n_rows):
    t, d = x.shape
    nc, ns = _sc_workers()
    per_w, rem = divmod(t, nc * ns)
    assert rem == 0 and per_w % SC_CHUNK == 0
    mesh = plsc.VectorSubcoreMesh(core_axis_name="core", subcore_axis_name="subcore")

    @functools.partial(
        pl.kernel,
        out_type=jax.ShapeDtypeStruct((n_rows, d), x.dtype),
        mesh=mesh,
        scratch_types=[pltpu.VMEM((SC_CHUNK,), jnp.int32) for _ in range(TOP_K)] +
                      [pltpu.VMEM((SC_CHUNK, d), x.dtype), pltpu.SemaphoreType.DMA],
    )
    def scatter(x_hbm, i_hbm, o_hbm, i0, i1, i2, i3, rows_v, sem):
        wid = lax.axis_index("subcore") * nc + lax.axis_index("core")
        base = wid * per_w
        idx = (i0, i1, i2, i3)

        @pl.loop(0, per_w // SC_CHUNK)
        def _(c):
            off = base + c * SC_CHUNK
            pltpu.sync_copy(x_hbm.at[pl.ds(off, SC_CHUNK)], rows_v)
            for k in range(TOP_K):
                pltpu.sync_copy(i_hbm.at[pl.ds(k * t + off, SC_CHUNK)], idx[k])
            copies = [pltpu.make_async_copy(rows_v, o_hbm.at[idx[k]], sem) for k in range(TOP_K)]
            for cp in copies:
                cp.start()
            for cp in copies:
                cp.wait()

    return scatter(x, dest_km)


def sc_gather_rows(table, idx):
    n = idx.shape[0]
    d = table.shape[1]
    nc, ns = _sc_workers()
    per_w, rem = divmod(n, nc * ns)
    n_chunks = per_w // SC_CHUNK
    assert rem == 0 and per_w % (2 * SC_CHUNK) == 0
    mesh = plsc.VectorSubcoreMesh(core_axis_name="core", subcore_axis_name="subcore")

    @functools.partial(
        pl.kernel,
        out_type=jax.ShapeDtypeStruct((n, d), table.dtype),
        mesh=mesh,
        scratch_types=[
            pltpu.VMEM((SC_CHUNK,), jnp.int32), pltpu.VMEM((SC_CHUNK,), jnp.int32),
            pltpu.VMEM((SC_CHUNK, d), table.dtype), pltpu.VMEM((SC_CHUNK, d), table.dtype),
            pltpu.SemaphoreType.DMA, pltpu.SemaphoreType.DMA,
            pltpu.SemaphoreType.DMA, pltpu.SemaphoreType.DMA,
        ],
    )
    def gather(x_hbm, i_hbm, o_hbm, idx0, idx1, rows0, rows1, sg0, sg1, sw0, sw1):
        wid = lax.axis_index("subcore") * nc + lax.axis_index("core")
        base = wid * per_w
        idx = (idx0, idx1)
        rows = (rows0, rows1)
        sg = (sg0, sg1)
        sw = (sw0, sw1)

        def gather_copy(slot):
            return pltpu.make_async_copy(x_hbm.at[idx[slot]], rows[slot], sg[slot])

        def write_copy(c, slot):
            return pltpu.make_async_copy(rows[slot], o_hbm.at[pl.ds(base + c * SC_CHUNK, SC_CHUNK)], sw[slot])

        pltpu.sync_copy(i_hbm.at[pl.ds(base, SC_CHUNK)], idx0)
        gather_copy(0).start()

        @pl.loop(0, n_chunks, step=2)
        def _(c0):
            for slot in range(2):
                c = c0 + slot
                nxt = 1 - slot

                @pl.when(c + 1 < n_chunks)
                def _():
                    pltpu.sync_copy(i_hbm.at[pl.ds(base + (c + 1) * SC_CHUNK, SC_CHUNK)], idx[nxt])

                    @pl.when(c >= 1)
                    def _():
                        write_copy(c - 1, nxt).wait()

                    gather_copy(nxt).start()

                gather_copy(slot).wait()
                write_copy(c, slot).start()

        write_copy(n_chunks - 2, 0).wait()
        write_copy(n_chunks - 1, 1).wait()

    return gather(table, idx)


def _moe_mlp_kernel(be_ref, nv_ref, nu_ref, x_ref, w1_ref, b1_ref, w2_ref, b2_ref, o_ref, w1b, w2b):
    i = pl.program_id(0)
    e = be_ref[i]
    prev = be_ref[jnp.maximum(i - 1, 0)]

    @pl.when((i == 0) | (e != prev))
    def _():
        w1b[...] = w1_ref[...].astype(BF16)
        w2b[...] = w2_ref[...].astype(BF16)

    @pl.when(i < nu_ref[0])
    def _():
        dff = w2b.shape[0]
        row = lax.broadcasted_iota(jnp.int32, x_ref.shape, 0)
        xw = jnp.where(row < nv_ref[i], x_ref[...], jnp.uint32(0))
        xb = _unpack_bf16_pairs(xw).astype(BF16)
        h = jnp.dot(xb, w1b[...], preferred_element_type=F32) + b1_ref[...]
        g = jnp.minimum(h[:, :dff], SWIGLU_LIMIT)
        lin = jnp.clip(h[:, dff:], -SWIGLU_LIMIT, SWIGLU_LIMIT)
        act = (g * jax.nn.sigmoid(SWIGLU_ALPHA * g) * (lin + 1.0)).astype(BF16)
        y = jnp.dot(act, w2b[...], preferred_element_type=F32) + b2_ref[...]
        o_ref[...] = _pack_bf16_pairs(y)


def moe_mlp(xp, block_expert, n_valid, n_used, layer, w1, b1, w2, b2):
    n_rows, dh = xp.shape
    _, n_e, d, two_dff = w1.shape
    dff = two_dff // 2
    n_blocks = n_rows // MOE_BLOCK
    row = lambda i, be, nv, nu: (jnp.minimum(i, nu[0] - 1), 0)
    wsel = lambda i, be, nv, nu: (layer, be[i], 0, 0)
    grid_spec = pltpu.PrefetchScalarGridSpec(
        num_scalar_prefetch=3,
        grid=(n_blocks,),
        in_specs=[
            pl.BlockSpec((MOE_BLOCK, dh), row),
            pl.BlockSpec((None, None, d, two_dff), wsel),
            pl.BlockSpec((None, None, 1, two_dff), wsel),
            pl.BlockSpec((None, None, dff, d), wsel),
            pl.BlockSpec((None, None, 1, d), wsel),
        ],
        out_specs=pl.BlockSpec((MOE_BLOCK, dh), row),
        scratch_shapes=[pltpu.VMEM((d, two_dff), BF16), pltpu.VMEM((dff, d), BF16)],
    )
    return pl.pallas_call(
        _moe_mlp_kernel,
        out_shape=jax.ShapeDtypeStruct((n_rows, dh), U32),
        grid_spec=grid_spec,
        compiler_params=pltpu.CompilerParams(
            dimension_semantics=("arbitrary",), vmem_limit_bytes=VMEM_LIMIT),
        name="moe_mlp",
    )(block_expert, n_valid, n_used, xp, w1, b1.reshape(b1.shape[0], n_e, 1, two_dff),
      w2, b2.reshape(b2.shape[0], n_e, 1, d))


def _combine_kernel(x1_ref, y0_ref, y1_ref, y2_ref, y3_ref, rg_ref, g2_ref, b2_ref, *rest):
    o_ref = rest[-1]
    rg = rg_ref[...]
    f = rg[:, 0:1] * _unpack_bf16_pairs(y0_ref[...])
    for k, y_ref in ((1, y1_ref), (2, y2_ref), (3, y3_ref)):
        f = f + rg[:, k:k + 1] * _unpack_bf16_pairs(y_ref[...])
    o_ref[...] = _ln(DN_ALPHA * x1_ref[...] + f, g2_ref[...], b2_ref[...])


def combine_block(x1, y4p, rg, g2, b2, part, partial_out):
    t, d = x1.shape
    tm = TOKEN_TILE
    ntp = t // tm // COMBINE_PARTS
    base = part * ntp
    tok = lambda w: pl.BlockSpec((tm, w), lambda i: (base + i, 0))
    y_specs = [pl.BlockSpec((tm, d // 2), functools.partial(lambda i, k: (k * ntp + i, 0), k=k))
               for k in range(TOP_K)]
    in_specs = [tok(d)] + y_specs + [tok(LANES), _const_spec((1, d)), _const_spec((1, d))]
    args = [x1, y4p, y4p, y4p, y4p, rg, g2.reshape(1, d), b2.reshape(1, d)]
    aliases = {}
    if partial_out is not None:
        in_specs.append(pl.BlockSpec(memory_space=pl.ANY))
        args.append(partial_out)
        aliases = {len(args) - 1: 0}
    return pl.pallas_call(
        _combine_kernel,
        out_shape=jax.ShapeDtypeStruct((t, d), F32),
        grid=(ntp,),
        in_specs=in_specs,
        out_specs=tok(d),
        input_output_aliases=aliases,
        compiler_params=pltpu.CompilerParams(
            dimension_semantics=("arbitrary",), vmem_limit_bytes=VMEM_LIMIT),
        name="moe_combine",
    )(*args)


def moe_block(x1, x1p, ri, rg, cnt, layer, w1, b1, w2, b2, g2, b2n):
    t, d = x1.shape
    n_assign = t * TOP_K
    n_blocks = -(-n_assign // MOE_BLOCK) + N_EXPERTS
    n_rows = n_blocks * MOE_BLOCK
    ar = jnp.arange(N_EXPERTS, dtype=jnp.int32)
    counts = cnt[:, 0].astype(jnp.int32)
    padded = (counts + MOE_BLOCK - 1) // MOE_BLOCK * MOE_BLOCK
    pend = jnp.sum(jnp.where(ar[None, :] <= ar[:, None], padded[None, :], 0), axis=1)
    pstart = pend - padded
    top_i = ri[:TOP_K]
    dest = jnp.sum(jnp.where(top_i[:, :, None] == ar, pstart, 0), axis=-1) + ri[TOP_K:]
    dest_km = dest.reshape(-1)
    n_used = (pend[-1] // MOE_BLOCK).reshape(1)
    block_start = jnp.arange(n_blocks, dtype=jnp.int32) * MOE_BLOCK
    block_expert = jnp.minimum(
        jnp.sum((pend[None, :] <= block_start[:, None]).astype(jnp.int32), axis=1), N_EXPERTS - 1)
    last = jnp.sum(jnp.where(jnp.arange(n_blocks) == n_used[0] - 1, block_expert, 0))
    block_expert = jnp.where(jnp.arange(n_blocks) < n_used[0], block_expert, last)
    vend = jnp.sum(jnp.where(block_expert[:, None] == ar, (pstart + counts)[None, :], 0), axis=1)
    n_valid = jnp.clip(vend - block_start, 0, MOE_BLOCK)

    xp = sc_scatter_rows(x1p, dest_km, n_rows)
    yp = moe_mlp(xp, block_expert, n_valid, n_used, layer, w1, b1, w2, b2)
    tp = t // COMBINE_PARTS
    out = None
    for part in range(COMBINE_PARTS):
        y4p = sc_gather_rows(yp, dest[:, part * tp:(part + 1) * tp].reshape(-1))
        out = combine_block(x1, y4p, rg, g2, b2n, part, out)
    return out


def kernel(x, a_w_in, a_b_in, a_ln_g, a_ln_b, a_w_s, a_b_s, a_w_out, a_b_out, b_w_in, b_conv_w, b_conv_b, b_dt_bias, b_a_log, b_d, b_norm_w, b_w_out, moe_w_router, moe_b_router, moe_w1, moe_b1, moe_w2, moe_b2, ln1_g, ln1_b, ln2_g, ln2_b):
    bsz, s_len, d = x.shape
    t = bsz * s_len
    xt = x.reshape(t, d)
    for i in range(DEPTH):
        j = i // 2
        w_r32 = jnp.pad(moe_w_router[i], ((0, 0), (0, LANES - N_EXPERTS)))
        w_r_hi = w_r32.astype(BF16)
        w_r = jnp.stack([w_r_hi, (w_r32 - w_r_hi.astype(F32)).astype(BF16)])
        b_r = moe_b_router[i].reshape(N_EXPERTS, 1)
        if i % 2 == 0:
            x1, x1p, ri, rg, cnt = mixer_a_block(
                xt, a_w_in[j], a_b_in[j], a_ln_g[j], a_ln_b[j], a_w_s[j], a_b_s[j],
                a_w_out[j], a_b_out[j], ln1_g[i], ln1_b[i], w_r, b_r)
        else:
            x3 = xt.reshape(bsz, s_len, d)
            x1, x1p, ri, rg, cnt = mamba_block(
                x3, b_w_in[j], b_conv_w[j], b_conv_b[j], b_dt_bias[j], b_a_log[j], b_d[j], b_norm_w[j],
                b_w_out[j], ln1_g[i], ln1_b[i], w_r, b_r)
            x1 = x1.reshape(t, d)
            x1p = x1p.reshape(t, d // 2)
            rg = rg.reshape(t, LANES)
        xt = moe_block(x1, x1p, ri, rg, cnt, i, moe_w1, moe_b1, moe_w2, moe_b2, ln2_g[i], ln2_b[i])
    return xt.reshape(bsz, s_len, d)
```

```python
import functools
import math

import jax
import jax.numpy as jnp
from jax import lax
from jax.experimental import pallas as pl
from jax.experimental.pallas import tpu as pltpu
from jax.experimental.pallas import tpu_sc as plsc

F32 = jnp.float32
BF16 = jnp.bfloat16
U32 = jnp.uint32

DEPTH = 4
N_EXPERTS = 32
TOP_K = 4
MOE_BLOCK = 512
SWIGLU_LIMIT = 7.0
SWIGLU_ALPHA = 1.702
A_BLOCK = 128
A_GROUPS = 8
CHUNK = 64
SSD_BLOCK = 128
SSM_HEADS = 32
SSM_HEAD_DIM = 64
SSM_GROUPS = 4
SSM_STATE = 128
SSM_CONV = 4
DN_ALPHA = (2 * DEPTH) ** 0.25
LN_EPS = 1e-5
LOG2E = 1.0 / math.log(2.0)
RMS_EPS = 1e-5

LANES = 128
SUBLANES = 8
TOKEN_TILE = 256
VMEM_LIMIT = 56 * 1024 * 1024
SC_CHUNK = 64
CONV_COLS = 256
COMBINE_PARTS = 2


def _ln(x, g, b):
    mu = jnp.mean(x, axis=-1, keepdims=True)
    xc = x - mu
    var = jnp.mean(xc * xc, axis=-1, keepdims=True)
    return xc * lax.rsqrt(var + LN_EPS) * g + b


def _gelu(x):
    return 0.5 * x * (1.0 + lax.erf(x * (1.0 / math.sqrt(2.0))))


def _silu(x):
    h = 0.5 * x
    return h + h * jnp.tanh(h)


def _pack_bf16_pairs(y):
    h = y.shape[1] // 2
    lo = lax.bitcast_convert_type(y[:, :h].astype(BF16).astype(F32), U32) >> 16
    hi = lax.bitcast_convert_type(y[:, h:].astype(BF16).astype(F32), U32) & jnp.uint32(0xFFFF0000)
    return hi | lo


def _unpack_bf16_pairs(w):
    lo = lax.bitcast_convert_type(w << 16, F32)
    hi = lax.bitcast_convert_type(w & jnp.uint32(0xFFFF0000), F32)
    return jnp.concatenate([lo, hi], axis=1)


def _route(x1, first, wr_ref, brc_ref, ri_ref, rg_ref, cnt_ref, cnt_scr):
    @pl.when(first)
    def _():
        cnt_scr[...] = jnp.zeros(cnt_scr.shape, F32)

    x_hi = x1.astype(BF16)
    x_lo = (x1 - x_hi.astype(F32)).astype(BF16)
    logits_tok = (jnp.dot(x_hi, wr_ref[0], preferred_element_type=F32)
                  + (jnp.dot(x_hi, wr_ref[1], preferred_element_type=F32)
                     + jnp.dot(x_lo, wr_ref[0], preferred_element_type=F32)))
    n_e = brc_ref.shape[0]
    logits = logits_tok.T[:n_e, :] + brc_ref[...]
    tm = logits.shape[1]
    sub_f = lax.broadcasted_iota(jnp.int32, logits.shape, 0).astype(F32)
    vals, idxs, hots = [], [], []
    l = logits
    for _ in range(TOP_K):
        m = jnp.max(l, axis=0, keepdims=True)
        i = jnp.min(jnp.where(l == m, sub_f, float(n_e)), axis=0, keepdims=True)
        hot = sub_f == i
        vals.append(m)
        idxs.append(i)
        hots.append(hot.astype(F32))
        l = jnp.where(hot, -jnp.inf, l)
    es = [jnp.exp(v - vals[0]) for v in vals]
    tot = es[0] + es[1] + es[2] + es[3]
    cnt = (hots[0] + hots[1] + hots[2] + hots[3]).astype(BF16)
    r_i = lax.broadcasted_iota(jnp.int32, (tm, tm), 0)
    c_i = lax.broadcasted_iota(jnp.int32, (tm, tm), 1)
    before = jnp.dot(cnt, (r_i < c_i).astype(BF16), preferred_element_type=F32)
    before = before + jnp.concatenate([cnt_scr[...]] * (tm // LANES), axis=1)
    row = lax.broadcasted_iota(jnp.int32, (2 * TOP_K, tm), 0)
    ri = jnp.zeros((2 * TOP_K, tm), F32)
    rg = jnp.zeros((2 * TOP_K, tm), F32)
    for k in range(TOP_K):
        rank = jnp.sum(before * hots[k], axis=0, keepdims=True)
        ri = jnp.where(row == k, idxs[k], ri)
        ri = jnp.where(row == TOP_K + k, rank, ri)
        rg = jnp.where(row == k, es[k] / tot, rg)
    ri_ref[...] = ri.astype(jnp.int32)
    rg_ref[...] = jnp.concatenate([rg, jnp.zeros((LANES - 2 * TOP_K, tm), F32)], axis=0).T
    total = cnt_scr[...] + jnp.dot(cnt, jnp.ones((tm, LANES), BF16), preferred_element_type=F32)
    cnt_scr[...] = total
    cnt_ref[...] = total


def _const_spec(shape):
    nd = len(shape)
    return pl.BlockSpec(shape, lambda *_: (0,) * nd)


def _mixer_a_kernel(x_ref, wu_ref, wv_ref, bu_ref, bv_ref, lg_ref, lb_ref, ws_ref, bst_ref,
                    wo_ref, bo_ref, g1_ref, b1_ref, wr_ref, br_ref,
                    x1_ref, x1p_ref, ri_ref, rg_ref, cnt_ref, v_scr, o_scr, cnt_scr):
    x = x_ref[...]
    xb = x.astype(BF16)
    tm, dff = v_scr.shape
    gd = dff // A_GROUPS
    gcols = [slice(g * gd, (g + 1) * gd) for g in range(A_GROUPS)]
    s1 = jnp.zeros((tm, 1), F32)
    s2 = jnp.zeros((tm, 1), F32)
    shift = None
    for cols in gcols:
        v_g = _gelu(jnp.dot(xb, wv_ref[:, cols], preferred_element_type=F32) + bv_ref[:, cols])
        v_scr[:, cols] = v_g
        if shift is None:
            shift = jnp.mean(v_g, axis=-1, keepdims=True)
        dv = v_g - shift
        s1 = s1 + jnp.sum(dv, axis=-1, keepdims=True)
        s2 = s2 + jnp.sum(dv * dv, axis=-1, keepdims=True)
    dmu = s1 * (1.0 / dff)
    mu = shift + dmu
    rstd = lax.rsqrt(s2 * (1.0 / dff) - dmu * dmu + LN_EPS)
    pi = lax.broadcasted_iota(jnp.int32, (A_BLOCK, A_BLOCK), 0) // CHUNK
    pj = lax.broadcasted_iota(jnp.int32, (A_BLOCK, A_BLOCK), 1) // CHUNK
    mask = pj <= pi
    bst = bst_ref[...]
    for g, cols in enumerate(gcols):
        u_g = _gelu(jnp.dot(xb, wu_ref[:, cols], preferred_element_type=F32) + bu_ref[:, cols])
        vb_g = ((v_scr[:, cols] - mu) * rstd * lg_ref[:, cols] + lb_ref[:, cols]).astype(BF16)
        wm = jnp.where(mask, ws_ref[g], 0.0).astype(BF16)
        for n in range(tm // A_BLOCK):
            rows = slice(n * A_BLOCK, (n + 1) * A_BLOCK)
            sv = jnp.dot(wm, vb_g[rows, :], preferred_element_type=F32) + bst[:, g:g + 1]
            o_scr[rows, cols] = (u_g[rows, :] * sv).astype(BF16)
    m = jnp.dot(o_scr[...], wo_ref[...], preferred_element_type=F32) + bo_ref[...]
    x1 = _ln(DN_ALPHA * x + m, g1_ref[...], b1_ref[...])
    x1_ref[...] = x1
    x1p_ref[...] = _pack_bf16_pairs(x1)
    _route(x1, pl.program_id(0) == 0, wr_ref, br_ref, ri_ref, rg_ref, cnt_ref, cnt_scr)


def mixer_a_block(xt, w_in, b_in, ln_g, ln_b, w_s, b_s, w_out, b_out, g1, b1, w_r, b_r):
    t, d = xt.shape
    dff = w_out.shape[0]
    tm = TOKEN_TILE
    wu = w_in[:, :dff].astype(BF16)
    wv = w_in[:, dff:].astype(BF16)
    bu = b_in[:dff].reshape(1, dff)
    bv = b_in[dff:].reshape(1, dff)
    args = (xt, wu, wv, bu, bv, ln_g.reshape(1, dff), ln_b.reshape(1, dff), w_s, b_s.T,
            w_out.astype(BF16), b_out.reshape(1, d), g1.reshape(1, d), b1.reshape(1, d), w_r, b_r)
    tok = lambda w: pl.BlockSpec((tm, w), lambda i: (i, 0))
    route = pl.BlockSpec((2 * TOP_K, tm), lambda i: (0, i))
    in_specs = [tok(d)] + [_const_spec(a.shape) for a in args[1:]]
    out_shape = (jax.ShapeDtypeStruct((t, d), F32),
                 jax.ShapeDtypeStruct((t, d // 2), U32),
                 jax.ShapeDtypeStruct((2 * TOP_K, t), jnp.int32),
                 jax.ShapeDtypeStruct((t, LANES), F32),
                 jax.ShapeDtypeStruct((N_EXPERTS, LANES), F32))
    out_specs = (tok(d), tok(d // 2), route, tok(LANES), _const_spec((N_EXPERTS, LANES)))
    return pl.pallas_call(
        _mixer_a_kernel,
        out_shape=out_shape,
        grid=(t // tm,),
        in_specs=in_specs,
        out_specs=out_specs,
        scratch_shapes=[pltpu.VMEM((tm, dff), F32), pltpu.VMEM((tm, dff), BF16),
                        pltpu.VMEM((N_EXPERTS, LANES), F32)],
        compiler_params=pltpu.CompilerParams(
            dimension_semantics=("arbitrary",), vmem_limit_bytes=VMEM_LIMIT),
        name="mixer_a",
    )(*args)


def _split3_bf16(q):
    hi = q.astype(BF16)
    r1 = q - hi.astype(F32)
    mid = r1.astype(BF16)
    lo = (r1 - mid.astype(F32)).astype(BF16)
    return hi, mid, lo


def _mamba_kernel(x_ref, wz_ref, wx_ref, wdt_ref, cw_ref, cb_ref, dtb_ref,
                  alog_ref, dfull_ref, nw_ref, exp_ref, wo_ref, g1_ref, b1_ref, wr_ref, br_ref,
                  x1_ref, x1p_ref, ri_ref, rg_ref, cnt_ref,
                  xs_scr, bm_scr, cm_scr, z_scr, state_scr, y_scr, dtw_scr, cnt_scr, *ext_scrs):
    s = pl.program_id(1)
    tm = x_ref.shape[0]
    d_inner = xs_scr.shape[1]
    gn = bm_scr.shape[1]
    L, P, N = SSD_BLOCK, SSM_HEAD_DIM, SSM_STATE
    R = SSM_HEADS // SSM_GROUPS
    gw = R * P
    nblk = tm // L

    @pl.when(s == 0)
    def _():
        state_scr[...] = jnp.zeros(state_scr.shape, F32)
        for ext in ext_scrs:
            ext[...] = jnp.zeros(ext.shape, F32)

    x = x_ref[...]
    xb = x.astype(BF16)
    dt = jnp.dot(xb, wdt_ref[...], preferred_element_type=F32) + dtb_ref[...]
    dt = jnp.maximum(dt, 0.0) + jnp.log1p(jnp.exp(-jnp.abs(dt)))

    cw = cw_ref[...]
    cbias = cb_ref[...]
    w = CONV_COLS
    pad = SUBLANES
    n_col = len(ext_scrs) // 2
    for c, (ext, qext) in enumerate(zip(ext_scrs[:n_col], ext_scrs[n_col:])):
        cols = slice(c * w, (c + 1) * w)
        ext[0:pad, :] = jnp.where(s > 0, ext[tm:tm + pad, :], 0.0)
        ext[pad:pad + tm, :] = jnp.dot(xb, wx_ref[:, cols], preferred_element_type=F32)
        x0 = ext[pad:pad + tm, :]
        xm1 = ext[pad - 1:pad - 1 + tm, :]
        q = cw[1:2, cols] * x0 + cw[0:1, cols] * xm1
        qext[0:pad, :] = jnp.where(s > 0, qext[tm:tm + pad, :], 0.0)
        qext[pad:pad + tm, :] = q
        acc = cbias[:, cols] + (cw[3:4, cols] * x0 + cw[2:3, cols] * xm1) + qext[pad - 2:pad - 2 + tm, :]
        act = _silu(acc).astype(BF16)
        lo = c * w
        if lo < d_inner:
            xs_scr[:, lo:lo + w] = act
        elif lo < d_inner + gn:
            bm_scr[:, lo - d_inner:lo - d_inner + w] = act
        else:
            cm_scr[:, lo - d_inner - gn:lo - d_inner - gn + w] = act

    def z_task(c):
        cols = slice(c * w, (c + 1) * w)
        z_scr[:, cols] = jnp.dot(xb, wz_ref[:, cols], preferred_element_type=F32).astype(BF16)

    z_tasks = [functools.partial(z_task, c) for c in range(d_inner // w)]

    a_row = -jnp.exp(alog_ref[...])
    adt = dt * a_row
    r_i = lax.broadcasted_iota(jnp.int32, (tm, tm), 0)
    c_i = lax.broadcasted_iota(jnp.int32, (tm, tm), 1)
    btril = ((r_i >= c_i) & (r_i // L == c_i // L)).astype(F32)
    acum = jnp.dot(btril, adt, preferred_element_type=F32,
                   precision=lax.Precision.HIGHEST)
    row_blk = lax.broadcasted_iota(jnp.int32, (tm, LANES), 0) // L
    alast = [acum[(b + 1) * L - 1:(b + 1) * L, :] for b in range(nblk)]
    alast_rows = alast[0]
    for b in range(1, nblk):
        alast_rows = jnp.where(row_blk == b, alast[b], alast_rows)
    dtw = dt * jnp.exp(alast_rows - acum)
    expand = exp_ref[...]
    dtw_scr[...] = jnp.dot(dtw.astype(BF16), expand, preferred_element_type=F32)
    acum2 = acum * LOG2E
    arow_t = acum2.T - jnp.log(dt.T) * LOG2E
    tril = lax.broadcasted_iota(jnp.int32, (L, L), 0) >= lax.broadcasted_iota(jnp.int32, (L, L), 1)
    lo_half = lax.broadcasted_iota(jnp.int32, (L, 2 * P), 1) < P

    for b in range(nblk):
        rows = slice(b * L, (b + 1) * L)
        acum_b = acum2[rows, :]
        elast = jnp.broadcast_to(jnp.exp(alast[b]), (SUBLANES, LANES))
        dec_all = sum(jnp.dot(piece, expand, preferred_element_type=F32)
                      for piece in _split3_bf16(elast))[0:1, :]
        for g in range(SSM_GROUPS):
            c_g = cm_scr[rows, g * N:(g + 1) * N]
            b_g = bm_scr[rows, g * N:(g + 1) * N]
            cb = lax.dot_general(c_g, b_g, (((1,), (1,)), ((), ())), preferred_element_type=F32)
            st = state_scr[g]
            y_off = jnp.dot(c_g, st.astype(BF16), preferred_element_type=F32)
            for q in range(R // 2):
                h0 = g * R + 2 * q
                ms, es = [], []
                for h in (h0, h0 + 1):
                    acol = jnp.broadcast_to(acum_b[:, h:h + 1], (L, L))
                    seg = acol - arow_t[h:h + 1, rows]
                    ms.append((cb * jnp.exp2(jnp.where(tril, seg, -jnp.inf))).astype(BF16))
                    es.append(jnp.exp2(acol))
                lhs = jnp.concatenate(ms, axis=1)
                cols = slice(h0 * P, (h0 + 2) * P)
                xp = xs_scr[rows, cols]
                zero = jnp.zeros_like(xp)
                rhs = jnp.concatenate([jnp.where(lo_half, xp, zero), jnp.where(lo_half, zero, xp)], axis=0)
                y_d = jnp.dot(lhs, rhs, preferred_element_type=F32)
                scale = jnp.where(lo_half, es[0], es[1])
                y_scr[rows, cols] = y_d + y_off[:, 2 * q * P:(2 * q + 2) * P] * scale
            gcols = slice(g * gw, (g + 1) * gw)
            xw = (xs_scr[rows, gcols].astype(F32) * dtw_scr[rows, gcols]).astype(BF16)
            contrib = lax.dot_general(b_g, xw, (((0,), (0,)), ((), ())), preferred_element_type=F32)
            state_scr[g] = st * dec_all[:, gcols] + contrib
            if z_tasks:
                z_tasks.pop(0)()
    for task in z_tasks:
        task()

    y = y_scr[...] + dfull_ref[...] * xs_scr[...].astype(F32)
    y = y * _silu(z_scr[...].astype(F32))
    nw = nw_ref[...]
    parts = []
    for g in range(SSM_GROUPS):
        yg = y[:, g * gw:(g + 1) * gw]
        yg = yg * lax.rsqrt(jnp.mean(yg * yg, axis=-1, keepdims=True) + RMS_EPS)
        parts.append((yg * nw[:, g * gw:(g + 1) * gw]).astype(BF16))
    yb = jnp.concatenate(parts, axis=1)
    m = jnp.dot(yb, wo_ref[...], preferred_element_type=F32)
    x1 = _ln(DN_ALPHA * x + m, g1_ref[...], b1_ref[...])
    x1_ref[...] = x1
    x1p_ref[...] = _pack_bf16_pairs(x1)
    first = (pl.program_id(0) == 0) & (s == 0)
    _route(x1, first, wr_ref, br_ref, ri_ref, rg_ref, cnt_ref, cnt_scr)


def mamba_block(x3, w_in, conv_w, conv_b, dt_bias, a_log, d_skip, norm_w, w_out, g1, b1, w_r, b_r):
    bsz, s_len, d = x3.shape
    d_inner = SSM_HEADS * SSM_HEAD_DIM
    gn = SSM_GROUPS * SSM_STATE
    conv_dim = d_inner + 2 * gn
    tm = TOKEN_TILE
    wz = w_in[:, :d_inner].astype(BF16)
    wx = w_in[:, d_inner:d_inner + conv_dim].astype(BF16)
    wdt = jnp.pad(w_in[:, d_inner + conv_dim:], ((0, 0), (0, LANES - SSM_HEADS))).astype(BF16)
    dtb = jnp.pad(dt_bias, (0, LANES - SSM_HEADS)).reshape(1, LANES)
    alog = jnp.pad(a_log, (0, LANES - SSM_HEADS)).reshape(1, LANES)
    dfull = jnp.repeat(d_skip, SSM_HEAD_DIM).reshape(1, d_inner)
    head_of_col = jnp.arange(d_inner, dtype=jnp.int32) // SSM_HEAD_DIM
    expand = (jnp.arange(LANES, dtype=jnp.int32)[:, None] == head_of_col[None, :]).astype(BF16)
    args = (x3, wz, wx, wdt, conv_w, conv_b.reshape(1, conv_dim), dtb, alog, dfull,
            norm_w.reshape(1, d_inner), expand, w_out.astype(BF16), g1.reshape(1, d), b1.reshape(1, d), w_r, b_r)
    tok = lambda w: pl.BlockSpec((None, tm, w), lambda b, s: (b, s, 0))
    in_specs = [tok(d)] + [_const_spec(a.shape) for a in args[1:]]
    ns = s_len // tm
    route = pl.BlockSpec((2 * TOP_K, tm), lambda b, s: (0, b * ns + s))
    out_shape = (jax.ShapeDtypeStruct((bsz, s_len, d), F32),
                 jax.ShapeDtypeStruct((bsz, s_len, d // 2), U32),
                 jax.ShapeDtypeStruct((2 * TOP_K, bsz * s_len), jnp.int32),
                 jax.ShapeDtypeStruct((bsz, s_len, LANES), F32),
                 jax.ShapeDtypeStruct((N_EXPERTS, LANES), F32))
    out_specs = (tok(d), tok(d // 2), route, tok(LANES), _const_spec((N_EXPERTS, LANES)))
    return pl.pallas_call(
        _mamba_kernel,
        out_shape=out_shape,
        grid=(bsz, ns),
        in_specs=in_specs,
        out_specs=out_specs,
        scratch_shapes=[pltpu.VMEM((tm, d_inner), BF16), pltpu.VMEM((tm, gn), BF16), pltpu.VMEM((tm, gn), BF16),
                        pltpu.VMEM((tm, d_inner), BF16),
                        pltpu.VMEM((SSM_GROUPS, SSM_STATE, d_inner // SSM_GROUPS), F32),
                        pltpu.VMEM((tm, d_inner), F32),
                        pltpu.VMEM((tm, d_inner), F32),
                        pltpu.VMEM((N_EXPERTS, LANES), F32)] +
                       [pltpu.VMEM((tm + SUBLANES, CONV_COLS), F32) for _ in range(2 * conv_dim // CONV_COLS)],
        compiler_params=pltpu.CompilerParams(
            dimension_semantics=("arbitrary", "arbitrary"), vmem_limit_bytes=VMEM_LIMIT),
        name="mamba",
    )(*args)


def _sc_workers():
    info = plsc.get_sparse_core_info()
    return info.num_cores, info.num_subcores


def sc_scatter_rows(x, dest_km, n_rows):
    t, d = x.shape
    nc, ns = _sc_workers()
    per_w, rem = divmod(t, nc * ns)
    assert rem == 0 and per_w % SC_CHUNK == 0
    mesh = plsc.VectorSubcoreMesh(core_axis_name="core", subcore_axis_name="subcore")

    @functools.partial(
        pl.kernel,
        out_type=jax.ShapeDtypeStruct((n_rows, d), x.dtype),
        mesh=mesh,
        scratch_types=[pltpu.VMEM((SC_CHUNK,), jnp.int32) for _ in range(TOP_K)] +
                      [pltpu.VMEM((SC_CHUNK, d), x.dtype), pltpu.SemaphoreType.DMA],
    )
    def scatter(x_hbm, i_hbm, o_hbm, i0, i1, i2, i3, rows_v, sem):
        wid = lax.axis_index("subcore") * nc + lax.axis_index("core")
        base = wid * per_w
        idx = (i0, i1, i2, i3)

        @pl.loop(0, per_w // SC_CHUNK)
        def _(c):
            off = base + c * SC_CHUNK
            pltpu.sync_copy(x_hbm.at[pl.ds(off, SC_CHUNK)], rows_v)
            for k in range(TOP_K):
                pltpu.sync_copy(i_hbm.at[pl.ds(k * t + off, SC_CHUNK)], idx[k])
            copies = [pltpu.make_async_copy(rows_v, o_hbm.at[idx[k]], sem) for k in range(TOP_K)]
            for cp in copies:
                cp.start()
            for cp in copies:
                cp.wait()

    return scatter(x, dest_km)


def sc_gather_rows(table, idx):
    n = idx.shape[0]
    d = table.shape[1]
    nc, ns = _sc_workers()
    per_w, rem = divmod(n, nc * ns)
    n_chunks = per_w // SC_CHUNK
    assert rem == 0 and per_w % (2 * SC_CHUNK) == 0
    mesh = plsc.VectorSubcoreMesh(core_axis_name="core", subcore_axis_name="subcore")

    @functools.partial(
        pl.kernel,
        out_type=jax.ShapeDtypeStruct((n, d), table.dtype),
        mesh=mesh,
        scratch_types=[
            pltpu.VMEM((SC_CHUNK,), jnp.int32), pltpu.VMEM((SC_CHUNK,), jnp.int32),
            pltpu.VMEM((SC_CHUNK, d), table.dtype), pltpu.VMEM((SC_CHUNK, d), table.dtype),
            pltpu.SemaphoreType.DMA, pltpu.SemaphoreType.DMA,
            pltpu.SemaphoreType.DMA, pltpu.SemaphoreType.DMA,
        ],
    )
    def gather(x_hbm, i_hbm, o_hbm, idx0, idx1, rows0, rows1, sg0, sg1, sw0, sw1):
        wid = lax.axis_index("subcore") * nc + lax.axis_index("core")
        base = wid * per_w
        idx = (idx0, idx1)
        rows = (rows0, rows1)
        sg = (sg0, sg1)
        sw = (sw0, sw1)

        def gather_copy(slot):
            return pltpu.make_async_copy(x_hbm.at[idx[slot]], rows[slot], sg[slot])

        def write_copy(c, slot):
            return pltpu.make_async_copy(rows[slot], o_hbm.at[pl.ds(base + c * SC_CHUNK, SC_CHUNK)], sw[slot])

        pltpu.sync_copy(i_hbm.at[pl.ds(base, SC_CHUNK)], idx0)
        gather_copy(0).start()

        @pl.loop(0, n_chunks, step=2)
        def _(c0):
            for slot in range(2):
                c = c0 + slot
                nxt = 1 - slot

                @pl.when(c + 1 < n_chunks)
                def _():
                    pltpu.sync_copy(i_hbm.at[pl.ds(base + (c + 1) * SC_CHUNK, SC_CHUNK)], idx[nxt])

                    @pl.when(c >= 1)
                    def _():
                        write_copy(c - 1, nxt).wait()

                    gather_copy(nxt).start()

                gather_copy(slot).wait()
                write_copy(c, slot).start()

        write_copy(n_chunks - 2, 0).wait()
        write_copy(n_chunks - 1, 1).wait()

    return gather(table, idx)


def _moe_mlp_kernel(be_ref, nv_ref, nu_ref, x_ref, w1_ref, b1_ref, w2_ref, b2_ref, o_ref, w1b, w2b):
    i = pl.program_id(0)
    e = be_ref[i]
    prev = be_ref[jnp.maximum(i - 1, 0)]

    @pl.when((i == 0) | (e != prev))
    def _():
        w1b[...] = w1_ref[...].astype(BF16)
        w2b[...] = w2_ref[...].astype(BF16)

    @pl.when(i < nu_ref[0])
    def _():
        dff = w2b.shape[0]
        row = lax.broadcasted_iota(jnp.int32, x_ref.shape, 0)
        xw = jnp.where(row < nv_ref[i], x_ref[...], jnp.uint32(0))
        xb = _unpack_bf16_pairs(xw).astype(BF16)
        h = jnp.dot(xb, w1b[...], preferred_element_type=F32) + b1_ref[...]
        g = jnp.minimum(h[:, :dff], SWIGLU_LIMIT)
        lin = jnp.clip(h[:, dff:], -SWIGLU_LIMIT, SWIGLU_LIMIT)
        act = (g * jax.nn.sigmoid(SWIGLU_ALPHA * g) * (lin + 1.0)).astype(BF16)
        y = jnp.dot(act, w2b[...], preferred_element_type=F32) + b2_ref[...]
        o_ref[...] = _pack_bf16_pairs(y)


def moe_mlp(xp, block_expert, n_valid, n_used, layer, w1, b1, w2, b2):
    n_rows, dh = xp.shape
    _, n_e, d, two_dff = w1.shape
    dff = two_dff // 2
    n_blocks = n_rows // MOE_BLOCK
    row = lambda i, be, nv, nu: (jnp.minimum(i, nu[0] - 1), 0)
    wsel = lambda i, be, nv, nu: (layer, be[i], 0, 0)
    grid_spec = pltpu.PrefetchScalarGridSpec(
        num_scalar_prefetch=3,
        grid=(n_blocks,),
        in_specs=[
            pl.BlockSpec((MOE_BLOCK, dh), row),
            pl.BlockSpec((None, None, d, two_dff), wsel),
            pl.BlockSpec((None, None, 1, two_dff), wsel),
            pl.BlockSpec((None, None, dff, d), wsel),
            pl.BlockSpec((None, None, 1, d), wsel),
        ],
        out_specs=pl.BlockSpec((MOE_BLOCK, dh), row),
        scratch_shapes=[pltpu.VMEM((d, two_dff), BF16), pltpu.VMEM((dff, d), BF16)],
    )
    return pl.pallas_call(
        _moe_mlp_kernel,
        out_shape=jax.ShapeDtypeStruct((n_rows, dh), U32),
        grid_spec=grid_spec,
        compiler_params=pltpu.CompilerParams(
            dimension_semantics=("arbitrary",), vmem_limit_bytes=VMEM_LIMIT),
        name="moe_mlp",
    )(block_expert, n_valid, n_used, xp, w1, b1.reshape(b1.shape[0], n_e, 1, two_dff),
      w2, b2.reshape(b2.shape[0], n_e, 1, d))


def _combine_kernel(x1_ref, y0_ref, y1_ref, y2_ref, y3_ref, rg_ref, g2_ref, b2_ref, *rest):
    o_ref = rest[-1]
    rg = rg_ref[...]
    f = rg[:, 0:1] * _unpack_bf16_pairs(y0_ref[...])
    for k, y_ref in ((1, y1_ref), (2, y2_ref), (3, y3_ref)):
        f = f + rg[:, k:k + 1] * _unpack_bf16_pairs(y_ref[...])
    o_ref[...] = _ln(DN_ALPHA * x1_ref[...] + f, g2_ref[...], b2_ref[...])


def combine_block(x1, y4p, rg, g2, b2, part, partial_out):
    t, d = x1.shape
    tm = TOKEN_TILE
    ntp = t // tm // COMBINE_PARTS
    base = part * ntp
    tok = lambda w: pl.BlockSpec((tm, w), lambda i: (base + i, 0))
    y_specs = [pl.BlockSpec((tm, d // 2), functools.partial(lambda i, k: (k * ntp + i, 0), k=k))
               for k in range(TOP_K)]
    in_specs = [tok(d)] + y_specs + [tok(LANES), _const_spec((1, d)), _const_spec((1, d))]
    args = [x1, y4p, y4p, y4p, y4p, rg, g2.reshape(1, d), b2.reshape(1, d)]
    aliases = {}
    if partial_out is not None:
        in_specs.append(pl.BlockSpec(memory_space=pl.ANY))
        args.append(partial_out)
        aliases = {len(args) - 1: 0}
    return pl.pallas_call(
        _combine_kernel,
        out_shape=jax.ShapeDtypeStruct((t, d), F32),
        grid=(ntp,),
        in_specs=in_specs,
        out_specs=tok(d),
        input_output_aliases=aliases,
        compiler_params=pltpu.CompilerParams(
            dimension_semantics=("arbitrary",), vmem_limit_bytes=VMEM_LIMIT),
        name="moe_combine",
    )(*args)


def moe_block(x1, x1p, ri, rg, cnt, layer, w1, b1, w2, b2, g2, b2n):
    t, d = x1.shape
    n_assign = t * TOP_K
    n_blocks = -(-n_assign // MOE_BLOCK) + N_EXPERTS
    n_rows = n_blocks * MOE_BLOCK
    ar = jnp.arange(N_EXPERTS, dtype=jnp.int32)
    counts = cnt[:, 0].astype(jnp.int32)
    padded = (counts + MOE_BLOCK - 1) // MOE_BLOCK * MOE_BLOCK
    pend = jnp.sum(jnp.where(ar[None, :] <= ar[:, None], padded[None, :], 0), axis=1)
    pstart = pend - padded
    top_i = ri[:TOP_K]
    dest = jnp.sum(jnp.where(top_i[:, :, None] == ar, pstart, 0), axis=-1) + ri[TOP_K:]
    dest_km = dest.reshape(-1)
    n_used = (pend[-1] // MOE_BLOCK).reshape(1)
    block_start = jnp.arange(n_blocks, dtype=jnp.int32) * MOE_BLOCK
    block_expert = jnp.minimum(
        jnp.sum((pend[None, :] <= block_start[:, None]).astype(jnp.int32), axis=1), N_EXPERTS - 1)
    last = jnp.sum(jnp.where(jnp.arange(n_blocks) == n_used[0] - 1, block_expert, 0))
    block_expert = jnp.where(jnp.arange(n_blocks) < n_used[0], block_expert, last)
    vend = jnp.sum(jnp.where(block_expert[:, None] == ar, (pstart + counts)[None, :], 0), axis=1)
    n_valid = jnp.clip(vend - block_start, 0, MOE_BLOCK)

    xp = sc_scatter_rows(x1p, dest_km, n_rows)
    yp = moe_mlp(xp, block_expert, n_valid, n_used, layer, w1, b1, w2, b2)
    tp = t // COMBINE_PARTS
    out = None
    for part in range(COMBINE_PARTS):
        y4p = sc_gather_rows(yp, dest[:, part * tp:(part + 1) * tp].reshape(-1))
        out = combine_block(x1, y4p, rg, g2, b2n, part, out)
    return out


def kernel(x, a_w_in, a_b_in, a_ln_g, a_ln_b, a_w_s, a_b_s, a_w_out, a_b_out, b_w_in, b_conv_w, b_conv_b, b_dt_bias, b_a_log, b_d, b_norm_w, b_w_out, moe_w_router, moe_b_router, moe_w1, moe_b1, moe_w2, moe_b2, ln1_g, ln1_b, ln2_g, ln2_b):
    bsz, s_len, d = x.shape
    t = bsz * s_len
    xt = x.reshape(t, d)
    for i in range(DEPTH):
        j = i // 2
        w_r32 = jnp.pad(moe_w_router[i], ((0, 0), (0, LANES - N_EXPERTS)))
        w_r_hi = w_r32.astype(BF16)
        w_r = jnp.stack([w_r_hi, (w_r32 - w_r_hi.astype(F32)).astype(BF16)])
        b_r = moe_b_router[i].reshape(N_EXPERTS, 1)
        if i % 2 == 0:
            x1, x1p, ri, rg, cnt = mixer_a_block(
                xt, a_w_in[j], a_b_in[j], a_ln_g[j], a_ln_b[j], a_w_s[j], a_b_s[j],
                a_w_out[j], a_b_out[j], ln1_g[i], ln1_b[i], w_r, b_r)
        else:
            x3 = xt.reshape(bsz, s_len, d)
            x1, x1p, ri, rg, cnt = mamba_block(
                x3, b_w_in[j], b_conv_w[j], b_conv_b[j], b_dt_bias[j], b_a_log[j], b_d[j], b_norm_w[j],
                b_w_out[j], ln1_g[i], ln1_b[i], w_r, b_r)
            x1 = x1.reshape(t, d)
            x1p = x1p.reshape(t, d // 2)
            rg = rg.reshape(t, LANES)
        xt = moe_block(x1, x1p, ri, rg, cnt, i, moe_w1, moe_b1, moe_w2, moe_b2, ln2_g[i], ln2_b[i])
    return xt.reshape(bsz, s_len, d)
```

```python
import functools
import math

import jax
import jax.numpy as jnp
from jax import lax
from jax.experimental import pallas as pl
from jax.experimental.pallas import tpu as pltpu
from jax.experimental.pallas import tpu_sc as plsc

F32 = jnp.float32
BF16 = jnp.bfloat16
U32 = jnp.uint32

DEPTH = 4
N_EXPERTS = 32
TOP_K = 4
MOE_BLOCK = 512
SWIGLU_LIMIT = 7.0
SWIGLU_ALPHA = 1.702
A_BLOCK = 128
A_GROUPS = 8
CHUNK = 64
SSD_BLOCK = 128
SSM_HEADS = 32
SSM_HEAD_DIM = 64
SSM_GROUPS = 4
SSM_STATE = 128
SSM_CONV = 4
DN_ALPHA = (2 * DEPTH) ** 0.25
LN_EPS = 1e-5
LOG2E = 1.0 / math.log(2.0)
RMS_EPS = 1e-5

LANES = 128
SUBLANES = 8
TOKEN_TILE = 256
VMEM_LIMIT = 56 * 1024 * 1024
SC_CHUNK = 64
CONV_COLS = 256
COMBINE_PARTS = 2


def _ln(x, g, b):
    mu = jnp.mean(x, axis=-1, keepdims=True)
    xc = x - mu
    var = jnp.mean(xc * xc, axis=-1, keepdims=True)
    return xc * lax.rsqrt(var + LN_EPS) * g + b


def _gelu(x):
    return 0.5 * x * (1.0 + lax.erf(x * (1.0 / math.sqrt(2.0))))


def _silu(x):
    h = 0.5 * x
    return h + h * jnp.tanh(h)


def _pack_bf16_pairs(y):
    h = y.shape[1] // 2
    lo = lax.bitcast_convert_type(y[:, :h].astype(BF16).astype(F32), U32) >> 16
    hi = lax.bitcast_convert_type(y[:, h:].astype(BF16).astype(F32), U32) & jnp.uint32(0xFFFF0000)
    return hi | lo


def _unpack_bf16_pairs(w):
    lo = lax.bitcast_convert_type(w << 16, F32)
    hi = lax.bitcast_convert_type(w & jnp.uint32(0xFFFF0000), F32)
    return jnp.concatenate([lo, hi], axis=1)


def _route(x1, first, wr_ref, brc_ref, ri_ref, rg_ref, cnt_ref, cnt_scr):
    @pl.when(first)
    def _():
        cnt_scr[...] = jnp.zeros(cnt_scr.shape, F32)

    x_hi = x1.astype(BF16)
    x_lo = (x1 - x_hi.astype(F32)).astype(BF16)
    logits_tok = (jnp.dot(x_hi, wr_ref[0], preferred_element_type=F32)
                  + (jnp.dot(x_hi, wr_ref[1], preferred_element_type=F32)
                     + jnp.dot(x_lo, wr_ref[0], preferred_element_type=F32)))
    n_e = brc_ref.shape[0]
    logits = logits_tok.T[:n_e, :] + brc_ref[...]
    tm = logits.shape[1]
    sub_f = lax.broadcasted_iota(jnp.int32, logits.shape, 0).astype(F32)
    vals, idxs, hots = [], [], []
    l = logits
    for _ in range(TOP_K):
        m = jnp.max(l, axis=0, keepdims=True)
        i = jnp.min(jnp.where(l == m, sub_f, float(n_e)), axis=0, keepdims=True)
        hot = sub_f == i
        vals.append(m)
        idxs.append(i)
        hots.append(hot.astype(F32))
        l = jnp.where(hot, -jnp.inf, l)
    es = [jnp.exp(v - vals[0]) for v in vals]
    tot = es[0] + es[1] + es[2] + es[3]
    cnt = (hots[0] + hots[1] + hots[2] + hots[3]).astype(BF16)
    r_i = lax.broadcasted_iota(jnp.int32, (tm, tm), 0)
    c_i = lax.broadcasted_iota(jnp.int32, (tm, tm), 1)
    before = jnp.dot(cnt, (r_i < c_i).astype(BF16), preferred_element_type=F32)
    before = before + jnp.concatenate([cnt_scr[...]] * (tm // LANES), axis=1)
    row = lax.broadcasted_iota(jnp.int32, (2 * TOP_K, tm), 0)
    ri = jnp.zeros((2 * TOP_K, tm), F32)
    rg = jnp.zeros((2 * TOP_K, tm), F32)
    for k in range(TOP_K):
        rank = jnp.sum(before * hots[k], axis=0, keepdims=True)
        ri = jnp.where(row == k, idxs[k], ri)
        ri = jnp.where(row == TOP_K + k, rank, ri)
        rg = jnp.where(row == k, es[k] / tot, rg)
    ri_ref[...] = ri.astype(jnp.int32)
    rg_ref[...] = jnp.concatenate([rg, jnp.zeros((LANES - 2 * TOP_K, tm), F32)], axis=0).T
    total = cnt_scr[...] + jnp.dot(cnt, jnp.ones((tm, LANES), BF16), preferred_element_type=F32)
    cnt_scr[...] = total
    cnt_ref[...] = total


def _const_spec(shape):
    nd = len(shape)
    return pl.BlockSpec(shape, lambda *_: (0,) * nd)


def _mixer_a_kernel(x_ref, wu_ref, wv_ref, bu_ref, bv_ref, lg_ref, lb_ref, ws_ref, bst_ref,
                    wo_ref, bo_ref, g1_ref, b1_ref, wr_ref, br_ref,
                    x1_ref, x1p_ref, ri_ref, rg_ref, cnt_ref, v_scr, o_scr, cnt_scr):
    x = x_ref[...]
    xb = x.astype(BF16)
    tm, dff = v_scr.shape
    gd = dff // A_GROUPS
    gcols = [slice(g * gd, (g + 1) * gd) for g in range(A_GROUPS)]
    s1 = jnp.zeros((tm, 1), F32)
    s2 = jnp.zeros((tm, 1), F32)
    shift = None
    for cols in gcols:
        v_g = _gelu(jnp.dot(xb, wv_ref[:, cols], preferred_element_type=F32) + bv_ref[:, cols])
        v_scr[:, cols] = v_g
        if shift is None:
            shift = jnp.mean(v_g, axis=-1, keepdims=True)
        dv = v_g - shift
        s1 = s1 + jnp.sum(dv, axis=-1, keepdims=True)
        s2 = s2 + jnp.sum(dv * dv, axis=-1, keepdims=True)
    dmu = s1 * (1.0 / dff)
    mu = shift + dmu
    rstd = lax.rsqrt(s2 * (1.0 / dff) - dmu * dmu + LN_EPS)
    pi = lax.broadcasted_iota(jnp.int32, (A_BLOCK, A_BLOCK), 0) // CHUNK
    pj = lax.broadcasted_iota(jnp.int32, (A_BLOCK, A_BLOCK), 1) // CHUNK
    mask = pj <= pi
    bst = bst_ref[...]
    for g, cols in enumerate(gcols):
        u_g = _gelu(jnp.dot(xb, wu_ref[:, cols], preferred_element_type=F32) + bu_ref[:, cols])
        vb_g = ((v_scr[:, cols] - mu) * rstd * lg_ref[:, cols] + lb_ref[:, cols]).astype(BF16)
        wm = jnp.where(mask, ws_ref[g], 0.0).astype(BF16)
        for n in range(tm // A_BLOCK):
            rows = slice(n * A_BLOCK, (n + 1) * A_BLOCK)
            sv = jnp.dot(wm, vb_g[rows, :], preferred_element_type=F32) + bst[:, g:g + 1]
            o_scr[rows, cols] = (u_g[rows, :] * sv).astype(BF16)
    m = jnp.dot(o_scr[...], wo_ref[...], preferred_element_type=F32) + bo_ref[...]
    x1 = _ln(DN_ALPHA * x + m, g1_ref[...], b1_ref[...])
    x1_ref[...] = x1
    x1p_ref[...] = _pack_bf16_pairs(x1)
    _route(x1, pl.program_id(0) == 0, wr_ref, br_ref, ri_ref, rg_ref, cnt_ref, cnt_scr)


def mixer_a_block(xt, w_in, b_in, ln_g, ln_b, w_s, b_s, w_out, b_out, g1, b1, w_r, b_r):
    t, d = xt.shape
    dff = w_out.shape[0]
    tm = TOKEN_TILE
    wu = w_in[:, :dff].astype(BF16)
    wv = w_in[:, dff:].astype(BF16)
    bu = b_in[:dff].reshape(1, dff)
    bv = b_in[dff:].reshape(1, dff)
    args = (xt, wu, wv, bu, bv, ln_g.reshape(1, dff), ln_b.reshape(1, dff), w_s, b_s.T,
            w_out.astype(BF16), b_out.reshape(1, d), g1.reshape(1, d), b1.reshape(1, d), w_r, b_r)
    tok = lambda w: pl.BlockSpec((tm, w), lambda i: (i, 0))
    route = pl.BlockSpec((2 * TOP_K, tm), lambda i: (0, i))
    in_specs = [tok(d)] + [_const_spec(a.shape) for a in args[1:]]
    out_shape = (jax.ShapeDtypeStruct((t, d), F32),
                 jax.ShapeDtypeStruct((t, d // 2), U32),
                 jax.ShapeDtypeStruct((2 * TOP_K, t), jnp.int32),
                 jax.ShapeDtypeStruct((t, LANES), F32),
                 jax.ShapeDtypeStruct((N_EXPERTS, LANES), F32))
    out_specs = (tok(d), tok(d // 2), route, tok(LANES), _const_spec((N_EXPERTS, LANES)))
    return pl.pallas_call(
        _mixer_a_kernel,
        out_shape=out_shape,
        grid=(t // tm,),
        in_specs=in_specs,
        out_specs=out_specs,
        scratch_shapes=[pltpu.VMEM((tm, dff), F32), pltpu.VMEM((tm, dff), BF16),
                        pltpu.VMEM((N_EXPERTS, LANES), F32)],
        compiler_params=pltpu.CompilerParams(
            dimension_semantics=("arbitrary",), vmem_limit_bytes=VMEM_LIMIT),
        name="mixer_a",
    )(*args)


def _split3_bf16(q):
    hi = q.astype(BF16)
    r1 = q - hi.astype(F32)
    mid = r1.astype(BF16)
    lo = (r1 - mid.astype(F32)).astype(BF16)
    return hi, mid, lo


def _mamba_kernel(x_ref, wz_ref, wx_ref, wdt_ref, cw_ref, cb_ref, dtb_ref,
                  alog_ref, dfull_ref, nw_ref, exp_ref, wo_ref, g1_ref, b1_ref, wr_ref, br_ref,
                  x1_ref, x1p_ref, ri_ref, rg_ref, cnt_ref,
                  xs_scr, bm_scr, cm_scr, z_scr, state_scr, y_scr, dtw_scr, cnt_scr, *ext_scrs):
    s = pl.program_id(1)
    tm = x_ref.shape[0]
    d_inner = xs_scr.shape[1]
    gn = bm_scr.shape[1]
    L, P, N = SSD_BLOCK, SSM_HEAD_DIM, SSM_STATE
    R = SSM_HEADS // SSM_GROUPS
    gw = R * P
    nblk = tm // L

    @pl.when(s == 0)
    def _():
        state_scr[...] = jnp.zeros(state_scr.shape, F32)
        for ext in ext_scrs:
            ext[...] = jnp.zeros(ext.shape, F32)

    x = x_ref[...]
    xb = x.astype(BF16)
    dt = jnp.dot(xb, wdt_ref[...], preferred_element_type=F32) + dtb_ref[...]
    dt = jnp.maximum(dt, 0.0) + jnp.log1p(jnp.exp(-jnp.abs(dt)))

    cw = cw_ref[...]
    cbias = cb_ref[...]
    w = CONV_COLS
    pad = SUBLANES
    n_col = len(ext_scrs) // 2
    for c, (ext, qext) in enumerate(zip(ext_scrs[:n_col], ext_scrs[n_col:])):
        cols = slice(c * w, (c + 1) * w)
        ext[0:pad, :] = jnp.where(s > 0, ext[tm:tm + pad, :], 0.0)
        ext[pad:pad + tm, :] = jnp.dot(xb, wx_ref[:, cols], preferred_element_type=F32)
        x0 = ext[pad:pad + tm, :]
        xm1 = ext[pad - 1:pad - 1 + tm, :]
        q = cw[1:2, cols] * x0 + cw[0:1, cols] * xm1
        qext[0:pad, :] = jnp.where(s > 0, qext[tm:tm + pad, :], 0.0)
        qext[pad:pad + tm, :] = q
        acc = cbias[:, cols] + (cw[3:4, cols] * x0 + cw[2:3, cols] * xm1) + qext[pad - 2:pad - 2 + tm, :]
        act = _silu(acc).astype(BF16)
        lo = c * w
        if lo < d_inner:
            xs_scr[:, lo:lo + w] = act
        elif lo < d_inner + gn:
            bm_scr[:, lo - d_inner:lo - d_inner + w] = act
        else:
            cm_scr[:, lo - d_inner - gn:lo - d_inner - gn + w] = act

    def z_task(c):
        cols = slice(c * w, (c + 1) * w)
        z_scr[:, cols] = jnp.dot(xb, wz_ref[:, cols], preferred_element_type=F32).astype(BF16)

    z_tasks = [functools.partial(z_task, c) for c in range(d_inner // w)]

    a_row = -jnp.exp(alog_ref[...])
    adt = dt * a_row
    r_i = lax.broadcasted_iota(jnp.int32, (tm, tm), 0)
    c_i = lax.broadcasted_iota(jnp.int32, (tm, tm), 1)
    btril = ((r_i >= c_i) & (r_i // L == c_i // L)).astype(F32)
    acum = jnp.dot(btril, adt, preferred_element_type=F32,
                   precision=lax.Precision.HIGHEST)
    row_blk = lax.broadcasted_iota(jnp.int32, (tm, LANES), 0) // L
    alast = [acum[(b + 1) * L - 1:(b + 1) * L, :] for b in range(nblk)]
    alast_rows = alast[0]
    for b in range(1, nblk):
        alast_rows = jnp.where(row_blk == b, alast[b], alast_rows)
    dtw = dt * jnp.exp(alast_rows - acum)
    expand = exp_ref[...]
    dtw_scr[...] = jnp.dot(dtw.astype(BF16), expand, preferred_element_type=F32)
    acum2 = acum * LOG2E
    arow_t = acum2.T - jnp.log(dt.T) * LOG2E
    tril = lax.broadcasted_iota(jnp.int32, (L, L), 0) >= lax.broadcasted_iota(jnp.int32, (L, L), 1)
    lo_half = lax.broadcasted_iota(jnp.int32, (L, 2 * P), 1) < P

    for b in range(nblk):
        rows = slice(b * L, (b + 1) * L)
        acum_b = acum2[rows, :]
        elast = jnp.broadcast_to(jnp.exp(alast[b]), (SUBLANES, LANES))
        dec_all = sum(jnp.dot(piece, expand, preferred_element_type=F32)
                      for piece in _split3_bf16(elast))[0:1, :]
        for g in range(SSM_GROUPS):
            c_g = cm_scr[rows, g * N:(g + 1) * N]
            b_g = bm_scr[rows, g * N:(g + 1) * N]
            cb = lax.dot_general(c_g, b_g, (((1,), (1,)), ((), ())), preferred_element_type=F32)
            st = state_scr[g]
            y_off = jnp.dot(c_g, st.astype(BF16), preferred_element_type=F32)
            for q in range(R // 2):
                h0 = g * R + 2 * q
                ms, es = [], []
                for h in (h0, h0 + 1):
                    acol = jnp.broadcast_to(acum_b[:, h:h + 1], (L, L))
                    seg = acol - arow_t[h:h + 1, rows]
                    ms.append((cb * jnp.exp2(jnp.where(tril, seg, -jnp.inf))).astype(BF16))
                    es.append(jnp.exp2(acol))
                lhs = jnp.concatenate(ms, axis=1)
                cols = slice(h0 * P, (h0 + 2) * P)
                xp = xs_scr[rows, cols]
                zero = jnp.zeros_like(xp)
                rhs = jnp.concatenate([jnp.where(lo_half, xp, zero), jnp.where(lo_half, zero, xp)], axis=0)
                y_d = jnp.dot(lhs, rhs, preferred_element_type=F32)
                scale = jnp.where(lo_half, es[0], es[1])
                y_scr[rows, cols] = y_d + y_off[:, 2 * q * P:(2 * q + 2) * P] * scale
            gcols = slice(g * gw, (g + 1) * gw)
            xw = (xs_scr[rows, gcols].astype(F32) * dtw_scr[rows, gcols]).astype(BF16)
            contrib = lax.dot_general(b_g, xw, (((0,), (0,)), ((), ())), preferred_element_type=F32)
            state_scr[g] = st * dec_all[:, gcols] + contrib
            if z_tasks:
                z_tasks.pop(0)()
    for task in z_tasks:
        task()

    y = y_scr[...] + dfull_ref[...] * xs_scr[...].astype(F32)
    y = y * _silu(z_scr[...].astype(F32))
    nw = nw_ref[...]
    parts = []
    for g in range(SSM_GROUPS):
        yg = y[:, g * gw:(g + 1) * gw]
        yg = yg * lax.rsqrt(jnp.mean(yg * yg, axis=-1, keepdims=True) + RMS_EPS)
        parts.append((yg * nw[:, g * gw:(g + 1) * gw]).astype(BF16))
    yb = jnp.concatenate(parts, axis=1)
    m = jnp.dot(yb, wo_ref[...], preferred_element_type=F32)
    x1 = _ln(DN_ALPHA * x + m, g1_ref[...], b1_ref[...])
    x1_ref[...] = x1
    x1p_ref[...] = _pack_bf16_pairs(x1)
    first = (pl.program_id(0) == 0) & (s == 0)
    _route(x1, first, wr_ref, br_ref, ri_ref, rg_ref, cnt_ref, cnt_scr)


def mamba_block(x3, w_in, conv_w, conv_b, dt_bias, a_log, d_skip, norm_w, w_out, g1, b1, w_r, b_r):
    bsz, s_len, d = x3.shape
    d_inner = SSM_HEADS * SSM_HEAD_DIM
    gn = SSM_GROUPS * SSM_STATE
    conv_dim = d_inner + 2 * gn
    tm = TOKEN_TILE
    wz = w_in[:, :d_inner].astype(BF16)
    wx = w_in[:, d_inner:d_inner + conv_dim].astype(BF16)
    wdt = jnp.pad(w_in[:, d_inner + conv_dim:], ((0, 0), (0, LANES - SSM_HEADS))).astype(BF16)
    dtb = jnp.pad(dt_bias, (0, LANES - SSM_HEADS)).reshape(1, LANES)
    alog = jnp.pad(a_log, (0, LANES - SSM_HEADS)).reshape(1, LANES)
    dfull = jnp.repeat(d_skip, SSM_HEAD_DIM).reshape(1, d_inner)
    head_of_col = jnp.arange(d_inner, dtype=jnp.int32) // SSM_HEAD_DIM
    expand = (jnp.arange(LANES, dtype=jnp.int32)[:, None] == head_of_col[None, :]).astype(BF16)
    args = (x3, wz, wx, wdt, conv_w, conv_b.reshape(1, conv_dim), dtb, alog, dfull,
            norm_w.reshape(1, d_inner), expand, w_out.astype(BF16), g1.reshape(1, d), b1.reshape(1, d), w_r, b_r)
    tok = lambda w: pl.BlockSpec((None, tm, w), lambda b, s: (b, s, 0))
    in_specs = [tok(d)] + [_const_spec(a.shape) for a in args[1:]]
    ns = s_len // tm
    route = pl.BlockSpec((2 * TOP_K, tm), lambda b, s: (0, b * ns + s))
    out_shape = (jax.ShapeDtypeStruct((bsz, s_len, d), F32),
                 jax.ShapeDtypeStruct((bsz, s_len, d // 2), U32),
                 jax.ShapeDtypeStruct((2 * TOP_K, bsz * s_len), jnp.int32),
                 jax.ShapeDtypeStruct((bsz, s_len, LANES), F32),
                 jax.ShapeDtypeStruct((N_EXPERTS, LANES), F32))
    out_specs = (tok(d), tok(d // 2), route, tok(LANES), _const_spec((N_EXPERTS, LANES)))
    return pl.pallas_call(
        _mamba_kernel,
        out_shape=out_shape,
        grid=(bsz, ns),
        in_specs=in_specs,
        out_specs=out_specs,
        scratch_shapes=[pltpu.VMEM((tm, d_inner), BF16), pltpu.VMEM((tm, gn), BF16), pltpu.VMEM((tm, gn), BF16),
                        pltpu.VMEM((tm, d_inner), BF16),
                        pltpu.VMEM((SSM_GROUPS, SSM_STATE, d_inner // SSM_GROUPS), F32),
                        pltpu.VMEM((tm, d_inner), F32),
                        pltpu.VMEM((tm, d_inner), F32),
                        pltpu.VMEM((N_EXPERTS, LANES), F32)] +
                       [pltpu.VMEM((tm + SUBLANES, CONV_COLS), F32) for _ in range(2 * conv_dim // CONV_COLS)],
        compiler_params=pltpu.CompilerParams(
            dimension_semantics=("arbitrary", "arbitrary"), vmem_limit_bytes=VMEM_LIMIT),
        name="mamba",
    )(*args)


def _sc_workers():
    info = plsc.get_sparse_core_info()
    return info.num_cores, info.num_subcores


def sc_scatter_rows(x, dest_km, n_rows):
    t, d = x.shape
    nc, ns = _sc_workers()
    per_w, rem = divmod(t, nc * ns)
    assert rem == 0 and per_w % SC_CHUNK == 0
    mesh = plsc.VectorSubcoreMesh(core_axis_name="core", subcore_axis_name="subcore")

    @functools.partial(
        pl.kernel,
        out_type=jax.ShapeDtypeStruct((n_rows, d), x.dtype),
        mesh=mesh,
        scratch_types=[pltpu.VMEM((SC_CHUNK,), jnp.int32) for _ in range(TOP_K)] +
                      [pltpu.VMEM((SC_CHUNK, d), x.dtype), pltpu.SemaphoreType.DMA],
    )
    def scatter(x_hbm, i_hbm, o_hbm, i0, i1, i2, i3, rows_v, sem):
        wid = lax.axis_index("subcore") * nc + lax.axis_index("core")
        base = wid * per_w
        idx = (i0, i1, i2, i3)

        @pl.loop(0, per_w // SC_CHUNK)
        def _(c):
            off = base + c * SC_CHUNK
            pltpu.sync_copy(x_hbm.at[pl.ds(off, SC_CHUNK)], rows_v)
            for k in range(TOP_K):
                pltpu.sync_copy(i_hbm.at[pl.ds(k * t + off, SC_CHUNK)], idx[k])
            copies = [pltpu.make_async_copy(rows_v, o_hbm.at[idx[k]], sem) for k in range(TOP_K)]
            for cp in copies:
                cp.start()
            for cp in copies:
                cp.wait()

    return scatter(x, dest_km)


def sc_gather_rows(table, idx):
    n = idx.shape[0]
    d = table.shape[1]
    nc, ns = _sc_workers()
    per_w, rem = divmod(n, nc * ns)
    n_chunks = per_w // SC_CHUNK
    assert rem == 0 and per_w % (2 * SC_CHUNK) == 0
    mesh = plsc.VectorSubcoreMesh(core_axis_name="core", subcore_axis_name="subcore")

    @functools.partial(
        pl.kernel,
        out_type=jax.ShapeDtypeStruct((n, d), table.dtype),
        mesh=mesh,
        scratch_types=[
            pltpu.VMEM((SC_CHUNK,), jnp.int32), pltpu.VMEM((SC_CHUNK,), jnp.int32),
            pltpu.VMEM((SC_CHUNK, d), table.dtype), pltpu.VMEM((SC_CHUNK, d), table.dtype),
            pltpu.SemaphoreType.DMA, pltpu.SemaphoreType.DMA,
            pltpu.SemaphoreType.DMA, pltpu.SemaphoreType.DMA,
        ],
    )
    def gather(x_hbm, i_hbm, o_hbm, idx0, idx1, rows0, rows1, sg0, sg1, sw0, sw1):
        wid = lax.axis_index("subcore") * nc + lax.axis_index("core")
        base = wid * per_w
        idx = (idx0, idx1)
        rows = (rows0, rows1)
        sg = (sg0, sg1)
        sw = (sw0, sw1)

        def gather_copy(slot):
            return pltpu.make_async_copy(x_hbm.at[idx[slot]], rows[slot], sg[slot])

        def write_copy(c, slot):
            return pltpu.make_async_copy(rows[slot], o_hbm.at[pl.ds(base + c * SC_CHUNK, SC_CHUNK)], sw[slot])

        pltpu.sync_copy(i_hbm.at[pl.ds(base, SC_CHUNK)], idx0)
        gather_copy(0).start()

        @pl.loop(0, n_chunks, step=2)
        def _(c0):
            for slot in range(2):
                c = c0 + slot
                nxt = 1 - slot

                @pl.when(c + 1 < n_chunks)
                def _():
                    pltpu.sync_copy(i_hbm.at[pl.ds(base + (c + 1) * SC_CHUNK, SC_CHUNK)], idx[nxt])

                    @pl.when(c >= 1)
                    def _():
                        write_copy(c - 1, nxt).wait()

                    gather_copy(nxt).start()

                gather_copy(slot).wait()
                write_copy(c, slot).start()

        write_copy(n_chunks - 2, 0).wait()
        write_copy(n_chunks - 1, 1).wait()

    return gather(table, idx)


def _moe_mlp_kernel(be_ref, nv_ref, nu_ref, x_ref, w1_ref, b1_ref, w2_ref, b2_ref, o_ref, w1b, w2b):
    i = pl.program_id(0)
    e = be_ref[i]
    prev = be_ref[jnp.maximum(i - 1, 0)]

    @pl.when((i == 0) | (e != prev))
    def _():
        w1b[...] = w1_ref[...].astype(BF16)
        w2b[...] = w2_ref[...].astype(BF16)

    nv = nv_ref[i]
    half = MOE_BLOCK // 2

    def mlp(n_rows):
        dff = w2b.shape[0]
        row = lax.broadcasted_iota(jnp.int32, (n_rows, x_ref.shape[1]), 0)
        xw = jnp.where(row < nv, x_ref[0:n_rows, :], jnp.uint32(0))
        xb = _unpack_bf16_pairs(xw).astype(BF16)
        h = jnp.dot(xb, w1b[...], preferred_element_type=F32) + b1_ref[...]
        g = jnp.minimum(h[:, :dff], SWIGLU_LIMIT)
        lin = jnp.clip(h[:, dff:], -SWIGLU_LIMIT, SWIGLU_LIMIT)
        act = (g * jax.nn.sigmoid(SWIGLU_ALPHA * g) * (lin + 1.0)).astype(BF16)
        y = jnp.dot(act, w2b[...], preferred_element_type=F32) + b2_ref[...]
        o_ref[0:n_rows, :] = _pack_bf16_pairs(y)

    pl.when(nv > half)(functools.partial(mlp, MOE_BLOCK))
    pl.when((nv > 0) & (nv <= half))(functools.partial(mlp, half))


def moe_mlp(xp, block_expert, n_valid, n_used, layer, w1, b1, w2, b2):
    n_rows, dh = xp.shape
    _, n_e, d, two_dff = w1.shape
    dff = two_dff // 2
    n_blocks = n_rows // MOE_BLOCK
    row = lambda i, be, nv, nu: (jnp.minimum(i, nu[0] - 1), 0)
    wsel = lambda i, be, nv, nu: (layer, be[i], 0, 0)
    grid_spec = pltpu.PrefetchScalarGridSpec(
        num_scalar_prefetch=3,
        grid=(n_blocks,),
        in_specs=[
            pl.BlockSpec((MOE_BLOCK, dh), row),
            pl.BlockSpec((None, None, d, two_dff), wsel),
            pl.BlockSpec((None, None, 1, two_dff), wsel),
            pl.BlockSpec((None, None, dff, d), wsel),
            pl.BlockSpec((None, None, 1, d), wsel),
        ],
        out_specs=pl.BlockSpec((MOE_BLOCK, dh), row),
        scratch_shapes=[pltpu.VMEM((d, two_dff), BF16), pltpu.VMEM((dff, d), BF16)],
    )
    return pl.pallas_call(
        _moe_mlp_kernel,
        out_shape=jax.ShapeDtypeStruct((n_rows, dh), U32),
        grid_spec=grid_spec,
        compiler_params=pltpu.CompilerParams(
            dimension_semantics=("arbitrary",), vmem_limit_bytes=VMEM_LIMIT),
        name="moe_mlp",
    )(block_expert, n_valid, n_used, xp, w1, b1.reshape(b1.shape[0], n_e, 1, two_dff),
      w2, b2.reshape(b2.shape[0], n_e, 1, d))


def _combine_kernel(x1_ref, y0_ref, y1_ref, y2_ref, y3_ref, rg_ref, g2_ref, b2_ref, *rest):
    o_ref = rest[-1]
    rg = rg_ref[...]
    f = rg[:, 0:1] * _unpack_bf16_pairs(y0_ref[...])
    for k, y_ref in ((1, y1_ref), (2, y2_ref), (3, y3_ref)):
        f = f + rg[:, k:k + 1] * _unpack_bf16_pairs(y_ref[...])
    o_ref[...] = _ln(DN_ALPHA * x1_ref[...] + f, g2_ref[...], b2_ref[...])


def combine_block(x1, y4p, rg, g2, b2, part, partial_out):
    t, d = x1.shape
    tm = TOKEN_TILE
    ntp = t // tm // COMBINE_PARTS
    base = part * ntp
    tok = lambda w: pl.BlockSpec((tm, w), lambda i: (base + i, 0))
    y_specs = [pl.BlockSpec((tm, d // 2), functools.partial(lambda i, k: (k * ntp + i, 0), k=k))
               for k in range(TOP_K)]
    in_specs = [tok(d)] + y_specs + [tok(LANES), _const_spec((1, d)), _const_spec((1, d))]
    args = [x1, y4p, y4p, y4p, y4p, rg, g2.reshape(1, d), b2.reshape(1, d)]
    aliases = {}
    if partial_out is not None:
        in_specs.append(pl.BlockSpec(memory_space=pl.ANY))
        args.append(partial_out)
        aliases = {len(args) - 1: 0}
    return pl.pallas_call(
        _combine_kernel,
        out_shape=jax.ShapeDtypeStruct((t, d), F32),
        grid=(ntp,),
        in_specs=in_specs,
        out_specs=tok(d),
        input_output_aliases=aliases,
        compiler_params=pltpu.CompilerParams(
            dimension_semantics=("arbitrary",), vmem_limit_bytes=VMEM_LIMIT),
        name="moe_combine",
    )(*args)


def moe_block(x1, x1p, ri, rg, cnt, layer, w1, b1, w2, b2, g2, b2n):
    t, d = x1.shape
    n_assign = t * TOP_K
    n_blocks = -(-n_assign // MOE_BLOCK) + N_EXPERTS
    n_rows = n_blocks * MOE_BLOCK
    ar = jnp.arange(N_EXPERTS, dtype=jnp.int32)
    counts = cnt[:, 0].astype(jnp.int32)
    padded = (counts + MOE_BLOCK - 1) // MOE_BLOCK * MOE_BLOCK
    pend = jnp.sum(jnp.where(ar[None, :] <= ar[:, None], padded[None, :], 0), axis=1)
    pstart = pend - padded
    top_i = ri[:TOP_K]
    dest = jnp.sum(jnp.where(top_i[:, :, None] == ar, pstart, 0), axis=-1) + ri[TOP_K:]
    dest_km = dest.reshape(-1)
    n_used = (pend[-1] // MOE_BLOCK).reshape(1)
    block_start = jnp.arange(n_blocks, dtype=jnp.int32) * MOE_BLOCK
    block_expert = jnp.minimum(
        jnp.sum((pend[None, :] <= block_start[:, None]).astype(jnp.int32), axis=1), N_EXPERTS - 1)
    last = jnp.sum(jnp.where(jnp.arange(n_blocks) == n_used[0] - 1, block_expert, 0))
    block_expert = jnp.where(jnp.arange(n_blocks) < n_used[0], block_expert, last)
    vend = jnp.sum(jnp.where(block_expert[:, None] == ar, (pstart + counts)[None, :], 0), axis=1)
    n_valid = jnp.clip(vend - block_start, 0, MOE_BLOCK)

    xp = sc_scatter_rows(x1p, dest_km, n_rows)
    yp = moe_mlp(xp, block_expert, n_valid, n_used, layer, w1, b1, w2, b2)
    tp = t // COMBINE_PARTS
    out = None
    for part in range(COMBINE_PARTS):
        y4p = sc_gather_rows(yp, dest[:, part * tp:(part + 1) * tp].reshape(-1))
        out = combine_block(x1, y4p, rg, g2, b2n, part, out)
    return out


def kernel(x, a_w_in, a_b_in, a_ln_g, a_ln_b, a_w_s, a_b_s, a_w_out, a_b_out, b_w_in, b_conv_w, b_conv_b, b_dt_bias, b_a_log, b_d, b_norm_w, b_w_out, moe_w_router, moe_b_router, moe_w1, moe_b1, moe_w2, moe_b2, ln1_g, ln1_b, ln2_g, ln2_b):
    bsz, s_len, d = x.shape
    t = bsz * s_len
    xt = x.reshape(t, d)
    for i in range(DEPTH):
        j = i // 2
        w_r32 = jnp.pad(moe_w_router[i], ((0, 0), (0, LANES - N_EXPERTS)))
        w_r_hi = w_r32.astype(BF16)
        w_r = jnp.stack([w_r_hi, (w_r32 - w_r_hi.astype(F32)).astype(BF16)])
        b_r = moe_b_router[i].reshape(N_EXPERTS, 1)
        if i % 2 == 0:
            x1, x1p, ri, rg, cnt = mixer_a_block(
                xt, a_w_in[j], a_b_in[j], a_ln_g[j], a_ln_b[j], a_w_s[j], a_b_s[j],
                a_w_out[j], a_b_out[j], ln1_g[i], ln1_b[i], w_r, b_r)
        else:
            x3 = xt.reshape(bsz, s_len, d)
            x1, x1p, ri, rg, cnt = mamba_block(
                x3, b_w_in[j], b_conv_w[j], b_conv_b[j], b_dt_bias[j], b_a_log[j], b_d[j], b_norm_w[j],
                b_w_out[j], ln1_g[i], ln1_b[i], w_r, b_r)
            x1 = x1.reshape(t, d)
            x1p = x1p.reshape(t, d // 2)
            rg = rg.reshape(t, LANES)
        xt = moe_block(x1, x1p, ri, rg, cnt, i, moe_w1, moe_b1, moe_w2, moe_b2, ln2_g[i], ln2_b[i])
    return xt.reshape(bsz, s_len, d)
```

```python
import functools
import math

import jax
import jax.numpy as jnp
from jax import lax
from jax.experimental import pallas as pl
from jax.experimental.pallas import tpu as pltpu
from jax.experimental.pallas import tpu_sc as plsc

F32 = jnp.float32
BF16 = jnp.bfloat16
U32 = jnp.uint32

DEPTH = 4
N_EXPERTS = 32
TOP_K = 4
MOE_BLOCK = 512
SWIGLU_LIMIT = 7.0
SWIGLU_ALPHA = 1.702
A_BLOCK = 128
A_GROUPS = 8
CHUNK = 64
SSD_BLOCK = 128
SSM_HEADS = 32
SSM_HEAD_DIM = 64
SSM_GROUPS = 4
SSM_STATE = 128
SSM_CONV = 4
DN_ALPHA = (2 * DEPTH) ** 0.25
LN_EPS = 1e-5
LOG2E = 1.0 / math.log(2.0)
RMS_EPS = 1e-5

LANES = 128
SUBLANES = 8
TOKEN_TILE = 256
VMEM_LIMIT = 56 * 1024 * 1024
SC_CHUNK = 64
CONV_COLS = 256
COMBINE_PARTS = 4


def _ln(x, g, b):
    mu = jnp.mean(x, axis=-1, keepdims=True)
    xc = x - mu
    var = jnp.mean(xc * xc, axis=-1, keepdims=True)
    return xc * lax.rsqrt(var + LN_EPS) * g + b


def _gelu(x):
    return 0.5 * x * (1.0 + lax.erf(x * (1.0 / math.sqrt(2.0))))


def _silu(x):
    h = 0.5 * x
    return h + h * jnp.tanh(h)


def _pack_bf16_pairs(y):
    h = y.shape[1] // 2
    lo = lax.bitcast_convert_type(y[:, :h].astype(BF16).astype(F32), U32) >> 16
    hi = lax.bitcast_convert_type(y[:, h:].astype(BF16).astype(F32), U32) & jnp.uint32(0xFFFF0000)
    return hi | lo


def _unpack_bf16_pairs(w):
    lo = lax.bitcast_convert_type(w << 16, F32)
    hi = lax.bitcast_convert_type(w & jnp.uint32(0xFFFF0000), F32)
    return jnp.concatenate([lo, hi], axis=1)


def _route(x1, first, wr_ref, brc_ref, ri_ref, rg_ref, cnt_ref, cnt_scr):
    @pl.when(first)
    def _():
        cnt_scr[...] = jnp.zeros(cnt_scr.shape, F32)

    x_hi = x1.astype(BF16)
    x_lo = (x1 - x_hi.astype(F32)).astype(BF16)
    logits_tok = (jnp.dot(x_hi, wr_ref[0], preferred_element_type=F32)
                  + (jnp.dot(x_hi, wr_ref[1], preferred_element_type=F32)
                     + jnp.dot(x_lo, wr_ref[0], preferred_element_type=F32)))
    n_e = brc_ref.shape[0]
    logits = logits_tok.T[:n_e, :] + brc_ref[...]
    tm = logits.shape[1]
    sub_f = lax.broadcasted_iota(jnp.int32, logits.shape, 0).astype(F32)
    vals, idxs, hots = [], [], []
    l = logits
    for _ in range(TOP_K):
        m = jnp.max(l, axis=0, keepdims=True)
        i = jnp.min(jnp.where(l == m, sub_f, float(n_e)), axis=0, keepdims=True)
        hot = sub_f == i
        vals.append(m)
        idxs.append(i)
        hots.append(hot.astype(F32))
        l = jnp.where(hot, -jnp.inf, l)
    es = [jnp.exp(v - vals[0]) for v in vals]
    tot = es[0] + es[1] + es[2] + es[3]
    cnt = (hots[0] + hots[1] + hots[2] + hots[3]).astype(BF16)
    r_i = lax.broadcasted_iota(jnp.int32, (tm, tm), 0)
    c_i = lax.broadcasted_iota(jnp.int32, (tm, tm), 1)
    before = jnp.dot(cnt, (r_i < c_i).astype(BF16), preferred_element_type=F32)
    before = before + jnp.concatenate([cnt_scr[...]] * (tm // LANES), axis=1)
    row = lax.broadcasted_iota(jnp.int32, (2 * TOP_K, tm), 0)
    ri = jnp.zeros((2 * TOP_K, tm), F32)
    rg = jnp.zeros((2 * TOP_K, tm), F32)
    for k in range(TOP_K):
        rank = jnp.sum(before * hots[k], axis=0, keepdims=True)
        ri = jnp.where(row == k, idxs[k], ri)
        ri = jnp.where(row == TOP_K + k, rank, ri)
        rg = jnp.where(row == k, es[k] / tot, rg)
    ri_ref[...] = ri.astype(jnp.int32)
    rg_ref[...] = jnp.concatenate([rg, jnp.zeros((LANES - 2 * TOP_K, tm), F32)], axis=0).T
    total = cnt_scr[...] + jnp.dot(cnt, jnp.ones((tm, LANES), BF16), preferred_element_type=F32)
    cnt_scr[...] = total
    cnt_ref[...] = total


def _const_spec(shape):
    nd = len(shape)
    return pl.BlockSpec(shape, lambda *_: (0,) * nd)


def _mixer_a_kernel(x_ref, wu_ref, wv_ref, bu_ref, bv_ref, lg_ref, lb_ref, ws_ref, bst_ref,
                    wo_ref, bo_ref, g1_ref, b1_ref, wr_ref, br_ref,
                    x1_ref, x1p_ref, ri_ref, rg_ref, cnt_ref, v_scr, o_scr, cnt_scr):
    x = x_ref[...]
    xb = x.astype(BF16)
    tm, dff = v_scr.shape
    gd = dff // A_GROUPS
    gcols = [slice(g * gd, (g + 1) * gd) for g in range(A_GROUPS)]
    s1 = jnp.zeros((tm, 1), F32)
    s2 = jnp.zeros((tm, 1), F32)
    shift = None
    for cols in gcols:
        v_g = _gelu(jnp.dot(xb, wv_ref[:, cols], preferred_element_type=F32) + bv_ref[:, cols])
        v_scr[:, cols] = v_g
        if shift is None:
            shift = jnp.mean(v_g, axis=-1, keepdims=True)
        dv = v_g - shift
        s1 = s1 + jnp.sum(dv, axis=-1, keepdims=True)
        s2 = s2 + jnp.sum(dv * dv, axis=-1, keepdims=True)
    dmu = s1 * (1.0 / dff)
    mu = shift + dmu
    rstd = lax.rsqrt(s2 * (1.0 / dff) - dmu * dmu + LN_EPS)
    pi = lax.broadcasted_iota(jnp.int32, (A_BLOCK, A_BLOCK), 0) // CHUNK
    pj = lax.broadcasted_iota(jnp.int32, (A_BLOCK, A_BLOCK), 1) // CHUNK
    mask = pj <= pi
    bst = bst_ref[...]
    for g, cols in enumerate(gcols):
        u_g = _gelu(jnp.dot(xb, wu_ref[:, cols], preferred_element_type=F32) + bu_ref[:, cols])
        vb_g = ((v_scr[:, cols] - mu) * rstd * lg_ref[:, cols] + lb_ref[:, cols]).astype(BF16)
        wm = jnp.where(mask, ws_ref[g], 0.0).astype(BF16)
        for n in range(tm // A_BLOCK):
            rows = slice(n * A_BLOCK, (n + 1) * A_BLOCK)
            sv = jnp.dot(wm, vb_g[rows, :], preferred_element_type=F32) + bst[:, g:g + 1]
            o_scr[rows, cols] = (u_g[rows, :] * sv).astype(BF16)
    m = jnp.dot(o_scr[...], wo_ref[...], preferred_element_type=F32) + bo_ref[...]
    x1 = _ln(DN_ALPHA * x + m, g1_ref[...], b1_ref[...])
    x1_ref[...] = x1
    x1p_ref[...] = _pack_bf16_pairs(x1)
    _route(x1, pl.program_id(0) == 0, wr_ref, br_ref, ri_ref, rg_ref, cnt_ref, cnt_scr)


def mixer_a_block(xt, w_in, b_in, ln_g, ln_b, w_s, b_s, w_out, b_out, g1, b1, w_r, b_r):
    t, d = xt.shape
    dff = w_out.shape[0]
    tm = TOKEN_TILE
    wu = w_in[:, :dff].astype(BF16)
    wv = w_in[:, dff:].astype(BF16)
    bu = b_in[:dff].reshape(1, dff)
    bv = b_in[dff:].reshape(1, dff)
    args = (xt, wu, wv, bu, bv, ln_g.reshape(1, dff), ln_b.reshape(1, dff), w_s, b_s.T,
            w_out.astype(BF16), b_out.reshape(1, d), g1.reshape(1, d), b1.reshape(1, d), w_r, b_r)
    tok = lambda w: pl.BlockSpec((tm, w), lambda i: (i, 0))
    route = pl.BlockSpec((2 * TOP_K, tm), lambda i: (0, i))
    in_specs = [tok(d)] + [_const_spec(a.shape) for a in args[1:]]
    out_shape = (jax.ShapeDtypeStruct((t, d), F32),
                 jax.ShapeDtypeStruct((t, d // 2), U32),
                 jax.ShapeDtypeStruct((2 * TOP_K, t), jnp.int32),
                 jax.ShapeDtypeStruct((t, LANES), F32),
                 jax.ShapeDtypeStruct((N_EXPERTS, LANES), F32))
    out_specs = (tok(d), tok(d // 2), route, tok(LANES), _const_spec((N_EXPERTS, LANES)))
    return pl.pallas_call(
        _mixer_a_kernel,
        out_shape=out_shape,
        grid=(t // tm,),
        in_specs=in_specs,
        out_specs=out_specs,
        scratch_shapes=[pltpu.VMEM((tm, dff), F32), pltpu.VMEM((tm, dff), BF16),
                        pltpu.VMEM((N_EXPERTS, LANES), F32)],
        compiler_params=pltpu.CompilerParams(
            dimension_semantics=("arbitrary",), vmem_limit_bytes=VMEM_LIMIT),
        name="mixer_a",
    )(*args)


def _split3_bf16(q):
    hi = q.astype(BF16)
    r1 = q - hi.astype(F32)
    mid = r1.astype(BF16)
    lo = (r1 - mid.astype(F32)).astype(BF16)
    return hi, mid, lo


def _mamba_kernel(x_ref, wz_ref, wx_ref, wdt_ref, cw_ref, cb_ref, dtb_ref,
                  alog_ref, dfull_ref, nw_ref, exp_ref, wo_ref, g1_ref, b1_ref, wr_ref, br_ref,
                  x1_ref, x1p_ref, ri_ref, rg_ref, cnt_ref,
                  xs_scr, bm_scr, cm_scr, z_scr, state_scr, y_scr, dtw_scr, cnt_scr, *ext_scrs):
    s = pl.program_id(1)
    tm = x_ref.shape[0]
    d_inner = xs_scr.shape[1]
    gn = bm_scr.shape[1]
    L, P, N = SSD_BLOCK, SSM_HEAD_DIM, SSM_STATE
    R = SSM_HEADS // SSM_GROUPS
    gw = R * P
    nblk = tm // L

    @pl.when(s == 0)
    def _():
        state_scr[...] = jnp.zeros(state_scr.shape, F32)
        for ext in ext_scrs:
            ext[...] = jnp.zeros(ext.shape, F32)

    x = x_ref[...]
    xb = x.astype(BF16)
    dt = jnp.dot(xb, wdt_ref[...], preferred_element_type=F32) + dtb_ref[...]
    dt = jnp.maximum(dt, 0.0) + jnp.log1p(jnp.exp(-jnp.abs(dt)))

    cw = cw_ref[...]
    cbias = cb_ref[...]
    w = CONV_COLS
    pad = SUBLANES
    n_col = len(ext_scrs) // 2
    for c, (ext, qext) in enumerate(zip(ext_scrs[:n_col], ext_scrs[n_col:])):
        cols = slice(c * w, (c + 1) * w)
        ext[0:pad, :] = jnp.where(s > 0, ext[tm:tm + pad, :], 0.0)
        ext[pad:pad + tm, :] = jnp.dot(xb, wx_ref[:, cols], preferred_element_type=F32)
        x0 = ext[pad:pad + tm, :]
        xm1 = ext[pad - 1:pad - 1 + tm, :]
        q = cw[1:2, cols] * x0 + cw[0:1, cols] * xm1
        qext[0:pad, :] = jnp.where(s > 0, qext[tm:tm + pad, :], 0.0)
        qext[pad:pad + tm, :] = q
        acc = cbias[:, cols] + (cw[3:4, cols] * x0 + cw[2:3, cols] * xm1) + qext[pad - 2:pad - 2 + tm, :]
        act = _silu(acc).astype(BF16)
        lo = c * w
        if lo < d_inner:
            xs_scr[:, lo:lo + w] = act
        elif lo < d_inner + gn:
            bm_scr[:, lo - d_inner:lo - d_inner + w] = act
        else:
            cm_scr[:, lo - d_inner - gn:lo - d_inner - gn + w] = act

    def z_task(c):
        cols = slice(c * w, (c + 1) * w)
        z_scr[:, cols] = jnp.dot(xb, wz_ref[:, cols], preferred_element_type=F32).astype(BF16)

    z_tasks = [functools.partial(z_task, c) for c in range(d_inner // w)]

    a_row = -jnp.exp(alog_ref[...])
    adt = dt * a_row
    r_i = lax.broadcasted_iota(jnp.int32, (tm, tm), 0)
    c_i = lax.broadcasted_iota(jnp.int32, (tm, tm), 1)
    btril = ((r_i >= c_i) & (r_i // L == c_i // L)).astype(F32)
    acum = jnp.dot(btril, adt, preferred_element_type=F32,
                   precision=lax.Precision.HIGHEST)
    row_blk = lax.broadcasted_iota(jnp.int32, (tm, LANES), 0) // L
    alast = [acum[(b + 1) * L - 1:(b + 1) * L, :] for b in range(nblk)]
    alast_rows = alast[0]
    for b in range(1, nblk):
        alast_rows = jnp.where(row_blk == b, alast[b], alast_rows)
    dtw = dt * jnp.exp(alast_rows - acum)
    expand = exp_ref[...]
    dtw_scr[...] = jnp.dot(dtw.astype(BF16), expand, preferred_element_type=F32)
    acum2 = acum * LOG2E
    arow_t = acum2.T - jnp.log(dt.T) * LOG2E
    tril = lax.broadcasted_iota(jnp.int32, (L, L), 0) >= lax.broadcasted_iota(jnp.int32, (L, L), 1)
    lo_half = lax.broadcasted_iota(jnp.int32, (L, 2 * P), 1) < P

    for b in range(nblk):
        rows = slice(b * L, (b + 1) * L)
        acum_b = acum2[rows, :]
        elast = jnp.broadcast_to(jnp.exp(alast[b]), (SUBLANES, LANES))
        dec_all = sum(jnp.dot(piece, expand, preferred_element_type=F32)
                      for piece in _split3_bf16(elast))[0:1, :]
        for g in range(SSM_GROUPS):
            c_g = cm_scr[rows, g * N:(g + 1) * N]
            b_g = bm_scr[rows, g * N:(g + 1) * N]
            cb = lax.dot_general(c_g, b_g, (((1,), (1,)), ((), ())), preferred_element_type=F32)
            st = state_scr[g]
            y_off = jnp.dot(c_g, st.astype(BF16), preferred_element_type=F32)
            for q in range(R // 2):
                h0 = g * R + 2 * q
                ms, es = [], []
                for h in (h0, h0 + 1):
                    acol = jnp.broadcast_to(acum_b[:, h:h + 1], (L, L))
                    seg = acol - arow_t[h:h + 1, rows]
                    ms.append((cb * jnp.exp2(jnp.where(tril, seg, -jnp.inf))).astype(BF16))
                    es.append(jnp.exp2(acol))
                lhs = jnp.concatenate(ms, axis=1)
                cols = slice(h0 * P, (h0 + 2) * P)
                xp = xs_scr[rows, cols]
                zero = jnp.zeros_like(xp)
                rhs = jnp.concatenate([jnp.where(lo_half, xp, zero), jnp.where(lo_half, zero, xp)], axis=0)
                y_d = jnp.dot(lhs, rhs, preferred_element_type=F32)
                scale = jnp.where(lo_half, es[0], es[1])
                y_scr[rows, cols] = y_d + y_off[:, 2 * q * P:(2 * q + 2) * P] * scale
            gcols = slice(g * gw, (g + 1) * gw)
            xw = (xs_scr[rows, gcols].astype(F32) * dtw_scr[rows, gcols]).astype(BF16)
            contrib = lax.dot_general(b_g, xw, (((0,), (0,)), ((), ())), preferred_element_type=F32)
            state_scr[g] = st * dec_all[:, gcols] + contrib
            if z_tasks:
                z_tasks.pop(0)()
    for task in z_tasks:
        task()

    y = y_scr[...] + dfull_ref[...] * xs_scr[...].astype(F32)
    y = y * _silu(z_scr[...].astype(F32))
    nw = nw_ref[...]
    parts = []
    for g in range(SSM_GROUPS):
        yg = y[:, g * gw:(g + 1) * gw]
        yg = yg * lax.rsqrt(jnp.mean(yg * yg, axis=-1, keepdims=True) + RMS_EPS)
        parts.append((yg * nw[:, g * gw:(g + 1) * gw]).astype(BF16))
    yb = jnp.concatenate(parts, axis=1)
    m = jnp.dot(yb, wo_ref[...], preferred_element_type=F32)
    x1 = _ln(DN_ALPHA * x + m, g1_ref[...], b1_ref[...])
    x1_ref[...] = x1
    x1p_ref[...] = _pack_bf16_pairs(x1)
    first = (pl.program_id(0) == 0) & (s == 0)
    _route(x1, first, wr_ref, br_ref, ri_ref, rg_ref, cnt_ref, cnt_scr)


def mamba_block(x3, w_in, conv_w, conv_b, dt_bias, a_log, d_skip, norm_w, w_out, g1, b1, w_r, b_r):
    bsz, s_len, d = x3.shape
    d_inner = SSM_HEADS * SSM_HEAD_DIM
    gn = SSM_GROUPS * SSM_STATE
    conv_dim = d_inner + 2 * gn
    tm = TOKEN_TILE
    wz = w_in[:, :d_inner].astype(BF16)
    wx = w_in[:, d_inner:d_inner + conv_dim].astype(BF16)
    wdt = jnp.pad(w_in[:, d_inner + conv_dim:], ((0, 0), (0, LANES - SSM_HEADS))).astype(BF16)
    dtb = jnp.pad(dt_bias, (0, LANES - SSM_HEADS)).reshape(1, LANES)
    alog = jnp.pad(a_log, (0, LANES - SSM_HEADS)).reshape(1, LANES)
    dfull = jnp.repeat(d_skip, SSM_HEAD_DIM).reshape(1, d_inner)
    head_of_col = jnp.arange(d_inner, dtype=jnp.int32) // SSM_HEAD_DIM
    expand = (jnp.arange(LANES, dtype=jnp.int32)[:, None] == head_of_col[None, :]).astype(BF16)
    args = (x3, wz, wx, wdt, conv_w, conv_b.reshape(1, conv_dim), dtb, alog, dfull,
            norm_w.reshape(1, d_inner), expand, w_out.astype(BF16), g1.reshape(1, d), b1.reshape(1, d), w_r, b_r)
    tok = lambda w: pl.BlockSpec((None, tm, w), lambda b, s: (b, s, 0))
    in_specs = [tok(d)] + [_const_spec(a.shape) for a in args[1:]]
    ns = s_len // tm
    route = pl.BlockSpec((2 * TOP_K, tm), lambda b, s: (0, b * ns + s))
    out_shape = (jax.ShapeDtypeStruct((bsz, s_len, d), F32),
                 jax.ShapeDtypeStruct((bsz, s_len, d // 2), U32),
                 jax.ShapeDtypeStruct((2 * TOP_K, bsz * s_len), jnp.int32),
                 jax.ShapeDtypeStruct((bsz, s_len, LANES), F32),
                 jax.ShapeDtypeStruct((N_EXPERTS, LANES), F32))
    out_specs = (tok(d), tok(d // 2), route, tok(LANES), _const_spec((N_EXPERTS, LANES)))
    return pl.pallas_call(
        _mamba_kernel,
        out_shape=out_shape,
        grid=(bsz, ns),
        in_specs=in_specs,
        out_specs=out_specs,
        scratch_shapes=[pltpu.VMEM((tm, d_inner), BF16), pltpu.VMEM((tm, gn), BF16), pltpu.VMEM((tm, gn), BF16),
                        pltpu.VMEM((tm, d_inner), BF16),
                        pltpu.VMEM((SSM_GROUPS, SSM_STATE, d_inner // SSM_GROUPS), F32),
                        pltpu.VMEM((tm, d_inner), F32),
                        pltpu.VMEM((tm, d_inner), F32),
                        pltpu.VMEM((N_EXPERTS, LANES), F32)] +
                       [pltpu.VMEM((tm + SUBLANES, CONV_COLS), F32) for _ in range(2 * conv_dim // CONV_COLS)],
        compiler_params=pltpu.CompilerParams(
            dimension_semantics=("arbitrary", "arbitrary"), vmem_limit_bytes=VMEM_LIMIT),
        name="mamba",
    )(*args)


def _sc_workers():
    info = plsc.get_sparse_core_info()
    return info.num_cores, info.num_subcores


def sc_scatter_rows(x, dest_km, n_rows):
    t, d = x.shape
    nc, ns = _sc_workers()
    per_w, rem = divmod(t, nc * ns)
    assert rem == 0 and per_w % SC_CHUNK == 0
    mesh = plsc.VectorSubcoreMesh(core_axis_name="core", subcore_axis_name="subcore")

    @functools.partial(
        pl.kernel,
        out_type=jax.ShapeDtypeStruct((n_rows, d), x.dtype),
        mesh=mesh,
        scratch_types=[pltpu.VMEM((SC_CHUNK,), jnp.int32) for _ in range(TOP_K)] +
                      [pltpu.VMEM((SC_CHUNK, d), x.dtype), pltpu.SemaphoreType.DMA],
    )
    def scatter(x_hbm, i_hbm, o_hbm, i0, i1, i2, i3, rows_v, sem):
        wid = lax.axis_index("subcore") * nc + lax.axis_index("core")
        base = wid * per_w
        idx = (i0, i1, i2, i3)

        @pl.loop(0, per_w // SC_CHUNK)
        def _(c):
            off = base + c * SC_CHUNK
            pltpu.sync_copy(x_hbm.at[pl.ds(off, SC_CHUNK)], rows_v)
            for k in range(TOP_K):
                pltpu.sync_copy(i_hbm.at[pl.ds(k * t + off, SC_CHUNK)], idx[k])
            copies = [pltpu.make_async_copy(rows_v, o_hbm.at[idx[k]], sem) for k in range(TOP_K)]
            for cp in copies:
                cp.start()
            for cp in copies:
                cp.wait()

    return scatter(x, dest_km)


def sc_gather_rows(table, idx):
    n = idx.shape[0]
    d = table.shape[1]
    nc, ns = _sc_workers()
    per_w, rem = divmod(n, nc * ns)
    n_chunks = per_w // SC_CHUNK
    assert rem == 0 and per_w % (2 * SC_CHUNK) == 0
    mesh = plsc.VectorSubcoreMesh(core_axis_name="core", subcore_axis_name="subcore")

    @functools.partial(
        pl.kernel,
        out_type=jax.ShapeDtypeStruct((n, d), table.dtype),
        mesh=mesh,
        scratch_types=[
            pltpu.VMEM((SC_CHUNK,), jnp.int32), pltpu.VMEM((SC_CHUNK,), jnp.int32),
            pltpu.VMEM((SC_CHUNK, d), table.dtype), pltpu.VMEM((SC_CHUNK, d), table.dtype),
            pltpu.SemaphoreType.DMA, pltpu.SemaphoreType.DMA,
            pltpu.SemaphoreType.DMA, pltpu.SemaphoreType.DMA,
        ],
    )
    def gather(x_hbm, i_hbm, o_hbm, idx0, idx1, rows0, rows1, sg0, sg1, sw0, sw1):
        wid = lax.axis_index("subcore") * nc + lax.axis_index("core")
        base = wid * per_w
        idx = (idx0, idx1)
        rows = (rows0, rows1)
        sg = (sg0, sg1)
        sw = (sw0, sw1)

        def gather_copy(slot):
            return pltpu.make_async_copy(x_hbm.at[idx[slot]], rows[slot], sg[slot])

        def write_copy(c, slot):
            return pltpu.make_async_copy(rows[slot], o_hbm.at[pl.ds(base + c * SC_CHUNK, SC_CHUNK)], sw[slot])

        pltpu.sync_copy(i_hbm.at[pl.ds(base, SC_CHUNK)], idx0)
        gather_copy(0).start()

        @pl.loop(0, n_chunks, step=2)
        def _(c0):
            for slot in range(2):
                c = c0 + slot
                nxt = 1 - slot

                @pl.when(c + 1 < n_chunks)
                def _():
                    pltpu.sync_copy(i_hbm.at[pl.ds(base + (c + 1) * SC_CHUNK, SC_CHUNK)], idx[nxt])

                    @pl.when(c >= 1)
                    def _():
                        write_copy(c - 1, nxt).wait()

                    gather_copy(nxt).start()

                gather_copy(slot).wait()
                write_copy(c, slot).start()

        write_copy(n_chunks - 2, 0).wait()
        write_copy(n_chunks - 1, 1).wait()

    return gather(table, idx)


def _moe_mlp_kernel(be_ref, nv_ref, nu_ref, x_ref, w1_ref, b1_ref, w2_ref, b2_ref, o_ref, w1b, w2b):
    i = pl.program_id(0)
    e = be_ref[i]
    prev = be_ref[jnp.maximum(i - 1, 0)]

    @pl.when((i == 0) | (e != prev))
    def _():
        w1b[...] = w1_ref[...].astype(BF16)
        w2b[...] = w2_ref[...].astype(BF16)

    @pl.when(i < nu_ref[0])
    def _():
        dff = w2b.shape[0]
        row = lax.broadcasted_iota(jnp.int32, x_ref.shape, 0)
        xw = jnp.where(row < nv_ref[i], x_ref[...], jnp.uint32(0))
        xb = _unpack_bf16_pairs(xw).astype(BF16)
        h = jnp.dot(xb, w1b[...], preferred_element_type=F32) + b1_ref[...]
        g = jnp.minimum(h[:, :dff], SWIGLU_LIMIT)
        lin = jnp.clip(h[:, dff:], -SWIGLU_LIMIT, SWIGLU_LIMIT)
        act = (g * jax.nn.sigmoid(SWIGLU_ALPHA * g) * (lin + 1.0)).astype(BF16)
        y = jnp.dot(act, w2b[...], preferred_element_type=F32) + b2_ref[...]
        o_ref[...] = _pack_bf16_pairs(y)


def moe_mlp(xp, block_expert, n_valid, n_used, layer, w1, b1, w2, b2):
    n_rows, dh = xp.shape
    _, n_e, d, two_dff = w1.shape
    dff = two_dff // 2
    n_blocks = n_rows // MOE_BLOCK
    row = lambda i, be, nv, nu: (jnp.minimum(i, nu[0] - 1), 0)
    wsel = lambda i, be, nv, nu: (layer, be[i], 0, 0)
    grid_spec = pltpu.PrefetchScalarGridSpec(
        num_scalar_prefetch=3,
        grid=(n_blocks,),
        in_specs=[
            pl.BlockSpec((MOE_BLOCK, dh), row),
            pl.BlockSpec((None, None, d, two_dff), wsel),
            pl.BlockSpec((None, None, 1, two_dff), wsel),
            pl.BlockSpec((None, None, dff, d), wsel),
            pl.BlockSpec((None, None, 1, d), wsel),
        ],
        out_specs=pl.BlockSpec((MOE_BLOCK, dh), row),
        scratch_shapes=[pltpu.VMEM((d, two_dff), BF16), pltpu.VMEM((dff, d), BF16)],
    )
    return pl.pallas_call(
        _moe_mlp_kernel,
        out_shape=jax.ShapeDtypeStruct((n_rows, dh), U32),
        grid_spec=grid_spec,
        compiler_params=pltpu.CompilerParams(
            dimension_semantics=("arbitrary",), vmem_limit_bytes=VMEM_LIMIT),
        name="moe_mlp",
    )(block_expert, n_valid, n_used, xp, w1, b1.reshape(b1.shape[0], n_e, 1, two_dff),
      w2, b2.reshape(b2.shape[0], n_e, 1, d))


def _combine_kernel(x1_ref, y0_ref, y1_ref, y2_ref, y3_ref, rg_ref, g2_ref, b2_ref, *rest):
    o_ref = rest[-1]
    rg = rg_ref[...]
    f = rg[:, 0:1] * _unpack_bf16_pairs(y0_ref[...])
    for k, y_ref in ((1, y1_ref), (2, y2_ref), (3, y3_ref)):
        f = f + rg[:, k:k + 1] * _unpack_bf16_pairs(y_ref[...])
    o_ref[...] = _ln(DN_ALPHA * x1_ref[...] + f, g2_ref[...], b2_ref[...])


def combine_block(x1, y4p, rg, g2, b2, part, partial_out):
    t, d = x1.shape
    tm = TOKEN_TILE
    ntp = t // tm // COMBINE_PARTS
    base = part * ntp
    tok = lambda w: pl.BlockSpec((tm, w), lambda i: (base + i, 0))
    y_specs = [pl.BlockSpec((tm, d // 2), functools.partial(lambda i, k: (k * ntp + i, 0), k=k))
               for k in range(TOP_K)]
    in_specs = [tok(d)] + y_specs + [tok(LANES), _const_spec((1, d)), _const_spec((1, d))]
    args = [x1, y4p, y4p, y4p, y4p, rg, g2.reshape(1, d), b2.reshape(1, d)]
    aliases = {}
    if partial_out is not None:
        in_specs.append(pl.BlockSpec(memory_space=pl.ANY))
        args.append(partial_out)
        aliases = {len(args) - 1: 0}
    return pl.pallas_call(
        _combine_kernel,
        out_shape=jax.ShapeDtypeStruct((t, d), F32),
        grid=(ntp,),
        in_specs=in_specs,
        out_specs=tok(d),
        input_output_aliases=aliases,
        compiler_params=pltpu.CompilerParams(
            dimension_semantics=("arbitrary",), vmem_limit_bytes=VMEM_LIMIT),
        name="moe_combine",
    )(*args)


def moe_block(x1, x1p, ri, rg, cnt, layer, w1, b1, w2, b2, g2, b2n):
    t, d = x1.shape
    n_assign = t * TOP_K
    n_blocks = -(-n_assign // MOE_BLOCK) + N_EXPERTS
    n_rows = n_blocks * MOE_BLOCK
    ar = jnp.arange(N_EXPERTS, dtype=jnp.int32)
    counts = cnt[:, 0].astype(jnp.int32)
    padded = (counts + MOE_BLOCK - 1) // MOE_BLOCK * MOE_BLOCK
    pend = jnp.sum(jnp.where(ar[None, :] <= ar[:, None], padded[None, :], 0), axis=1)
    pstart = pend - padded
    top_i = ri[:TOP_K]
    dest = jnp.sum(jnp.where(top_i[:, :, None] == ar, pstart, 0), axis=-1) + ri[TOP_K:]
    dest_km = dest.reshape(-1)
    n_used = (pend[-1] // MOE_BLOCK).reshape(1)
    block_start = jnp.arange(n_blocks, dtype=jnp.int32) * MOE_BLOCK
    block_expert = jnp.minimum(
        jnp.sum((pend[None, :] <= block_start[:, None]).astype(jnp.int32), axis=1), N_EXPERTS - 1)
    last = jnp.sum(jnp.where(jnp.arange(n_blocks) == n_used[0] - 1, block_expert, 0))
    block_expert = jnp.where(jnp.arange(n_blocks) < n_used[0], block_expert, last)
    vend = jnp.sum(jnp.where(block_expert[:, None] == ar, (pstart + counts)[None, :], 0), axis=1)
    n_valid = jnp.clip(vend - block_start, 0, MOE_BLOCK)

    xp = sc_scatter_rows(x1p, dest_km, n_rows)
    yp = moe_mlp(xp, block_expert, n_valid, n_used, layer, w1, b1, w2, b2)
    tp = t // COMBINE_PARTS
    out = None
    for part in range(COMBINE_PARTS):
        y4p = sc_gather_rows(yp, dest[:, part * tp:(part + 1) * tp].reshape(-1))
        out = combine_block(x1, y4p, rg, g2, b2n, part, out)
    return out


def kernel(x, a_w_in, a_b_in, a_ln_g, a_ln_b, a_w_s, a_b_s, a_w_out, a_b_out, b_w_in, b_conv_w, b_conv_b, b_dt_bias, b_a_log, b_d, b_norm_w, b_w_out, moe_w_router, moe_b_router, moe_w1, moe_b1, moe_w2, moe_b2, ln1_g, ln1_b, ln2_g, ln2_b):
    bsz, s_len, d = x.shape
    t = bsz * s_len
    xt = x.reshape(t, d)
    for i in range(DEPTH):
        j = i // 2
        w_r32 = jnp.pad(moe_w_router[i], ((0, 0), (0, LANES - N_EXPERTS)))
        w_r_hi = w_r32.astype(BF16)
        w_r = jnp.stack([w_r_hi, (w_r32 - w_r_hi.astype(F32)).astype(BF16)])
        b_r = moe_b_router[i].reshape(N_EXPERTS, 1)
        if i % 2 == 0:
            x1, x1p, ri, rg, cnt = mixer_a_block(
                xt, a_w_in[j], a_b_in[j], a_ln_g[j], a_ln_b[j], a_w_s[j], a_b_s[j],
                a_w_out[j], a_b_out[j], ln1_g[i], ln1_b[i], w_r, b_r)
        else:
            x3 = xt.reshape(bsz, s_len, d)
            x1, x1p, ri, rg, cnt = mamba_block(
                x3, b_w_in[j], b_conv_w[j], b_conv_b[j], b_dt_bias[j], b_a_log[j], b_d[j], b_norm_w[j],
                b_w_out[j], ln1_g[i], ln1_b[i], w_r, b_r)
            x1 = x1.reshape(t, d)
            x1p = x1p.reshape(t, d // 2)
            rg = rg.reshape(t, LANES)
        xt = moe_block(x1, x1p, ri, rg, cnt, i, moe_w1, moe_b1, moe_w2, moe_b2, ln2_g[i], ln2_b[i])
    return xt.reshape(bsz, s_len, d)
```

```python
import functools
import math

import jax
import jax.numpy as jnp
from jax import lax
from jax.experimental import pallas as pl
from jax.experimental.pallas import tpu as pltpu
from jax.experimental.pallas import tpu_sc as plsc

F32 = jnp.float32
BF16 = jnp.bfloat16
U32 = jnp.uint32

DEPTH = 4
N_EXPERTS = 32
TOP_K = 4
MOE_BLOCK = 512
SWIGLU_LIMIT = 7.0
SWIGLU_ALPHA = 1.702
A_BLOCK = 128
A_GROUPS = 8
CHUNK = 64
SSD_BLOCK = 128
SSM_HEADS = 32
SSM_HEAD_DIM = 64
SSM_GROUPS = 4
SSM_STATE = 128
SSM_CONV = 4
DN_ALPHA = (2 * DEPTH) ** 0.25
LN_EPS = 1e-5
LOG2E = 1.0 / math.log(2.0)
RMS_EPS = 1e-5

LANES = 128
SUBLANES = 8
TOKEN_TILE = 256
VMEM_LIMIT = 56 * 1024 * 1024
SC_CHUNK = 64
CONV_COLS = 256
COMBINE_PARTS = 4


def _ln(x, g, b):
    mu = jnp.mean(x, axis=-1, keepdims=True)
    xc = x - mu
    var = jnp.mean(xc * xc, axis=-1, keepdims=True)
    return xc * lax.rsqrt(var + LN_EPS) * g + b


def _gelu(x):
    return 0.5 * x * (1.0 + lax.erf(x * (1.0 / math.sqrt(2.0))))


def _silu(x):
    h = 0.5 * x
    return h + h * jnp.tanh(h)


def _pack_bf16_pairs(y):
    h = y.shape[1] // 2
    lo = lax.bitcast_convert_type(y[:, :h].astype(BF16).astype(F32), U32) >> 16
    hi = lax.bitcast_convert_type(y[:, h:].astype(BF16).astype(F32), U32) & jnp.uint32(0xFFFF0000)
    return hi | lo


def _unpack_bf16_pairs(w):
    lo = lax.bitcast_convert_type(w << 16, F32)
    hi = lax.bitcast_convert_type(w & jnp.uint32(0xFFFF0000), F32)
    return jnp.concatenate([lo, hi], axis=1)


def _route(x1, first, wr_ref, brc_ref, ri_ref, rg_ref, cnt_ref, cnt_scr):
    @pl.when(first)
    def _():
        cnt_scr[...] = jnp.zeros(cnt_scr.shape, F32)

    x_hi = x1.astype(BF16)
    x_lo = (x1 - x_hi.astype(F32)).astype(BF16)
    logits_tok = (jnp.dot(x_hi, wr_ref[0], preferred_element_type=F32)
                  + (jnp.dot(x_hi, wr_ref[1], preferred_element_type=F32)
                     + jnp.dot(x_lo, wr_ref[0], preferred_element_type=F32)))
    n_e = brc_ref.shape[0]
    logits = logits_tok.T[:n_e, :] + brc_ref[...]
    tm = logits.shape[1]
    sub_f = lax.broadcasted_iota(jnp.int32, logits.shape, 0).astype(F32)
    vals, idxs, hots = [], [], []
    l = logits
    for _ in range(TOP_K):
        m = jnp.max(l, axis=0, keepdims=True)
        i = jnp.min(jnp.where(l == m, sub_f, float(n_e)), axis=0, keepdims=True)
        hot = sub_f == i
        vals.append(m)
        idxs.append(i)
        hots.append(hot.astype(F32))
        l = jnp.where(hot, -jnp.inf, l)
    es = [jnp.exp(v - vals[0]) for v in vals]
    tot = es[0] + es[1] + es[2] + es[3]
    cnt = (hots[0] + hots[1] + hots[2] + hots[3]).astype(BF16)
    r_i = lax.broadcasted_iota(jnp.int32, (tm, tm), 0)
    c_i = lax.broadcasted_iota(jnp.int32, (tm, tm), 1)
    before = jnp.dot(cnt, (r_i < c_i).astype(BF16), preferred_element_type=F32)
    before = before + jnp.concatenate([cnt_scr[...]] * (tm // LANES), axis=1)
    row = lax.broadcasted_iota(jnp.int32, (2 * TOP_K, tm), 0)
    ri = jnp.zeros((2 * TOP_K, tm), F32)
    rg = jnp.zeros((2 * TOP_K, tm), F32)
    for k in range(TOP_K):
        rank = jnp.sum(before * hots[k], axis=0, keepdims=True)
        ri = jnp.where(row == k, idxs[k], ri)
        ri = jnp.where(row == TOP_K + k, rank, ri)
        rg = jnp.where(row == k, es[k] / tot, rg)
    ri_ref[...] = ri.astype(jnp.int32)
    rg_ref[...] = jnp.concatenate([rg, jnp.zeros((LANES - 2 * TOP_K, tm), F32)], axis=0).T
    total = cnt_scr[...] + jnp.dot(cnt, jnp.ones((tm, LANES), BF16), preferred_element_type=F32)
    cnt_scr[...] = total
    cnt_ref[...] = total


def _const_spec(shape):
    nd = len(shape)
    return pl.BlockSpec(shape, lambda *_: (0,) * nd)


def _mixer_a_kernel(x_ref, wu_ref, wv_ref, bu_ref, bv_ref, lg_ref, lb_ref, ws_ref, bst_ref,
                    wo_ref, bo_ref, g1_ref, b1_ref, wr_ref, br_ref,
                    x1_ref, x1p_ref, ri_ref, rg_ref, cnt_ref, v_scr, o_scr, cnt_scr):
    x = x_ref[...]
    xb = x.astype(BF16)
    tm, dff = v_scr.shape
    gd = dff // A_GROUPS
    gcols = [slice(g * gd, (g + 1) * gd) for g in range(A_GROUPS)]
    s1 = jnp.zeros((tm, 1), F32)
    s2 = jnp.zeros((tm, 1), F32)
    shift = None
    for cols in gcols:
        v_g = _gelu(jnp.dot(xb, wv_ref[:, cols], preferred_element_type=F32) + bv_ref[:, cols])
        v_scr[:, cols] = v_g
        if shift is None:
            shift = jnp.mean(v_g, axis=-1, keepdims=True)
        dv = v_g - shift
        s1 = s1 + jnp.sum(dv, axis=-1, keepdims=True)
        s2 = s2 + jnp.sum(dv * dv, axis=-1, keepdims=True)
    dmu = s1 * (1.0 / dff)
    mu = shift + dmu
    rstd = lax.rsqrt(s2 * (1.0 / dff) - dmu * dmu + LN_EPS)
    pi = lax.broadcasted_iota(jnp.int32, (A_BLOCK, A_BLOCK), 0) // CHUNK
    pj = lax.broadcasted_iota(jnp.int32, (A_BLOCK, A_BLOCK), 1) // CHUNK
    mask = pj <= pi
    bst = bst_ref[...]
    for g, cols in enumerate(gcols):
        u_g = _gelu(jnp.dot(xb, wu_ref[:, cols], preferred_element_type=F32) + bu_ref[:, cols])
        vb_g = ((v_scr[:, cols] - mu) * rstd * lg_ref[:, cols] + lb_ref[:, cols]).astype(BF16)
        wm = jnp.where(mask, ws_ref[g], 0.0).astype(BF16)
        for n in range(tm // A_BLOCK):
            rows = slice(n * A_BLOCK, (n + 1) * A_BLOCK)
            sv = jnp.dot(wm, vb_g[rows, :], preferred_element_type=F32) + bst[:, g:g + 1]
            o_scr[rows, cols] = (u_g[rows, :] * sv).astype(BF16)
    m = jnp.dot(o_scr[...], wo_ref[...], preferred_element_type=F32) + bo_ref[...]
    x1 = _ln(DN_ALPHA * x + m, g1_ref[...], b1_ref[...])
    x1_ref[...] = x1
    x1p_ref[...] = _pack_bf16_pairs(x1)
    _route(x1, pl.program_id(0) == 0, wr_ref, br_ref, ri_ref, rg_ref, cnt_ref, cnt_scr)


def mixer_a_block(xt, w_in, b_in, ln_g, ln_b, w_s, b_s, w_out, b_out, g1, b1, w_r, b_r):
    t, d = xt.shape
    dff = w_out.shape[0]
    tm = TOKEN_TILE
    wu = w_in[:, :dff].astype(BF16)
    wv = w_in[:, dff:].astype(BF16)
    bu = b_in[:dff].reshape(1, dff)
    bv = b_in[dff:].reshape(1, dff)
    args = (xt, wu, wv, bu, bv, ln_g.reshape(1, dff), ln_b.reshape(1, dff), w_s, b_s.T,
            w_out.astype(BF16), b_out.reshape(1, d), g1.reshape(1, d), b1.reshape(1, d), w_r, b_r)
    tok = lambda w: pl.BlockSpec((tm, w), lambda i: (i, 0))
    route = pl.BlockSpec((2 * TOP_K, tm), lambda i: (0, i))
    in_specs = [tok(d)] + [_const_spec(a.shape) for a in args[1:]]
    out_shape = (jax.ShapeDtypeStruct((t, d), F32),
                 jax.ShapeDtypeStruct((t, d // 2), U32),
                 jax.ShapeDtypeStruct((2 * TOP_K, t), jnp.int32),
                 jax.ShapeDtypeStruct((t, LANES), F32),
                 jax.ShapeDtypeStruct((N_EXPERTS, LANES), F32))
    out_specs = (tok(d), tok(d // 2), route, tok(LANES), _const_spec((N_EXPERTS, LANES)))
    return pl.pallas_call(
        _mixer_a_kernel,
        out_shape=out_shape,
        grid=(t // tm,),
        in_specs=in_specs,
        out_specs=out_specs,
        scratch_shapes=[pltpu.VMEM((tm, dff), F32), pltpu.VMEM((tm, dff), BF16),
                        pltpu.VMEM((N_EXPERTS, LANES), F32)],
        compiler_params=pltpu.CompilerParams(
            dimension_semantics=("arbitrary",), vmem_limit_bytes=VMEM_LIMIT),
        name="mixer_a",
    )(*args)


def _split3_bf16(q):
    hi = q.astype(BF16)
    r1 = q - hi.astype(F32)
    mid = r1.astype(BF16)
    lo = (r1 - mid.astype(F32)).astype(BF16)
    return hi, mid, lo


def _mamba_kernel(x_ref, wz_ref, wx_ref, wdt_ref, cw_ref, cb_ref, dtb_ref,
                  alog_ref, dfull_ref, nw_ref, exp_ref, wo_ref, g1_ref, b1_ref, wr_ref, br_ref,
                  x1_ref, x1p_ref, ri_ref, rg_ref, cnt_ref,
                  xs_scr, bm_scr, cm_scr, z_scr, state_scr, y_scr, dtw_scr, cnt_scr, *ext_scrs):
    s = pl.program_id(1)
    tm = x_ref.shape[0]
    d_inner = xs_scr.shape[1]
    gn = bm_scr.shape[1]
    L, P, N = SSD_BLOCK, SSM_HEAD_DIM, SSM_STATE
    R = SSM_HEADS // SSM_GROUPS
    gw = R * P
    nblk = tm // L

    @pl.when(s == 0)
    def _():
        state_scr[...] = jnp.zeros(state_scr.shape, F32)
        for ext in ext_scrs:
            ext[...] = jnp.zeros(ext.shape, F32)

    x = x_ref[...]
    xb = x.astype(BF16)
    dt = jnp.dot(xb, wdt_ref[...], preferred_element_type=F32) + dtb_ref[...]
    dt = jnp.maximum(dt, 0.0) + jnp.log1p(jnp.exp(-jnp.abs(dt)))

    cw = cw_ref[...]
    cbias = cb_ref[...]
    w = CONV_COLS
    pad = SUBLANES
    n_col = len(ext_scrs) // 2
    for c, (ext, qext) in enumerate(zip(ext_scrs[:n_col], ext_scrs[n_col:])):
        cols = slice(c * w, (c + 1) * w)
        ext[0:pad, :] = jnp.where(s > 0, ext[tm:tm + pad, :], 0.0)
        ext[pad:pad + tm, :] = jnp.dot(xb, wx_ref[:, cols], preferred_element_type=F32)
        x0 = ext[pad:pad + tm, :]
        xm1 = ext[pad - 1:pad - 1 + tm, :]
        q = cw[1:2, cols] * x0 + cw[0:1, cols] * xm1
        qext[0:pad, :] = jnp.where(s > 0, qext[tm:tm + pad, :], 0.0)
        qext[pad:pad + tm, :] = q
        acc = cbias[:, cols] + (cw[3:4, cols] * x0 + cw[2:3, cols] * xm1) + qext[pad - 2:pad - 2 + tm, :]
        act = _silu(acc).astype(BF16)
        lo = c * w
        if lo < d_inner:
            xs_scr[:, lo:lo + w] = act
        elif lo < d_inner + gn:
            bm_scr[:, lo - d_inner:lo - d_inner + w] = act
        else:
            cm_scr[:, lo - d_inner - gn:lo - d_inner - gn + w] = act

    def z_task(c):
        cols = slice(c * w, (c + 1) * w)
        z_scr[:, cols] = jnp.dot(xb, wz_ref[:, cols], preferred_element_type=F32).astype(BF16)

    z_tasks = [functools.partial(z_task, c) for c in range(d_inner // w)]

    a_row = -jnp.exp(alog_ref[...])
    adt = dt * a_row
    r_i = lax.broadcasted_iota(jnp.int32, (tm, tm), 0)
    c_i = lax.broadcasted_iota(jnp.int32, (tm, tm), 1)
    btril = ((r_i >= c_i) & (r_i // L == c_i // L)).astype(F32)
    acum = jnp.dot(btril, adt, preferred_element_type=F32,
                   precision=lax.Precision.HIGHEST)
    row_blk = lax.broadcasted_iota(jnp.int32, (tm, LANES), 0) // L
    alast = [acum[(b + 1) * L - 1:(b + 1) * L, :] for b in range(nblk)]
    alast_rows = alast[0]
    for b in range(1, nblk):
        alast_rows = jnp.where(row_blk == b, alast[b], alast_rows)
    dtw = dt * jnp.exp(alast_rows - acum)
    expand = exp_ref[...]
    dtw_scr[...] = jnp.dot(dtw.astype(BF16), expand, preferred_element_type=F32)
    acum2 = acum * LOG2E
    arow_t = acum2.T - jnp.log(dt.T) * LOG2E
    tril = lax.broadcasted_iota(jnp.int32, (L, L), 0) >= lax.broadcasted_iota(jnp.int32, (L, L), 1)
    lo_half = lax.broadcasted_iota(jnp.int32, (L, 2 * P), 1) < P

    for b in range(nblk):
        rows = slice(b * L, (b + 1) * L)
        acum_b = acum2[rows, :]
        elast = jnp.broadcast_to(jnp.exp(alast[b]), (SUBLANES, LANES))
        dec_all = sum(jnp.dot(piece, expand, preferred_element_type=F32)
                      for piece in _split3_bf16(elast))[0:1, :]
        for g in range(SSM_GROUPS):
            c_g = cm_scr[rows, g * N:(g + 1) * N]
            b_g = bm_scr[rows, g * N:(g + 1) * N]
            cb = lax.dot_general(c_g, b_g, (((1,), (1,)), ((), ())), preferred_element_type=F32)
            st = state_scr[g]
            y_off = jnp.dot(c_g, st.astype(BF16), preferred_element_type=F32)
            for q in range(R // 2):
                h0 = g * R + 2 * q
                ms, es = [], []
                for h in (h0, h0 + 1):
                    acol = jnp.broadcast_to(acum_b[:, h:h + 1], (L, L))
                    seg = acol - arow_t[h:h + 1, rows]
                    ms.append((cb * jnp.exp2(jnp.where(tril, seg, -jnp.inf))).astype(BF16))
                    es.append(jnp.exp2(acol))
                lhs = jnp.concatenate(ms, axis=1)
                cols = slice(h0 * P, (h0 + 2) * P)
                xp = xs_scr[rows, cols]
                zero = jnp.zeros_like(xp)
                rhs = jnp.concatenate([jnp.where(lo_half, xp, zero), jnp.where(lo_half, zero, xp)], axis=0)
                y_d = jnp.dot(lhs, rhs, preferred_element_type=F32)
                scale = jnp.where(lo_half, es[0], es[1])
                y_scr[rows, cols] = y_d + y_off[:, 2 * q * P:(2 * q + 2) * P] * scale
            gcols = slice(g * gw, (g + 1) * gw)
            xw = (xs_scr[rows, gcols].astype(F32) * dtw_scr[rows, gcols]).astype(BF16)
            contrib = lax.dot_general(b_g, xw, (((0,), (0,)), ((), ())), preferred_element_type=F32)
            state_scr[g] = st * dec_all[:, gcols] + contrib
            if z_tasks:
                z_tasks.pop(0)()
    for task in z_tasks:
        task()

    y = y_scr[...] + dfull_ref[...] * xs_scr[...].astype(F32)
    y = y * _silu(z_scr[...].astype(F32))
    nw = nw_ref[...]
    parts = []
    for g in range(SSM_GROUPS):
        yg = y[:, g * gw:(g + 1) * gw]
        yg = yg * lax.rsqrt(jnp.mean(yg * yg, axis=-1, keepdims=True) + RMS_EPS)
        parts.append((yg * nw[:, g * gw:(g + 1) * gw]).astype(BF16))
    yb = jnp.concatenate(parts, axis=1)
    m = jnp.dot(yb, wo_ref[...], preferred_element_type=F32)
    x1 = _ln(DN_ALPHA * x + m, g1_ref[...], b1_ref[...])
    x1_ref[...] = x1
    x1p_ref[...] = _pack_bf16_pairs(x1)
    first = (pl.program_id(0) == 0) & (s == 0)
    _route(x1, first, wr_ref, br_ref, ri_ref, rg_ref, cnt_ref, cnt_scr)


def mamba_block(x3, w_in, conv_w, conv_b, dt_bias, a_log, d_skip, norm_w, w_out, g1, b1, w_r, b_r):
    bsz, s_len, d = x3.shape
    d_inner = SSM_HEADS * SSM_HEAD_DIM
    gn = SSM_GROUPS * SSM_STATE
    conv_dim = d_inner + 2 * gn
    tm = TOKEN_TILE
    wz = w_in[:, :d_inner].astype(BF16)
    wx = w_in[:, d_inner:d_inner + conv_dim].astype(BF16)
    wdt = jnp.pad(w_in[:, d_inner + conv_dim:], ((0, 0), (0, LANES - SSM_HEADS))).astype(BF16)
    dtb = jnp.pad(dt_bias, (0, LANES - SSM_HEADS)).reshape(1, LANES)
    alog = jnp.pad(a_log, (0, LANES - SSM_HEADS)).reshape(1, LANES)
    dfull = jnp.repeat(d_skip, SSM_HEAD_DIM).reshape(1, d_inner)
    head_of_col = jnp.arange(d_inner, dtype=jnp.int32) // SSM_HEAD_DIM
    expand = (jnp.arange(LANES, dtype=jnp.int32)[:, None] == head_of_col[None, :]).astype(BF16)
    args = (x3, wz, wx, wdt, conv_w, conv_b.reshape(1, conv_dim), dtb, alog, dfull,
            norm_w.reshape(1, d_inner), expand, w_out.astype(BF16), g1.reshape(1, d), b1.reshape(1, d), w_r, b_r)
    tok = lambda w: pl.BlockSpec((None, tm, w), lambda b, s: (b, s, 0))
    in_specs = [tok(d)] + [_const_spec(a.shape) for a in args[1:]]
    ns = s_len // tm
    route = pl.BlockSpec((2 * TOP_K, tm), lambda b, s: (0, b * ns + s))
    out_shape = (jax.ShapeDtypeStruct((bsz, s_len, d), F32),
                 jax.ShapeDtypeStruct((bsz, s_len, d // 2), U32),
                 jax.ShapeDtypeStruct((2 * TOP_K, bsz * s_len), jnp.int32),
                 jax.ShapeDtypeStruct((bsz, s_len, LANES), F32),
                 jax.ShapeDtypeStruct((N_EXPERTS, LANES), F32))
    out_specs = (tok(d), tok(d // 2), route, tok(LANES), _const_spec((N_EXPERTS, LANES)))
    return pl.pallas_call(
        _mamba_kernel,
        out_shape=out_shape,
        grid=(bsz, ns),
        in_specs=in_specs,
        out_specs=out_specs,
        scratch_shapes=[pltpu.VMEM((tm, d_inner), BF16), pltpu.VMEM((tm, gn), BF16), pltpu.VMEM((tm, gn), BF16),
                        pltpu.VMEM((tm, d_inner), BF16),
                        pltpu.VMEM((SSM_GROUPS, SSM_STATE, d_inner // SSM_GROUPS), F32),
                        pltpu.VMEM((tm, d_inner), F32),
                        pltpu.VMEM((tm, d_inner), F32),
                        pltpu.VMEM((N_EXPERTS, LANES), F32)] +
                       [pltpu.VMEM((tm + SUBLANES, CONV_COLS), F32) for _ in range(2 * conv_dim // CONV_COLS)],
        compiler_params=pltpu.CompilerParams(
            dimension_semantics=("arbitrary", "arbitrary"), vmem_limit_bytes=VMEM_LIMIT),
        name="mamba",
    )(*args)


def _sc_workers():
    info = plsc.get_sparse_core_info()
    return info.num_cores, info.num_subcores


def sc_scatter_rows(x, dest_km, n_rows):
    t, d = x.shape
    nc, ns = _sc_workers()
    per_w, rem = divmod(t, nc * ns)
    assert rem == 0 and per_w % SC_CHUNK == 0
    mesh = plsc.VectorSubcoreMesh(core_axis_name="core", subcore_axis_name="subcore")

    @functools.partial(
        pl.kernel,
        out_type=jax.ShapeDtypeStruct((n_rows, d), x.dtype),
        mesh=mesh,
        scratch_types=[pltpu.VMEM((SC_CHUNK,), jnp.int32) for _ in range(TOP_K)] +
                      [pltpu.VMEM((SC_CHUNK, d), x.dtype), pltpu.SemaphoreType.DMA],
    )
    def scatter(x_hbm, i_hbm, o_hbm, i0, i1, i2, i3, rows_v, sem):
        wid = lax.axis_index("subcore") * nc + lax.axis_index("core")
        base = wid * per_w
        idx = (i0, i1, i2, i3)

        @pl.loop(0, per_w // SC_CHUNK)
        def _(c):
            off = base + c * SC_CHUNK
            pltpu.sync_copy(x_hbm.at[pl.ds(off, SC_CHUNK)], rows_v)
            for k in range(TOP_K):
                pltpu.sync_copy(i_hbm.at[pl.ds(k * t + off, SC_CHUNK)], idx[k])
            copies = [pltpu.make_async_copy(rows_v, o_hbm.at[idx[k]], sem) for k in range(TOP_K)]
            for cp in copies:
                cp.start()
            for cp in copies:
                cp.wait()

    return scatter(x, dest_km)


def sc_gather_rows(table, idx):
    n = idx.shape[0]
    d = table.shape[1]
    nc, ns = _sc_workers()
    per_w, rem = divmod(n, nc * ns)
    n_chunks = per_w // SC_CHUNK
    assert rem == 0 and per_w % (2 * SC_CHUNK) == 0
    mesh = plsc.VectorSubcoreMesh(core_axis_name="core", subcore_axis_name="subcore")

    @functools.partial(
        pl.kernel,
        out_type=jax.ShapeDtypeStruct((n, d), table.dtype),
        mesh=mesh,
        scratch_types=[
            pltpu.VMEM((SC_CHUNK,), jnp.int32), pltpu.VMEM((SC_CHUNK,), jnp.int32),
            pltpu.VMEM((SC_CHUNK, d), table.dtype), pltpu.VMEM((SC_CHUNK, d), table.dtype),
            pltpu.SemaphoreType.DMA, pltpu.SemaphoreType.DMA,
            pltpu.SemaphoreType.DMA, pltpu.SemaphoreType.DMA,
        ],
    )
    def gather(x_hbm, i_hbm, o_hbm, idx0, idx1, rows0, rows1, sg0, sg1, sw0, sw1):
        wid = lax.axis_index("subcore") * nc + lax.axis_index("core")
        base = wid * per_w
        idx = (idx0, idx1)
        rows = (rows0, rows1)
        sg = (sg0, sg1)
        sw = (sw0, sw1)

        def gather_copy(slot):
            return pltpu.make_async_copy(x_hbm.at[idx[slot]], rows[slot], sg[slot])

        def write_copy(c, slot):
            return pltpu.make_async_copy(rows[slot], o_hbm.at[pl.ds(base + c * SC_CHUNK, SC_CHUNK)], sw[slot])

        pltpu.sync_copy(i_hbm.at[pl.ds(base, SC_CHUNK)], idx0)
        gather_copy(0).start()

        @pl.loop(0, n_chunks, step=2)
        def _(c0):
            for slot in range(2):
                c = c0 + slot
                nxt = 1 - slot

                @pl.when(c + 1 < n_chunks)
                def _():
                    pltpu.sync_copy(i_hbm.at[pl.ds(base + (c + 1) * SC_CHUNK, SC_CHUNK)], idx[nxt])

                    @pl.when(c >= 1)
                    def _():
                        write_copy(c - 1, nxt).wait()

                    gather_copy(nxt).start()

                gather_copy(slot).wait()
                write_copy(c, slot).start()

        write_copy(n_chunks - 2, 0).wait()
        write_copy(n_chunks - 1, 1).wait()

    return gather(table, idx)


def _moe_mlp_kernel(layer, be_ref, nv_ref, nu_ref, seg_ref, nxt_ref,
                    x_ref, w1_hbm, b1_ref, w2_hbm, b2_ref, o_ref, w1f, w2f, w1b, w2b, sems):
    i = pl.program_id(0)

    def fetch(e, slot):
        return (pltpu.make_async_copy(w1_hbm.at[layer, e], w1f.at[slot], sems.at[slot, 0]),
                pltpu.make_async_copy(w2_hbm.at[layer, e], w2f.at[slot], sems.at[slot, 1]))

    @pl.when(i == 0)
    def _():
        for cp in fetch(be_ref[0], 0):
            cp.start()

    slot = seg_ref[i]

    @pl.when(slot >= 0)
    def _():
        for cp in fetch(be_ref[i], slot):
            cp.wait()
        w1b[...] = w1f[slot].astype(BF16)
        w2b[...] = w2f[slot].astype(BF16)

        @pl.when(nxt_ref[i] >= 0)
        def _():
            for cp in fetch(nxt_ref[i], 1 - slot):
                cp.start()

    @pl.when(i < nu_ref[0])
    def _():
        dff = w2b.shape[0]
        row = lax.broadcasted_iota(jnp.int32, x_ref.shape, 0)
        xw = jnp.where(row < nv_ref[i], x_ref[...], jnp.uint32(0))
        xb = _unpack_bf16_pairs(xw).astype(BF16)
        h = jnp.dot(xb, w1b[...], preferred_element_type=F32) + b1_ref[...]
        g = jnp.minimum(h[:, :dff], SWIGLU_LIMIT)
        lin = jnp.clip(h[:, dff:], -SWIGLU_LIMIT, SWIGLU_LIMIT)
        act = (g * jax.nn.sigmoid(SWIGLU_ALPHA * g) * (lin + 1.0)).astype(BF16)
        y = jnp.dot(act, w2b[...], preferred_element_type=F32) + b2_ref[...]
        o_ref[...] = _pack_bf16_pairs(y)


def moe_mlp(xp, block_expert, n_valid, n_used, seg_slot, next_expert, layer, w1, b1, w2, b2):
    n_rows, dh = xp.shape
    _, n_e, d, two_dff = w1.shape
    dff = two_dff // 2
    n_blocks = n_rows // MOE_BLOCK
    row = lambda i, be, nv, nu, sg, nx: (jnp.minimum(i, nu[0] - 1), 0)
    bsel = lambda i, be, nv, nu, sg, nx: (layer, be[i], 0, 0)
    grid_spec = pltpu.PrefetchScalarGridSpec(
        num_scalar_prefetch=5,
        grid=(n_blocks,),
        in_specs=[
            pl.BlockSpec((MOE_BLOCK, dh), row),
            pl.BlockSpec(memory_space=pl.ANY),
            pl.BlockSpec((None, None, 1, two_dff), bsel),
            pl.BlockSpec(memory_space=pl.ANY),
            pl.BlockSpec((None, None, 1, d), bsel),
        ],
        out_specs=pl.BlockSpec((MOE_BLOCK, dh), row),
        scratch_shapes=[pltpu.VMEM((2, d, two_dff), F32), pltpu.VMEM((2, dff, d), F32),
                        pltpu.VMEM((d, two_dff), BF16), pltpu.VMEM((dff, d), BF16),
                        pltpu.SemaphoreType.DMA((2, 2))],
    )
    return pl.pallas_call(
        functools.partial(_moe_mlp_kernel, layer),
        out_shape=jax.ShapeDtypeStruct((n_rows, dh), U32),
        grid_spec=grid_spec,
        compiler_params=pltpu.CompilerParams(
            dimension_semantics=("arbitrary",), vmem_limit_bytes=VMEM_LIMIT),
        name="moe_mlp",
    )(block_expert, n_valid, n_used, seg_slot, next_expert, xp, w1, b1.reshape(b1.shape[0], n_e, 1, two_dff),
      w2, b2.reshape(b2.shape[0], n_e, 1, d))


def _combine_kernel(x1_ref, y0_ref, y1_ref, y2_ref, y3_ref, rg_ref, g2_ref, b2_ref, *rest):
    o_ref = rest[-1]
    rg = rg_ref[...]
    f = rg[:, 0:1] * _unpack_bf16_pairs(y0_ref[...])
    for k, y_ref in ((1, y1_ref), (2, y2_ref), (3, y3_ref)):
        f = f + rg[:, k:k + 1] * _unpack_bf16_pairs(y_ref[...])
    o_ref[...] = _ln(DN_ALPHA * x1_ref[...] + f, g2_ref[...], b2_ref[...])


def combine_block(x1, y4p, rg, g2, b2, part, partial_out):
    t, d = x1.shape
    tm = TOKEN_TILE
    ntp = t // tm // COMBINE_PARTS
    base = part * ntp
    tok = lambda w: pl.BlockSpec((tm, w), lambda i: (base + i, 0))
    y_specs = [pl.BlockSpec((tm, d // 2), functools.partial(lambda i, k: (k * ntp + i, 0), k=k))
               for k in range(TOP_K)]
    in_specs = [tok(d)] + y_specs + [tok(LANES), _const_spec((1, d)), _const_spec((1, d))]
    args = [x1, y4p, y4p, y4p, y4p, rg, g2.reshape(1, d), b2.reshape(1, d)]
    aliases = {}
    if partial_out is not None:
        in_specs.append(pl.BlockSpec(memory_space=pl.ANY))
        args.append(partial_out)
        aliases = {len(args) - 1: 0}
    return pl.pallas_call(
        _combine_kernel,
        out_shape=jax.ShapeDtypeStruct((t, d), F32),
        grid=(ntp,),
        in_specs=in_specs,
        out_specs=tok(d),
        input_output_aliases=aliases,
        compiler_params=pltpu.CompilerParams(
            dimension_semantics=("arbitrary",), vmem_limit_bytes=VMEM_LIMIT),
        name="moe_combine",
    )(*args)


def moe_block(x1, x1p, ri, rg, cnt, layer, w1, b1, w2, b2, g2, b2n):
    t, d = x1.shape
    n_assign = t * TOP_K
    n_blocks = -(-n_assign // MOE_BLOCK) + N_EXPERTS
    n_rows = n_blocks * MOE_BLOCK
    ar = jnp.arange(N_EXPERTS, dtype=jnp.int32)
    counts = cnt[:, 0].astype(jnp.int32)
    padded = (counts + MOE_BLOCK - 1) // MOE_BLOCK * MOE_BLOCK
    pend = jnp.sum(jnp.where(ar[None, :] <= ar[:, None], padded[None, :], 0), axis=1)
    pstart = pend - padded
    top_i = ri[:TOP_K]
    dest = jnp.sum(jnp.where(top_i[:, :, None] == ar, pstart, 0), axis=-1) + ri[TOP_K:]
    dest_km = dest.reshape(-1)
    n_used = (pend[-1] // MOE_BLOCK).reshape(1)
    block_start = jnp.arange(n_blocks, dtype=jnp.int32) * MOE_BLOCK
    block_expert = jnp.minimum(
        jnp.sum((pend[None, :] <= block_start[:, None]).astype(jnp.int32), axis=1), N_EXPERTS - 1)
    last = jnp.sum(jnp.where(jnp.arange(n_blocks) == n_used[0] - 1, block_expert, 0))
    block_expert = jnp.where(jnp.arange(n_blocks) < n_used[0], block_expert, last)
    vend = jnp.sum(jnp.where(block_expert[:, None] == ar, (pstart + counts)[None, :], 0), axis=1)
    n_valid = jnp.clip(vend - block_start, 0, MOE_BLOCK)
    blk = jnp.arange(n_blocks)
    starts = (blk < n_used[0]) & ((blk == 0) | (block_expert != jnp.roll(block_expert, 1)))
    seg_slot = jnp.where(starts, (jnp.cumsum(starts.astype(jnp.int32)) - 1) % 2, -1).astype(jnp.int32)
    later = (counts[None, :] > 0) & (ar[None, :] > ar[:, None])
    next_of = jnp.min(jnp.where(later, ar[None, :], N_EXPERTS), axis=1)
    next_of = jnp.where(next_of == N_EXPERTS, -1, next_of)
    next_expert = jnp.sum(jnp.where(block_expert[:, None] == ar, next_of[None, :], 0), axis=1).astype(jnp.int32)

    xp = sc_scatter_rows(x1p, dest_km, n_rows)
    yp = moe_mlp(xp, block_expert, n_valid, n_used, seg_slot, next_expert, layer, w1, b1, w2, b2)
    tp = t // COMBINE_PARTS
    out = None
    for part in range(COMBINE_PARTS):
        y4p = sc_gather_rows(yp, dest[:, part * tp:(part + 1) * tp].reshape(-1))
        out = combine_block(x1, y4p, rg, g2, b2n, part, out)
    return out


def kernel(x, a_w_in, a_b_in, a_ln_g, a_ln_b, a_w_s, a_b_s, a_w_out, a_b_out, b_w_in, b_conv_w, b_conv_b, b_dt_bias, b_a_log, b_d, b_norm_w, b_w_out, moe_w_router, moe_b_router, moe_w1, moe_b1, moe_w2, moe_b2, ln1_g, ln1_b, ln2_g, ln2_b):
    bsz, s_len, d = x.shape
    t = bsz * s_len
    xt = x.reshape(t, d)
    for i in range(DEPTH):
        j = i // 2
        w_r32 = jnp.pad(moe_w_router[i], ((0, 0), (0, LANES - N_EXPERTS)))
        w_r_hi = w_r32.astype(BF16)
        w_r = jnp.stack([w_r_hi, (w_r32 - w_r_hi.astype(F32)).astype(BF16)])
        b_r = moe_b_router[i].reshape(N_EXPERTS, 1)
        if i % 2 == 0:
            x1, x1p, ri, rg, cnt = mixer_a_block(
                xt, a_w_in[j], a_b_in[j], a_ln_g[j], a_ln_b[j], a_w_s[j], a_b_s[j],
                a_w_out[j], a_b_out[j], ln1_g[i], ln1_b[i], w_r, b_r)
        else:
            x3 = xt.reshape(bsz, s_len, d)
            x1, x1p, ri, rg, cnt = mamba_block(
                x3, b_w_in[j], b_conv_w[j], b_conv_b[j], b_dt_bias[j], b_a_log[j], b_d[j], b_norm_w[j],
                b_w_out[j], ln1_g[i], ln1_b[i], w_r, b_r)
            x1 = x1.reshape(t, d)
            x1p = x1p.reshape(t, d // 2)
            rg = rg.reshape(t, LANES)
        xt = moe_block(x1, x1p, ri, rg, cnt, i, moe_w1, moe_b1, moe_w2, moe_b2, ln2_g[i], ln2_b[i])
    return xt.reshape(bsz, s_len, d)
```

```python
import functools
import math

import jax
import jax.numpy as jnp
from jax import lax
from jax.experimental import pallas as pl
from jax.experimental.pallas import tpu as pltpu
from jax.experimental.pallas import tpu_sc as plsc

F32 = jnp.float32
BF16 = jnp.bfloat16
U32 = jnp.uint32

DEPTH = 4
N_EXPERTS = 32
TOP_K = 4
MOE_BLOCK = 512
SWIGLU_LIMIT = 7.0
SWIGLU_ALPHA = 1.702
A_BLOCK = 128
A_GROUPS = 8
CHUNK = 64
SSD_BLOCK = 128
SSM_HEADS = 32
SSM_HEAD_DIM = 64
SSM_GROUPS = 4
SSM_STATE = 128
SSM_CONV = 4
DN_ALPHA = (2 * DEPTH) ** 0.25
LN_EPS = 1e-5
LOG2E = 1.0 / math.log(2.0)
RMS_EPS = 1e-5

LANES = 128
SUBLANES = 8
TOKEN_TILE = 256
VMEM_LIMIT = 56 * 1024 * 1024
SC_CHUNK = 64
CONV_COLS = 256
COMBINE_PARTS = 4


def _ln(x, g, b):
    mu = jnp.mean(x, axis=-1, keepdims=True)
    xc = x - mu
    var = jnp.mean(xc * xc, axis=-1, keepdims=True)
    return xc * lax.rsqrt(var + LN_EPS) * g + b


def _gelu(x):
    return 0.5 * x * (1.0 + lax.erf(x * (1.0 / math.sqrt(2.0))))


def _silu(x):
    h = 0.5 * x
    return h + h * jnp.tanh(h)


def _pack_bf16_pairs(y):
    h = y.shape[1] // 2
    lo = lax.bitcast_convert_type(y[:, :h].astype(BF16).astype(F32), U32) >> 16
    hi = lax.bitcast_convert_type(y[:, h:].astype(BF16).astype(F32), U32) & jnp.uint32(0xFFFF0000)
    return hi | lo


def _unpack_bf16_pairs(w):
    lo = lax.bitcast_convert_type(w << 16, F32)
    hi = lax.bitcast_convert_type(w & jnp.uint32(0xFFFF0000), F32)
    return jnp.concatenate([lo, hi], axis=1)


def _route(x1, first, wr_ref, brc_ref, ri_ref, rg_ref, cnt_ref, cnt_scr):
    @pl.when(first)
    def _():
        cnt_scr[...] = jnp.zeros(cnt_scr.shape, F32)

    x_hi = x1.astype(BF16)
    x_lo = (x1 - x_hi.astype(F32)).astype(BF16)
    logits_tok = (jnp.dot(x_hi, wr_ref[0], preferred_element_type=F32)
                  + (jnp.dot(x_hi, wr_ref[1], preferred_element_type=F32)
                     + jnp.dot(x_lo, wr_ref[0], preferred_element_type=F32)))
    n_e = brc_ref.shape[0]
    logits = logits_tok.T[:n_e, :] + brc_ref[...]
    tm = logits.shape[1]
    sub_f = lax.broadcasted_iota(jnp.int32, logits.shape, 0).astype(F32)
    vals, idxs, hots = [], [], []
    l = logits
    for _ in range(TOP_K):
        m = jnp.max(l, axis=0, keepdims=True)
        i = jnp.min(jnp.where(l == m, sub_f, float(n_e)), axis=0, keepdims=True)
        hot = sub_f == i
        vals.append(m)
        idxs.append(i)
        hots.append(hot.astype(F32))
        l = jnp.where(hot, -jnp.inf, l)
    es = [jnp.exp(v - vals[0]) for v in vals]
    tot = es[0] + es[1] + es[2] + es[3]
    cnt = (hots[0] + hots[1] + hots[2] + hots[3]).astype(BF16)
    r_i = lax.broadcasted_iota(jnp.int32, (tm, tm), 0)
    c_i = lax.broadcasted_iota(jnp.int32, (tm, tm), 1)
    before = jnp.dot(cnt, (r_i < c_i).astype(BF16), preferred_element_type=F32)
    before = before + jnp.concatenate([cnt_scr[...]] * (tm // LANES), axis=1)
    row = lax.broadcasted_iota(jnp.int32, (2 * TOP_K, tm), 0)
    ri = jnp.zeros((2 * TOP_K, tm), F32)
    rg = jnp.zeros((2 * TOP_K, tm), F32)
    for k in range(TOP_K):
        rank = jnp.sum(before * hots[k], axis=0, keepdims=True)
        ri = jnp.where(row == k, idxs[k], ri)
        ri = jnp.where(row == TOP_K + k, rank, ri)
        rg = jnp.where(row == k, es[k] / tot, rg)
    ri_ref[...] = ri.astype(jnp.int32)
    rg_ref[...] = jnp.concatenate([rg, jnp.zeros((LANES - 2 * TOP_K, tm), F32)], axis=0).T
    total = cnt_scr[...] + jnp.dot(cnt, jnp.ones((tm, LANES), BF16), preferred_element_type=F32)
    cnt_scr[...] = total
    cnt_ref[...] = total


def _const_spec(shape):
    nd = len(shape)
    return pl.BlockSpec(shape, lambda *_: (0,) * nd)


def _mixer_a_kernel(x_ref, wu_ref, wv_ref, bu_ref, bv_ref, lg_ref, lb_ref, ws_ref, bst_ref,
                    wo_ref, bo_ref, g1_ref, b1_ref, wr_ref, br_ref,
                    x1_ref, x1p_ref, ri_ref, rg_ref, cnt_ref, v_scr, o_scr, cnt_scr):
    x = x_ref[...]
    xb = x.astype(BF16)
    tm, dff = v_scr.shape
    gd = dff // A_GROUPS
    gcols = [slice(g * gd, (g + 1) * gd) for g in range(A_GROUPS)]
    s1 = jnp.zeros((tm, 1), F32)
    s2 = jnp.zeros((tm, 1), F32)
    shift = None
    for cols in gcols:
        v_g = _gelu(jnp.dot(xb, wv_ref[:, cols], preferred_element_type=F32) + bv_ref[:, cols])
        v_scr[:, cols] = v_g
        if shift is None:
            shift = jnp.mean(v_g, axis=-1, keepdims=True)
        dv = v_g - shift
        s1 = s1 + jnp.sum(dv, axis=-1, keepdims=True)
        s2 = s2 + jnp.sum(dv * dv, axis=-1, keepdims=True)
    dmu = s1 * (1.0 / dff)
    mu = shift + dmu
    rstd = lax.rsqrt(s2 * (1.0 / dff) - dmu * dmu + LN_EPS)
    pi = lax.broadcasted_iota(jnp.int32, (A_BLOCK, A_BLOCK), 0) // CHUNK
    pj = lax.broadcasted_iota(jnp.int32, (A_BLOCK, A_BLOCK), 1) // CHUNK
    mask = pj <= pi
    bst = bst_ref[...]
    for g, cols in enumerate(gcols):
        u_g = _gelu(jnp.dot(xb, wu_ref[:, cols], preferred_element_type=F32) + bu_ref[:, cols])
        vb_g = ((v_scr[:, cols] - mu) * rstd * lg_ref[:, cols] + lb_ref[:, cols]).astype(BF16)
        wm = jnp.where(mask, ws_ref[g], 0.0).astype(BF16)
        for n in range(tm // A_BLOCK):
            rows = slice(n * A_BLOCK, (n + 1) * A_BLOCK)
            sv = jnp.dot(wm, vb_g[rows, :], preferred_element_type=F32) + bst[:, g:g + 1]
            o_scr[rows, cols] = (u_g[rows, :] * sv).astype(BF16)
    m = jnp.dot(o_scr[...], wo_ref[...], preferred_element_type=F32) + bo_ref[...]
    x1 = _ln(DN_ALPHA * x + m, g1_ref[...], b1_ref[...])
    x1_ref[...] = x1
    x1p_ref[...] = _pack_bf16_pairs(x1)
    _route(x1, pl.program_id(0) == 0, wr_ref, br_ref, ri_ref, rg_ref, cnt_ref, cnt_scr)


def mixer_a_block(xt, w_in, b_in, ln_g, ln_b, w_s, b_s, w_out, b_out, g1, b1, w_r, b_r):
    t, d = xt.shape
    dff = w_out.shape[0]
    tm = TOKEN_TILE
    wu = w_in[:, :dff].astype(BF16)
    wv = w_in[:, dff:].astype(BF16)
    bu = b_in[:dff].reshape(1, dff)
    bv = b_in[dff:].reshape(1, dff)
    args = (xt, wu, wv, bu, bv, ln_g.reshape(1, dff), ln_b.reshape(1, dff), w_s, b_s.T,
            w_out.astype(BF16), b_out.reshape(1, d), g1.reshape(1, d), b1.reshape(1, d), w_r, b_r)
    tok = lambda w: pl.BlockSpec((tm, w), lambda i: (i, 0))
    route = pl.BlockSpec((2 * TOP_K, tm), lambda i: (0, i))
    in_specs = [tok(d)] + [_const_spec(a.shape) for a in args[1:]]
    out_shape = (jax.ShapeDtypeStruct((t, d), F32),
                 jax.ShapeDtypeStruct((t, d // 2), U32),
                 jax.ShapeDtypeStruct((2 * TOP_K, t), jnp.int32),
                 jax.ShapeDtypeStruct((t, LANES), F32),
                 jax.ShapeDtypeStruct((N_EXPERTS, LANES), F32))
    out_specs = (tok(d), tok(d // 2), route, tok(LANES), _const_spec((N_EXPERTS, LANES)))
    return pl.pallas_call(
        _mixer_a_kernel,
        out_shape=out_shape,
        grid=(t // tm,),
        in_specs=in_specs,
        out_specs=out_specs,
        scratch_shapes=[pltpu.VMEM((tm, dff), F32), pltpu.VMEM((tm, dff), BF16),
                        pltpu.VMEM((N_EXPERTS, LANES), F32)],
        compiler_params=pltpu.CompilerParams(
            dimension_semantics=("arbitrary",), vmem_limit_bytes=VMEM_LIMIT),
        name="mixer_a",
    )(*args)


def _split3_bf16(q):
    hi = q.astype(BF16)
    r1 = q - hi.astype(F32)
    mid = r1.astype(BF16)
    lo = (r1 - mid.astype(F32)).astype(BF16)
    return hi, mid, lo


def _mamba_kernel(x_ref, wz_ref, wx_ref, wdt_ref, cw_ref, cb_ref, dtb_ref,
                  alog_ref, dfull_ref, nw_ref, exp_ref, wo_ref, g1_ref, b1_ref, wr_ref, br_ref,
                  x1_ref, x1p_ref, ri_ref, rg_ref, cnt_ref,
                  xs_scr, bm_scr, cm_scr, z_scr, state_scr, y_scr, dtw_scr, cnt_scr, *ext_scrs):
    s = pl.program_id(1)
    tm = x_ref.shape[0]
    d_inner = xs_scr.shape[1]
    gn = bm_scr.shape[1]
    L, P, N = SSD_BLOCK, SSM_HEAD_DIM, SSM_STATE
    R = SSM_HEADS // SSM_GROUPS
    gw = R * P
    nblk = tm // L

    @pl.when(s == 0)
    def _():
        state_scr[...] = jnp.zeros(state_scr.shape, F32)
        for ext in ext_scrs:
            ext[...] = jnp.zeros(ext.shape, F32)

    x = x_ref[...]
    xb = x.astype(BF16)
    dt = jnp.dot(xb, wdt_ref[...], preferred_element_type=F32) + dtb_ref[...]
    dt = jnp.maximum(dt, 0.0) + jnp.log1p(jnp.exp(-jnp.abs(dt)))

    cw = cw_ref[...]
    cbias = cb_ref[...]
    w = CONV_COLS
    pad = SUBLANES
    n_col = len(ext_scrs) // 2
    for c, (ext, qext) in enumerate(zip(ext_scrs[:n_col], ext_scrs[n_col:])):
        cols = slice(c * w, (c + 1) * w)
        ext[0:pad, :] = jnp.where(s > 0, ext[tm:tm + pad, :], 0.0)
        ext[pad:pad + tm, :] = jnp.dot(xb, wx_ref[:, cols], preferred_element_type=F32)
        x0 = ext[pad:pad + tm, :]
        xm1 = ext[pad - 1:pad - 1 + tm, :]
        q = cw[1:2, cols] * x0 + cw[0:1, cols] * xm1
        qext[0:pad, :] = jnp.where(s > 0, qext[tm:tm + pad, :], 0.0)
        qext[pad:pad + tm, :] = q
        acc = cbias[:, cols] + (cw[3:4, cols] * x0 + cw[2:3, cols] * xm1) + qext[pad - 2:pad - 2 + tm, :]
        act = _silu(acc).astype(BF16)
        lo = c * w
        if lo < d_inner:
            xs_scr[:, lo:lo + w] = act
        elif lo < d_inner + gn:
            bm_scr[:, lo - d_inner:lo - d_inner + w] = act
        else:
            cm_scr[:, lo - d_inner - gn:lo - d_inner - gn + w] = act

    def z_task(c):
        cols = slice(c * w, (c + 1) * w)
        z_scr[:, cols] = jnp.dot(xb, wz_ref[:, cols], preferred_element_type=F32).astype(BF16)

    z_tasks = [functools.partial(z_task, c) for c in range(d_inner // w)]

    a_row = -jnp.exp(alog_ref[...])
    adt = dt * a_row
    r_i = lax.broadcasted_iota(jnp.int32, (tm, tm), 0)
    c_i = lax.broadcasted_iota(jnp.int32, (tm, tm), 1)
    btril = ((r_i >= c_i) & (r_i // L == c_i // L)).astype(F32)
    acum = jnp.dot(btril, adt, preferred_element_type=F32,
                   precision=lax.Precision.HIGHEST)
    row_blk = lax.broadcasted_iota(jnp.int32, (tm, LANES), 0) // L
    alast = [acum[(b + 1) * L - 1:(b + 1) * L, :] for b in range(nblk)]
    alast_rows = alast[0]
    for b in range(1, nblk):
        alast_rows = jnp.where(row_blk == b, alast[b], alast_rows)
    dtw = dt * jnp.exp(alast_rows - acum)
    expand = exp_ref[...]
    dtw_scr[...] = jnp.dot(dtw.astype(BF16), expand, preferred_element_type=F32)
    acum2 = acum * LOG2E
    arow_t = acum2.T - jnp.log(dt.T) * LOG2E
    tril = lax.broadcasted_iota(jnp.int32, (L, L), 0) >= lax.broadcasted_iota(jnp.int32, (L, L), 1)
    lo_half = lax.broadcasted_iota(jnp.int32, (L, 2 * P), 1) < P

    for b in range(nblk):
        rows = slice(b * L, (b + 1) * L)
        acum_b = acum2[rows, :]
        elast = jnp.broadcast_to(jnp.exp(alast[b]), (SUBLANES, LANES))
        dec_all = sum(jnp.dot(piece, expand, preferred_element_type=F32)
                      for piece in _split3_bf16(elast))[0:1, :]
        for g in range(SSM_GROUPS):
            c_g = cm_scr[rows, g * N:(g + 1) * N]
            b_g = bm_scr[rows, g * N:(g + 1) * N]
            cb = lax.dot_general(c_g, b_g, (((1,), (1,)), ((), ())), preferred_element_type=F32)
            st = state_scr[g]
            y_off = jnp.dot(c_g, st.astype(BF16), preferred_element_type=F32)
            for q in range(R // 2):
                h0 = g * R + 2 * q
                ms, es = [], []
                for h in (h0, h0 + 1):
                    acol = jnp.broadcast_to(acum_b[:, h:h + 1], (L, L))
                    seg = acol - arow_t[h:h + 1, rows]
                    ms.append((cb * jnp.exp2(jnp.where(tril, seg, -jnp.inf))).astype(BF16))
                    es.append(jnp.exp2(acol))
                lhs = jnp.concatenate(ms, axis=1)
                cols = slice(h0 * P, (h0 + 2) * P)
                xp = xs_scr[rows, cols]
                zero = jnp.zeros_like(xp)
                rhs = jnp.concatenate([jnp.where(lo_half, xp, zero), jnp.where(lo_half, zero, xp)], axis=0)
                y_d = jnp.dot(lhs, rhs, preferred_element_type=F32)
                scale = jnp.where(lo_half, es[0], es[1])
                y_scr[rows, cols] = y_d + y_off[:, 2 * q * P:(2 * q + 2) * P] * scale
            gcols = slice(g * gw, (g + 1) * gw)
            xw = (xs_scr[rows, gcols].astype(F32) * dtw_scr[rows, gcols]).astype(BF16)
            contrib = lax.dot_general(b_g, xw, (((0,), (0,)), ((), ())), preferred_element_type=F32)
            state_scr[g] = st * dec_all[:, gcols] + contrib
            if z_tasks:
                z_tasks.pop(0)()
    for task in z_tasks:
        task()

    y = y_scr[...] + dfull_ref[...] * xs_scr[...].astype(F32)
    y = y * _silu(z_scr[...].astype(F32))
    nw = nw_ref[...]
    parts = []
    for g in range(SSM_GROUPS):
        yg = y[:, g * gw:(g + 1) * gw]
        yg = yg * lax.rsqrt(jnp.mean(yg * yg, axis=-1, keepdims=True) + RMS_EPS)
        parts.append((yg * nw[:, g * gw:(g + 1) * gw]).astype(BF16))
    yb = jnp.concatenate(parts, axis=1)
    m = jnp.dot(yb, wo_ref[...], preferred_element_type=F32)
    x1 = _ln(DN_ALPHA * x + m, g1_ref[...], b1_ref[...])
    x1_ref[...] = x1
    x1p_ref[...] = _pack_bf16_pairs(x1)
    first = (pl.program_id(0) == 0) & (s == 0)
    _route(x1, first, wr_ref, br_ref, ri_ref, rg_ref, cnt_ref, cnt_scr)


def mamba_block(x3, w_in, conv_w, conv_b, dt_bias, a_log, d_skip, norm_w, w_out, g1, b1, w_r, b_r):
    bsz, s_len, d = x3.shape
    d_inner = SSM_HEADS * SSM_HEAD_DIM
    gn = SSM_GROUPS * SSM_STATE
    conv_dim = d_inner + 2 * gn
    tm = TOKEN_TILE
    wz = w_in[:, :d_inner].astype(BF16)
    wx = w_in[:, d_inner:d_inner + conv_dim].astype(BF16)
    wdt = jnp.pad(w_in[:, d_inner + conv_dim:], ((0, 0), (0, LANES - SSM_HEADS))).astype(BF16)
    dtb = jnp.pad(dt_bias, (0, LANES - SSM_HEADS)).reshape(1, LANES)
    alog = jnp.pad(a_log, (0, LANES - SSM_HEADS)).reshape(1, LANES)
    dfull = jnp.repeat(d_skip, SSM_HEAD_DIM).reshape(1, d_inner)
    head_of_col = jnp.arange(d_inner, dtype=jnp.int32) // SSM_HEAD_DIM
    expand = (jnp.arange(LANES, dtype=jnp.int32)[:, None] == head_of_col[None, :]).astype(BF16)
    args = (x3, wz, wx, wdt, conv_w, conv_b.reshape(1, conv_dim), dtb, alog, dfull,
            norm_w.reshape(1, d_inner), expand, w_out.astype(BF16), g1.reshape(1, d), b1.reshape(1, d), w_r, b_r)
    tok = lambda w: pl.BlockSpec((None, tm, w), lambda b, s: (b, s, 0))
    in_specs = [tok(d)] + [_const_spec(a.shape) for a in args[1:]]
    ns = s_len // tm
    route = pl.BlockSpec((2 * TOP_K, tm), lambda b, s: (0, b * ns + s))
    out_shape = (jax.ShapeDtypeStruct((bsz, s_len, d), F32),
                 jax.ShapeDtypeStruct((bsz, s_len, d // 2), U32),
                 jax.ShapeDtypeStruct((2 * TOP_K, bsz * s_len), jnp.int32),
                 jax.ShapeDtypeStruct((bsz, s_len, LANES), F32),
                 jax.ShapeDtypeStruct((N_EXPERTS, LANES), F32))
    out_specs = (tok(d), tok(d // 2), route, tok(LANES), _const_spec((N_EXPERTS, LANES)))
    return pl.pallas_call(
        _mamba_kernel,
        out_shape=out_shape,
        grid=(bsz, ns),
        in_specs=in_specs,
        out_specs=out_specs,
        scratch_shapes=[pltpu.VMEM((tm, d_inner), BF16), pltpu.VMEM((tm, gn), BF16), pltpu.VMEM((tm, gn), BF16),
                        pltpu.VMEM((tm, d_inner), BF16),
                        pltpu.VMEM((SSM_GROUPS, SSM_STATE, d_inner // SSM_GROUPS), F32),
                        pltpu.VMEM((tm, d_inner), F32),
                        pltpu.VMEM((tm, d_inner), F32),
                        pltpu.VMEM((N_EXPERTS, LANES), F32)] +
                       [pltpu.VMEM((tm + SUBLANES, CONV_COLS), F32) for _ in range(2 * conv_dim // CONV_COLS)],
        compiler_params=pltpu.CompilerParams(
            dimension_semantics=("arbitrary", "arbitrary"), vmem_limit_bytes=VMEM_LIMIT),
        name="mamba",
    )(*args)


def _sc_workers():
    info = plsc.get_sparse_core_info()
    return info.num_cores, info.num_subcores


def sc_scatter_rows(x, dest_km, n_rows):
    t, d = x.shape
    nc, ns = _sc_workers()
    per_w, rem = divmod(t, nc * ns)
    assert rem == 0 and per_w % SC_CHUNK == 0
    mesh = plsc.VectorSubcoreMesh(core_axis_name="core", subcore_axis_name="subcore")

    @functools.partial(
        pl.kernel,
        out_type=jax.ShapeDtypeStruct((n_rows, d), x.dtype),
        mesh=mesh,
        scratch_types=[pltpu.VMEM((SC_CHUNK,), jnp.int32) for _ in range(TOP_K)] +
                      [pltpu.VMEM((SC_CHUNK, d), x.dtype), pltpu.SemaphoreType.DMA],
    )
    def scatter(x_hbm, i_hbm, o_hbm, i0, i1, i2, i3, rows_v, sem):
        wid = lax.axis_index("subcore") * nc + lax.axis_index("core")
        base = wid * per_w
        idx = (i0, i1, i2, i3)

        @pl.loop(0, per_w // SC_CHUNK)
        def _(c):
            off = base + c * SC_CHUNK
            pltpu.sync_copy(x_hbm.at[pl.ds(off, SC_CHUNK)], rows_v)
            for k in range(TOP_K):
                pltpu.sync_copy(i_hbm.at[pl.ds(k * t + off, SC_CHUNK)], idx[k])
            copies = [pltpu.make_async_copy(rows_v, o_hbm.at[idx[k]], sem) for k in range(TOP_K)]
            for cp in copies:
                cp.start()
            for cp in copies:
                cp.wait()

    return scatter(x, dest_km)


def sc_gather_rows(table, idx):
    n = idx.shape[0]
    d = table.shape[1]
    nc, ns = _sc_workers()
    per_w, rem = divmod(n, nc * ns)
    n_chunks = per_w // SC_CHUNK
    assert rem == 0 and per_w % (2 * SC_CHUNK) == 0
    mesh = plsc.VectorSubcoreMesh(core_axis_name="core", subcore_axis_name="subcore")

    @functools.partial(
        pl.kernel,
        out_type=jax.ShapeDtypeStruct((n, d), table.dtype),
        mesh=mesh,
        scratch_types=[
            pltpu.VMEM((SC_CHUNK,), jnp.int32), pltpu.VMEM((SC_CHUNK,), jnp.int32),
            pltpu.VMEM((SC_CHUNK, d), table.dtype), pltpu.VMEM((SC_CHUNK, d), table.dtype),
            pltpu.SemaphoreType.DMA, pltpu.SemaphoreType.DMA,
            pltpu.SemaphoreType.DMA, pltpu.SemaphoreType.DMA,
        ],
    )
    def gather(x_hbm, i_hbm, o_hbm, idx0, idx1, rows0, rows1, sg0, sg1, sw0, sw1):
        wid = lax.axis_index("subcore") * nc + lax.axis_index("core")
        base = wid * per_w
        idx = (idx0, idx1)
        rows = (rows0, rows1)
        sg = (sg0, sg1)
        sw = (sw0, sw1)

        def gather_copy(slot):
            return pltpu.make_async_copy(x_hbm.at[idx[slot]], rows[slot], sg[slot])

        def write_copy(c, slot):
            return pltpu.make_async_copy(rows[slot], o_hbm.at[pl.ds(base + c * SC_CHUNK, SC_CHUNK)], sw[slot])

        pltpu.sync_copy(i_hbm.at[pl.ds(base, SC_CHUNK)], idx0)
        gather_copy(0).start()

        @pl.loop(0, n_chunks, step=2)
        def _(c0):
            for slot in range(2):
                c = c0 + slot
                nxt = 1 - slot

                @pl.when(c + 1 < n_chunks)
                def _():
                    pltpu.sync_copy(i_hbm.at[pl.ds(base + (c + 1) * SC_CHUNK, SC_CHUNK)], idx[nxt])

                    @pl.when(c >= 1)
                    def _():
                        write_copy(c - 1, nxt).wait()

                    gather_copy(nxt).start()

                gather_copy(slot).wait()
                write_copy(c, slot).start()

        write_copy(n_chunks - 2, 0).wait()
        write_copy(n_chunks - 1, 1).wait()

    return gather(table, idx)


def _moe_mlp_kernel(layer, be_ref, nv_ref, nu_ref, seg_ref, nxt_ref,
                    x_ref, w1_hbm, b1_ref, w2_hbm, b2_ref, o_ref, w1f, w2f, w1b, w2b, sems):
    i = pl.program_id(0)

    def fetch(e, slot):
        return (pltpu.make_async_copy(w1_hbm.at[layer, e], w1f.at[slot], sems.at[slot, 0]),
                pltpu.make_async_copy(w2_hbm.at[layer, e], w2f.at[slot], sems.at[slot, 1]))

    @pl.when(i == 0)
    def _():
        for cp in fetch(be_ref[0], 0):
            cp.start()

    slot = seg_ref[i]

    @pl.when(slot >= 0)
    def _():
        for cp in fetch(be_ref[i], slot):
            cp.wait()
        w1b[...] = w1f[slot].astype(BF16)
        w2b[...] = w2f[slot].astype(BF16)

        @pl.when(nxt_ref[i] >= 0)
        def _():
            for cp in fetch(nxt_ref[i], 1 - slot):
                cp.start()

    nv = nv_ref[i]
    half = MOE_BLOCK // 2

    def mlp(n_rows):
        dff = w2b.shape[0]
        row = lax.broadcasted_iota(jnp.int32, (n_rows, x_ref.shape[1]), 0)
        xw = jnp.where(row < nv, x_ref[0:n_rows, :], jnp.uint32(0))
        xb = _unpack_bf16_pairs(xw).astype(BF16)
        h = jnp.dot(xb, w1b[...], preferred_element_type=F32) + b1_ref[...]
        g = jnp.minimum(h[:, :dff], SWIGLU_LIMIT)
        lin = jnp.clip(h[:, dff:], -SWIGLU_LIMIT, SWIGLU_LIMIT)
        act = (g * jax.nn.sigmoid(SWIGLU_ALPHA * g) * (lin + 1.0)).astype(BF16)
        y = jnp.dot(act, w2b[...], preferred_element_type=F32) + b2_ref[...]
        o_ref[0:n_rows, :] = _pack_bf16_pairs(y)

    pl.when(nv > half)(functools.partial(mlp, MOE_BLOCK))
    pl.when((nv > 0) & (nv <= half))(functools.partial(mlp, half))


def moe_mlp(xp, block_expert, n_valid, n_used, seg_slot, next_expert, layer, w1, b1, w2, b2):
    n_rows, dh = xp.shape
    _, n_e, d, two_dff = w1.shape
    dff = two_dff // 2
    n_blocks = n_rows // MOE_BLOCK
    row = lambda i, be, nv, nu, sg, nx: (jnp.minimum(i, nu[0] - 1), 0)
    bsel = lambda i, be, nv, nu, sg, nx: (layer, be[i], 0, 0)
    grid_spec = pltpu.PrefetchScalarGridSpec(
        num_scalar_prefetch=5,
        grid=(n_blocks,),
        in_specs=[
            pl.BlockSpec((MOE_BLOCK, dh), row),
            pl.BlockSpec(memory_space=pl.ANY),
            pl.BlockSpec((None, None, 1, two_dff), bsel),
            pl.BlockSpec(memory_space=pl.ANY),
            pl.BlockSpec((None, None, 1, d), bsel),
        ],
        out_specs=pl.BlockSpec((MOE_BLOCK, dh), row),
        scratch_shapes=[pltpu.VMEM((2, d, two_dff), F32), pltpu.VMEM((2, dff, d), F32),
                        pltpu.VMEM((d, two_dff), BF16), pltpu.VMEM((dff, d), BF16),
                        pltpu.SemaphoreType.DMA((2, 2))],
    )
    return pl.pallas_call(
        functools.partial(_moe_mlp_kernel, layer),
        out_shape=jax.ShapeDtypeStruct((n_rows, dh), U32),
        grid_spec=grid_spec,
        compiler_params=pltpu.CompilerParams(
            dimension_semantics=("arbitrary",), vmem_limit_bytes=VMEM_LIMIT),
        name="moe_mlp",
    )(block_expert, n_valid, n_used, seg_slot, next_expert, xp, w1, b1.reshape(b1.shape[0], n_e, 1, two_dff),
      w2, b2.reshape(b2.shape[0], n_e, 1, d))


def _combine_kernel(x1_ref, y0_ref, y1_ref, y2_ref, y3_ref, rg_ref, g2_ref, b2_ref, *rest):
    o_ref = rest[-1]
    rg = rg_ref[...]
    f = rg[:, 0:1] * _unpack_bf16_pairs(y0_ref[...])
    for k, y_ref in ((1, y1_ref), (2, y2_ref), (3, y3_ref)):
        f = f + rg[:, k:k + 1] * _unpack_bf16_pairs(y_ref[...])
    o_ref[...] = _ln(DN_ALPHA * x1_ref[...] + f, g2_ref[...], b2_ref[...])


def combine_block(x1, y4p, rg, g2, b2, part, partial_out):
    t, d = x1.shape
    tm = TOKEN_TILE
    ntp = t // tm // COMBINE_PARTS
    base = part * ntp
    tok = lambda w: pl.BlockSpec((tm, w), lambda i: (base + i, 0))
    y_specs = [pl.BlockSpec((tm, d // 2), functools.partial(lambda i, k: (k * ntp + i, 0), k=k))
               for k in range(TOP_K)]
    in_specs = [tok(d)] + y_specs + [tok(LANES), _const_spec((1, d)), _const_spec((1, d))]
    args = [x1, y4p, y4p, y4p, y4p, rg, g2.reshape(1, d), b2.reshape(1, d)]
    aliases = {}
    if partial_out is not None:
        in_specs.append(pl.BlockSpec(memory_space=pl.ANY))
        args.append(partial_out)
        aliases = {len(args) - 1: 0}
    return pl.pallas_call(
        _combine_kernel,
        out_shape=jax.ShapeDtypeStruct((t, d), F32),
        grid=(ntp,),
        in_specs=in_specs,
        out_specs=tok(d),
        input_output_aliases=aliases,
        compiler_params=pltpu.CompilerParams(
            dimension_semantics=("arbitrary",), vmem_limit_bytes=VMEM_LIMIT),
        name="moe_combine",
    )(*args)


def moe_block(x1, x1p, ri, rg, cnt, layer, w1, b1, w2, b2, g2, b2n):
    t, d = x1.shape
    n_assign = t * TOP_K
    n_blocks = -(-n_assign // MOE_BLOCK) + N_EXPERTS
    n_rows = n_blocks * MOE_BLOCK
    ar = jnp.arange(N_EXPERTS, dtype=jnp.int32)
    counts = cnt[:, 0].astype(jnp.int32)
    padded = (counts + MOE_BLOCK - 1) // MOE_BLOCK * MOE_BLOCK
    pend = jnp.sum(jnp.where(ar[None, :] <= ar[:, None], padded[None, :], 0), axis=1)
    pstart = pend - padded
    top_i = ri[:TOP_K]
    dest = jnp.sum(jnp.where(top_i[:, :, None] == ar, pstart, 0), axis=-1) + ri[TOP_K:]
    dest_km = dest.reshape(-1)
    n_used = (pend[-1] // MOE_BLOCK).reshape(1)
    block_start = jnp.arange(n_blocks, dtype=jnp.int32) * MOE_BLOCK
    block_expert = jnp.minimum(
        jnp.sum((pend[None, :] <= block_start[:, None]).astype(jnp.int32), axis=1), N_EXPERTS - 1)
    last = jnp.sum(jnp.where(jnp.arange(n_blocks) == n_used[0] - 1, block_expert, 0))
    block_expert = jnp.where(jnp.arange(n_blocks) < n_used[0], block_expert, last)
    vend = jnp.sum(jnp.where(block_expert[:, None] == ar, (pstart + counts)[None, :], 0), axis=1)
    n_valid = jnp.clip(vend - block_start, 0, MOE_BLOCK)
    blk = jnp.arange(n_blocks)
    starts = (blk < n_used[0]) & ((blk == 0) | (block_expert != jnp.roll(block_expert, 1)))
    seg_slot = jnp.where(starts, (jnp.cumsum(starts.astype(jnp.int32)) - 1) % 2, -1).astype(jnp.int32)
    later = (counts[None, :] > 0) & (ar[None, :] > ar[:, None])
    next_of = jnp.min(jnp.where(later, ar[None, :], N_EXPERTS), axis=1)
    next_of = jnp.where(next_of == N_EXPERTS, -1, next_of)
    next_expert = jnp.sum(jnp.where(block_expert[:, None] == ar, next_of[None, :], 0), axis=1).astype(jnp.int32)

    xp = sc_scatter_rows(x1p, dest_km, n_rows)
    yp = moe_mlp(xp, block_expert, n_valid, n_used, seg_slot, next_expert, layer, w1, b1, w2, b2)
    tp = t // COMBINE_PARTS
    out = None
    for part in range(COMBINE_PARTS):
        y4p = sc_gather_rows(yp, dest[:, part * tp:(part + 1) * tp].reshape(-1))
        out = combine_block(x1, y4p, rg, g2, b2n, part, out)
    return out


def kernel(x, a_w_in, a_b_in, a_ln_g, a_ln_b, a_w_s, a_b_s, a_w_out, a_b_out, b_w_in, b_conv_w, b_conv_b, b_dt_bias, b_a_log, b_d, b_norm_w, b_w_out, moe_w_router, moe_b_router, moe_w1, moe_b1, moe_w2, moe_b2, ln1_g, ln1_b, ln2_g, ln2_b):
    bsz, s_len, d = x.shape
    t = bsz * s_len
    xt = x.reshape(t, d)
    for i in range(DEPTH):
        j = i // 2
        w_r32 = jnp.pad(moe_w_router[i], ((0, 0), (0, LANES - N_EXPERTS)))
        w_r_hi = w_r32.astype(BF16)
        w_r = jnp.stack([w_r_hi, (w_r32 - w_r_hi.astype(F32)).astype(BF16)])
        b_r = moe_b_router[i].reshape(N_EXPERTS, 1)
        if i % 2 == 0:
            x1, x1p, ri, rg, cnt = mixer_a_block(
                xt, a_w_in[j], a_b_in[j], a_ln_g[j], a_ln_b[j], a_w_s[j], a_b_s[j],
                a_w_out[j], a_b_out[j], ln1_g[i], ln1_b[i], w_r, b_r)
        else:
            x3 = xt.reshape(bsz, s_len, d)
            x1, x1p, ri, rg, cnt = mamba_block(
                x3, b_w_in[j], b_conv_w[j], b_conv_b[j], b_dt_bias[j], b_a_log[j], b_d[j], b_norm_w[j],
                b_w_out[j], ln1_g[i], ln1_b[i], w_r, b_r)
            x1 = x1.reshape(t, d)
            x1p = x1p.reshape(t, d // 2)
            rg = rg.reshape(t, LANES)
        xt = moe_block(x1, x1p, ri, rg, cnt, i, moe_w1, moe_b1, moe_w2, moe_b2, ln2_g[i], ln2_b[i])
    return xt.reshape(bsz, s_len, d)
```

```python
import functools
import math

import jax
import jax.numpy as jnp
from jax import lax
from jax.experimental import pallas as pl
from jax.experimental.pallas import tpu as pltpu
from jax.experimental.pallas import tpu_sc as plsc

F32 = jnp.float32
BF16 = jnp.bfloat16
U32 = jnp.uint32

DEPTH = 4
N_EXPERTS = 32
TOP_K = 4
MOE_BLOCK = 512
MOE_STEP = 128
SWIGLU_LIMIT = 7.0
SWIGLU_ALPHA = 1.702
A_BLOCK = 128
A_GROUPS = 8
CHUNK = 64
SSD_BLOCK = 128
SSM_HEADS = 32
SSM_HEAD_DIM = 64
SSM_GROUPS = 4
SSM_STATE = 128
SSM_CONV = 4
DN_ALPHA = (2 * DEPTH) ** 0.25
LN_EPS = 1e-5
LOG2E = 1.0 / math.log(2.0)
RMS_EPS = 1e-5

LANES = 128
SUBLANES = 8
TOKEN_TILE = 256
VMEM_LIMIT = 56 * 1024 * 1024
SC_CHUNK = 64
CONV_COLS = 256
COMBINE_PARTS = 4


def _ln(x, g, b):
    mu = jnp.mean(x, axis=-1, keepdims=True)
    xc = x - mu
    var = jnp.mean(xc * xc, axis=-1, keepdims=True)
    return xc * lax.rsqrt(var + LN_EPS) * g + b


def _gelu(x):
    return 0.5 * x * (1.0 + lax.erf(x * (1.0 / math.sqrt(2.0))))


def _silu(x):
    h = 0.5 * x
    return h + h * jnp.tanh(h)


def _pack_bf16_pairs(y):
    h = y.shape[1] // 2
    lo = lax.bitcast_convert_type(y[:, :h].astype(BF16).astype(F32), U32) >> 16
    hi = lax.bitcast_convert_type(y[:, h:].astype(BF16).astype(F32), U32) & jnp.uint32(0xFFFF0000)
    return hi | lo


def _unpack_bf16_pairs(w):
    lo = lax.bitcast_convert_type(w << 16, F32)
    hi = lax.bitcast_convert_type(w & jnp.uint32(0xFFFF0000), F32)
    return jnp.concatenate([lo, hi], axis=1)


def _route(x1, first, wr_ref, brc_ref, ri_ref, rg_ref, cnt_ref, cnt_scr):
    @pl.when(first)
    def _():
        cnt_scr[...] = jnp.zeros(cnt_scr.shape, F32)

    x_hi = x1.astype(BF16)
    x_lo = (x1 - x_hi.astype(F32)).astype(BF16)
    logits_tok = (jnp.dot(x_hi, wr_ref[0], preferred_element_type=F32)
                  + (jnp.dot(x_hi, wr_ref[1], preferred_element_type=F32)
                     + jnp.dot(x_lo, wr_ref[0], preferred_element_type=F32)))
    n_e = brc_ref.shape[0]
    logits = logits_tok.T[:n_e, :] + brc_ref[...]
    tm = logits.shape[1]
    sub_f = lax.broadcasted_iota(jnp.int32, logits.shape, 0).astype(F32)
    vals, idxs, hots = [], [], []
    l = logits
    for _ in range(TOP_K):
        m = jnp.max(l, axis=0, keepdims=True)
        i = jnp.min(jnp.where(l == m, sub_f, float(n_e)), axis=0, keepdims=True)
        hot = sub_f == i
        vals.append(m)
        idxs.append(i)
        hots.append(hot.astype(F32))
        l = jnp.where(hot, -jnp.inf, l)
    es = [jnp.exp(v - vals[0]) for v in vals]
    tot = es[0] + es[1] + es[2] + es[3]
    cnt = (hots[0] + hots[1] + hots[2] + hots[3]).astype(BF16)
    r_i = lax.broadcasted_iota(jnp.int32, (tm, tm), 0)
    c_i = lax.broadcasted_iota(jnp.int32, (tm, tm), 1)
    before = jnp.dot(cnt, (r_i < c_i).astype(BF16), preferred_element_type=F32)
    before = before + jnp.concatenate([cnt_scr[...]] * (tm // LANES), axis=1)
    row = lax.broadcasted_iota(jnp.int32, (2 * TOP_K, tm), 0)
    ri = jnp.zeros((2 * TOP_K, tm), F32)
    rg = jnp.zeros((2 * TOP_K, tm), F32)
    for k in range(TOP_K):
        rank = jnp.sum(before * hots[k], axis=0, keepdims=True)
        ri = jnp.where(row == k, idxs[k], ri)
        ri = jnp.where(row == TOP_K + k, rank, ri)
        rg = jnp.where(row == k, es[k] / tot, rg)
    ri_ref[...] = ri.astype(jnp.int32)
    rg_ref[...] = jnp.concatenate([rg, jnp.zeros((LANES - 2 * TOP_K, tm), F32)], axis=0).T
    total = cnt_scr[...] + jnp.dot(cnt, jnp.ones((tm, LANES), BF16), preferred_element_type=F32)
    cnt_scr[...] = total
    cnt_ref[...] = total


def _const_spec(shape):
    nd = len(shape)
    return pl.BlockSpec(shape, lambda *_: (0,) * nd)


def _mixer_a_kernel(x_ref, wu_ref, wv_ref, bu_ref, bv_ref, lg_ref, lb_ref, ws_ref, bst_ref,
                    wo_ref, bo_ref, g1_ref, b1_ref, wr_ref, br_ref,
                    x1_ref, x1p_ref, ri_ref, rg_ref, cnt_ref, v_scr, o_scr, cnt_scr):
    x = x_ref[...]
    xb = x.astype(BF16)
    tm, dff = v_scr.shape
    gd = dff // A_GROUPS
    gcols = [slice(g * gd, (g + 1) * gd) for g in range(A_GROUPS)]
    s1 = jnp.zeros((tm, 1), F32)
    s2 = jnp.zeros((tm, 1), F32)
    shift = None
    for cols in gcols:
        v_g = _gelu(jnp.dot(xb, wv_ref[:, cols], preferred_element_type=F32) + bv_ref[:, cols])
        v_scr[:, cols] = v_g
        if shift is None:
            shift = jnp.mean(v_g, axis=-1, keepdims=True)
        dv = v_g - shift
        s1 = s1 + jnp.sum(dv, axis=-1, keepdims=True)
        s2 = s2 + jnp.sum(dv * dv, axis=-1, keepdims=True)
    dmu = s1 * (1.0 / dff)
    mu = shift + dmu
    rstd = lax.rsqrt(s2 * (1.0 / dff) - dmu * dmu + LN_EPS)
    pi = lax.broadcasted_iota(jnp.int32, (A_BLOCK, A_BLOCK), 0) // CHUNK
    pj = lax.broadcasted_iota(jnp.int32, (A_BLOCK, A_BLOCK), 1) // CHUNK
    mask = pj <= pi
    bst = bst_ref[...]
    for g, cols in enumerate(gcols):
        u_g = _gelu(jnp.dot(xb, wu_ref[:, cols], preferred_element_type=F32) + bu_ref[:, cols])
        vb_g = ((v_scr[:, cols] - mu) * rstd * lg_ref[:, cols] + lb_ref[:, cols]).astype(BF16)
        wm = jnp.where(mask, ws_ref[g], 0.0).astype(BF16)
        for n in range(tm // A_BLOCK):
            rows = slice(n * A_BLOCK, (n + 1) * A_BLOCK)
            sv = jnp.dot(wm, vb_g[rows, :], preferred_element_type=F32) + bst[:, g:g + 1]
            o_scr[rows, cols] = (u_g[rows, :] * sv).astype(BF16)
    m = jnp.dot(o_scr[...], wo_ref[...], preferred_element_type=F32) + bo_ref[...]
    x1 = _ln(DN_ALPHA * x + m, g1_ref[...], b1_ref[...])
    x1_ref[...] = x1
    x1p_ref[...] = _pack_bf16_pairs(x1)
    _route(x1, pl.program_id(0) == 0, wr_ref, br_ref, ri_ref, rg_ref, cnt_ref, cnt_scr)


def mixer_a_block(xt, w_in, b_in, ln_g, ln_b, w_s, b_s, w_out, b_out, g1, b1, w_r, b_r):
    t, d = xt.shape
    dff = w_out.shape[0]
    tm = TOKEN_TILE
    wu = w_in[:, :dff].astype(BF16)
    wv = w_in[:, dff:].astype(BF16)
    bu = b_in[:dff].reshape(1, dff)
    bv = b_in[dff:].reshape(1, dff)
    args = (xt, wu, wv, bu, bv, ln_g.reshape(1, dff), ln_b.reshape(1, dff), w_s, b_s.T,
            w_out.astype(BF16), b_out.reshape(1, d), g1.reshape(1, d), b1.reshape(1, d), w_r, b_r)
    tok = lambda w: pl.BlockSpec((tm, w), lambda i: (i, 0))
    route = pl.BlockSpec((2 * TOP_K, tm), lambda i: (0, i))
    in_specs = [tok(d)] + [_const_spec(a.shape) for a in args[1:]]
    out_shape = (jax.ShapeDtypeStruct((t, d), F32),
                 jax.ShapeDtypeStruct((t, d // 2), U32),
                 jax.ShapeDtypeStruct((2 * TOP_K, t), jnp.int32),
                 jax.ShapeDtypeStruct((t, LANES), F32),
                 jax.ShapeDtypeStruct((N_EXPERTS, LANES), F32))
    out_specs = (tok(d), tok(d // 2), route, tok(LANES), _const_spec((N_EXPERTS, LANES)))
    return pl.pallas_call(
        _mixer_a_kernel,
        out_shape=out_shape,
        grid=(t // tm,),
        in_specs=in_specs,
        out_specs=out_specs,
        scratch_shapes=[pltpu.VMEM((tm, dff), F32), pltpu.VMEM((tm, dff), BF16),
                        pltpu.VMEM((N_EXPERTS, LANES), F32)],
        compiler_params=pltpu.CompilerParams(
            dimension_semantics=("arbitrary",), vmem_limit_bytes=VMEM_LIMIT),
        name="mixer_a",
    )(*args)


def _split3_bf16(q):
    hi = q.astype(BF16)
    r1 = q - hi.astype(F32)
    mid = r1.astype(BF16)
    lo = (r1 - mid.astype(F32)).astype(BF16)
    return hi, mid, lo


def _mamba_kernel(x_ref, wz_ref, wx_ref, wdt_ref, cw_ref, cb_ref, dtb_ref,
                  alog_ref, dfull_ref, nw_ref, exp_ref, wo_ref, g1_ref, b1_ref, wr_ref, br_ref,
                  x1_ref, x1p_ref, ri_ref, rg_ref, cnt_ref,
                  xs_scr, bm_scr, cm_scr, z_scr, state_scr, y_scr, dtw_scr, cnt_scr, *ext_scrs):
    s = pl.program_id(1)
    tm = x_ref.shape[0]
    d_inner = xs_scr.shape[1]
    gn = bm_scr.shape[1]
    L, P, N = SSD_BLOCK, SSM_HEAD_DIM, SSM_STATE
    R = SSM_HEADS // SSM_GROUPS
    gw = R * P
    nblk = tm // L

    @pl.when(s == 0)
    def _():
        state_scr[...] = jnp.zeros(state_scr.shape, F32)
        for ext in ext_scrs:
            ext[...] = jnp.zeros(ext.shape, F32)

    x = x_ref[...]
    xb = x.astype(BF16)
    dt = jnp.dot(xb, wdt_ref[...], preferred_element_type=F32) + dtb_ref[...]
    dt = jnp.maximum(dt, 0.0) + jnp.log1p(jnp.exp(-jnp.abs(dt)))

    cw = cw_ref[...]
    cbias = cb_ref[...]
    w = CONV_COLS
    pad = SUBLANES
    n_col = len(ext_scrs) // 2
    for c, (ext, qext) in enumerate(zip(ext_scrs[:n_col], ext_scrs[n_col:])):
        cols = slice(c * w, (c + 1) * w)
        ext[0:pad, :] = jnp.where(s > 0, ext[tm:tm + pad, :], 0.0)
        ext[pad:pad + tm, :] = jnp.dot(xb, wx_ref[:, cols], preferred_element_type=F32)
        x0 = ext[pad:pad + tm, :]
        xm1 = ext[pad - 1:pad - 1 + tm, :]
        q = cw[1:2, cols] * x0 + cw[0:1, cols] * xm1
        qext[0:pad, :] = jnp.where(s > 0, qext[tm:tm + pad, :], 0.0)
        qext[pad:pad + tm, :] = q
        acc = cbias[:, cols] + (cw[3:4, cols] * x0 + cw[2:3, cols] * xm1) + qext[pad - 2:pad - 2 + tm, :]
        act = _silu(acc).astype(BF16)
        lo = c * w
        if lo < d_inner:
            xs_scr[:, lo:lo + w] = act
        elif lo < d_inner + gn:
            bm_scr[:, lo - d_inner:lo - d_inner + w] = act
        else:
            cm_scr[:, lo - d_inner - gn:lo - d_inner - gn + w] = act

    def z_task(c):
        cols = slice(c * w, (c + 1) * w)
        z_scr[:, cols] = jnp.dot(xb, wz_ref[:, cols], preferred_element_type=F32).astype(BF16)

    z_tasks = [functools.partial(z_task, c) for c in range(d_inner // w)]

    a_row = -jnp.exp(alog_ref[...])
    adt = dt * a_row
    r_i = lax.broadcasted_iota(jnp.int32, (tm, tm), 0)
    c_i = lax.broadcasted_iota(jnp.int32, (tm, tm), 1)
    btril = ((r_i >= c_i) & (r_i // L == c_i // L)).astype(F32)
    acum = jnp.dot(btril, adt, preferred_element_type=F32,
                   precision=lax.Precision.HIGHEST)
    row_blk = lax.broadcasted_iota(jnp.int32, (tm, LANES), 0) // L
    alast = [acum[(b + 1) * L - 1:(b + 1) * L, :] for b in range(nblk)]
    alast_rows = alast[0]
    for b in range(1, nblk):
        alast_rows = jnp.where(row_blk == b, alast[b], alast_rows)
    dtw = dt * jnp.exp(alast_rows - acum)
    expand = exp_ref[...]
    dtw_scr[...] = jnp.dot(dtw.astype(BF16), expand, preferred_element_type=F32)
    acum2 = acum * LOG2E
    arow_t = acum2.T - jnp.log(dt.T) * LOG2E
    tril = lax.broadcasted_iota(jnp.int32, (L, L), 0) >= lax.broadcasted_iota(jnp.int32, (L, L), 1)
    lo_half = lax.broadcasted_iota(jnp.int32, (L, 2 * P), 1) < P

    for b in range(nblk):
        rows = slice(b * L, (b + 1) * L)
        acum_b = acum2[rows, :]
        elast = jnp.broadcast_to(jnp.exp(alast[b]), (SUBLANES, LANES))
        dec_all = sum(jnp.dot(piece, expand, preferred_element_type=F32)
                      for piece in _split3_bf16(elast))[0:1, :]
        for g in range(SSM_GROUPS):
            c_g = cm_scr[rows, g * N:(g + 1) * N]
            b_g = bm_scr[rows, g * N:(g + 1) * N]
            cb = lax.dot_general(c_g, b_g, (((1,), (1,)), ((), ())), preferred_element_type=F32)
            st = state_scr[g]
            y_off = jnp.dot(c_g, st.astype(BF16), preferred_element_type=F32)
            for q in range(R // 2):
                h0 = g * R + 2 * q
                ms, es = [], []
                for h in (h0, h0 + 1):
                    acol = jnp.broadcast_to(acum_b[:, h:h + 1], (L, L))
                    seg = acol - arow_t[h:h + 1, rows]
                    ms.append((cb * jnp.exp2(jnp.where(tril, seg, -jnp.inf))).astype(BF16))
                    es.append(jnp.exp2(acol))
                lhs = jnp.concatenate(ms, axis=1)
                cols = slice(h0 * P, (h0 + 2) * P)
                xp = xs_scr[rows, cols]
                zero = jnp.zeros_like(xp)
                rhs = jnp.concatenate([jnp.where(lo_half, xp, zero), jnp.where(lo_half, zero, xp)], axis=0)
                y_d = jnp.dot(lhs, rhs, preferred_element_type=F32)
                scale = jnp.where(lo_half, es[0], es[1])
                y_scr[rows, cols] = y_d + y_off[:, 2 * q * P:(2 * q + 2) * P] * scale
            gcols = slice(g * gw, (g + 1) * gw)
            xw = (xs_scr[rows, gcols].astype(F32) * dtw_scr[rows, gcols]).astype(BF16)
            contrib = lax.dot_general(b_g, xw, (((0,), (0,)), ((), ())), preferred_element_type=F32)
            state_scr[g] = st * dec_all[:, gcols] + contrib
            if z_tasks:
                z_tasks.pop(0)()
    for task in z_tasks:
        task()

    y = y_scr[...] + dfull_ref[...] * xs_scr[...].astype(F32)
    y = y * _silu(z_scr[...].astype(F32))
    nw = nw_ref[...]
    parts = []
    for g in range(SSM_GROUPS):
        yg = y[:, g * gw:(g + 1) * gw]
        yg = yg * lax.rsqrt(jnp.mean(yg * yg, axis=-1, keepdims=True) + RMS_EPS)
        parts.append((yg * nw[:, g * gw:(g + 1) * gw]).astype(BF16))
    yb = jnp.concatenate(parts, axis=1)
    m = jnp.dot(yb, wo_ref[...], preferred_element_type=F32)
    x1 = _ln(DN_ALPHA * x + m, g1_ref[...], b1_ref[...])
    x1_ref[...] = x1
    x1p_ref[...] = _pack_bf16_pairs(x1)
    first = (pl.program_id(0) == 0) & (s == 0)
    _route(x1, first, wr_ref, br_ref, ri_ref, rg_ref, cnt_ref, cnt_scr)


def mamba_block(x3, w_in, conv_w, conv_b, dt_bias, a_log, d_skip, norm_w, w_out, g1, b1, w_r, b_r):
    bsz, s_len, d = x3.shape
    d_inner = SSM_HEADS * SSM_HEAD_DIM
    gn = SSM_GROUPS * SSM_STATE
    conv_dim = d_inner + 2 * gn
    tm = TOKEN_TILE
    wz = w_in[:, :d_inner].astype(BF16)
    wx = w_in[:, d_inner:d_inner + conv_dim].astype(BF16)
    wdt = jnp.pad(w_in[:, d_inner + conv_dim:], ((0, 0), (0, LANES - SSM_HEADS))).astype(BF16)
    dtb = jnp.pad(dt_bias, (0, LANES - SSM_HEADS)).reshape(1, LANES)
    alog = jnp.pad(a_log, (0, LANES - SSM_HEADS)).reshape(1, LANES)
    dfull = jnp.repeat(d_skip, SSM_HEAD_DIM).reshape(1, d_inner)
    head_of_col = jnp.arange(d_inner, dtype=jnp.int32) // SSM_HEAD_DIM
    expand = (jnp.arange(LANES, dtype=jnp.int32)[:, None] == head_of_col[None, :]).astype(BF16)
    args = (x3, wz, wx, wdt, conv_w, conv_b.reshape(1, conv_dim), dtb, alog, dfull,
            norm_w.reshape(1, d_inner), expand, w_out.astype(BF16), g1.reshape(1, d), b1.reshape(1, d), w_r, b_r)
    tok = lambda w: pl.BlockSpec((None, tm, w), lambda b, s: (b, s, 0))
    in_specs = [tok(d)] + [_const_spec(a.shape) for a in args[1:]]
    ns = s_len // tm
    route = pl.BlockSpec((2 * TOP_K, tm), lambda b, s: (0, b * ns + s))
    out_shape = (jax.ShapeDtypeStruct((bsz, s_len, d), F32),
                 jax.ShapeDtypeStruct((bsz, s_len, d // 2), U32),
                 jax.ShapeDtypeStruct((2 * TOP_K, bsz * s_len), jnp.int32),
                 jax.ShapeDtypeStruct((bsz, s_len, LANES), F32),
                 jax.ShapeDtypeStruct((N_EXPERTS, LANES), F32))
    out_specs = (tok(d), tok(d // 2), route, tok(LANES), _const_spec((N_EXPERTS, LANES)))
    return pl.pallas_call(
        _mamba_kernel,
        out_shape=out_shape,
        grid=(bsz, ns),
        in_specs=in_specs,
        out_specs=out_specs,
        scratch_shapes=[pltpu.VMEM((tm, d_inner), BF16), pltpu.VMEM((tm, gn), BF16), pltpu.VMEM((tm, gn), BF16),
                        pltpu.VMEM((tm, d_inner), BF16),
                        pltpu.VMEM((SSM_GROUPS, SSM_STATE, d_inner // SSM_GROUPS), F32),
                        pltpu.VMEM((tm, d_inner), F32),
                        pltpu.VMEM((tm, d_inner), F32),
                        pltpu.VMEM((N_EXPERTS, LANES), F32)] +
                       [pltpu.VMEM((tm + SUBLANES, CONV_COLS), F32) for _ in range(2 * conv_dim // CONV_COLS)],
        compiler_params=pltpu.CompilerParams(
            dimension_semantics=("arbitrary", "arbitrary"), vmem_limit_bytes=VMEM_LIMIT),
        name="mamba",
    )(*args)


def _sc_workers():
    info = plsc.get_sparse_core_info()
    return info.num_cores, info.num_subcores


def sc_scatter_rows(x, dest_km, n_rows):
    t, d = x.shape
    nc, ns = _sc_workers()
    per_w, rem = divmod(t, nc * ns)
    assert rem == 0 and per_w % SC_CHUNK == 0
    mesh = plsc.VectorSubcoreMesh(core_axis_name="core", subcore_axis_name="subcore")

    @functools.partial(
        pl.kernel,
        out_type=jax.ShapeDtypeStruct((n_rows, d), x.dtype),
        mesh=mesh,
        scratch_types=[pltpu.VMEM((SC_CHUNK,), jnp.int32) for _ in range(TOP_K)] +
                      [pltpu.VMEM((SC_CHUNK, d), x.dtype), pltpu.SemaphoreType.DMA],
    )
    def scatter(x_hbm, i_hbm, o_hbm, i0, i1, i2, i3, rows_v, sem):
        wid = lax.axis_index("subcore") * nc + lax.axis_index("core")
        base = wid * per_w
        idx = (i0, i1, i2, i3)

        @pl.loop(0, per_w // SC_CHUNK)
        def _(c):
            off = base + c * SC_CHUNK
            pltpu.sync_copy(x_hbm.at[pl.ds(off, SC_CHUNK)], rows_v)
            for k in range(TOP_K):
                pltpu.sync_copy(i_hbm.at[pl.ds(k * t + off, SC_CHUNK)], idx[k])
            copies = [pltpu.make_async_copy(rows_v, o_hbm.at[idx[k]], sem) for k in range(TOP_K)]
            for cp in copies:
                cp.start()
            for cp in copies:
                cp.wait()

    return scatter(x, dest_km)


def sc_gather_rows(table, idx):
    n = idx.shape[0]
    d = table.shape[1]
    nc, ns = _sc_workers()
    per_w, rem = divmod(n, nc * ns)
    n_chunks = per_w // SC_CHUNK
    assert rem == 0 and per_w % (2 * SC_CHUNK) == 0
    mesh = plsc.VectorSubcoreMesh(core_axis_name="core", subcore_axis_name="subcore")

    @functools.partial(
        pl.kernel,
        out_type=jax.ShapeDtypeStruct((n, d), table.dtype),
        mesh=mesh,
        scratch_types=[
            pltpu.VMEM((SC_CHUNK,), jnp.int32), pltpu.VMEM((SC_CHUNK,), jnp.int32),
            pltpu.VMEM((SC_CHUNK, d), table.dtype), pltpu.VMEM((SC_CHUNK, d), table.dtype),
            pltpu.SemaphoreType.DMA, pltpu.SemaphoreType.DMA,
            pltpu.SemaphoreType.DMA, pltpu.SemaphoreType.DMA,
        ],
    )
    def gather(x_hbm, i_hbm, o_hbm, idx0, idx1, rows0, rows1, sg0, sg1, sw0, sw1):
        wid = lax.axis_index("subcore") * nc + lax.axis_index("core")
        base = wid * per_w
        idx = (idx0, idx1)
        rows = (rows0, rows1)
        sg = (sg0, sg1)
        sw = (sw0, sw1)

        def gather_copy(slot):
            return pltpu.make_async_copy(x_hbm.at[idx[slot]], rows[slot], sg[slot])

        def write_copy(c, slot):
            return pltpu.make_async_copy(rows[slot], o_hbm.at[pl.ds(base + c * SC_CHUNK, SC_CHUNK)], sw[slot])

        pltpu.sync_copy(i_hbm.at[pl.ds(base, SC_CHUNK)], idx0)
        gather_copy(0).start()

        @pl.loop(0, n_chunks, step=2)
        def _(c0):
            for slot in range(2):
                c = c0 + slot
                nxt = 1 - slot

                @pl.when(c + 1 < n_chunks)
                def _():
                    pltpu.sync_copy(i_hbm.at[pl.ds(base + (c + 1) * SC_CHUNK, SC_CHUNK)], idx[nxt])

                    @pl.when(c >= 1)
                    def _():
                        write_copy(c - 1, nxt).wait()

                    gather_copy(nxt).start()

                gather_copy(slot).wait()
                write_copy(c, slot).start()

        write_copy(n_chunks - 2, 0).wait()
        write_copy(n_chunks - 1, 1).wait()

    return gather(table, idx)


def _moe_mlp_kernel(layer, be_ref, nv_ref, nu_ref, seg_ref, nxt_ref,
                    x_ref, w1_hbm, b1_ref, w2_hbm, b2_ref, o_ref, w1f, w2f, w1b, w2b, sems):
    i = pl.program_id(0)

    def fetch(e, slot):
        return (pltpu.make_async_copy(w1_hbm.at[layer, e], w1f.at[slot], sems.at[slot, 0]),
                pltpu.make_async_copy(w2_hbm.at[layer, e], w2f.at[slot], sems.at[slot, 1]))

    @pl.when(i == 0)
    def _():
        for cp in fetch(be_ref[0], 0):
            cp.start()

    slot = seg_ref[i]

    @pl.when(slot >= 0)
    def _():
        for cp in fetch(be_ref[i], slot):
            cp.wait()
        w1b[...] = w1f[slot].astype(BF16)
        w2b[...] = w2f[slot].astype(BF16)

        @pl.when(nxt_ref[i] >= 0)
        def _():
            for cp in fetch(nxt_ref[i], 1 - slot):
                cp.start()

    nv = nv_ref[i]

    def mlp(n_rows):
        dff = w2b.shape[0]
        row = lax.broadcasted_iota(jnp.int32, (n_rows, x_ref.shape[1]), 0)
        xw = jnp.where(row < nv, x_ref[0:n_rows, :], jnp.uint32(0))
        xb = _unpack_bf16_pairs(xw).astype(BF16)
        h = jnp.dot(xb, w1b[...], preferred_element_type=F32) + b1_ref[...]
        g = jnp.minimum(h[:, :dff], SWIGLU_LIMIT)
        lin = jnp.clip(h[:, dff:], -SWIGLU_LIMIT, SWIGLU_LIMIT)
        act = (g * jax.nn.sigmoid(SWIGLU_ALPHA * g) * (lin + 1.0)).astype(BF16)
        y = jnp.dot(act, w2b[...], preferred_element_type=F32) + b2_ref[...]
        o_ref[0:n_rows, :] = _pack_bf16_pairs(y)

    for n_rows in range(MOE_STEP, MOE_BLOCK + 1, MOE_STEP):
        pl.when((nv > n_rows - MOE_STEP) & (nv <= n_rows))(functools.partial(mlp, n_rows))


def moe_mlp(xp, block_expert, n_valid, n_used, seg_slot, next_expert, layer, w1, b1, w2, b2):
    n_rows, dh = xp.shape
    _, n_e, d, two_dff = w1.shape
    dff = two_dff // 2
    n_blocks = n_rows // MOE_BLOCK
    row = lambda i, be, nv, nu, sg, nx: (jnp.minimum(i, nu[0] - 1), 0)
    bsel = lambda i, be, nv, nu, sg, nx: (layer, be[i], 0, 0)
    grid_spec = pltpu.PrefetchScalarGridSpec(
        num_scalar_prefetch=5,
        grid=(n_blocks,),
        in_specs=[
            pl.BlockSpec((MOE_BLOCK, dh), row),
            pl.BlockSpec(memory_space=pl.ANY),
            pl.BlockSpec((None, None, 1, two_dff), bsel),
            pl.BlockSpec(memory_space=pl.ANY),
            pl.BlockSpec((None, None, 1, d), bsel),
        ],
        out_specs=pl.BlockSpec((MOE_BLOCK, dh), row),
        scratch_shapes=[pltpu.VMEM((2, d, two_dff), F32), pltpu.VMEM((2, dff, d), F32),
                        pltpu.VMEM((d, two_dff), BF16), pltpu.VMEM((dff, d), BF16),
                        pltpu.SemaphoreType.DMA((2, 2))],
    )
    return pl.pallas_call(
        functools.partial(_moe_mlp_kernel, layer),
        out_shape=jax.ShapeDtypeStruct((n_rows, dh), U32),
        grid_spec=grid_spec,
        compiler_params=pltpu.CompilerParams(
            dimension_semantics=("arbitrary",), vmem_limit_bytes=VMEM_LIMIT),
        name="moe_mlp",
    )(block_expert, n_valid, n_used, seg_slot, next_expert, xp, w1, b1.reshape(b1.shape[0], n_e, 1, two_dff),
      w2, b2.reshape(b2.shape[0], n_e, 1, d))


def _combine_kernel(x1_ref, y0_ref, y1_ref, y2_ref, y3_ref, rg_ref, g2_ref, b2_ref, *rest):
    o_ref = rest[-1]
    rg = rg_ref[...]
    f = rg[:, 0:1] * _unpack_bf16_pairs(y0_ref[...])
    for k, y_ref in ((1, y1_ref), (2, y2_ref), (3, y3_ref)):
        f = f + rg[:, k:k + 1] * _unpack_bf16_pairs(y_ref[...])
    o_ref[...] = _ln(DN_ALPHA * x1_ref[...] + f, g2_ref[...], b2_ref[...])


def combine_block(x1, y4p, rg, g2, b2, part, partial_out):
    t, d = x1.shape
    tm = TOKEN_TILE
    ntp = t // tm // COMBINE_PARTS
    base = part * ntp
    tok = lambda w: pl.BlockSpec((tm, w), lambda i: (base + i, 0))
    y_specs = [pl.BlockSpec((tm, d // 2), functools.partial(lambda i, k: (k * ntp + i, 0), k=k))
               for k in range(TOP_K)]
    in_specs = [tok(d)] + y_specs + [tok(LANES), _const_spec((1, d)), _const_spec((1, d))]
    args = [x1, y4p, y4p, y4p, y4p, rg, g2.reshape(1, d), b2.reshape(1, d)]
    aliases = {}
    if partial_out is not None:
        in_specs.append(pl.BlockSpec(memory_space=pl.ANY))
        args.append(partial_out)
        aliases = {len(args) - 1: 0}
    return pl.pallas_call(
        _combine_kernel,
        out_shape=jax.ShapeDtypeStruct((t, d), F32),
        grid=(ntp,),
        in_specs=in_specs,
        out_specs=tok(d),
        input_output_aliases=aliases,
        compiler_params=pltpu.CompilerParams(
            dimension_semantics=("arbitrary",), vmem_limit_bytes=VMEM_LIMIT),
        name="moe_combine",
    )(*args)


def moe_block(x1, x1p, ri, rg, cnt, layer, w1, b1, w2, b2, g2, b2n):
    t, d = x1.shape
    n_assign = t * TOP_K
    n_blocks = -(-n_assign // MOE_BLOCK) + N_EXPERTS
    n_rows = n_blocks * MOE_BLOCK
    ar = jnp.arange(N_EXPERTS, dtype=jnp.int32)
    counts = cnt[:, 0].astype(jnp.int32)
    padded = (counts + MOE_BLOCK - 1) // MOE_BLOCK * MOE_BLOCK
    pend = jnp.sum(jnp.where(ar[None, :] <= ar[:, None], padded[None, :], 0), axis=1)
    pstart = pend - padded
    top_i = ri[:TOP_K]
    dest = jnp.sum(jnp.where(top_i[:, :, None] == ar, pstart, 0), axis=-1) + ri[TOP_K:]
    dest_km = dest.reshape(-1)
    n_used = (pend[-1] // MOE_BLOCK).reshape(1)
    block_start = jnp.arange(n_blocks, dtype=jnp.int32) * MOE_BLOCK
    block_expert = jnp.minimum(
        jnp.sum((pend[None, :] <= block_start[:, None]).astype(jnp.int32), axis=1), N_EXPERTS - 1)
    last = jnp.sum(jnp.where(jnp.arange(n_blocks) == n_used[0] - 1, block_expert, 0))
    block_expert = jnp.where(jnp.arange(n_blocks) < n_used[0], block_expert, last)
    vend = jnp.sum(jnp.where(block_expert[:, None] == ar, (pstart + counts)[None, :], 0), axis=1)
    n_valid = jnp.clip(vend - block_start, 0, MOE_BLOCK)
    blk = jnp.arange(n_blocks)
    starts = (blk < n_used[0]) & ((blk == 0) | (block_expert != jnp.roll(block_expert, 1)))
    seg_slot = jnp.where(starts, (jnp.cumsum(starts.astype(jnp.int32)) - 1) % 2, -1).astype(jnp.int32)
    later = (counts[None, :] > 0) & (ar[None, :] > ar[:, None])
    next_of = jnp.min(jnp.where(later, ar[None, :], N_EXPERTS), axis=1)
    next_of = jnp.where(next_of == N_EXPERTS, -1, next_of)
    next_expert = jnp.sum(jnp.where(block_expert[:, None] == ar, next_of[None, :], 0), axis=1).astype(jnp.int32)

    xp = sc_scatter_rows(x1p, dest_km, n_rows)
    yp = moe_mlp(xp, block_expert, n_valid, n_used, seg_slot, next_expert, layer, w1, b1, w2, b2)
    tp = t // COMBINE_PARTS
    out = None
    for part in range(COMBINE_PARTS):
        y4p = sc_gather_rows(yp, dest[:, part * tp:(part + 1) * tp].reshape(-1))
        out = combine_block(x1, y4p, rg, g2, b2n, part, out)
    return out


def kernel(x, a_w_in, a_b_in, a_ln_g, a_ln_b, a_w_s, a_b_s, a_w_out, a_b_out, b_w_in, b_conv_w, b_conv_b, b_dt_bias, b_a_log, b_d, b_norm_w, b_w_out, moe_w_router, moe_b_router, moe_w1, moe_b1, moe_w2, moe_b2, ln1_g, ln1_b, ln2_g, ln2_b):
    bsz, s_len, d = x.shape
    t = bsz * s_len
    xt = x.reshape(t, d)
    for i in range(DEPTH):
        j = i // 2
        w_r32 = jnp.pad(moe_w_router[i], ((0, 0), (0, LANES - N_EXPERTS)))
        w_r_hi = w_r32.astype(BF16)
        w_r = jnp.stack([w_r_hi, (w_r32 - w_r_hi.astype(F32)).astype(BF16)])
        b_r = moe_b_router[i].reshape(N_EXPERTS, 1)
        if i % 2 == 0:
            x1, x1p, ri, rg, cnt = mixer_a_block(
                xt, a_w_in[j], a_b_in[j], a_ln_g[j], a_ln_b[j], a_w_s[j], a_b_s[j],
                a_w_out[j], a_b_out[j], ln1_g[i], ln1_b[i], w_r, b_r)
        else:
            x3 = xt.reshape(bsz, s_len, d)
            x1, x1p, ri, rg, cnt = mamba_block(
                x3, b_w_in[j], b_conv_w[j], b_conv_b[j], b_dt_bias[j], b_a_log[j], b_d[j], b_norm_w[j],
                b_w_out[j], ln1_g[i], ln1_b[i], w_r, b_r)
            x1 = x1.reshape(t, d)
            x1p = x1p.reshape(t, d // 2)
            rg = rg.reshape(t, LANES)
        xt = moe_block(x1, x1p, ri, rg, cnt, i, moe_w1, moe_b1, moe_w2, moe_b2, ln2_g[i], ln2_b[i])
    return xt.reshape(bsz, s_len, d)
```

```python
import functools
import math

import jax
import jax.numpy as jnp
from jax import lax
from jax.experimental import pallas as pl
from jax.experimental.pallas import tpu as pltpu
from jax.experimental.pallas import tpu_sc as plsc

F32 = jnp.float32
BF16 = jnp.bfloat16
U32 = jnp.uint32

DEPTH = 4
N_EXPERTS = 32
TOP_K = 4
MOE_BLOCK = 512
SWIGLU_LIMIT = 7.0
SWIGLU_ALPHA = 1.702
A_BLOCK = 128
A_GROUPS = 8
CHUNK = 64
SSD_BLOCK = 128
SSM_HEADS = 32
SSM_HEAD_DIM = 64
SSM_GROUPS = 4
SSM_STATE = 128
SSM_CONV = 4
DN_ALPHA = (2 * DEPTH) ** 0.25
LN_EPS = 1e-5
LOG2E = 1.0 / math.log(2.0)
RMS_EPS = 1e-5

LANES = 128
SUBLANES = 8
TOKEN_TILE = 256
VMEM_LIMIT = 56 * 1024 * 1024
SC_CHUNK = 64
CONV_COLS = 256
COMBINE_PARTS = 4


def _ln(x, g, b):
    mu = jnp.mean(x, axis=-1, keepdims=True)
    xc = x - mu
    var = jnp.mean(xc * xc, axis=-1, keepdims=True)
    return xc * lax.rsqrt(var + LN_EPS) * g + b


def _gelu(x):
    return 0.5 * x * (1.0 + lax.erf(x * (1.0 / math.sqrt(2.0))))


def _silu(x):
    h = 0.5 * x
    return h + h * jnp.tanh(h)


def _pack_bf16_pairs(y):
    h = y.shape[1] // 2
    lo = lax.bitcast_convert_type(y[:, :h].astype(BF16).astype(F32), U32) >> 16
    hi = lax.bitcast_convert_type(y[:, h:].astype(BF16).astype(F32), U32) & jnp.uint32(0xFFFF0000)
    return hi | lo


def _unpack_bf16_pairs(w):
    lo = lax.bitcast_convert_type(w << 16, F32)
    hi = lax.bitcast_convert_type(w & jnp.uint32(0xFFFF0000), F32)
    return jnp.concatenate([lo, hi], axis=1)


def _route(x1, first, wr_ref, brc_ref, ri_ref, rg_ref, cnt_ref, cnt_scr):
    @pl.when(first)
    def _():
        cnt_scr[...] = jnp.zeros(cnt_scr.shape, F32)

    x_hi = x1.astype(BF16)
    x_lo = (x1 - x_hi.astype(F32)).astype(BF16)
    logits_tok = (jnp.dot(x_hi, wr_ref[0], preferred_element_type=F32)
                  + (jnp.dot(x_hi, wr_ref[1], preferred_element_type=F32)
                     + jnp.dot(x_lo, wr_ref[0], preferred_element_type=F32)))
    n_e = brc_ref.shape[0]
    logits = logits_tok.T[:n_e, :] + brc_ref[...]
    tm = logits.shape[1]
    sub_f = lax.broadcasted_iota(jnp.int32, logits.shape, 0).astype(F32)
    vals, idxs, hots = [], [], []
    l = logits
    for _ in range(TOP_K):
        m = jnp.max(l, axis=0, keepdims=True)
        i = jnp.min(jnp.where(l == m, sub_f, float(n_e)), axis=0, keepdims=True)
        hot = sub_f == i
        vals.append(m)
        idxs.append(i)
        hots.append(hot.astype(F32))
        l = jnp.where(hot, -jnp.inf, l)
    es = [jnp.exp(v - vals[0]) for v in vals]
    tot = es[0] + es[1] + es[2] + es[3]
    cnt = (hots[0] + hots[1] + hots[2] + hots[3]).astype(BF16)
    r_i = lax.broadcasted_iota(jnp.int32, (tm, tm), 0)
    c_i = lax.broadcasted_iota(jnp.int32, (tm, tm), 1)
    before = jnp.dot(cnt, (r_i < c_i).astype(BF16), preferred_element_type=F32)
    before = before + jnp.concatenate([cnt_scr[...]] * (tm // LANES), axis=1)
    row = lax.broadcasted_iota(jnp.int32, (2 * TOP_K, tm), 0)
    ri = jnp.zeros((2 * TOP_K, tm), F32)
    rg = jnp.zeros((2 * TOP_K, tm), F32)
    for k in range(TOP_K):
        rank = jnp.sum(before * hots[k], axis=0, keepdims=True)
        ri = jnp.where(row == k, idxs[k], ri)
        ri = jnp.where(row == TOP_K + k, rank, ri)
        rg = jnp.where(row == k, es[k] / tot, rg)
    ri_ref[...] = ri.astype(jnp.int32)
    rg_ref[...] = jnp.concatenate([rg, jnp.zeros((LANES - 2 * TOP_K, tm), F32)], axis=0).T
    total = cnt_scr[...] + jnp.dot(cnt, jnp.ones((tm, LANES), BF16), preferred_element_type=F32)
    cnt_scr[...] = total
    cnt_ref[...] = total


def _const_spec(shape):
    nd = len(shape)
    return pl.BlockSpec(shape, lambda *_: (0,) * nd)


def _mixer_a_kernel(x_ref, wu_ref, wv_ref, bu_ref, bv_ref, lg_ref, lb_ref, ws_ref, bst_ref,
                    wo_ref, bo_ref, g1_ref, b1_ref, wr_ref, br_ref,
                    x1_ref, x1p_ref, ri_ref, rg_ref, cnt_ref, v_scr, o_scr, cnt_scr):
    x = x_ref[...]
    xb = x.astype(BF16)
    tm, dff = v_scr.shape
    gd = dff // A_GROUPS
    gcols = [slice(g * gd, (g + 1) * gd) for g in range(A_GROUPS)]
    s1 = jnp.zeros((tm, 1), F32)
    s2 = jnp.zeros((tm, 1), F32)
    shift = None
    for cols in gcols:
        v_g = _gelu(jnp.dot(xb, wv_ref[:, cols], preferred_element_type=F32) + bv_ref[:, cols])
        v_scr[:, cols] = v_g
        if shift is None:
            shift = jnp.mean(v_g, axis=-1, keepdims=True)
        dv = v_g - shift
        s1 = s1 + jnp.sum(dv, axis=-1, keepdims=True)
        s2 = s2 + jnp.sum(dv * dv, axis=-1, keepdims=True)
    dmu = s1 * (1.0 / dff)
    mu = shift + dmu
    rstd = lax.rsqrt(s2 * (1.0 / dff) - dmu * dmu + LN_EPS)
    pi = lax.broadcasted_iota(jnp.int32, (A_BLOCK, A_BLOCK), 0) // CHUNK
    pj = lax.broadcasted_iota(jnp.int32, (A_BLOCK, A_BLOCK), 1) // CHUNK
    mask = pj <= pi
    bst = bst_ref[...]
    for g, cols in enumerate(gcols):
        u_g = _gelu(jnp.dot(xb, wu_ref[:, cols], preferred_element_type=F32) + bu_ref[:, cols])
        vb_g = ((v_scr[:, cols] - mu) * rstd * lg_ref[:, cols] + lb_ref[:, cols]).astype(BF16)
        wm = jnp.where(mask, ws_ref[g], 0.0).astype(BF16)
        for n in range(tm // A_BLOCK):
            rows = slice(n * A_BLOCK, (n + 1) * A_BLOCK)
            sv = jnp.dot(wm, vb_g[rows, :], preferred_element_type=F32) + bst[:, g:g + 1]
            o_scr[rows, cols] = (u_g[rows, :] * sv).astype(BF16)
    m = jnp.dot(o_scr[...], wo_ref[...], preferred_element_type=F32) + bo_ref[...]
    x1 = _ln(DN_ALPHA * x + m, g1_ref[...], b1_ref[...])
    x1_ref[...] = x1
    x1p_ref[...] = _pack_bf16_pairs(x1)
    _route(x1, pl.program_id(0) == 0, wr_ref, br_ref, ri_ref, rg_ref, cnt_ref, cnt_scr)


def mixer_a_block(xt, j, w_in_b, b_in, ln_g, ln_b, w_s, b_s, w_out_b, b_out, g1, b1, w_r, b_r):
    t, d = xt.shape
    dff = w_out_b.shape[1]
    tm = TOKEN_TILE
    bu = b_in[:dff].reshape(1, dff)
    bv = b_in[dff:].reshape(1, dff)
    args = (xt, w_in_b, w_in_b, bu, bv, ln_g.reshape(1, dff), ln_b.reshape(1, dff), w_s, b_s.T,
            w_out_b, b_out.reshape(1, d), g1.reshape(1, d), b1.reshape(1, d), w_r, b_r)
    tok = lambda w: pl.BlockSpec((tm, w), lambda i: (i, 0))
    route = pl.BlockSpec((2 * TOP_K, tm), lambda i: (0, i))
    in_specs = [tok(d)] + [_const_spec(a.shape) for a in args[1:]]
    in_specs[1] = pl.BlockSpec((None, d, dff), lambda i: (j, 0, 0))
    in_specs[2] = pl.BlockSpec((None, d, dff), lambda i: (j, 0, 1))
    in_specs[9] = pl.BlockSpec((None, dff, d), lambda i: (j, 0, 0))
    out_shape = (jax.ShapeDtypeStruct((t, d), F32),
                 jax.ShapeDtypeStruct((t, d // 2), U32),
                 jax.ShapeDtypeStruct((2 * TOP_K, t), jnp.int32),
                 jax.ShapeDtypeStruct((t, LANES), F32),
                 jax.ShapeDtypeStruct((N_EXPERTS, LANES), F32))
    out_specs = (tok(d), tok(d // 2), route, tok(LANES), _const_spec((N_EXPERTS, LANES)))
    return pl.pallas_call(
        _mixer_a_kernel,
        out_shape=out_shape,
        grid=(t // tm,),
        in_specs=in_specs,
        out_specs=out_specs,
        scratch_shapes=[pltpu.VMEM((tm, dff), F32), pltpu.VMEM((tm, dff), BF16),
                        pltpu.VMEM((N_EXPERTS, LANES), F32)],
        compiler_params=pltpu.CompilerParams(
            dimension_semantics=("arbitrary",), vmem_limit_bytes=VMEM_LIMIT),
        name="mixer_a",
    )(*args)


def _split3_bf16(q):
    hi = q.astype(BF16)
    r1 = q - hi.astype(F32)
    mid = r1.astype(BF16)
    lo = (r1 - mid.astype(F32)).astype(BF16)
    return hi, mid, lo


def _mamba_kernel(x_ref, wz_ref, wxs_ref, wbc_ref, wdt_ref, cw_ref, cb_ref, dtb_ref,
                  alog_ref, dfull_ref, nw_ref, exp_ref, wo_ref, g1_ref, b1_ref, wr_ref, br_ref,
                  x1_ref, x1p_ref, ri_ref, rg_ref, cnt_ref,
                  xs_scr, bm_scr, cm_scr, z_scr, state_scr, y_scr, dtw_scr, cnt_scr, *ext_scrs):
    s = pl.program_id(1)
    tm = x_ref.shape[0]
    d_inner = xs_scr.shape[1]
    gn = bm_scr.shape[1]
    L, P, N = SSD_BLOCK, SSM_HEAD_DIM, SSM_STATE
    R = SSM_HEADS // SSM_GROUPS
    gw = R * P
    nblk = tm // L

    @pl.when(s == 0)
    def _():
        state_scr[...] = jnp.zeros(state_scr.shape, F32)
        for ext in ext_scrs:
            ext[...] = jnp.zeros(ext.shape, F32)

    x = x_ref[...]
    xb = x.astype(BF16)
    dt = jnp.dot(xb, wdt_ref[...], preferred_element_type=F32) + dtb_ref[...]
    dt = jnp.maximum(dt, 0.0) + jnp.log1p(jnp.exp(-jnp.abs(dt)))

    cw = cw_ref[...]
    cbias = cb_ref[...]
    w = CONV_COLS
    pad = SUBLANES
    n_col = len(ext_scrs) // 2
    for c, (ext, qext) in enumerate(zip(ext_scrs[:n_col], ext_scrs[n_col:])):
        cols = slice(c * w, (c + 1) * w)
        ext[0:pad, :] = jnp.where(s > 0, ext[tm:tm + pad, :], 0.0)
        w_blk = wxs_ref[:, cols] if c * w < d_inner else wbc_ref[:, c * w - d_inner:(c + 1) * w - d_inner]
        ext[pad:pad + tm, :] = jnp.dot(xb, w_blk, preferred_element_type=F32)
        x0 = ext[pad:pad + tm, :]
        xm1 = ext[pad - 1:pad - 1 + tm, :]
        q = cw[1:2, cols] * x0 + cw[0:1, cols] * xm1
        qext[0:pad, :] = jnp.where(s > 0, qext[tm:tm + pad, :], 0.0)
        qext[pad:pad + tm, :] = q
        acc = cbias[:, cols] + (cw[3:4, cols] * x0 + cw[2:3, cols] * xm1) + qext[pad - 2:pad - 2 + tm, :]
        act = _silu(acc).astype(BF16)
        lo = c * w
        if lo < d_inner:
            xs_scr[:, lo:lo + w] = act
        elif lo < d_inner + gn:
            bm_scr[:, lo - d_inner:lo - d_inner + w] = act
        else:
            cm_scr[:, lo - d_inner - gn:lo - d_inner - gn + w] = act

    def z_task(c):
        cols = slice(c * w, (c + 1) * w)
        z_scr[:, cols] = jnp.dot(xb, wz_ref[:, cols], preferred_element_type=F32).astype(BF16)

    z_tasks = [functools.partial(z_task, c) for c in range(d_inner // w)]

    a_row = -jnp.exp(alog_ref[...])
    adt = dt * a_row
    r_i = lax.broadcasted_iota(jnp.int32, (tm, tm), 0)
    c_i = lax.broadcasted_iota(jnp.int32, (tm, tm), 1)
    btril = ((r_i >= c_i) & (r_i // L == c_i // L)).astype(F32)
    acum = jnp.dot(btril, adt, preferred_element_type=F32,
                   precision=lax.Precision.HIGHEST)
    row_blk = lax.broadcasted_iota(jnp.int32, (tm, LANES), 0) // L
    alast = [acum[(b + 1) * L - 1:(b + 1) * L, :] for b in range(nblk)]
    alast_rows = alast[0]
    for b in range(1, nblk):
        alast_rows = jnp.where(row_blk == b, alast[b], alast_rows)
    dtw = dt * jnp.exp(alast_rows - acum)
    expand = exp_ref[...]
    dtw_scr[...] = jnp.dot(dtw.astype(BF16), expand, preferred_element_type=F32)
    acum2 = acum * LOG2E
    arow_t = acum2.T - jnp.log(dt.T) * LOG2E
    tril = lax.broadcasted_iota(jnp.int32, (L, L), 0) >= lax.broadcasted_iota(jnp.int32, (L, L), 1)
    lo_half = lax.broadcasted_iota(jnp.int32, (L, 2 * P), 1) < P

    for b in range(nblk):
        rows = slice(b * L, (b + 1) * L)
        acum_b = acum2[rows, :]
        elast = jnp.broadcast_to(jnp.exp(alast[b]), (SUBLANES, LANES))
        dec_all = sum(jnp.dot(piece, expand, preferred_element_type=F32)
                      for piece in _split3_bf16(elast))[0:1, :]
        for g in range(SSM_GROUPS):
            c_g = cm_scr[rows, g * N:(g + 1) * N]
            b_g = bm_scr[rows, g * N:(g + 1) * N]
            cb = lax.dot_general(c_g, b_g, (((1,), (1,)), ((), ())), preferred_element_type=F32)
            st = state_scr[g]
            y_off = jnp.dot(c_g, st.astype(BF16), preferred_element_type=F32)
            for q in range(R // 2):
                h0 = g * R + 2 * q
                ms, es = [], []
                for h in (h0, h0 + 1):
                    acol = jnp.broadcast_to(acum_b[:, h:h + 1], (L, L))
                    seg = acol - arow_t[h:h + 1, rows]
                    ms.append((cb * jnp.exp2(jnp.where(tril, seg, -jnp.inf))).astype(BF16))
                    es.append(jnp.exp2(acol))
                lhs = jnp.concatenate(ms, axis=1)
                cols = slice(h0 * P, (h0 + 2) * P)
                xp = xs_scr[rows, cols]
                zero = jnp.zeros_like(xp)
                rhs = jnp.concatenate([jnp.where(lo_half, xp, zero), jnp.where(lo_half, zero, xp)], axis=0)
                y_d = jnp.dot(lhs, rhs, preferred_element_type=F32)
                scale = jnp.where(lo_half, es[0], es[1])
                y_scr[rows, cols] = y_d + y_off[:, 2 * q * P:(2 * q + 2) * P] * scale
            gcols = slice(g * gw, (g + 1) * gw)
            xw = (xs_scr[rows, gcols].astype(F32) * dtw_scr[rows, gcols]).astype(BF16)
            contrib = lax.dot_general(b_g, xw, (((0,), (0,)), ((), ())), preferred_element_type=F32)
            state_scr[g] = st * dec_all[:, gcols] + contrib
            if z_tasks:
                z_tasks.pop(0)()
    for task in z_tasks:
        task()

    y = y_scr[...] + dfull_ref[...] * xs_scr[...].astype(F32)
    y = y * _silu(z_scr[...].astype(F32))
    nw = nw_ref[...]
    parts = []
    for g in range(SSM_GROUPS):
        yg = y[:, g * gw:(g + 1) * gw]
        yg = yg * lax.rsqrt(jnp.mean(yg * yg, axis=-1, keepdims=True) + RMS_EPS)
        parts.append((yg * nw[:, g * gw:(g + 1) * gw]).astype(BF16))
    yb = jnp.concatenate(parts, axis=1)
    m = jnp.dot(yb, wo_ref[...], preferred_element_type=F32)
    x1 = _ln(DN_ALPHA * x + m, g1_ref[...], b1_ref[...])
    x1_ref[...] = x1
    x1p_ref[...] = _pack_bf16_pairs(x1)
    first = (pl.program_id(0) == 0) & (s == 0)
    _route(x1, first, wr_ref, br_ref, ri_ref, rg_ref, cnt_ref, cnt_scr)


def mamba_block(x3, j, w_in_b, w_dt, conv_w, conv_b, dt_bias, a_log, d_skip, norm_w, w_out_b, g1, b1, w_r, b_r):
    bsz, s_len, d = x3.shape
    d_inner = SSM_HEADS * SSM_HEAD_DIM
    gn = SSM_GROUPS * SSM_STATE
    conv_dim = d_inner + 2 * gn
    tm = TOKEN_TILE
    wdt = jnp.pad(w_dt, ((0, 0), (0, LANES - SSM_HEADS))).astype(BF16)
    dtb = jnp.pad(dt_bias, (0, LANES - SSM_HEADS)).reshape(1, LANES)
    alog = jnp.pad(a_log, (0, LANES - SSM_HEADS)).reshape(1, LANES)
    dfull = jnp.repeat(d_skip, SSM_HEAD_DIM).reshape(1, d_inner)
    head_of_col = jnp.arange(d_inner, dtype=jnp.int32) // SSM_HEAD_DIM
    expand = (jnp.arange(LANES, dtype=jnp.int32)[:, None] == head_of_col[None, :]).astype(BF16)
    args = (x3, w_in_b, w_in_b, w_in_b, wdt, conv_w, conv_b.reshape(1, conv_dim), dtb, alog, dfull,
            norm_w.reshape(1, d_inner), expand, w_out_b, g1.reshape(1, d), b1.reshape(1, d), w_r, b_r)
    tok = lambda w: pl.BlockSpec((None, tm, w), lambda b, s: (b, s, 0))
    in_specs = [tok(d)] + [_const_spec(a.shape) for a in args[1:]]
    in_specs[1] = pl.BlockSpec((None, d, d_inner), lambda b, s: (j, 0, 0))
    in_specs[2] = pl.BlockSpec((None, d, d_inner), lambda b, s: (j, 0, 1))
    in_specs[3] = pl.BlockSpec((None, d, 2 * gn), lambda b, s: (j, 0, 2 * d_inner // (2 * gn)))
    in_specs[12] = pl.BlockSpec((None, d_inner, d), lambda b, s: (j, 0, 0))
    ns = s_len // tm
    route = pl.BlockSpec((2 * TOP_K, tm), lambda b, s: (0, b * ns + s))
    out_shape = (jax.ShapeDtypeStruct((bsz, s_len, d), F32),
                 jax.ShapeDtypeStruct((bsz, s_len, d // 2), U32),
                 jax.ShapeDtypeStruct((2 * TOP_K, bsz * s_len), jnp.int32),
                 jax.ShapeDtypeStruct((bsz, s_len, LANES), F32),
                 jax.ShapeDtypeStruct((N_EXPERTS, LANES), F32))
    out_specs = (tok(d), tok(d // 2), route, tok(LANES), _const_spec((N_EXPERTS, LANES)))
    return pl.pallas_call(
        _mamba_kernel,
        out_shape=out_shape,
        grid=(bsz, ns),
        in_specs=in_specs,
        out_specs=out_specs,
        scratch_shapes=[pltpu.VMEM((tm, d_inner), BF16), pltpu.VMEM((tm, gn), BF16), pltpu.VMEM((tm, gn), BF16),
                        pltpu.VMEM((tm, d_inner), BF16),
                        pltpu.VMEM((SSM_GROUPS, SSM_STATE, d_inner // SSM_GROUPS), F32),
                        pltpu.VMEM((tm, d_inner), F32),
                        pltpu.VMEM((tm, d_inner), F32),
                        pltpu.VMEM((N_EXPERTS, LANES), F32)] +
                       [pltpu.VMEM((tm + SUBLANES, CONV_COLS), F32) for _ in range(2 * conv_dim // CONV_COLS)],
        compiler_params=pltpu.CompilerParams(
            dimension_semantics=("arbitrary", "arbitrary"), vmem_limit_bytes=VMEM_LIMIT),
        name="mamba",
    )(*args)


def _sc_workers():
    info = plsc.get_sparse_core_info()
    return info.num_cores, info.num_subcores


def sc_scatter_rows(x, dest_km, n_rows):
    t, d = x.shape
    nc, ns = _sc_workers()
    per_w, rem = divmod(t, nc * ns)
    assert rem == 0 and per_w % SC_CHUNK == 0
    mesh = plsc.VectorSubcoreMesh(core_axis_name="core", subcore_axis_name="subcore")

    @functools.partial(
        pl.kernel,
        out_type=jax.ShapeDtypeStruct((n_rows, d), x.dtype),
        mesh=mesh,
        scratch_types=[pltpu.VMEM((SC_CHUNK,), jnp.int32) for _ in range(TOP_K)] +
                      [pltpu.VMEM((SC_CHUNK, d), x.dtype), pltpu.SemaphoreType.DMA],
    )
    def scatter(x_hbm, i_hbm, o_hbm, i0, i1, i2, i3, rows_v, sem):
        wid = lax.axis_index("subcore") * nc + lax.axis_index("core")
        base = wid * per_w
        idx = (i0, i1, i2, i3)

        @pl.loop(0, per_w // SC_CHUNK)
        def _(c):
            off = base + c * SC_CHUNK
            pltpu.sync_copy(x_hbm.at[pl.ds(off, SC_CHUNK)], rows_v)
            for k in range(TOP_K):
                pltpu.sync_copy(i_hbm.at[pl.ds(k * t + off, SC_CHUNK)], idx[k])
            copies = [pltpu.make_async_copy(rows_v, o_hbm.at[idx[k]], sem) for k in range(TOP_K)]
            for cp in copies:
                cp.start()
            for cp in copies:
                cp.wait()

    return scatter(x, dest_km)


def sc_gather_rows(table, idx):
    n = idx.shape[0]
    d = table.shape[1]
    nc, ns = _sc_workers()
    per_w, rem = divmod(n, nc * ns)
    n_chunks = per_w // SC_CHUNK
    assert rem == 0 and per_w % (2 * SC_CHUNK) == 0
    mesh = plsc.VectorSubcoreMesh(core_axis_name="core", subcore_axis_name="subcore")

    @functools.partial(
        pl.kernel,
        out_type=jax.ShapeDtypeStruct((n, d), table.dtype),
        mesh=mesh,
        scratch_types=[
            pltpu.VMEM((SC_CHUNK,), jnp.int32), pltpu.VMEM((SC_CHUNK,), jnp.int32),
            pltpu.VMEM((SC_CHUNK, d), table.dtype), pltpu.VMEM((SC_CHUNK, d), table.dtype),
            pltpu.SemaphoreType.DMA, pltpu.SemaphoreType.DMA,
            pltpu.SemaphoreType.DMA, pltpu.SemaphoreType.DMA,
        ],
    )
    def gather(x_hbm, i_hbm, o_hbm, idx0, idx1, rows0, rows1, sg0, sg1, sw0, sw1):
        wid = lax.axis_index("subcore") * nc + lax.axis_index("core")
        base = wid * per_w
        idx = (idx0, idx1)
        rows = (rows0, rows1)
        sg = (sg0, sg1)
        sw = (sw0, sw1)

        def gather_copy(slot):
            return pltpu.make_async_copy(x_hbm.at[idx[slot]], rows[slot], sg[slot])

        def write_copy(c, slot):
            return pltpu.make_async_copy(rows[slot], o_hbm.at[pl.ds(base + c * SC_CHUNK, SC_CHUNK)], sw[slot])

        pltpu.sync_copy(i_hbm.at[pl.ds(base, SC_CHUNK)], idx0)
        gather_copy(0).start()

        @pl.loop(0, n_chunks, step=2)
        def _(c0):
            for slot in range(2):
                c = c0 + slot
                nxt = 1 - slot

                @pl.when(c + 1 < n_chunks)
                def _():
                    pltpu.sync_copy(i_hbm.at[pl.ds(base + (c + 1) * SC_CHUNK, SC_CHUNK)], idx[nxt])

                    @pl.when(c >= 1)
                    def _():
                        write_copy(c - 1, nxt).wait()

                    gather_copy(nxt).start()

                gather_copy(slot).wait()
                write_copy(c, slot).start()

        write_copy(n_chunks - 2, 0).wait()
        write_copy(n_chunks - 1, 1).wait()

    return gather(table, idx)


def _moe_mlp_kernel(layer, be_ref, nv_ref, nu_ref, seg_ref, nxt_ref,
                    x_ref, w1_hbm, b1_ref, w2_hbm, b2_ref, o_ref, w1f, w2f, w1b, w2b, sems):
    i = pl.program_id(0)

    def fetch(e, slot):
        return (pltpu.make_async_copy(w1_hbm.at[layer, e], w1f.at[slot], sems.at[slot, 0]),
                pltpu.make_async_copy(w2_hbm.at[layer, e], w2f.at[slot], sems.at[slot, 1]))

    @pl.when(i == 0)
    def _():
        for cp in fetch(be_ref[0], 0):
            cp.start()

    slot = seg_ref[i]

    @pl.when(slot >= 0)
    def _():
        for cp in fetch(be_ref[i], slot):
            cp.wait()
        w1b[...] = w1f[slot].astype(BF16)
        w2b[...] = w2f[slot].astype(BF16)

        @pl.when(nxt_ref[i] >= 0)
        def _():
            for cp in fetch(nxt_ref[i], 1 - slot):
                cp.start()

    nv = nv_ref[i]
    half = MOE_BLOCK // 2

    def mlp(n_rows):
        dff = w2b.shape[0]
        row = lax.broadcasted_iota(jnp.int32, (n_rows, x_ref.shape[1]), 0)
        xw = jnp.where(row < nv, x_ref[0:n_rows, :], jnp.uint32(0))
        xb = _unpack_bf16_pairs(xw).astype(BF16)
        h = jnp.dot(xb, w1b[...], preferred_element_type=F32) + b1_ref[...]
        g = jnp.minimum(h[:, :dff], SWIGLU_LIMIT)
        lin = jnp.clip(h[:, dff:], -SWIGLU_LIMIT, SWIGLU_LIMIT)
        act = (g * jax.nn.sigmoid(SWIGLU_ALPHA * g) * (lin + 1.0)).astype(BF16)
        y = jnp.dot(act, w2b[...], preferred_element_type=F32) + b2_ref[...]
        o_ref[0:n_rows, :] = _pack_bf16_pairs(y)

    pl.when(nv > half)(functools.partial(mlp, MOE_BLOCK))
    pl.when((nv > 0) & (nv <= half))(functools.partial(mlp, half))


def moe_mlp(xp, block_expert, n_valid, n_used, seg_slot, next_expert, layer, w1, b1, w2, b2):
    n_rows, dh = xp.shape
    _, n_e, d, two_dff = w1.shape
    dff = two_dff // 2
    n_blocks = n_rows // MOE_BLOCK
    row = lambda i, be, nv, nu, sg, nx: (jnp.minimum(i, nu[0] - 1), 0)
    bsel = lambda i, be, nv, nu, sg, nx: (layer, be[i], 0, 0)
    grid_spec = pltpu.PrefetchScalarGridSpec(
        num_scalar_prefetch=5,
        grid=(n_blocks,),
        in_specs=[
            pl.BlockSpec((MOE_BLOCK, dh), row),
            pl.BlockSpec(memory_space=pl.ANY),
            pl.BlockSpec((None, None, 1, two_dff), bsel),
            pl.BlockSpec(memory_space=pl.ANY),
            pl.BlockSpec((None, None, 1, d), bsel),
        ],
        out_specs=pl.BlockSpec((MOE_BLOCK, dh), row),
        scratch_shapes=[pltpu.VMEM((2, d, two_dff), F32), pltpu.VMEM((2, dff, d), F32),
                        pltpu.VMEM((d, two_dff), BF16), pltpu.VMEM((dff, d), BF16),
                        pltpu.SemaphoreType.DMA((2, 2))],
    )
    return pl.pallas_call(
        functools.partial(_moe_mlp_kernel, layer),
        out_shape=jax.ShapeDtypeStruct((n_rows, dh), U32),
        grid_spec=grid_spec,
        compiler_params=pltpu.CompilerParams(
            dimension_semantics=("arbitrary",), vmem_limit_bytes=VMEM_LIMIT),
        name="moe_mlp",
    )(block_expert, n_valid, n_used, seg_slot, next_expert, xp, w1, b1.reshape(b1.shape[0], n_e, 1, two_dff),
      w2, b2.reshape(b2.shape[0], n_e, 1, d))


def _combine_kernel(x1_ref, y0_ref, y1_ref, y2_ref, y3_ref, rg_ref, g2_ref, b2_ref, *rest):
    o_ref = rest[-1]
    rg = rg_ref[...]
    f = rg[:, 0:1] * _unpack_bf16_pairs(y0_ref[...])
    for k, y_ref in ((1, y1_ref), (2, y2_ref), (3, y3_ref)):
        f = f + rg[:, k:k + 1] * _unpack_bf16_pairs(y_ref[...])
    o_ref[...] = _ln(DN_ALPHA * x1_ref[...] + f, g2_ref[...], b2_ref[...])


def combine_block(x1, y4p, rg, g2, b2, part, partial_out):
    t, d = x1.shape
    tm = TOKEN_TILE
    ntp = t // tm // COMBINE_PARTS
    base = part * ntp
    tok = lambda w: pl.BlockSpec((tm, w), lambda i: (base + i, 0))
    y_specs = [pl.BlockSpec((tm, d // 2), functools.partial(lambda i, k: (k * ntp + i, 0), k=k))
               for k in range(TOP_K)]
    in_specs = [tok(d)] + y_specs + [tok(LANES), _const_spec((1, d)), _const_spec((1, d))]
    args = [x1, y4p, y4p, y4p, y4p, rg, g2.reshape(1, d), b2.reshape(1, d)]
    aliases = {}
    if partial_out is not None:
        in_specs.append(pl.BlockSpec(memory_space=pl.ANY))
        args.append(partial_out)
        aliases = {len(args) - 1: 0}
    return pl.pallas_call(
        _combine_kernel,
        out_shape=jax.ShapeDtypeStruct((t, d), F32),
        grid=(ntp,),
        in_specs=in_specs,
        out_specs=tok(d),
        input_output_aliases=aliases,
        compiler_params=pltpu.CompilerParams(
            dimension_semantics=("arbitrary",), vmem_limit_bytes=VMEM_LIMIT),
        name="moe_combine",
    )(*args)


def moe_block(x1, x1p, ri, rg, cnt, layer, w1, b1, w2, b2, g2, b2n):
    t, d = x1.shape
    n_assign = t * TOP_K
    n_blocks = -(-n_assign // MOE_BLOCK) + N_EXPERTS
    n_rows = n_blocks * MOE_BLOCK
    ar = jnp.arange(N_EXPERTS, dtype=jnp.int32)
    counts = cnt[:, 0].astype(jnp.int32)
    padded = (counts + MOE_BLOCK - 1) // MOE_BLOCK * MOE_BLOCK
    pend = jnp.sum(jnp.where(ar[None, :] <= ar[:, None], padded[None, :], 0), axis=1)
    pstart = pend - padded
    top_i = ri[:TOP_K]
    dest = jnp.sum(jnp.where(top_i[:, :, None] == ar, pstart, 0), axis=-1) + ri[TOP_K:]
    dest_km = dest.reshape(-1)
    n_used = (pend[-1] // MOE_BLOCK).reshape(1)
    block_start = jnp.arange(n_blocks, dtype=jnp.int32) * MOE_BLOCK
    block_expert = jnp.minimum(
        jnp.sum((pend[None, :] <= block_start[:, None]).astype(jnp.int32), axis=1), N_EXPERTS - 1)
    last = jnp.sum(jnp.where(jnp.arange(n_blocks) == n_used[0] - 1, block_expert, 0))
    block_expert = jnp.where(jnp.arange(n_blocks) < n_used[0], block_expert, last)
    vend = jnp.sum(jnp.where(block_expert[:, None] == ar, (pstart + counts)[None, :], 0), axis=1)
    n_valid = jnp.clip(vend - block_start, 0, MOE_BLOCK)
    blk = jnp.arange(n_blocks)
    starts = (blk < n_used[0]) & ((blk == 0) | (block_expert != jnp.roll(block_expert, 1)))
    seg_slot = jnp.where(starts, (jnp.cumsum(starts.astype(jnp.int32)) - 1) % 2, -1).astype(jnp.int32)
    later = (counts[None, :] > 0) & (ar[None, :] > ar[:, None])
    next_of = jnp.min(jnp.where(later, ar[None, :], N_EXPERTS), axis=1)
    next_of = jnp.where(next_of == N_EXPERTS, -1, next_of)
    next_expert = jnp.sum(jnp.where(block_expert[:, None] == ar, next_of[None, :], 0), axis=1).astype(jnp.int32)

    xp = sc_scatter_rows(x1p, dest_km, n_rows)
    yp = moe_mlp(xp, block_expert, n_valid, n_used, seg_slot, next_expert, layer, w1, b1, w2, b2)
    tp = t // COMBINE_PARTS
    out = None
    for part in range(COMBINE_PARTS):
        y4p = sc_gather_rows(yp, dest[:, part * tp:(part + 1) * tp].reshape(-1))
        out = combine_block(x1, y4p, rg, g2, b2n, part, out)
    return out


def kernel(x, a_w_in, a_b_in, a_ln_g, a_ln_b, a_w_s, a_b_s, a_w_out, a_b_out, b_w_in, b_conv_w, b_conv_b, b_dt_bias, b_a_log, b_d, b_norm_w, b_w_out, moe_w_router, moe_b_router, moe_w1, moe_b1, moe_w2, moe_b2, ln1_g, ln1_b, ln2_g, ln2_b):
    bsz, s_len, d = x.shape
    t = bsz * s_len
    xt = x.reshape(t, d)
    a_w_in_b, a_w_out_b = a_w_in.astype(BF16), a_w_out.astype(BF16)
    b_w_in_b, b_w_out_b = b_w_in.astype(BF16), b_w_out.astype(BF16)
    dt_col0 = b_w_in.shape[2] - SSM_HEADS
    for i in range(DEPTH):
        j = i // 2
        w_r32 = jnp.pad(moe_w_router[i], ((0, 0), (0, LANES - N_EXPERTS)))
        w_r_hi = w_r32.astype(BF16)
        w_r = jnp.stack([w_r_hi, (w_r32 - w_r_hi.astype(F32)).astype(BF16)])
        b_r = moe_b_router[i].reshape(N_EXPERTS, 1)
        if i % 2 == 0:
            x1, x1p, ri, rg, cnt = mixer_a_block(
                xt, j, a_w_in_b, a_b_in[j], a_ln_g[j], a_ln_b[j], a_w_s[j], a_b_s[j],
                a_w_out_b, a_b_out[j], ln1_g[i], ln1_b[i], w_r, b_r)
        else:
            x3 = xt.reshape(bsz, s_len, d)
            x1, x1p, ri, rg, cnt = mamba_block(
                x3, j, b_w_in_b, b_w_in[j, :, dt_col0:], b_conv_w[j], b_conv_b[j], b_dt_bias[j], b_a_log[j],
                b_d[j], b_norm_w[j], b_w_out_b, ln1_g[i], ln1_b[i], w_r, b_r)
            x1 = x1.reshape(t, d)
            x1p = x1p.reshape(t, d // 2)
            rg = rg.reshape(t, LANES)
        xt = moe_block(x1, x1p, ri, rg, cnt, i, moe_w1, moe_b1, moe_w2, moe_b2, ln2_g[i], ln2_b[i])
    return xt.reshape(bsz, s_len, d)
```

```python
import functools
import math

import jax
import jax.numpy as jnp
from jax import lax
from jax.experimental import pallas as pl
from jax.experimental.pallas import tpu as pltpu
from jax.experimental.pallas import tpu_sc as plsc

F32 = jnp.float32
BF16 = jnp.bfloat16
U32 = jnp.uint32

DEPTH = 4
N_EXPERTS = 32
TOP_K = 4
MOE_BLOCK = 512
SWIGLU_LIMIT = 7.0
SWIGLU_ALPHA = 1.702
A_BLOCK = 128
A_GROUPS = 8
CHUNK = 64
SSD_BLOCK = 128
SSM_HEADS = 32
SSM_HEAD_DIM = 64
SSM_GROUPS = 4
SSM_STATE = 128
SSM_CONV = 4
DN_ALPHA = (2 * DEPTH) ** 0.25
LN_EPS = 1e-5
LOG2E = 1.0 / math.log(2.0)
RMS_EPS = 1e-5

LANES = 128
SUBLANES = 8
TOKEN_TILE = 256
VMEM_LIMIT = 56 * 1024 * 1024
SC_CHUNK = 64
CONV_COLS = 256
COMBINE_PARTS = 4


def _ln(x, g, b):
    mu = jnp.mean(x, axis=-1, keepdims=True)
    xc = x - mu
    var = jnp.mean(xc * xc, axis=-1, keepdims=True)
    return xc * lax.rsqrt(var + LN_EPS) * g + b


def _gelu(x):
    return 0.5 * x * (1.0 + lax.erf(x * (1.0 / math.sqrt(2.0))))


def _silu(x):
    h = 0.5 * x
    return h + h * jnp.tanh(h)


def _pack_bf16_pairs(y):
    h = y.shape[1] // 2
    lo = lax.bitcast_convert_type(y[:, :h].astype(BF16).astype(F32), U32) >> 16
    hi = lax.bitcast_convert_type(y[:, h:].astype(BF16).astype(F32), U32) & jnp.uint32(0xFFFF0000)
    return hi | lo


def _unpack_bf16_pairs(w):
    lo = lax.bitcast_convert_type(w << 16, F32)
    hi = lax.bitcast_convert_type(w & jnp.uint32(0xFFFF0000), F32)
    return jnp.concatenate([lo, hi], axis=1)


def _route(x1, first, wr_ref, brc_ref, ri_ref, rg_ref, cnt_ref, cnt_scr):
    @pl.when(first)
    def _():
        cnt_scr[...] = jnp.zeros(cnt_scr.shape, F32)

    x_hi = x1.astype(BF16)
    x_lo = (x1 - x_hi.astype(F32)).astype(BF16)
    logits_tok = (jnp.dot(x_hi, wr_ref[0], preferred_element_type=F32)
                  + (jnp.dot(x_hi, wr_ref[1], preferred_element_type=F32)
                     + jnp.dot(x_lo, wr_ref[0], preferred_element_type=F32)))
    n_e = brc_ref.shape[0]
    logits = logits_tok.T[:n_e, :] + brc_ref[...]
    tm = logits.shape[1]
    sub_f = lax.broadcasted_iota(jnp.int32, logits.shape, 0).astype(F32)
    vals, idxs, hots = [], [], []
    l = logits
    for _ in range(TOP_K):
        m = jnp.max(l, axis=0, keepdims=True)
        i = jnp.min(jnp.where(l == m, sub_f, float(n_e)), axis=0, keepdims=True)
        hot = sub_f == i
        vals.append(m)
        idxs.append(i)
        hots.append(hot.astype(F32))
        l = jnp.where(hot, -jnp.inf, l)
    es = [jnp.exp(v - vals[0]) for v in vals]
    tot = es[0] + es[1] + es[2] + es[3]
    cnt = (hots[0] + hots[1] + hots[2] + hots[3]).astype(BF16)
    r_i = lax.broadcasted_iota(jnp.int32, (tm, tm), 0)
    c_i = lax.broadcasted_iota(jnp.int32, (tm, tm), 1)
    before = jnp.dot(cnt, (r_i < c_i).astype(BF16), preferred_element_type=F32)
    before = before + jnp.concatenate([cnt_scr[...]] * (tm // LANES), axis=1)
    row = lax.broadcasted_iota(jnp.int32, (2 * TOP_K, tm), 0)
    ri = jnp.zeros((2 * TOP_K, tm), F32)
    rg = jnp.zeros((2 * TOP_K, tm), F32)
    for k in range(TOP_K):
        rank = jnp.sum(before * hots[k], axis=0, keepdims=True)
        ri = jnp.where(row == k, idxs[k], ri)
        ri = jnp.where(row == TOP_K + k, rank, ri)
        rg = jnp.where(row == k, es[k] / tot, rg)
    ri_ref[...] = ri.astype(jnp.int32)
    rg_ref[...] = jnp.concatenate([rg, jnp.zeros((LANES - 2 * TOP_K, tm), F32)], axis=0).T
    total = cnt_scr[...] + jnp.dot(cnt, jnp.ones((tm, LANES), BF16), preferred_element_type=F32)
    cnt_scr[...] = total
    cnt_ref[...] = total


def _const_spec(shape):
    nd = len(shape)
    return pl.BlockSpec(shape, lambda *_: (0,) * nd)


def _mixer_a_kernel(x_ref, wu_ref, wv_ref, bu_ref, bv_ref, lg_ref, lb_ref, ws_ref, bst_ref,
                    wo_ref, bo_ref, g1_ref, b1_ref, wr_ref, br_ref,
                    x1_ref, x1p_ref, ri_ref, rg_ref, cnt_ref, v_scr, o_scr, cnt_scr):
    x = x_ref[...]
    xb = x.astype(BF16)
    tm, dff = v_scr.shape
    gd = dff // A_GROUPS
    gcols = [slice(g * gd, (g + 1) * gd) for g in range(A_GROUPS)]
    s1 = jnp.zeros((tm, 1), F32)
    s2 = jnp.zeros((tm, 1), F32)
    shift = None
    for cols in gcols:
        v_g = _gelu(jnp.dot(xb, wv_ref[:, cols], preferred_element_type=F32) + bv_ref[:, cols])
        v_scr[:, cols] = v_g
        if shift is None:
            shift = jnp.mean(v_g, axis=-1, keepdims=True)
        dv = v_g - shift
        s1 = s1 + jnp.sum(dv, axis=-1, keepdims=True)
        s2 = s2 + jnp.sum(dv * dv, axis=-1, keepdims=True)
    dmu = s1 * (1.0 / dff)
    mu = shift + dmu
    rstd = lax.rsqrt(s2 * (1.0 / dff) - dmu * dmu + LN_EPS)
    pi = lax.broadcasted_iota(jnp.int32, (A_BLOCK, A_BLOCK), 0) // CHUNK
    pj = lax.broadcasted_iota(jnp.int32, (A_BLOCK, A_BLOCK), 1) // CHUNK
    mask = pj <= pi
    bst = bst_ref[...]
    for g, cols in enumerate(gcols):
        u_g = _gelu(jnp.dot(xb, wu_ref[:, cols], preferred_element_type=F32) + bu_ref[:, cols])
        vb_g = ((v_scr[:, cols] - mu) * rstd * lg_ref[:, cols] + lb_ref[:, cols]).astype(BF16)
        wm = jnp.where(mask, ws_ref[g], 0.0).astype(BF16)
        for n in range(tm // A_BLOCK):
            rows = slice(n * A_BLOCK, (n + 1) * A_BLOCK)
            sv = jnp.dot(wm, vb_g[rows, :], preferred_element_type=F32) + bst[:, g:g + 1]
            o_scr[rows, cols] = (u_g[rows, :] * sv).astype(BF16)
    m = jnp.dot(o_scr[...], wo_ref[...], preferred_element_type=F32) + bo_ref[...]
    x1 = _ln(DN_ALPHA * x + m, g1_ref[...], b1_ref[...])
    x1_ref[...] = x1
    x1p_ref[...] = _pack_bf16_pairs(x1)
    _route(x1, pl.program_id(0) == 0, wr_ref, br_ref, ri_ref, rg_ref, cnt_ref, cnt_scr)


def mixer_a_block(xt, j, w_in_b, b_in, ln_g, ln_b, w_s, b_s, w_out_b, b_out, g1, b1, w_r, b_r):
    t, d = xt.shape
    dff = w_out_b.shape[1]
    tm = TOKEN_TILE
    bu = b_in[:dff].reshape(1, dff)
    bv = b_in[dff:].reshape(1, dff)
    args = (xt, w_in_b, w_in_b, bu, bv, ln_g.reshape(1, dff), ln_b.reshape(1, dff), w_s, b_s.T,
            w_out_b, b_out.reshape(1, d), g1.reshape(1, d), b1.reshape(1, d), w_r, b_r)
    tok = lambda w: pl.BlockSpec((tm, w), lambda i: (i, 0))
    route = pl.BlockSpec((2 * TOP_K, tm), lambda i: (0, i))
    in_specs = [tok(d)] + [_const_spec(a.shape) for a in args[1:]]
    in_specs[1] = pl.BlockSpec((None, d, dff), lambda i: (j, 0, 0))
    in_specs[2] = pl.BlockSpec((None, d, dff), lambda i: (j, 0, 1))
    in_specs[9] = pl.BlockSpec((None, dff, d), lambda i: (j, 0, 0))
    out_shape = (jax.ShapeDtypeStruct((t, d), F32),
                 jax.ShapeDtypeStruct((t, d // 2), U32),
                 jax.ShapeDtypeStruct((2 * TOP_K, t), jnp.int32),
                 jax.ShapeDtypeStruct((t, LANES), F32),
                 jax.ShapeDtypeStruct((N_EXPERTS, LANES), F32))
    out_specs = (tok(d), tok(d // 2), route, tok(LANES), _const_spec((N_EXPERTS, LANES)))
    return pl.pallas_call(
        _mixer_a_kernel,
        out_shape=out_shape,
        grid=(t // tm,),
        in_specs=in_specs,
        out_specs=out_specs,
        scratch_shapes=[pltpu.VMEM((tm, dff), F32), pltpu.VMEM((tm, dff), BF16),
                        pltpu.VMEM((N_EXPERTS, LANES), F32)],
        compiler_params=pltpu.CompilerParams(
            dimension_semantics=("arbitrary",), vmem_limit_bytes=VMEM_LIMIT),
        name="mixer_a",
    )(*args)


def _split3_bf16(q):
    hi = q.astype(BF16)
    r1 = q - hi.astype(F32)
    mid = r1.astype(BF16)
    lo = (r1 - mid.astype(F32)).astype(BF16)
    return hi, mid, lo


def _mamba_kernel(x_ref, wz_ref, wxs_ref, wbc_ref, wdt_ref, cw_ref, cb_ref, dtb_ref,
                  alog_ref, dfull_ref, nw_ref, exp_ref, wo_ref, g1_ref, b1_ref, wr_ref, br_ref,
                  x1_ref, x1p_ref, ri_ref, rg_ref, cnt_ref,
                  xs_scr, bm_scr, cm_scr, z_scr, state_scr, y_scr, dtw_scr, cnt_scr, *ext_scrs):
    s = pl.program_id(1)
    tm = x_ref.shape[0]
    d_inner = xs_scr.shape[1]
    gn = bm_scr.shape[1]
    L, P, N = SSD_BLOCK, SSM_HEAD_DIM, SSM_STATE
    R = SSM_HEADS // SSM_GROUPS
    gw = R * P
    nblk = tm // L

    @pl.when(s == 0)
    def _():
        state_scr[...] = jnp.zeros(state_scr.shape, F32)
        for ext in ext_scrs:
            ext[...] = jnp.zeros(ext.shape, F32)

    x = x_ref[...]
    xb = x.astype(BF16)
    dt = jnp.dot(xb, wdt_ref[...], preferred_element_type=F32) + dtb_ref[...]
    dt = jnp.maximum(dt, 0.0) + jnp.log1p(jnp.exp(-jnp.abs(dt)))

    cw = cw_ref[...]
    cbias = cb_ref[...]
    w = CONV_COLS
    pad = SUBLANES
    n_col = len(ext_scrs) // 2
    for c, (ext, qext) in enumerate(zip(ext_scrs[:n_col], ext_scrs[n_col:])):
        cols = slice(c * w, (c + 1) * w)
        ext[0:pad, :] = jnp.where(s > 0, ext[tm:tm + pad, :], 0.0)
        w_blk = wxs_ref[:, cols] if c * w < d_inner else wbc_ref[:, c * w - d_inner:(c + 1) * w - d_inner]
        ext[pad:pad + tm, :] = jnp.dot(xb, w_blk, preferred_element_type=F32)
        x0 = ext[pad:pad + tm, :]
        xm1 = ext[pad - 1:pad - 1 + tm, :]
        q = cw[1:2, cols] * x0 + cw[0:1, cols] * xm1
        qext[0:pad, :] = jnp.where(s > 0, qext[tm:tm + pad, :], 0.0)
        qext[pad:pad + tm, :] = q
        acc = cbias[:, cols] + (cw[3:4, cols] * x0 + cw[2:3, cols] * xm1) + qext[pad - 2:pad - 2 + tm, :]
        act = _silu(acc).astype(BF16)
        lo = c * w
        if lo < d_inner:
            xs_scr[:, lo:lo + w] = act
        elif lo < d_inner + gn:
            bm_scr[:, lo - d_inner:lo - d_inner + w] = act
        else:
            cm_scr[:, lo - d_inner - gn:lo - d_inner - gn + w] = act

    def z_task(c):
        cols = slice(c * w, (c + 1) * w)
        z_scr[:, cols] = jnp.dot(xb, wz_ref[:, cols], preferred_element_type=F32).astype(BF16)

    z_tasks = [functools.partial(z_task, c) for c in range(d_inner // w)]

    a_row = -jnp.exp(alog_ref[...])
    adt = dt * a_row
    r_i = lax.broadcasted_iota(jnp.int32, (tm, tm), 0)
    c_i = lax.broadcasted_iota(jnp.int32, (tm, tm), 1)
    btril = ((r_i >= c_i) & (r_i // L == c_i // L)).astype(F32)
    acum = jnp.dot(btril, adt, preferred_element_type=F32,
                   precision=lax.Precision.HIGHEST)
    row_blk = lax.broadcasted_iota(jnp.int32, (tm, LANES), 0) // L
    alast = [acum[(b + 1) * L - 1:(b + 1) * L, :] for b in range(nblk)]
    alast_rows = alast[0]
    for b in range(1, nblk):
        alast_rows = jnp.where(row_blk == b, alast[b], alast_rows)
    dtw = dt * jnp.exp(alast_rows - acum)
    expand = exp_ref[...]
    dtw_scr[...] = jnp.dot(dtw.astype(BF16), expand, preferred_element_type=F32)
    acum2 = acum * LOG2E
    arow_t = acum2.T - jnp.log(dt.T) * LOG2E
    tril = lax.broadcasted_iota(jnp.int32, (L, L), 0) >= lax.broadcasted_iota(jnp.int32, (L, L), 1)
    lo_half = lax.broadcasted_iota(jnp.int32, (L, 2 * P), 1) < P

    for b in range(nblk):
        rows = slice(b * L, (b + 1) * L)
        acum_b = acum2[rows, :]
        elast = jnp.broadcast_to(jnp.exp(alast[b]), (SUBLANES, LANES))
        dec_all = sum(jnp.dot(piece, expand, preferred_element_type=F32)
                      for piece in _split3_bf16(elast))[0:1, :]
        for g in range(SSM_GROUPS):
            c_g = cm_scr[rows, g * N:(g + 1) * N]
            b_g = bm_scr[rows, g * N:(g + 1) * N]
            cb = lax.dot_general(c_g, b_g, (((1,), (1,)), ((), ())), preferred_element_type=F32)
            st = state_scr[g]
            y_off = jnp.dot(c_g, st.astype(BF16), preferred_element_type=F32)
            for q in range(R // 2):
                h0 = g * R + 2 * q
                ms, es = [], []
                for h in (h0, h0 + 1):
                    acol = jnp.broadcast_to(acum_b[:, h:h + 1], (L, L))
                    seg = acol - arow_t[h:h + 1, rows]
                    ms.append((cb * jnp.exp2(jnp.where(tril, seg, -jnp.inf))).astype(BF16))
                    es.append(jnp.exp2(acol))
                lhs = jnp.concatenate(ms, axis=1)
                cols = slice(h0 * P, (h0 + 2) * P)
                xp = xs_scr[rows, cols]
                zero = jnp.zeros_like(xp)
                rhs = jnp.concatenate([jnp.where(lo_half, xp, zero), jnp.where(lo_half, zero, xp)], axis=0)
                y_d = jnp.dot(lhs, rhs, preferred_element_type=F32)
                scale = jnp.where(lo_half, es[0], es[1])
                y_scr[rows, cols] = y_d + y_off[:, 2 * q * P:(2 * q + 2) * P] * scale
            gcols = slice(g * gw, (g + 1) * gw)
            xw = (xs_scr[rows, gcols].astype(F32) * dtw_scr[rows, gcols]).astype(BF16)
            contrib = lax.dot_general(b_g, xw, (((0,), (0,)), ((), ())), preferred_element_type=F32)
            state_scr[g] = st * dec_all[:, gcols] + contrib
            if z_tasks:
                z_tasks.pop(0)()
    for task in z_tasks:
        task()

    y = y_scr[...] + dfull_ref[...] * xs_scr[...].astype(F32)
    y = y * _silu(z_scr[...].astype(F32))
    nw = nw_ref[...]
    parts = []
    for g in range(SSM_GROUPS):
        yg = y[:, g * gw:(g + 1) * gw]
        yg = yg * lax.rsqrt(jnp.mean(yg * yg, axis=-1, keepdims=True) + RMS_EPS)
        parts.append((yg * nw[:, g * gw:(g + 1) * gw]).astype(BF16))
    yb = jnp.concatenate(parts, axis=1)
    m = jnp.dot(yb, wo_ref[...], preferred_element_type=F32)
    x1 = _ln(DN_ALPHA * x + m, g1_ref[...], b1_ref[...])
    x1_ref[...] = x1
    x1p_ref[...] = _pack_bf16_pairs(x1)
    first = (pl.program_id(0) == 0) & (s == 0)
    _route(x1, first, wr_ref, br_ref, ri_ref, rg_ref, cnt_ref, cnt_scr)


def mamba_block(x3, j, w_in_b, w_dt, conv_w, conv_b, dt_bias, a_log, d_skip, norm_w, w_out_b, g1, b1, w_r, b_r):
    bsz, s_len, d = x3.shape
    d_inner = SSM_HEADS * SSM_HEAD_DIM
    gn = SSM_GROUPS * SSM_STATE
    conv_dim = d_inner + 2 * gn
    tm = TOKEN_TILE
    wdt = jnp.pad(w_dt, ((0, 0), (0, LANES - SSM_HEADS))).astype(BF16)
    dtb = jnp.pad(dt_bias, (0, LANES - SSM_HEADS)).reshape(1, LANES)
    alog = jnp.pad(a_log, (0, LANES - SSM_HEADS)).reshape(1, LANES)
    dfull = jnp.repeat(d_skip, SSM_HEAD_DIM).reshape(1, d_inner)
    head_of_col = jnp.arange(d_inner, dtype=jnp.int32) // SSM_HEAD_DIM
    expand = (jnp.arange(LANES, dtype=jnp.int32)[:, None] == head_of_col[None, :]).astype(BF16)
    args = (x3, w_in_b, w_in_b, w_in_b, wdt, conv_w, conv_b.reshape(1, conv_dim), dtb, alog, dfull,
            norm_w.reshape(1, d_inner), expand, w_out_b, g1.reshape(1, d), b1.reshape(1, d), w_r, b_r)
    tok = lambda w: pl.BlockSpec((None, tm, w), lambda b, s: (b, s, 0))
    in_specs = [tok(d)] + [_const_spec(a.shape) for a in args[1:]]
    in_specs[1] = pl.BlockSpec((None, d, d_inner), lambda b, s: (j, 0, 0))
    in_specs[2] = pl.BlockSpec((None, d, d_inner), lambda b, s: (j, 0, 1))
    in_specs[3] = pl.BlockSpec((None, d, 2 * gn), lambda b, s: (j, 0, 2 * d_inner // (2 * gn)))
    in_specs[12] = pl.BlockSpec((None, d_inner, d), lambda b, s: (j, 0, 0))
    ns = s_len // tm
    route = pl.BlockSpec((2 * TOP_K, tm), lambda b, s: (0, b * ns + s))
    out_shape = (jax.ShapeDtypeStruct((bsz, s_len, d), F32),
                 jax.ShapeDtypeStruct((bsz, s_len, d // 2), U32),
                 jax.ShapeDtypeStruct((2 * TOP_K, bsz * s_len), jnp.int32),
                 jax.ShapeDtypeStruct((bsz, s_len, LANES), F32),
                 jax.ShapeDtypeStruct((N_EXPERTS, LANES), F32))
    out_specs = (tok(d), tok(d // 2), route, tok(LANES), _const_spec((N_EXPERTS, LANES)))
    return pl.pallas_call(
        _mamba_kernel,
        out_shape=out_shape,
        grid=(bsz, ns),
        in_specs=in_specs,
        out_specs=out_specs,
        scratch_shapes=[pltpu.VMEM((tm, d_inner), BF16), pltpu.VMEM((tm, gn), BF16), pltpu.VMEM((tm, gn), BF16),
                        pltpu.VMEM((tm, d_inner), BF16),
                        pltpu.VMEM((SSM_GROUPS, SSM_STATE, d_inner // SSM_GROUPS), F32),
                        pltpu.VMEM((tm, d_inner), F32),
                        pltpu.VMEM((tm, d_inner), F32),
                        pltpu.VMEM((N_EXPERTS, LANES), F32)] +
                       [pltpu.VMEM((tm + SUBLANES, CONV_COLS), F32) for _ in range(2 * conv_dim // CONV_COLS)],
        compiler_params=pltpu.CompilerParams(
            dimension_semantics=("arbitrary", "arbitrary"), vmem_limit_bytes=VMEM_LIMIT),
        name="mamba",
    )(*args)


def _sc_workers():
    info = plsc.get_sparse_core_info()
    return info.num_cores, info.num_subcores


def sc_scatter_rows(x, dest_km, n_rows):
    t, d = x.shape
    nc, ns = _sc_workers()
    per_w, rem = divmod(t, nc * ns)
    assert rem == 0 and per_w % SC_CHUNK == 0
    mesh = plsc.VectorSubcoreMesh(core_axis_name="core", subcore_axis_name="subcore")

    @functools.partial(
        pl.kernel,
        out_type=jax.ShapeDtypeStruct((n_rows, d), x.dtype),
        mesh=mesh,
        scratch_types=[pltpu.VMEM((SC_CHUNK,), jnp.int32) for _ in range(TOP_K)] +
                      [pltpu.VMEM((SC_CHUNK, d), x.dtype), pltpu.SemaphoreType.DMA],
    )
    def scatter(x_hbm, i_hbm, o_hbm, i0, i1, i2, i3, rows_v, sem):
        wid = lax.axis_index("subcore") * nc + lax.axis_index("core")
        base = wid * per_w
        idx = (i0, i1, i2, i3)

        @pl.loop(0, per_w // SC_CHUNK)
        def _(c):
            off = base + c * SC_CHUNK
            pltpu.sync_copy(x_hbm.at[pl.ds(off, SC_CHUNK)], rows_v)
            for k in range(TOP_K):
                pltpu.sync_copy(i_hbm.at[pl.ds(k * t + off, SC_CHUNK)], idx[k])
            copies = [pltpu.make_async_copy(rows_v, o_hbm.at[idx[k]], sem) for k in range(TOP_K)]
            for cp in copies:
                cp.start()
            for cp in copies:
                cp.wait()

    return scatter(x, dest_km)


def sc_gather_rows(table, idx):
    n = idx.shape[0]
    d = table.shape[1]
    nc, ns = _sc_workers()
    per_w, rem = divmod(n, nc * ns)
    n_chunks = per_w // SC_CHUNK
    assert rem == 0 and per_w % (2 * SC_CHUNK) == 0
    mesh = plsc.VectorSubcoreMesh(core_axis_name="core", subcore_axis_name="subcore")

    @functools.partial(
        pl.kernel,
        out_type=jax.ShapeDtypeStruct((n, d), table.dtype),
        mesh=mesh,
        scratch_types=[
            pltpu.VMEM((SC_CHUNK,), jnp.int32), pltpu.VMEM((SC_CHUNK,), jnp.int32),
            pltpu.VMEM((SC_CHUNK, d), table.dtype), pltpu.VMEM((SC_CHUNK, d), table.dtype),
            pltpu.SemaphoreType.DMA, pltpu.SemaphoreType.DMA,
            pltpu.SemaphoreType.DMA, pltpu.SemaphoreType.DMA,
        ],
    )
    def gather(x_hbm, i_hbm, o_hbm, idx0, idx1, rows0, rows1, sg0, sg1, sw0, sw1):
        wid = lax.axis_index("subcore") * nc + lax.axis_index("core")
        base = wid * per_w
        idx = (idx0, idx1)
        rows = (rows0, rows1)
        sg = (sg0, sg1)
        sw = (sw0, sw1)

        def gather_copy(slot):
            return pltpu.make_async_copy(x_hbm.at[idx[slot]], rows[slot], sg[slot])

        def write_copy(c, slot):
            return pltpu.make_async_copy(rows[slot], o_hbm.at[pl.ds(base + c * SC_CHUNK, SC_CHUNK)], sw[slot])

        pltpu.sync_copy(i_hbm.at[pl.ds(base, SC_CHUNK)], idx0)
        gather_copy(0).start()

        @pl.loop(0, n_chunks, step=2)
        def _(c0):
            for slot in range(2):
                c = c0 + slot
                nxt = 1 - slot

                @pl.when(c + 1 < n_chunks)
                def _():
                    pltpu.sync_copy(i_hbm.at[pl.ds(base + (c + 1) * SC_CHUNK, SC_CHUNK)], idx[nxt])

                    @pl.when(c >= 1)
                    def _():
                        write_copy(c - 1, nxt).wait()

                    gather_copy(nxt).start()

                gather_copy(slot).wait()
                write_copy(c, slot).start()

        write_copy(n_chunks - 2, 0).wait()
        write_copy(n_chunks - 1, 1).wait()

    return gather(table, idx)


def _moe_mlp_kernel(layer, be_ref, nv_ref, nu_ref, seg_ref, nxt_ref,
                    x_ref, w1_hbm, b1_ref, w2_hbm, b2_ref, o_ref, w1f, w2f, w1b, w2b, sems):
    i = pl.program_id(0)

    def fetch(e, slot):
        return (pltpu.make_async_copy(w1_hbm.at[layer, e], w1f.at[slot], sems.at[slot, 0]),
                pltpu.make_async_copy(w2_hbm.at[layer, e], w2f.at[slot], sems.at[slot, 1]))

    @pl.when(i == 0)
    def _():
        for cp in fetch(be_ref[0], 0):
            cp.start()

    slot = seg_ref[i]

    @pl.when(slot >= 0)
    def _():
        for cp in fetch(be_ref[i], slot):
            cp.wait()
        w1b[...] = w1f[slot].astype(BF16)
        w2b[...] = w2f[slot].astype(BF16)

        @pl.when(nxt_ref[i] >= 0)
        def _():
            for cp in fetch(nxt_ref[i], 1 - slot):
                cp.start()

    nv = nv_ref[i]
    half = MOE_BLOCK // 2

    def mlp(n_rows):
        dff = w2b.shape[0]
        row = lax.broadcasted_iota(jnp.int32, (n_rows, x_ref.shape[1]), 0)
        xw = jnp.where(row < nv, x_ref[0:n_rows, :], jnp.uint32(0))
        xb = _unpack_bf16_pairs(xw).astype(BF16)
        h = jnp.dot(xb, w1b[...], preferred_element_type=F32) + b1_ref[...]
        g = jnp.minimum(h[:, :dff], SWIGLU_LIMIT)
        lin = jnp.clip(h[:, dff:], -SWIGLU_LIMIT, SWIGLU_LIMIT)
        act = (g * jax.nn.sigmoid(SWIGLU_ALPHA * g) * (lin + 1.0)).astype(BF16)
        y = jnp.dot(act, w2b[...], preferred_element_type=F32) + b2_ref[...]
        o_ref[0:n_rows, :] = _pack_bf16_pairs(y)

    pl.when(nv > half)(functools.partial(mlp, MOE_BLOCK))
    pl.when((nv > 0) & (nv <= half))(functools.partial(mlp, half))


def moe_mlp(xp, block_expert, n_valid, n_used, seg_slot, next_expert, layer, w1, b1, w2, b2):
    n_rows, dh = xp.shape
    _, n_e, d, two_dff = w1.shape
    dff = two_dff // 2
    n_blocks = n_rows // MOE_BLOCK
    row = lambda i, be, nv, nu, sg, nx: (jnp.minimum(i, nu[0] - 1), 0)
    bsel = lambda i, be, nv, nu, sg, nx: (layer, be[i], 0, 0)
    grid_spec = pltpu.PrefetchScalarGridSpec(
        num_scalar_prefetch=5,
        grid=(n_blocks,),
        in_specs=[
            pl.BlockSpec((MOE_BLOCK, dh), row),
            pl.BlockSpec(memory_space=pl.ANY),
            pl.BlockSpec((None, None, 1, two_dff), bsel),
            pl.BlockSpec(memory_space=pl.ANY),
            pl.BlockSpec((None, None, 1, d), bsel),
        ],
        out_specs=pl.BlockSpec((MOE_BLOCK, dh), row),
        scratch_shapes=[pltpu.VMEM((2, d, two_dff), F32), pltpu.VMEM((2, dff, d), F32),
                        pltpu.VMEM((d, two_dff), BF16), pltpu.VMEM((dff, d), BF16),
                        pltpu.SemaphoreType.DMA((2, 2))],
    )
    return pl.pallas_call(
        functools.partial(_moe_mlp_kernel, layer),
        out_shape=jax.ShapeDtypeStruct((n_rows, dh), U32),
        grid_spec=grid_spec,
        compiler_params=pltpu.CompilerParams(
            dimension_semantics=("arbitrary",), vmem_limit_bytes=VMEM_LIMIT),
        name="moe_mlp",
    )(block_expert, n_valid, n_used, seg_slot, next_expert, xp, w1, b1.reshape(b1.shape[0], n_e, 1, two_dff),
      w2, b2.reshape(b2.shape[0], n_e, 1, d))


def _combine_kernel(x1_ref, y0_ref, y1_ref, y2_ref, y3_ref, rg_ref, g2_ref, b2_ref, *rest):
    o_ref = rest[-1]
    rg = rg_ref[...]
    f = rg[:, 0:1] * _unpack_bf16_pairs(y0_ref[...])
    for k, y_ref in ((1, y1_ref), (2, y2_ref), (3, y3_ref)):
        f = f + rg[:, k:k + 1] * _unpack_bf16_pairs(y_ref[...])
    o_ref[...] = _ln(DN_ALPHA * x1_ref[...] + f, g2_ref[...], b2_ref[...])


def combine_block(x1, y4p, rg, g2, b2, part, partial_out):
    t, d = x1.shape
    tm = TOKEN_TILE
    ntp = t // tm // COMBINE_PARTS
    base = part * ntp
    tok = lambda w: pl.BlockSpec((tm, w), lambda i: (base + i, 0))
    y_specs = [pl.BlockSpec((tm, d // 2), functools.partial(lambda i, k: (k * ntp + i, 0), k=k))
               for k in range(TOP_K)]
    in_specs = [tok(d)] + y_specs + [tok(LANES), _const_spec((1, d)), _const_spec((1, d))]
    args = [x1, y4p, y4p, y4p, y4p, rg, g2.reshape(1, d), b2.reshape(1, d)]
    aliases = {}
    if partial_out is not None:
        in_specs.append(pl.BlockSpec(memory_space=pl.ANY))
        args.append(partial_out)
        aliases = {len(args) - 1: 0}
    return pl.pallas_call(
        _combine_kernel,
        out_shape=jax.ShapeDtypeStruct((t, d), F32),
        grid=(ntp,),
        in_specs=in_specs,
        out_specs=tok(d),
        input_output_aliases=aliases,
        compiler_params=pltpu.CompilerParams(
            dimension_semantics=("arbitrary",), vmem_limit_bytes=VMEM_LIMIT),
        name="moe_combine",
    )(*args)


def moe_block(x1, x1p, ri, rg, cnt, layer, w1, b1, w2, b2, g2, b2n):
    t, d = x1.shape
    n_assign = t * TOP_K
    n_blocks = -(-n_assign // MOE_BLOCK) + N_EXPERTS
    n_rows = n_blocks * MOE_BLOCK
    ar = jnp.arange(N_EXPERTS, dtype=jnp.int32)
    counts = cnt[:, 0].astype(jnp.int32)
    padded = (counts + MOE_BLOCK - 1) // MOE_BLOCK * MOE_BLOCK
    pend = jnp.sum(jnp.where(ar[None, :] <= ar[:, None], padded[None, :], 0), axis=1)
    pstart = pend - padded
    top_i = ri[:TOP_K]
    dest = jnp.sum(jnp.where(top_i[:, :, None] == ar, pstart, 0), axis=-1) + ri[TOP_K:]
    dest_km = dest.reshape(-1)
    n_used = (pend[-1] // MOE_BLOCK).reshape(1)
    block_start = jnp.arange(n_blocks, dtype=jnp.int32) * MOE_BLOCK
    block_expert = jnp.minimum(
        jnp.sum((pend[None, :] <= block_start[:, None]).astype(jnp.int32), axis=1), N_EXPERTS - 1)
    last = jnp.sum(jnp.where(jnp.arange(n_blocks) == n_used[0] - 1, block_expert, 0))
    block_expert = jnp.where(jnp.arange(n_blocks) < n_used[0], block_expert, last)
    vend = jnp.sum(jnp.where(block_expert[:, None] == ar, (pstart + counts)[None, :], 0), axis=1)
    n_valid = jnp.clip(vend - block_start, 0, MOE_BLOCK)
    blk = jnp.arange(n_blocks)
    starts = (blk < n_used[0]) & ((blk == 0) | (block_expert != jnp.roll(block_expert, 1)))
    seg_slot = jnp.where(starts, (jnp.cumsum(starts.astype(jnp.int32)) - 1) % 2, -1).astype(jnp.int32)
    later = (counts[None, :] > 0) & (ar[None, :] > ar[:, None])
    next_of = jnp.min(jnp.where(later, ar[None, :], N_EXPERTS), axis=1)
    next_of = jnp.where(next_of == N_EXPERTS, -1, next_of)
    next_expert = jnp.sum(jnp.where(block_expert[:, None] == ar, next_of[None, :], 0), axis=1).astype(jnp.int32)

    xp = sc_scatter_rows(x1p, dest_km, n_rows)
    yp = moe_mlp(xp, block_expert, n_valid, n_used, seg_slot, next_expert, layer, w1, b1, w2, b2)
    tp = t // COMBINE_PARTS
    out = None
    for part in range(COMBINE_PARTS):
        y4p = sc_gather_rows(yp, dest[:, part * tp:(part + 1) * tp].reshape(-1))
        out = combine_block(x1, y4p, rg, g2, b2n, part, out)
    return out


def kernel(x, a_w_in, a_b_in, a_ln_g, a_ln_b, a_w_s, a_b_s, a_w_out, a_b_out, b_w_in, b_conv_w, b_conv_b, b_dt_bias, b_a_log, b_d, b_norm_w, b_w_out, moe_w_router, moe_b_router, moe_w1, moe_b1, moe_w2, moe_b2, ln1_g, ln1_b, ln2_g, ln2_b):
    bsz, s_len, d = x.shape
    t = bsz * s_len
    xt = x.reshape(t, d)
    a_w_in_b, a_w_out_b = a_w_in.astype(BF16), a_w_out.astype(BF16)
    dt_col0 = b_w_in.shape[2] - SSM_HEADS
    b_w_in_b, b_w_out_b = b_w_in[:, :, :dt_col0].astype(BF16), b_w_out.astype(BF16)
    for i in range(DEPTH):
        j = i // 2
        w_r32 = jnp.pad(moe_w_router[i], ((0, 0), (0, LANES - N_EXPERTS)))
        w_r_hi = w_r32.astype(BF16)
        w_r = jnp.stack([w_r_hi, (w_r32 - w_r_hi.astype(F32)).astype(BF16)])
        b_r = moe_b_router[i].reshape(N_EXPERTS, 1)
        if i % 2 == 0:
            x1, x1p, ri, rg, cnt = mixer_a_block(
                xt, j, a_w_in_b, a_b_in[j], a_ln_g[j], a_ln_b[j], a_w_s[j], a_b_s[j],
                a_w_out_b, a_b_out[j], ln1_g[i], ln1_b[i], w_r, b_r)
        else:
            x3 = xt.reshape(bsz, s_len, d)
            x1, x1p, ri, rg, cnt = mamba_block(
                x3, j, b_w_in_b, b_w_in[j, :, dt_col0:], b_conv_w[j], b_conv_b[j], b_dt_bias[j], b_a_log[j],
                b_d[j], b_norm_w[j], b_w_out_b, ln1_g[i], ln1_b[i], w_r, b_r)
            x1 = x1.reshape(t, d)
            x1p = x1p.reshape(t, d // 2)
            rg = rg.reshape(t, LANES)
        xt = moe_block(x1, x1p, ri, rg, cnt, i, moe_w1, moe_b1, moe_w2, moe_b2, ln2_g[i], ln2_b[i])
    return xt.reshape(bsz, s_len, d)
```

```python
import functools
import math

import jax
import jax.numpy as jnp
from jax import lax
from jax.experimental import pallas as pl
from jax.experimental.pallas import tpu as pltpu
from jax.experimental.pallas import tpu_sc as plsc

F32 = jnp.float32
BF16 = jnp.bfloat16
U32 = jnp.uint32

DEPTH = 4
N_EXPERTS = 32
TOP_K = 4
MOE_BLOCK = 512
SWIGLU_LIMIT = 7.0
SWIGLU_ALPHA = 1.702
A_BLOCK = 128
A_GROUPS = 8
CHUNK = 64
SSD_BLOCK = 128
SSM_HEADS = 32
SSM_HEAD_DIM = 64
SSM_GROUPS = 4
SSM_STATE = 128
SSM_CONV = 4
DN_ALPHA = (2 * DEPTH) ** 0.25
LN_EPS = 1e-5
LOG2E = 1.0 / math.log(2.0)
RMS_EPS = 1e-5

LANES = 128
SUBLANES = 8
TOKEN_TILE = 256
VMEM_LIMIT = 56 * 1024 * 1024
SC_CHUNK = 64
CONV_COLS = 256
COMBINE_PARTS = 4


def _ln(x, g, b):
    mu = jnp.mean(x, axis=-1, keepdims=True)
    xc = x - mu
    var = jnp.mean(xc * xc, axis=-1, keepdims=True)
    return xc * lax.rsqrt(var + LN_EPS) * g + b


def _gelu(x):
    return 0.5 * x * (1.0 + lax.erf(x * (1.0 / math.sqrt(2.0))))


def _silu(x):
    h = 0.5 * x
    return h + h * jnp.tanh(h)


def _pack_bf16_pairs(y):
    h = y.shape[1] // 2
    lo = lax.bitcast_convert_type(y[:, :h].astype(BF16).astype(F32), U32) >> 16
    hi = lax.bitcast_convert_type(y[:, h:].astype(BF16).astype(F32), U32) & jnp.uint32(0xFFFF0000)
    return hi | lo


def _unpack_bf16_pairs(w):
    lo = lax.bitcast_convert_type(w << 16, F32)
    hi = lax.bitcast_convert_type(w & jnp.uint32(0xFFFF0000), F32)
    return jnp.concatenate([lo, hi], axis=1)


def _route(x1, first, wr_ref, brc_ref, ri_ref, rg_ref, cnt_ref, cnt_scr):
    @pl.when(first)
    def _():
        cnt_scr[...] = jnp.zeros(cnt_scr.shape, F32)

    x_hi = x1.astype(BF16)
    x_lo = (x1 - x_hi.astype(F32)).astype(BF16)
    logits_tok = (jnp.dot(x_hi, wr_ref[0], preferred_element_type=F32)
                  + (jnp.dot(x_hi, wr_ref[1], preferred_element_type=F32)
                     + jnp.dot(x_lo, wr_ref[0], preferred_element_type=F32)))
    n_e = brc_ref.shape[0]
    logits = logits_tok.T[:n_e, :] + brc_ref[...]
    tm = logits.shape[1]
    sub_f = lax.broadcasted_iota(jnp.int32, logits.shape, 0).astype(F32)
    vals, idxs, hots = [], [], []
    l = logits
    for _ in range(TOP_K):
        m = jnp.max(l, axis=0, keepdims=True)
        i = jnp.min(jnp.where(l == m, sub_f, float(n_e)), axis=0, keepdims=True)
        hot = sub_f == i
        vals.append(m)
        idxs.append(i)
        hots.append(hot.astype(F32))
        l = jnp.where(hot, -jnp.inf, l)
    es = [jnp.exp(v - vals[0]) for v in vals]
    tot = es[0] + es[1] + es[2] + es[3]
    cnt = (hots[0] + hots[1] + hots[2] + hots[3]).astype(BF16)
    r_i = lax.broadcasted_iota(jnp.int32, (tm, tm), 0)
    c_i = lax.broadcasted_iota(jnp.int32, (tm, tm), 1)
    before = jnp.dot(cnt, (r_i < c_i).astype(BF16), preferred_element_type=F32)
    before = before + jnp.concatenate([cnt_scr[...]] * (tm // LANES), axis=1)
    row = lax.broadcasted_iota(jnp.int32, (2 * TOP_K, tm), 0)
    ri = jnp.zeros((2 * TOP_K, tm), F32)
    rg = jnp.zeros((2 * TOP_K, tm), F32)
    for k in range(TOP_K):
        rank = jnp.sum(before * hots[k], axis=0, keepdims=True)
        ri = jnp.where(row == k, idxs[k], ri)
        ri = jnp.where(row == TOP_K + k, rank, ri)
        rg = jnp.where(row == k, es[k] / tot, rg)
    ri_ref[...] = ri.astype(jnp.int32)
    rg_ref[...] = jnp.concatenate([rg, jnp.zeros((LANES - 2 * TOP_K, tm), F32)], axis=0).T
    total = cnt_scr[...] + jnp.dot(cnt, jnp.ones((tm, LANES), BF16), preferred_element_type=F32)
    cnt_scr[...] = total
    cnt_ref[...] = total


def _const_spec(shape):
    nd = len(shape)
    return pl.BlockSpec(shape, lambda *_: (0,) * nd)


def _mixer_a_kernel(x_ref, wu_ref, wv_ref, bu_ref, bv_ref, lg_ref, lb_ref, ws_ref, bst_ref,
                    wo_ref, bo_ref, g1_ref, b1_ref, wr_ref, br_ref,
                    x1_ref, x1p_ref, ri_ref, rg_ref, cnt_ref, v_scr, o_scr, cnt_scr):
    x = x_ref[...]
    xb = x.astype(BF16)
    tm, dff = v_scr.shape
    gd = dff // A_GROUPS
    gcols = [slice(g * gd, (g + 1) * gd) for g in range(A_GROUPS)]
    s1 = jnp.zeros((tm, 1), F32)
    s2 = jnp.zeros((tm, 1), F32)
    shift = None
    for cols in gcols:
        v_g = _gelu(jnp.dot(xb, wv_ref[:, cols], preferred_element_type=F32) + bv_ref[:, cols])
        v_scr[:, cols] = v_g
        if shift is None:
            shift = jnp.mean(v_g, axis=-1, keepdims=True)
        dv = v_g - shift
        s1 = s1 + jnp.sum(dv, axis=-1, keepdims=True)
        s2 = s2 + jnp.sum(dv * dv, axis=-1, keepdims=True)
    dmu = s1 * (1.0 / dff)
    mu = shift + dmu
    rstd = lax.rsqrt(s2 * (1.0 / dff) - dmu * dmu + LN_EPS)
    pi = lax.broadcasted_iota(jnp.int32, (A_BLOCK, A_BLOCK), 0) // CHUNK
    pj = lax.broadcasted_iota(jnp.int32, (A_BLOCK, A_BLOCK), 1) // CHUNK
    mask = pj <= pi
    bst = bst_ref[...]
    for g, cols in enumerate(gcols):
        u_g = _gelu(jnp.dot(xb, wu_ref[:, cols], preferred_element_type=F32) + bu_ref[:, cols])
        vb_g = ((v_scr[:, cols] - mu) * rstd * lg_ref[:, cols] + lb_ref[:, cols]).astype(BF16)
        wm = jnp.where(mask, ws_ref[g], 0.0).astype(BF16)
        for n in range(tm // A_BLOCK):
            rows = slice(n * A_BLOCK, (n + 1) * A_BLOCK)
            sv = jnp.dot(wm, vb_g[rows, :], preferred_element_type=F32) + bst[:, g:g + 1]
            o_scr[rows, cols] = (u_g[rows, :] * sv).astype(BF16)
    m = jnp.dot(o_scr[...], wo_ref[...], preferred_element_type=F32) + bo_ref[...]
    x1 = _ln(DN_ALPHA * x + m, g1_ref[...], b1_ref[...])
    x1_ref[...] = x1
    x1p_ref[...] = _pack_bf16_pairs(x1)
    _route(x1, pl.program_id(0) == 0, wr_ref, br_ref, ri_ref, rg_ref, cnt_ref, cnt_scr)


def mixer_a_block(xt, j, w_in_b, b_in, ln_g, ln_b, w_s, b_s, w_out_b, b_out, g1, b1, w_r, b_r):
    t, d = xt.shape
    dff = w_out_b.shape[1]
    tm = TOKEN_TILE
    bu = b_in[:dff].reshape(1, dff)
    bv = b_in[dff:].reshape(1, dff)
    args = (xt, w_in_b, w_in_b, bu, bv, ln_g.reshape(1, dff), ln_b.reshape(1, dff), w_s, b_s.T,
            w_out_b, b_out.reshape(1, d), g1.reshape(1, d), b1.reshape(1, d), w_r, b_r)
    tok = lambda w: pl.BlockSpec((tm, w), lambda i: (i, 0))
    route = pl.BlockSpec((2 * TOP_K, tm), lambda i: (0, i))
    in_specs = [tok(d)] + [_const_spec(a.shape) for a in args[1:]]
    in_specs[1] = pl.BlockSpec((None, d, dff), lambda i: (j, 0, 0))
    in_specs[2] = pl.BlockSpec((None, d, dff), lambda i: (j, 0, 1))
    in_specs[9] = pl.BlockSpec((None, dff, d), lambda i: (j, 0, 0))
    out_shape = (jax.ShapeDtypeStruct((t, d), F32),
                 jax.ShapeDtypeStruct((t, d // 2), U32),
                 jax.ShapeDtypeStruct((2 * TOP_K, t), jnp.int32),
                 jax.ShapeDtypeStruct((t, LANES), F32),
                 jax.ShapeDtypeStruct((N_EXPERTS, LANES), F32))
    out_specs = (tok(d), tok(d // 2), route, tok(LANES), _const_spec((N_EXPERTS, LANES)))
    return pl.pallas_call(
        _mixer_a_kernel,
        out_shape=out_shape,
        grid=(t // tm,),
        in_specs=in_specs,
        out_specs=out_specs,
        scratch_shapes=[pltpu.VMEM((tm, dff), F32), pltpu.VMEM((tm, dff), BF16),
                        pltpu.VMEM((N_EXPERTS, LANES), F32)],
        compiler_params=pltpu.CompilerParams(
            dimension_semantics=("arbitrary",), vmem_limit_bytes=VMEM_LIMIT),
        name="mixer_a",
    )(*args)


def _split3_bf16(q):
    hi = q.astype(BF16)
    r1 = q - hi.astype(F32)
    mid = r1.astype(BF16)
    lo = (r1 - mid.astype(F32)).astype(BF16)
    return hi, mid, lo


def _mamba_kernel(x_ref, wz_ref, wxs_ref, wbc_ref, wdt_ref, cw_ref, cb_ref, dtb_ref,
                  alog_ref, dfull_ref, nw_ref, exp_ref, wo_ref, g1_ref, b1_ref, wr_ref, br_ref,
                  x1_ref, x1p_ref, ri_ref, rg_ref, cnt_ref,
                  xs_scr, bm_scr, cm_scr, z_scr, state_scr, y_scr, dtw_scr, cnt_scr, *ext_scrs):
    s = pl.program_id(1)
    tm = x_ref.shape[0]
    d_inner = xs_scr.shape[1]
    gn = bm_scr.shape[1]
    L, P, N = SSD_BLOCK, SSM_HEAD_DIM, SSM_STATE
    R = SSM_HEADS // SSM_GROUPS
    gw = R * P
    nblk = tm // L

    @pl.when(s == 0)
    def _():
        state_scr[...] = jnp.zeros(state_scr.shape, F32)
        for ext in ext_scrs:
            ext[...] = jnp.zeros(ext.shape, F32)

    x = x_ref[...]
    xb = x.astype(BF16)
    dt = jnp.dot(xb, wdt_ref[...], preferred_element_type=F32) + dtb_ref[...]
    dt = jnp.maximum(dt, 0.0) + jnp.log1p(jnp.exp(-jnp.abs(dt)))

    cw = cw_ref[...]
    cbias = cb_ref[...]
    w = CONV_COLS
    pad = SUBLANES
    n_col = len(ext_scrs) // 2
    for c, (ext, qext) in enumerate(zip(ext_scrs[:n_col], ext_scrs[n_col:])):
        cols = slice(c * w, (c + 1) * w)
        ext[0:pad, :] = jnp.where(s > 0, ext[tm:tm + pad, :], 0.0)
        w_blk = wxs_ref[:, cols] if c * w < d_inner else wbc_ref[:, c * w - d_inner:(c + 1) * w - d_inner]
        ext[pad:pad + tm, :] = jnp.dot(xb, w_blk, preferred_element_type=F32)
        x0 = ext[pad:pad + tm, :]
        xm1 = ext[pad - 1:pad - 1 + tm, :]
        q = cw[1:2, cols] * x0 + cw[0:1, cols] * xm1
        qext[0:pad, :] = jnp.where(s > 0, qext[tm:tm + pad, :], 0.0)
        qext[pad:pad + tm, :] = q
        acc = cbias[:, cols] + (cw[3:4, cols] * x0 + cw[2:3, cols] * xm1) + qext[pad - 2:pad - 2 + tm, :]
        act = _silu(acc).astype(BF16)
        lo = c * w
        if lo < d_inner:
            xs_scr[:, lo:lo + w] = act
        elif lo < d_inner + gn:
            bm_scr[:, lo - d_inner:lo - d_inner + w] = act
        else:
            cm_scr[:, lo - d_inner - gn:lo - d_inner - gn + w] = act

    def z_task(c):
        cols = slice(c * w, (c + 1) * w)
        z_scr[:, cols] = jnp.dot(xb, wz_ref[:, cols], preferred_element_type=F32).astype(BF16)

    z_tasks = [functools.partial(z_task, c) for c in range(d_inner // w)]

    a_row = -jnp.exp(alog_ref[...])
    adt = dt * a_row
    r_i = lax.broadcasted_iota(jnp.int32, (tm, tm), 0)
    c_i = lax.broadcasted_iota(jnp.int32, (tm, tm), 1)
    btril = ((r_i >= c_i) & (r_i // L == c_i // L)).astype(F32)
    acum = jnp.dot(btril, adt, preferred_element_type=F32,
                   precision=lax.Precision.HIGHEST)
    row_blk = lax.broadcasted_iota(jnp.int32, (tm, LANES), 0) // L
    alast = [acum[(b + 1) * L - 1:(b + 1) * L, :] for b in range(nblk)]
    alast_rows = alast[0]
    for b in range(1, nblk):
        alast_rows = jnp.where(row_blk == b, alast[b], alast_rows)
    dtw = dt * jnp.exp(alast_rows - acum)
    expand = exp_ref[...]
    dtw_scr[...] = jnp.dot(dtw.astype(BF16), expand, preferred_element_type=F32)
    acum2 = acum * LOG2E
    arow_t = acum2.T - jnp.log(dt.T) * LOG2E
    tril = lax.broadcasted_iota(jnp.int32, (L, L), 0) >= lax.broadcasted_iota(jnp.int32, (L, L), 1)
    lo_half = lax.broadcasted_iota(jnp.int32, (L, 2 * P), 1) < P

    for b in range(nblk):
        rows = slice(b * L, (b + 1) * L)
        acum_b = acum2[rows, :]
        elast = jnp.broadcast_to(jnp.exp(alast[b]), (SUBLANES, LANES))
        dec_all = sum(jnp.dot(piece, expand, preferred_element_type=F32)
                      for piece in _split3_bf16(elast))[0:1, :]
        for g in range(SSM_GROUPS):
            c_g = cm_scr[rows, g * N:(g + 1) * N]
            b_g = bm_scr[rows, g * N:(g + 1) * N]
            cb = lax.dot_general(c_g, b_g, (((1,), (1,)), ((), ())), preferred_element_type=F32)
            st = state_scr[g]
            y_off = jnp.dot(c_g, st.astype(BF16), preferred_element_type=F32)
            for q in range(R // 2):
                h0 = g * R + 2 * q
                ms, es = [], []
                for h in (h0, h0 + 1):
                    acol = jnp.broadcast_to(acum_b[:, h:h + 1], (L, L))
                    seg = acol - arow_t[h:h + 1, rows]
                    ms.append((cb * jnp.exp2(jnp.where(tril, seg, -jnp.inf))).astype(BF16))
                    es.append(jnp.exp2(acol))
                lhs = jnp.concatenate(ms, axis=1)
                cols = slice(h0 * P, (h0 + 2) * P)
                xp = xs_scr[rows, cols]
                zero = jnp.zeros_like(xp)
                rhs = jnp.concatenate([jnp.where(lo_half, xp, zero), jnp.where(lo_half, zero, xp)], axis=0)
                y_d = jnp.dot(lhs, rhs, preferred_element_type=F32)
                scale = jnp.where(lo_half, es[0], es[1])
                y_scr[rows, cols] = y_d + y_off[:, 2 * q * P:(2 * q + 2) * P] * scale
            gcols = slice(g * gw, (g + 1) * gw)
            xw = (xs_scr[rows, gcols].astype(F32) * dtw_scr[rows, gcols]).astype(BF16)
            contrib = lax.dot_general(b_g, xw, (((0,), (0,)), ((), ())), preferred_element_type=F32)
            state_scr[g] = st * dec_all[:, gcols] + contrib
            if z_tasks:
                z_tasks.pop(0)()
    for task in z_tasks:
        task()

    y = y_scr[...] + dfull_ref[...] * xs_scr[...].astype(F32)
    y = y * _silu(z_scr[...].astype(F32))
    nw = nw_ref[...]
    parts = []
    for g in range(SSM_GROUPS):
        yg = y[:, g * gw:(g + 1) * gw]
        yg = yg * lax.rsqrt(jnp.mean(yg * yg, axis=-1, keepdims=True) + RMS_EPS)
        parts.append((yg * nw[:, g * gw:(g + 1) * gw]).astype(BF16))
    yb = jnp.concatenate(parts, axis=1)
    m = jnp.dot(yb, wo_ref[...], preferred_element_type=F32)
    x1 = _ln(DN_ALPHA * x + m, g1_ref[...], b1_ref[...])
    x1_ref[...] = x1
    x1p_ref[...] = _pack_bf16_pairs(x1)
    first = (pl.program_id(0) == 0) & (s == 0)
    _route(x1, first, wr_ref, br_ref, ri_ref, rg_ref, cnt_ref, cnt_scr)


def mamba_block(x3, j, w_in_b, w_dt, conv_w, conv_b, dt_bias, a_log, d_skip, norm_w, w_out_b, g1, b1, w_r, b_r):
    bsz, s_len, d = x3.shape
    d_inner = SSM_HEADS * SSM_HEAD_DIM
    gn = SSM_GROUPS * SSM_STATE
    conv_dim = d_inner + 2 * gn
    tm = TOKEN_TILE
    wdt = jnp.pad(w_dt, ((0, 0), (0, LANES - SSM_HEADS))).astype(BF16)
    dtb = jnp.pad(dt_bias, (0, LANES - SSM_HEADS)).reshape(1, LANES)
    alog = jnp.pad(a_log, (0, LANES - SSM_HEADS)).reshape(1, LANES)
    dfull = jnp.repeat(d_skip, SSM_HEAD_DIM).reshape(1, d_inner)
    head_of_col = jnp.arange(d_inner, dtype=jnp.int32) // SSM_HEAD_DIM
    expand = (jnp.arange(LANES, dtype=jnp.int32)[:, None] == head_of_col[None, :]).astype(BF16)
    args = (x3, w_in_b, w_in_b, w_in_b, wdt, conv_w, conv_b.reshape(1, conv_dim), dtb, alog, dfull,
            norm_w.reshape(1, d_inner), expand, w_out_b, g1.reshape(1, d), b1.reshape(1, d), w_r, b_r)
    tok = lambda w: pl.BlockSpec((None, tm, w), lambda b, s: (b, s, 0))
    in_specs = [tok(d)] + [_const_spec(a.shape) for a in args[1:]]
    in_specs[1] = pl.BlockSpec((None, d, d_inner), lambda b, s: (j, 0, 0))
    in_specs[2] = pl.BlockSpec((None, d, d_inner), lambda b, s: (j, 0, 1))
    in_specs[3] = pl.BlockSpec((None, d, 2 * gn), lambda b, s: (j, 0, 2 * d_inner // (2 * gn)))
    in_specs[12] = pl.BlockSpec((None, d_inner, d), lambda b, s: (j, 0, 0))
    ns = s_len // tm
    route = pl.BlockSpec((2 * TOP_K, tm), lambda b, s: (0, b * ns + s))
    out_shape = (jax.ShapeDtypeStruct((bsz, s_len, d), F32),
                 jax.ShapeDtypeStruct((bsz, s_len, d // 2), U32),
                 jax.ShapeDtypeStruct((2 * TOP_K, bsz * s_len), jnp.int32),
                 jax.ShapeDtypeStruct((bsz, s_len, LANES), F32),
                 jax.ShapeDtypeStruct((N_EXPERTS, LANES), F32))
    out_specs = (tok(d), tok(d // 2), route, tok(LANES), _const_spec((N_EXPERTS, LANES)))
    return pl.pallas_call(
        _mamba_kernel,
        out_shape=out_shape,
        grid=(bsz, ns),
        in_specs=in_specs,
        out_specs=out_specs,
        scratch_shapes=[pltpu.VMEM((tm, d_inner), BF16), pltpu.VMEM((tm, gn), BF16), pltpu.VMEM((tm, gn), BF16),
                        pltpu.VMEM((tm, d_inner), BF16),
                        pltpu.VMEM((SSM_GROUPS, SSM_STATE, d_inner // SSM_GROUPS), F32),
                        pltpu.VMEM((tm, d_inner), F32),
                        pltpu.VMEM((tm, d_inner), F32),
                        pltpu.VMEM((N_EXPERTS, LANES), F32)] +
                       [pltpu.VMEM((tm + SUBLANES, CONV_COLS), F32) for _ in range(2 * conv_dim // CONV_COLS)],
        compiler_params=pltpu.CompilerParams(
            dimension_semantics=("arbitrary", "arbitrary"), vmem_limit_bytes=VMEM_LIMIT),
        name="mamba",
    )(*args)


def _sc_workers():
    info = plsc.get_sparse_core_info()
    return info.num_cores, info.num_subcores


def sc_scatter_rows(x, dest, n_rows):
    t, d = x.shape
    nc, ns = _sc_workers()
    per_w, rem = divmod(t, nc * ns)
    assert rem == 0 and per_w % SC_CHUNK == 0
    chunks_per_w = per_w // SC_CHUNK
    dest_chunks = dest.reshape(TOP_K, t // SC_CHUNK, SC_CHUNK).transpose(1, 0, 2)
    mesh = plsc.VectorSubcoreMesh(core_axis_name="core", subcore_axis_name="subcore")

    @functools.partial(
        pl.kernel,
        out_type=jax.ShapeDtypeStruct((n_rows, d), x.dtype),
        mesh=mesh,
        scratch_types=[pltpu.VMEM((TOP_K, SC_CHUNK), jnp.int32), pltpu.VMEM((SC_CHUNK, d), x.dtype),
                       pltpu.SemaphoreType.DMA],
    )
    def scatter(x_hbm, i_hbm, o_hbm, idx_v, rows_v, sem):
        wid = lax.axis_index("subcore") * nc + lax.axis_index("core")

        @pl.loop(0, chunks_per_w)
        def _(c):
            chunk = wid * chunks_per_w + c
            pltpu.sync_copy(x_hbm.at[pl.ds(chunk * SC_CHUNK, SC_CHUNK)], rows_v)
            pltpu.sync_copy(i_hbm.at[chunk], idx_v)
            copies = [pltpu.make_async_copy(rows_v, o_hbm.at[idx_v.at[k]], sem) for k in range(TOP_K)]
            for cp in copies:
                cp.start()
            for cp in copies:
                cp.wait()

    return scatter(x, dest_chunks)


def sc_gather_rows(table, idx):
    n = idx.shape[0]
    d = table.shape[1]
    nc, ns = _sc_workers()
    per_w, rem = divmod(n, nc * ns)
    n_chunks = per_w // SC_CHUNK
    assert rem == 0 and per_w % (2 * SC_CHUNK) == 0
    mesh = plsc.VectorSubcoreMesh(core_axis_name="core", subcore_axis_name="subcore")

    @functools.partial(
        pl.kernel,
        out_type=jax.ShapeDtypeStruct((n, d), table.dtype),
        mesh=mesh,
        scratch_types=[
            pltpu.VMEM((SC_CHUNK,), jnp.int32), pltpu.VMEM((SC_CHUNK,), jnp.int32),
            pltpu.VMEM((SC_CHUNK, d), table.dtype), pltpu.VMEM((SC_CHUNK, d), table.dtype),
            pltpu.SemaphoreType.DMA, pltpu.SemaphoreType.DMA,
            pltpu.SemaphoreType.DMA, pltpu.SemaphoreType.DMA,
        ],
    )
    def gather(x_hbm, i_hbm, o_hbm, idx0, idx1, rows0, rows1, sg0, sg1, sw0, sw1):
        wid = lax.axis_index("subcore") * nc + lax.axis_index("core")
        base = wid * per_w
        idx = (idx0, idx1)
        rows = (rows0, rows1)
        sg = (sg0, sg1)
        sw = (sw0, sw1)

        def gather_copy(slot):
            return pltpu.make_async_copy(x_hbm.at[idx[slot]], rows[slot], sg[slot])

        def write_copy(c, slot):
            return pltpu.make_async_copy(rows[slot], o_hbm.at[pl.ds(base + c * SC_CHUNK, SC_CHUNK)], sw[slot])

        pltpu.sync_copy(i_hbm.at[pl.ds(base, SC_CHUNK)], idx0)
        gather_copy(0).start()

        @pl.loop(0, n_chunks, step=2)
        def _(c0):
            for slot in range(2):
                c = c0 + slot
                nxt = 1 - slot

                @pl.when(c + 1 < n_chunks)
                def _():
                    pltpu.sync_copy(i_hbm.at[pl.ds(base + (c + 1) * SC_CHUNK, SC_CHUNK)], idx[nxt])

                    @pl.when(c >= 1)
                    def _():
                        write_copy(c - 1, nxt).wait()

                    gather_copy(nxt).start()

                gather_copy(slot).wait()
                write_copy(c, slot).start()

        write_copy(n_chunks - 2, 0).wait()
        write_copy(n_chunks - 1, 1).wait()

    return gather(table, idx)


def _moe_mlp_kernel(layer, be_ref, nv_ref, nu_ref, seg_ref, nxt_ref,
                    x_ref, w1_hbm, b1_ref, w2_hbm, b2_ref, o_ref, w1f, w2f, w1b, w2b, sems):
    i = pl.program_id(0)

    def fetch(e, slot):
        return (pltpu.make_async_copy(w1_hbm.at[layer, e], w1f.at[slot], sems.at[slot, 0]),
                pltpu.make_async_copy(w2_hbm.at[layer, e], w2f.at[slot], sems.at[slot, 1]))

    @pl.when(i == 0)
    def _():
        for cp in fetch(be_ref[0], 0):
            cp.start()

    slot = seg_ref[i]

    @pl.when(slot >= 0)
    def _():
        for cp in fetch(be_ref[i], slot):
            cp.wait()
        w1b[...] = w1f[slot].astype(BF16)
        w2b[...] = w2f[slot].astype(BF16)

        @pl.when(nxt_ref[i] >= 0)
        def _():
            for cp in fetch(nxt_ref[i], 1 - slot):
                cp.start()

    nv = nv_ref[i]
    half = MOE_BLOCK // 2

    def mlp(n_rows):
        dff = w2b.shape[0]
        row = lax.broadcasted_iota(jnp.int32, (n_rows, x_ref.shape[1]), 0)
        xw = jnp.where(row < nv, x_ref[0:n_rows, :], jnp.uint32(0))
        xb = _unpack_bf16_pairs(xw).astype(BF16)
        h = jnp.dot(xb, w1b[...], preferred_element_type=F32) + b1_ref[...]
        g = jnp.minimum(h[:, :dff], SWIGLU_LIMIT)
        lin = jnp.clip(h[:, dff:], -SWIGLU_LIMIT, SWIGLU_LIMIT)
        act = (g * jax.nn.sigmoid(SWIGLU_ALPHA * g) * (lin + 1.0)).astype(BF16)
        y = jnp.dot(act, w2b[...], preferred_element_type=F32) + b2_ref[...]
        o_ref[0:n_rows, :] = _pack_bf16_pairs(y)

    pl.when(nv > half)(functools.partial(mlp, MOE_BLOCK))
    pl.when((nv > 0) & (nv <= half))(functools.partial(mlp, half))


def moe_mlp(xp, block_expert, n_valid, n_used, seg_slot, next_expert, layer, w1, b1, w2, b2):
    n_rows, dh = xp.shape
    _, n_e, d, two_dff = w1.shape
    dff = two_dff // 2
    n_blocks = n_rows // MOE_BLOCK
    row = lambda i, be, nv, nu, sg, nx: (jnp.minimum(i, nu[0] - 1), 0)
    bsel = lambda i, be, nv, nu, sg, nx: (layer, be[i], 0, 0)
    grid_spec = pltpu.PrefetchScalarGridSpec(
        num_scalar_prefetch=5,
        grid=(n_blocks,),
        in_specs=[
            pl.BlockSpec((MOE_BLOCK, dh), row),
            pl.BlockSpec(memory_space=pl.ANY),
            pl.BlockSpec((None, None, 1, two_dff), bsel),
            pl.BlockSpec(memory_space=pl.ANY),
            pl.BlockSpec((None, None, 1, d), bsel),
        ],
        out_specs=pl.BlockSpec((MOE_BLOCK, dh), row),
        scratch_shapes=[pltpu.VMEM((2, d, two_dff), F32), pltpu.VMEM((2, dff, d), F32),
                        pltpu.VMEM((d, two_dff), BF16), pltpu.VMEM((dff, d), BF16),
                        pltpu.SemaphoreType.DMA((2, 2))],
    )
    return pl.pallas_call(
        functools.partial(_moe_mlp_kernel, layer),
        out_shape=jax.ShapeDtypeStruct((n_rows, dh), U32),
        grid_spec=grid_spec,
        compiler_params=pltpu.CompilerParams(
            dimension_semantics=("arbitrary",), vmem_limit_bytes=VMEM_LIMIT),
        name="moe_mlp",
    )(block_expert, n_valid, n_used, seg_slot, next_expert, xp, w1, b1.reshape(b1.shape[0], n_e, 1, two_dff),
      w2, b2.reshape(b2.shape[0], n_e, 1, d))


def _combine_kernel(x1_ref, y0_ref, y1_ref, y2_ref, y3_ref, rg_ref, g2_ref, b2_ref, *rest):
    o_ref = rest[-1]
    rg = rg_ref[...]
    f = rg[:, 0:1] * _unpack_bf16_pairs(y0_ref[...])
    for k, y_ref in ((1, y1_ref), (2, y2_ref), (3, y3_ref)):
        f = f + rg[:, k:k + 1] * _unpack_bf16_pairs(y_ref[...])
    o_ref[...] = _ln(DN_ALPHA * x1_ref[...] + f, g2_ref[...], b2_ref[...])


def combine_block(x1, y4p, rg, g2, b2, part, partial_out):
    t, d = x1.shape
    tm = TOKEN_TILE
    ntp = t // tm // COMBINE_PARTS
    base = part * ntp
    tok = lambda w: pl.BlockSpec((tm, w), lambda i: (base + i, 0))
    y_specs = [pl.BlockSpec((tm, d // 2), functools.partial(lambda i, k: (k * ntp + i, 0), k=k))
               for k in range(TOP_K)]
    in_specs = [tok(d)] + y_specs + [tok(LANES), _const_spec((1, d)), _const_spec((1, d))]
    args = [x1, y4p, y4p, y4p, y4p, rg, g2.reshape(1, d), b2.reshape(1, d)]
    aliases = {}
    if partial_out is not None:
        in_specs.append(pl.BlockSpec(memory_space=pl.ANY))
        args.append(partial_out)
        aliases = {len(args) - 1: 0}
    return pl.pallas_call(
        _combine_kernel,
        out_shape=jax.ShapeDtypeStruct((t, d), F32),
        grid=(ntp,),
        in_specs=in_specs,
        out_specs=tok(d),
        input_output_aliases=aliases,
        compiler_params=pltpu.CompilerParams(
            dimension_semantics=("arbitrary",), vmem_limit_bytes=VMEM_LIMIT),
        name="moe_combine",
    )(*args)


def moe_block(x1, x1p, ri, rg, cnt, layer, w1, b1, w2, b2, g2, b2n):
    t, d = x1.shape
    n_assign = t * TOP_K
    n_blocks = -(-n_assign // MOE_BLOCK) + N_EXPERTS
    n_rows = n_blocks * MOE_BLOCK
    ar = jnp.arange(N_EXPERTS, dtype=jnp.int32)
    counts = cnt[:, 0].astype(jnp.int32)
    padded = (counts + MOE_BLOCK - 1) // MOE_BLOCK * MOE_BLOCK
    pend = jnp.sum(jnp.where(ar[None, :] <= ar[:, None], padded[None, :], 0), axis=1)
    pstart = pend - padded
    top_i = ri[:TOP_K]
    dest = jnp.sum(jnp.where(top_i[:, :, None] == ar, pstart, 0), axis=-1) + ri[TOP_K:]
    n_used = (pend[-1] // MOE_BLOCK).reshape(1)
    block_start = jnp.arange(n_blocks, dtype=jnp.int32) * MOE_BLOCK
    block_expert = jnp.minimum(
        jnp.sum((pend[None, :] <= block_start[:, None]).astype(jnp.int32), axis=1), N_EXPERTS - 1)
    last = jnp.sum(jnp.where(jnp.arange(n_blocks) == n_used[0] - 1, block_expert, 0))
    block_expert = jnp.where(jnp.arange(n_blocks) < n_used[0], block_expert, last)
    vend = jnp.sum(jnp.where(block_expert[:, None] == ar, (pstart + counts)[None, :], 0), axis=1)
    n_valid = jnp.clip(vend - block_start, 0, MOE_BLOCK)
    blk = jnp.arange(n_blocks)
    starts = (blk < n_used[0]) & ((blk == 0) | (block_expert != jnp.roll(block_expert, 1)))
    seg_slot = jnp.where(starts, (jnp.cumsum(starts.astype(jnp.int32)) - 1) % 2, -1).astype(jnp.int32)
    later = (counts[None, :] > 0) & (ar[None, :] > ar[:, None])
    next_of = jnp.min(jnp.where(later, ar[None, :], N_EXPERTS), axis=1)
    next_of = jnp.where(next_of == N_EXPERTS, -1, next_of)
    next_expert = jnp.sum(jnp.where(block_expert[:, None] == ar, next_of[None, :], 0), axis=1).astype(jnp.int32)

    xp = sc_scatter_rows(x1p, dest, n_rows)
    yp = moe_mlp(xp, block_expert, n_valid, n_used, seg_slot, next_expert, layer, w1, b1, w2, b2)
    tp = t // COMBINE_PARTS
    out = None
    for part in range(COMBINE_PARTS):
        y4p = sc_gather_rows(yp, dest[:, part * tp:(part + 1) * tp].reshape(-1))
        out = combine_block(x1, y4p, rg, g2, b2n, part, out)
    return out


def kernel(x, a_w_in, a_b_in, a_ln_g, a_ln_b, a_w_s, a_b_s, a_w_out, a_b_out, b_w_in, b_conv_w, b_conv_b, b_dt_bias, b_a_log, b_d, b_norm_w, b_w_out, moe_w_router, moe_b_router, moe_w1, moe_b1, moe_w2, moe_b2, ln1_g, ln1_b, ln2_g, ln2_b):
    bsz, s_len, d = x.shape
    t = bsz * s_len
    xt = x.reshape(t, d)
    a_w_in_b, a_w_out_b = a_w_in.astype(BF16), a_w_out.astype(BF16)
    b_w_in_b, b_w_out_b = b_w_in.astype(BF16), b_w_out.astype(BF16)
    dt_col0 = b_w_in.shape[2] - SSM_HEADS
    for i in range(DEPTH):
        j = i // 2
        w_r32 = jnp.pad(moe_w_router[i], ((0, 0), (0, LANES - N_EXPERTS)))
        w_r_hi = w_r32.astype(BF16)
        w_r = jnp.stack([w_r_hi, (w_r32 - w_r_hi.astype(F32)).astype(BF16)])
        b_r = moe_b_router[i].reshape(N_EXPERTS, 1)
        if i % 2 == 0:
            x1, x1p, ri, rg, cnt = mixer_a_block(
                xt, j, a_w_in_b, a_b_in[j], a_ln_g[j], a_ln_b[j], a_w_s[j], a_b_s[j],
                a_w_out_b, a_b_out[j], ln1_g[i], ln1_b[i], w_r, b_r)
        else:
            x3 = xt.reshape(bsz, s_len, d)
            x1, x1p, ri, rg, cnt = mamba_block(
                x3, j, b_w_in_b, b_w_in[j, :, dt_col0:], b_conv_w[j], b_conv_b[j], b_dt_bias[j], b_a_log[j],
                b_d[j], b_norm_w[j], b_w_out_b, ln1_g[i], ln1_b[i], w_r, b_r)
            x1 = x1.reshape(t, d)
            x1p = x1p.reshape(t, d // 2)
            rg = rg.reshape(t, LANES)
        xt = moe_block(x1, x1p, ri, rg, cnt, i, moe_w1, moe_b1, moe_w2, moe_b2, ln2_g[i], ln2_b[i])
    return xt.reshape(bsz, s_len, d)
```

```python
import functools
import math

import jax
import jax.numpy as jnp
from jax import lax
from jax.experimental import pallas as pl
from jax.experimental.pallas import tpu as pltpu
from jax.experimental.pallas import tpu_sc as plsc

F32 = jnp.float32
BF16 = jnp.bfloat16
U32 = jnp.uint32

DEPTH = 4
N_EXPERTS = 32
TOP_K = 4
MOE_BLOCK = 512
SWIGLU_LIMIT = 7.0
SWIGLU_ALPHA = 1.702
A_BLOCK = 128
A_GROUPS = 8
CHUNK = 64
SSD_BLOCK = 128
SSM_HEADS = 32
SSM_HEAD_DIM = 64
SSM_GROUPS = 4
SSM_STATE = 128
SSM_CONV = 4
DN_ALPHA = (2 * DEPTH) ** 0.25
LN_EPS = 1e-5
LOG2E = 1.0 / math.log(2.0)
RMS_EPS = 1e-5

LANES = 128
SUBLANES = 8
TOKEN_TILE = 256
VMEM_LIMIT = 56 * 1024 * 1024
SC_CHUNK = 64
SC_SCATTER_CHUNK = 128
CONV_COLS = 256
COMBINE_PARTS = 4


def _ln(x, g, b):
    mu = jnp.mean(x, axis=-1, keepdims=True)
    xc = x - mu
    var = jnp.mean(xc * xc, axis=-1, keepdims=True)
    return xc * lax.rsqrt(var + LN_EPS) * g + b


def _gelu(x):
    return 0.5 * x * (1.0 + lax.erf(x * (1.0 / math.sqrt(2.0))))


def _silu(x):
    h = 0.5 * x
    return h + h * jnp.tanh(h)


def _pack_bf16_pairs(y):
    h = y.shape[1] // 2
    lo = lax.bitcast_convert_type(y[:, :h].astype(BF16).astype(F32), U32) >> 16
    hi = lax.bitcast_convert_type(y[:, h:].astype(BF16).astype(F32), U32) & jnp.uint32(0xFFFF0000)
    return hi | lo


def _unpack_bf16_pairs(w):
    lo = lax.bitcast_convert_type(w << 16, F32)
    hi = lax.bitcast_convert_type(w & jnp.uint32(0xFFFF0000), F32)
    return jnp.concatenate([lo, hi], axis=1)


def _route(x1, first, wr_ref, brc_ref, ri_ref, rg_ref, cnt_ref, cnt_scr):
    @pl.when(first)
    def _():
        cnt_scr[...] = jnp.zeros(cnt_scr.shape, F32)

    x_hi = x1.astype(BF16)
    x_lo = (x1 - x_hi.astype(F32)).astype(BF16)
    logits_tok = (jnp.dot(x_hi, wr_ref[0], preferred_element_type=F32)
                  + (jnp.dot(x_hi, wr_ref[1], preferred_element_type=F32)
                     + jnp.dot(x_lo, wr_ref[0], preferred_element_type=F32)))
    n_e = brc_ref.shape[0]
    logits = logits_tok.T[:n_e, :] + brc_ref[...]
    tm = logits.shape[1]
    sub_f = lax.broadcasted_iota(jnp.int32, logits.shape, 0).astype(F32)
    vals, idxs, hots = [], [], []
    l = logits
    for _ in range(TOP_K):
        m = jnp.max(l, axis=0, keepdims=True)
        i = jnp.min(jnp.where(l == m, sub_f, float(n_e)), axis=0, keepdims=True)
        hot = sub_f == i
        vals.append(m)
        idxs.append(i)
        hots.append(hot.astype(F32))
        l = jnp.where(hot, -jnp.inf, l)
    es = [jnp.exp(v - vals[0]) for v in vals]
    tot = es[0] + es[1] + es[2] + es[3]
    cnt = (hots[0] + hots[1] + hots[2] + hots[3]).astype(BF16)
    r_i = lax.broadcasted_iota(jnp.int32, (tm, tm), 0)
    c_i = lax.broadcasted_iota(jnp.int32, (tm, tm), 1)
    before = jnp.dot(cnt, (r_i < c_i).astype(BF16), preferred_element_type=F32)
    before = before + jnp.concatenate([cnt_scr[...]] * (tm // LANES), axis=1)
    row = lax.broadcasted_iota(jnp.int32, (2 * TOP_K, tm), 0)
    ri = jnp.zeros((2 * TOP_K, tm), F32)
    rg = jnp.zeros((2 * TOP_K, tm), F32)
    for k in range(TOP_K):
        rank = jnp.sum(before * hots[k], axis=0, keepdims=True)
        ri = jnp.where(row == k, idxs[k], ri)
        ri = jnp.where(row == TOP_K + k, rank, ri)
        rg = jnp.where(row == k, es[k] / tot, rg)
    ri_ref[...] = ri.astype(jnp.int32)
    rg_ref[...] = jnp.concatenate([rg, jnp.zeros((LANES - 2 * TOP_K, tm), F32)], axis=0).T
    total = cnt_scr[...] + jnp.dot(cnt, jnp.ones((tm, LANES), BF16), preferred_element_type=F32)
    cnt_scr[...] = total
    cnt_ref[...] = total


def _const_spec(shape):
    nd = len(shape)
    return pl.BlockSpec(shape, lambda *_: (0,) * nd)


def _mixer_a_kernel(x_ref, wu_ref, wv_ref, bu_ref, bv_ref, lg_ref, lb_ref, ws_ref, bst_ref,
                    wo_ref, bo_ref, g1_ref, b1_ref, wr_ref, br_ref,
                    x1_ref, x1p_ref, ri_ref, rg_ref, cnt_ref, v_scr, o_scr, cnt_scr):
    x = x_ref[...]
    xb = x.astype(BF16)
    tm, dff = v_scr.shape
    gd = dff // A_GROUPS
    gcols = [slice(g * gd, (g + 1) * gd) for g in range(A_GROUPS)]
    s1 = jnp.zeros((tm, 1), F32)
    s2 = jnp.zeros((tm, 1), F32)
    shift = None
    for cols in gcols:
        v_g = _gelu(jnp.dot(xb, wv_ref[:, cols], preferred_element_type=F32) + bv_ref[:, cols])
        v_scr[:, cols] = v_g
        if shift is None:
            shift = jnp.mean(v_g, axis=-1, keepdims=True)
        dv = v_g - shift
        s1 = s1 + jnp.sum(dv, axis=-1, keepdims=True)
        s2 = s2 + jnp.sum(dv * dv, axis=-1, keepdims=True)
    dmu = s1 * (1.0 / dff)
    mu = shift + dmu
    rstd = lax.rsqrt(s2 * (1.0 / dff) - dmu * dmu + LN_EPS)
    pi = lax.broadcasted_iota(jnp.int32, (A_BLOCK, A_BLOCK), 0) // CHUNK
    pj = lax.broadcasted_iota(jnp.int32, (A_BLOCK, A_BLOCK), 1) // CHUNK
    mask = pj <= pi
    bst = bst_ref[...]
    for g, cols in enumerate(gcols):
        u_g = _gelu(jnp.dot(xb, wu_ref[:, cols], preferred_element_type=F32) + bu_ref[:, cols])
        vb_g = ((v_scr[:, cols] - mu) * rstd * lg_ref[:, cols] + lb_ref[:, cols]).astype(BF16)
        wm = jnp.where(mask, ws_ref[g], 0.0).astype(BF16)
        for n in range(tm // A_BLOCK):
            rows = slice(n * A_BLOCK, (n + 1) * A_BLOCK)
            sv = jnp.dot(wm, vb_g[rows, :], preferred_element_type=F32) + bst[:, g:g + 1]
            o_scr[rows, cols] = (u_g[rows, :] * sv).astype(BF16)
    m = jnp.dot(o_scr[...], wo_ref[...], preferred_element_type=F32) + bo_ref[...]
    x1 = _ln(DN_ALPHA * x + m, g1_ref[...], b1_ref[...])
    x1_ref[...] = x1
    x1p_ref[...] = _pack_bf16_pairs(x1)
    _route(x1, pl.program_id(0) == 0, wr_ref, br_ref, ri_ref, rg_ref, cnt_ref, cnt_scr)


def mixer_a_block(xt, j, w_in_b, b_in, ln_g, ln_b, w_s, b_s, w_out_b, b_out, g1, b1, w_r, b_r):
    t, d = xt.shape
    dff = w_out_b.shape[1]
    tm = TOKEN_TILE
    bu = b_in[:dff].reshape(1, dff)
    bv = b_in[dff:].reshape(1, dff)
    args = (xt, w_in_b, w_in_b, bu, bv, ln_g.reshape(1, dff), ln_b.reshape(1, dff), w_s, b_s.T,
            w_out_b, b_out.reshape(1, d), g1.reshape(1, d), b1.reshape(1, d), w_r, b_r)
    tok = lambda w: pl.BlockSpec((tm, w), lambda i: (i, 0))
    route = pl.BlockSpec((2 * TOP_K, tm), lambda i: (0, i))
    in_specs = [tok(d)] + [_const_spec(a.shape) for a in args[1:]]
    in_specs[1] = pl.BlockSpec((None, d, dff), lambda i: (j, 0, 0))
    in_specs[2] = pl.BlockSpec((None, d, dff), lambda i: (j, 0, 1))
    in_specs[9] = pl.BlockSpec((None, dff, d), lambda i: (j, 0, 0))
    out_shape = (jax.ShapeDtypeStruct((t, d), F32),
                 jax.ShapeDtypeStruct((t, d // 2), U32),
                 jax.ShapeDtypeStruct((2 * TOP_K, t), jnp.int32),
                 jax.ShapeDtypeStruct((t, LANES), F32),
                 jax.ShapeDtypeStruct((N_EXPERTS, LANES), F32))
    out_specs = (tok(d), tok(d // 2), route, tok(LANES), _const_spec((N_EXPERTS, LANES)))
    return pl.pallas_call(
        _mixer_a_kernel,
        out_shape=out_shape,
        grid=(t // tm,),
        in_specs=in_specs,
        out_specs=out_specs,
        scratch_shapes=[pltpu.VMEM((tm, dff), F32), pltpu.VMEM((tm, dff), BF16),
                        pltpu.VMEM((N_EXPERTS, LANES), F32)],
        compiler_params=pltpu.CompilerParams(
            dimension_semantics=("arbitrary",), vmem_limit_bytes=VMEM_LIMIT),
        name="mixer_a",
    )(*args)


def _split3_bf16(q):
    hi = q.astype(BF16)
    r1 = q - hi.astype(F32)
    mid = r1.astype(BF16)
    lo = (r1 - mid.astype(F32)).astype(BF16)
    return hi, mid, lo


def _mamba_kernel(x_ref, wz_ref, wxs_ref, wbc_ref, wdt_ref, cw_ref, cb_ref, dtb_ref,
                  alog_ref, dfull_ref, nw_ref, exp_ref, wo_ref, g1_ref, b1_ref, wr_ref, br_ref,
                  x1_ref, x1p_ref, ri_ref, rg_ref, cnt_ref,
                  xs_scr, bm_scr, cm_scr, z_scr, state_scr, y_scr, dtw_scr, cnt_scr, *ext_scrs):
    s = pl.program_id(1)
    tm = x_ref.shape[0]
    d_inner = xs_scr.shape[1]
    gn = bm_scr.shape[1]
    L, P, N = SSD_BLOCK, SSM_HEAD_DIM, SSM_STATE
    R = SSM_HEADS // SSM_GROUPS
    gw = R * P
    nblk = tm // L

    @pl.when(s == 0)
    def _():
        state_scr[...] = jnp.zeros(state_scr.shape, F32)
        for ext in ext_scrs:
            ext[...] = jnp.zeros(ext.shape, F32)

    x = x_ref[...]
    xb = x.astype(BF16)
    dt = jnp.dot(xb, wdt_ref[...], preferred_element_type=F32) + dtb_ref[...]
    dt = jnp.maximum(dt, 0.0) + jnp.log1p(jnp.exp(-jnp.abs(dt)))

    cw = cw_ref[...]
    cbias = cb_ref[...]
    w = CONV_COLS
    pad = SUBLANES
    n_col = len(ext_scrs) // 2
    for c, (ext, qext) in enumerate(zip(ext_scrs[:n_col], ext_scrs[n_col:])):
        cols = slice(c * w, (c + 1) * w)
        ext[0:pad, :] = jnp.where(s > 0, ext[tm:tm + pad, :], 0.0)
        w_blk = wxs_ref[:, cols] if c * w < d_inner else wbc_ref[:, c * w - d_inner:(c + 1) * w - d_inner]
        ext[pad:pad + tm, :] = jnp.dot(xb, w_blk, preferred_element_type=F32)
        x0 = ext[pad:pad + tm, :]
        xm1 = ext[pad - 1:pad - 1 + tm, :]
        q = cw[1:2, cols] * x0 + cw[0:1, cols] * xm1
        qext[0:pad, :] = jnp.where(s > 0, qext[tm:tm + pad, :], 0.0)
        qext[pad:pad + tm, :] = q
        acc = cbias[:, cols] + (cw[3:4, cols] * x0 + cw[2:3, cols] * xm1) + qext[pad - 2:pad - 2 + tm, :]
        act = _silu(acc).astype(BF16)
        lo = c * w
        if lo < d_inner:
            xs_scr[:, lo:lo + w] = act
        elif lo < d_inner + gn:
            bm_scr[:, lo - d_inner:lo - d_inner + w] = act
        else:
            cm_scr[:, lo - d_inner - gn:lo - d_inner - gn + w] = act

    def z_task(c):
        cols = slice(c * w, (c + 1) * w)
        z_scr[:, cols] = jnp.dot(xb, wz_ref[:, cols], preferred_element_type=F32).astype(BF16)

    z_tasks = [functools.partial(z_task, c) for c in range(d_inner // w)]

    a_row = -jnp.exp(alog_ref[...])
    adt = dt * a_row
    r_i = lax.broadcasted_iota(jnp.int32, (tm, tm), 0)
    c_i = lax.broadcasted_iota(jnp.int32, (tm, tm), 1)
    btril = ((r_i >= c_i) & (r_i // L == c_i // L)).astype(F32)
    acum = jnp.dot(btril, adt, preferred_element_type=F32,
                   precision=lax.Precision.HIGHEST)
    row_blk = lax.broadcasted_iota(jnp.int32, (tm, LANES), 0) // L
    alast = [acum[(b + 1) * L - 1:(b + 1) * L, :] for b in range(nblk)]
    alast_rows = alast[0]
    for b in range(1, nblk):
        alast_rows = jnp.where(row_blk == b, alast[b], alast_rows)
    dtw = dt * jnp.exp(alast_rows - acum)
    expand = exp_ref[...]
    dtw_scr[...] = jnp.dot(dtw.astype(BF16), expand, preferred_element_type=F32)
    acum2 = acum * LOG2E
    arow_t = acum2.T - jnp.log(dt.T) * LOG2E
    tril = lax.broadcasted_iota(jnp.int32, (L, L), 0) >= lax.broadcasted_iota(jnp.int32, (L, L), 1)
    lo_half = lax.broadcasted_iota(jnp.int32, (L, 2 * P), 1) < P

    for b in range(nblk):
        rows = slice(b * L, (b + 1) * L)
        acum_b = acum2[rows, :]
        elast = jnp.broadcast_to(jnp.exp(alast[b]), (SUBLANES, LANES))
        dec_all = sum(jnp.dot(piece, expand, preferred_element_type=F32)
                      for piece in _split3_bf16(elast))[0:1, :]
        for g in range(SSM_GROUPS):
            c_g = cm_scr[rows, g * N:(g + 1) * N]
            b_g = bm_scr[rows, g * N:(g + 1) * N]
            cb = lax.dot_general(c_g, b_g, (((1,), (1,)), ((), ())), preferred_element_type=F32)
            st = state_scr[g]
            y_off = jnp.dot(c_g, st.astype(BF16), preferred_element_type=F32)
            for q in range(R // 2):
                h0 = g * R + 2 * q
                ms, es = [], []
                for h in (h0, h0 + 1):
                    acol = jnp.broadcast_to(acum_b[:, h:h + 1], (L, L))
                    seg = acol - arow_t[h:h + 1, rows]
                    ms.append((cb * jnp.exp2(jnp.where(tril, seg, -jnp.inf))).astype(BF16))
                    es.append(jnp.exp2(acol))
                lhs = jnp.concatenate(ms, axis=1)
                cols = slice(h0 * P, (h0 + 2) * P)
                xp = xs_scr[rows, cols]
                zero = jnp.zeros_like(xp)
                rhs = jnp.concatenate([jnp.where(lo_half, xp, zero), jnp.where(lo_half, zero, xp)], axis=0)
                y_d = jnp.dot(lhs, rhs, preferred_element_type=F32)
                scale = jnp.where(lo_half, es[0], es[1])
                y_scr[rows, cols] = y_d + y_off[:, 2 * q * P:(2 * q + 2) * P] * scale
            gcols = slice(g * gw, (g + 1) * gw)
            xw = (xs_scr[rows, gcols].astype(F32) * dtw_scr[rows, gcols]).astype(BF16)
            contrib = lax.dot_general(b_g, xw, (((0,), (0,)), ((), ())), preferred_element_type=F32)
            state_scr[g] = st * dec_all[:, gcols] + contrib
            if z_tasks:
                z_tasks.pop(0)()
    for task in z_tasks:
        task()

    y = y_scr[...] + dfull_ref[...] * xs_scr[...].astype(F32)
    y = y * _silu(z_scr[...].astype(F32))
    nw = nw_ref[...]
    parts = []
    for g in range(SSM_GROUPS):
        yg = y[:, g * gw:(g + 1) * gw]
        yg = yg * lax.rsqrt(jnp.mean(yg * yg, axis=-1, keepdims=True) + RMS_EPS)
        parts.append((yg * nw[:, g * gw:(g + 1) * gw]).astype(BF16))
    yb = jnp.concatenate(parts, axis=1)
    m = jnp.dot(yb, wo_ref[...], preferred_element_type=F32)
    x1 = _ln(DN_ALPHA * x + m, g1_ref[...], b1_ref[...])
    x1_ref[...] = x1
    x1p_ref[...] = _pack_bf16_pairs(x1)
    first = (pl.program_id(0) == 0) & (s == 0)
    _route(x1, first, wr_ref, br_ref, ri_ref, rg_ref, cnt_ref, cnt_scr)


def mamba_block(x3, j, w_in_b, w_dt, conv_w, conv_b, dt_bias, a_log, d_skip, norm_w, w_out_b, g1, b1, w_r, b_r):
    bsz, s_len, d = x3.shape
    d_inner = SSM_HEADS * SSM_HEAD_DIM
    gn = SSM_GROUPS * SSM_STATE
    conv_dim = d_inner + 2 * gn
    tm = TOKEN_TILE
    wdt = jnp.pad(w_dt, ((0, 0), (0, LANES - SSM_HEADS))).astype(BF16)
    dtb = jnp.pad(dt_bias, (0, LANES - SSM_HEADS)).reshape(1, LANES)
    alog = jnp.pad(a_log, (0, LANES - SSM_HEADS)).reshape(1, LANES)
    dfull = jnp.repeat(d_skip, SSM_HEAD_DIM).reshape(1, d_inner)
    head_of_col = jnp.arange(d_inner, dtype=jnp.int32) // SSM_HEAD_DIM
    expand = (jnp.arange(LANES, dtype=jnp.int32)[:, None] == head_of_col[None, :]).astype(BF16)
    args = (x3, w_in_b, w_in_b, w_in_b, wdt, conv_w, conv_b.reshape(1, conv_dim), dtb, alog, dfull,
            norm_w.reshape(1, d_inner), expand, w_out_b, g1.reshape(1, d), b1.reshape(1, d), w_r, b_r)
    tok = lambda w: pl.BlockSpec((None, tm, w), lambda b, s: (b, s, 0))
    in_specs = [tok(d)] + [_const_spec(a.shape) for a in args[1:]]
    in_specs[1] = pl.BlockSpec((None, d, d_inner), lambda b, s: (j, 0, 0))
    in_specs[2] = pl.BlockSpec((None, d, d_inner), lambda b, s: (j, 0, 1))
    in_specs[3] = pl.BlockSpec((None, d, 2 * gn), lambda b, s: (j, 0, 2 * d_inner // (2 * gn)))
    in_specs[12] = pl.BlockSpec((None, d_inner, d), lambda b, s: (j, 0, 0))
    ns = s_len // tm
    route = pl.BlockSpec((2 * TOP_K, tm), lambda b, s: (0, b * ns + s))
    out_shape = (jax.ShapeDtypeStruct((bsz, s_len, d), F32),
                 jax.ShapeDtypeStruct((bsz, s_len, d // 2), U32),
                 jax.ShapeDtypeStruct((2 * TOP_K, bsz * s_len), jnp.int32),
                 jax.ShapeDtypeStruct((bsz, s_len, LANES), F32),
                 jax.ShapeDtypeStruct((N_EXPERTS, LANES), F32))
    out_specs = (tok(d), tok(d // 2), route, tok(LANES), _const_spec((N_EXPERTS, LANES)))
    return pl.pallas_call(
        _mamba_kernel,
        out_shape=out_shape,
        grid=(bsz, ns),
        in_specs=in_specs,
        out_specs=out_specs,
        scratch_shapes=[pltpu.VMEM((tm, d_inner), BF16), pltpu.VMEM((tm, gn), BF16), pltpu.VMEM((tm, gn), BF16),
                        pltpu.VMEM((tm, d_inner), BF16),
                        pltpu.VMEM((SSM_GROUPS, SSM_STATE, d_inner // SSM_GROUPS), F32),
                        pltpu.VMEM((tm, d_inner), F32),
                        pltpu.VMEM((tm, d_inner), F32),
                        pltpu.VMEM((N_EXPERTS, LANES), F32)] +
                       [pltpu.VMEM((tm + SUBLANES, CONV_COLS), F32) for _ in range(2 * conv_dim // CONV_COLS)],
        compiler_params=pltpu.CompilerParams(
            dimension_semantics=("arbitrary", "arbitrary"), vmem_limit_bytes=VMEM_LIMIT),
        name="mamba",
    )(*args)


def _sc_workers():
    info = plsc.get_sparse_core_info()
    return info.num_cores, info.num_subcores


def sc_scatter_rows(x, dest, n_rows):
    t, d = x.shape
    nc, ns = _sc_workers()
    per_w, rem = divmod(t, nc * ns)
    assert rem == 0 and per_w % SC_SCATTER_CHUNK == 0
    chunks_per_w = per_w // SC_SCATTER_CHUNK
    dest_chunks = dest.reshape(TOP_K, t // SC_SCATTER_CHUNK, SC_SCATTER_CHUNK).transpose(1, 0, 2)
    mesh = plsc.VectorSubcoreMesh(core_axis_name="core", subcore_axis_name="subcore")

    @functools.partial(
        pl.kernel,
        out_type=jax.ShapeDtypeStruct((n_rows, d), x.dtype),
        mesh=mesh,
        scratch_types=[pltpu.VMEM((TOP_K, SC_SCATTER_CHUNK), jnp.int32), pltpu.VMEM((SC_SCATTER_CHUNK, d), x.dtype),
                       pltpu.SemaphoreType.DMA],
    )
    def scatter(x_hbm, i_hbm, o_hbm, idx_v, rows_v, sem):
        wid = lax.axis_index("subcore") * nc + lax.axis_index("core")

        @pl.loop(0, chunks_per_w)
        def _(c):
            chunk = wid * chunks_per_w + c
            pltpu.sync_copy(x_hbm.at[pl.ds(chunk * SC_SCATTER_CHUNK, SC_SCATTER_CHUNK)], rows_v)
            pltpu.sync_copy(i_hbm.at[chunk], idx_v)
            copies = [pltpu.make_async_copy(rows_v, o_hbm.at[idx_v.at[k]], sem) for k in range(TOP_K)]
            for cp in copies:
                cp.start()
            for cp in copies:
                cp.wait()

    return scatter(x, dest_chunks)


def sc_gather_rows(table, idx):
    n = idx.shape[0]
    d = table.shape[1]
    nc, ns = _sc_workers()
    per_w, rem = divmod(n, nc * ns)
    n_chunks = per_w // SC_CHUNK
    assert rem == 0 and per_w % (2 * SC_CHUNK) == 0
    mesh = plsc.VectorSubcoreMesh(core_axis_name="core", subcore_axis_name="subcore")

    @functools.partial(
        pl.kernel,
        out_type=jax.ShapeDtypeStruct((n, d), table.dtype),
        mesh=mesh,
        scratch_types=[
            pltpu.VMEM((SC_CHUNK,), jnp.int32), pltpu.VMEM((SC_CHUNK,), jnp.int32),
            pltpu.VMEM((SC_CHUNK, d), table.dtype), pltpu.VMEM((SC_CHUNK, d), table.dtype),
            pltpu.SemaphoreType.DMA, pltpu.SemaphoreType.DMA,
            pltpu.SemaphoreType.DMA, pltpu.SemaphoreType.DMA,
        ],
    )
    def gather(x_hbm, i_hbm, o_hbm, idx0, idx1, rows0, rows1, sg0, sg1, sw0, sw1):
        wid = lax.axis_index("subcore") * nc + lax.axis_index("core")
        base = wid * per_w
        idx = (idx0, idx1)
        rows = (rows0, rows1)
        sg = (sg0, sg1)
        sw = (sw0, sw1)

        def gather_copy(slot):
            return pltpu.make_async_copy(x_hbm.at[idx[slot]], rows[slot], sg[slot])

        def write_copy(c, slot):
            return pltpu.make_async_copy(rows[slot], o_hbm.at[pl.ds(base + c * SC_CHUNK, SC_CHUNK)], sw[slot])

        pltpu.sync_copy(i_hbm.at[pl.ds(base, SC_CHUNK)], idx0)
        gather_copy(0).start()

        @pl.loop(0, n_chunks, step=2)
        def _(c0):
            for slot in range(2):
                c = c0 + slot
                nxt = 1 - slot

                @pl.when(c + 1 < n_chunks)
                def _():
                    pltpu.sync_copy(i_hbm.at[pl.ds(base + (c + 1) * SC_CHUNK, SC_CHUNK)], idx[nxt])

                    @pl.when(c >= 1)
                    def _():
                        write_copy(c - 1, nxt).wait()

                    gather_copy(nxt).start()

                gather_copy(slot).wait()
                write_copy(c, slot).start()

        write_copy(n_chunks - 2, 0).wait()
        write_copy(n_chunks - 1, 1).wait()

    return gather(table, idx)


def _moe_mlp_kernel(layer, be_ref, nv_ref, nu_ref, seg_ref, nxt_ref,
                    x_ref, w1_hbm, b1_ref, w2_hbm, b2_ref, o_ref, w1f, w2f, w1b, w2b, sems):
    i = pl.program_id(0)

    def fetch(e, slot):
        return (pltpu.make_async_copy(w1_hbm.at[layer, e], w1f.at[slot], sems.at[slot, 0]),
                pltpu.make_async_copy(w2_hbm.at[layer, e], w2f.at[slot], sems.at[slot, 1]))

    @pl.when(i == 0)
    def _():
        for cp in fetch(be_ref[0], 0):
            cp.start()

    slot = seg_ref[i]

    @pl.when(slot >= 0)
    def _():
        for cp in fetch(be_ref[i], slot):
            cp.wait()
        w1b[...] = w1f[slot].astype(BF16)
        w2b[...] = w2f[slot].astype(BF16)

        @pl.when(nxt_ref[i] >= 0)
        def _():
            for cp in fetch(nxt_ref[i], 1 - slot):
                cp.start()

    nv = nv_ref[i]
    half = MOE_BLOCK // 2

    def mlp(n_rows):
        dff = w2b.shape[0]
        row = lax.broadcasted_iota(jnp.int32, (n_rows, x_ref.shape[1]), 0)
        xw = jnp.where(row < nv, x_ref[0:n_rows, :], jnp.uint32(0))
        xb = _unpack_bf16_pairs(xw).astype(BF16)
        h = jnp.dot(xb, w1b[...], preferred_element_type=F32) + b1_ref[...]
        g = jnp.minimum(h[:, :dff], SWIGLU_LIMIT)
        lin = jnp.clip(h[:, dff:], -SWIGLU_LIMIT, SWIGLU_LIMIT)
        act = (g * jax.nn.sigmoid(SWIGLU_ALPHA * g) * (lin + 1.0)).astype(BF16)
        y = jnp.dot(act, w2b[...], preferred_element_type=F32) + b2_ref[...]
        o_ref[0:n_rows, :] = _pack_bf16_pairs(y)

    pl.when(nv > half)(functools.partial(mlp, MOE_BLOCK))
    pl.when((nv > 0) & (nv <= half))(functools.partial(mlp, half))


def moe_mlp(xp, block_expert, n_valid, n_used, seg_slot, next_expert, layer, w1, b1, w2, b2):
    n_rows, dh = xp.shape
    _, n_e, d, two_dff = w1.shape
    dff = two_dff // 2
    n_blocks = n_rows // MOE_BLOCK
    row = lambda i, be, nv, nu, sg, nx: (jnp.minimum(i, nu[0] - 1), 0)
    bsel = lambda i, be, nv, nu, sg, nx: (layer, be[i], 0, 0)
    grid_spec = pltpu.PrefetchScalarGridSpec(
        num_scalar_prefetch=5,
        grid=(n_blocks,),
        in_specs=[
            pl.BlockSpec((MOE_BLOCK, dh), row),
            pl.BlockSpec(memory_space=pl.ANY),
            pl.BlockSpec((None, None, 1, two_dff), bsel),
            pl.BlockSpec(memory_space=pl.ANY),
            pl.BlockSpec((None, None, 1, d), bsel),
        ],
        out_specs=pl.BlockSpec((MOE_BLOCK, dh), row),
        scratch_shapes=[pltpu.VMEM((2, d, two_dff), F32), pltpu.VMEM((2, dff, d), F32),
                        pltpu.VMEM((d, two_dff), BF16), pltpu.VMEM((dff, d), BF16),
                        pltpu.SemaphoreType.DMA((2, 2))],
    )
    return pl.pallas_call(
        functools.partial(_moe_mlp_kernel, layer),
        out_shape=jax.ShapeDtypeStruct((n_rows, dh), U32),
        grid_spec=grid_spec,
        compiler_params=pltpu.CompilerParams(
            dimension_semantics=("arbitrary",), vmem_limit_bytes=VMEM_LIMIT),
        name="moe_mlp",
    )(block_expert, n_valid, n_used, seg_slot, next_expert, xp, w1, b1.reshape(b1.shape[0], n_e, 1, two_dff),
      w2, b2.reshape(b2.shape[0], n_e, 1, d))


def _combine_kernel(x1_ref, y0_ref, y1_ref, y2_ref, y3_ref, rg_ref, g2_ref, b2_ref, *rest):
    o_ref = rest[-1]
    rg = rg_ref[...]
    f = rg[:, 0:1] * _unpack_bf16_pairs(y0_ref[...])
    for k, y_ref in ((1, y1_ref), (2, y2_ref), (3, y3_ref)):
        f = f + rg[:, k:k + 1] * _unpack_bf16_pairs(y_ref[...])
    o_ref[...] = _ln(DN_ALPHA * x1_ref[...] + f, g2_ref[...], b2_ref[...])


def combine_block(x1, y4p, rg, g2, b2, part, partial_out):
    t, d = x1.shape
    tm = TOKEN_TILE
    ntp = t // tm // COMBINE_PARTS
    base = part * ntp
    tok = lambda w: pl.BlockSpec((tm, w), lambda i: (base + i, 0))
    y_specs = [pl.BlockSpec((tm, d // 2), functools.partial(lambda i, k: (k * ntp + i, 0), k=k))
               for k in range(TOP_K)]
    in_specs = [tok(d)] + y_specs + [tok(LANES), _const_spec((1, d)), _const_spec((1, d))]
    args = [x1, y4p, y4p, y4p, y4p, rg, g2.reshape(1, d), b2.reshape(1, d)]
    aliases = {}
    if partial_out is not None:
        in_specs.append(pl.BlockSpec(memory_space=pl.ANY))
        args.append(partial_out)
        aliases = {len(args) - 1: 0}
    return pl.pallas_call(
        _combine_kernel,
        out_shape=jax.ShapeDtypeStruct((t, d), F32),
        grid=(ntp,),
        in_specs=in_specs,
        out_specs=tok(d),
        input_output_aliases=aliases,
        compiler_params=pltpu.CompilerParams(
            dimension_semantics=("arbitrary",), vmem_limit_bytes=VMEM_LIMIT),
        name="moe_combine",
    )(*args)


def moe_block(x1, x1p, ri, rg, cnt, layer, w1, b1, w2, b2, g2, b2n):
    t, d = x1.shape
    n_assign = t * TOP_K
    n_blocks = -(-n_assign // MOE_BLOCK) + N_EXPERTS
    n_rows = n_blocks * MOE_BLOCK
    ar = jnp.arange(N_EXPERTS, dtype=jnp.int32)
    counts = cnt[:, 0].astype(jnp.int32)
    padded = (counts + MOE_BLOCK - 1) // MOE_BLOCK * MOE_BLOCK
    pend = jnp.sum(jnp.where(ar[None, :] <= ar[:, None], padded[None, :], 0), axis=1)
    pstart = pend - padded
    top_i = ri[:TOP_K]
    dest = jnp.sum(jnp.where(top_i[:, :, None] == ar, pstart, 0), axis=-1) + ri[TOP_K:]
    n_used = (pend[-1] // MOE_BLOCK).reshape(1)
    block_start = jnp.arange(n_blocks, dtype=jnp.int32) * MOE_BLOCK
    block_expert = jnp.minimum(
        jnp.sum((pend[None, :] <= block_start[:, None]).astype(jnp.int32), axis=1), N_EXPERTS - 1)
    last = jnp.sum(jnp.where(jnp.arange(n_blocks) == n_used[0] - 1, block_expert, 0))
    block_expert = jnp.where(jnp.arange(n_blocks) < n_used[0], block_expert, last)
    vend = jnp.sum(jnp.where(block_expert[:, None] == ar, (pstart + counts)[None, :], 0), axis=1)
    n_valid = jnp.clip(vend - block_start, 0, MOE_BLOCK)
    blk = jnp.arange(n_blocks)
    starts = (blk < n_used[0]) & ((blk == 0) | (block_expert != jnp.roll(block_expert, 1)))
    seg_slot = jnp.where(starts, (jnp.cumsum(starts.astype(jnp.int32)) - 1) % 2, -1).astype(jnp.int32)
    later = (counts[None, :] > 0) & (ar[None, :] > ar[:, None])
    next_of = jnp.min(jnp.where(later, ar[None, :], N_EXPERTS), axis=1)
    next_of = jnp.where(next_of == N_EXPERTS, -1, next_of)
    next_expert = jnp.sum(jnp.where(block_expert[:, None] == ar, next_of[None, :], 0), axis=1).astype(jnp.int32)

    xp = sc_scatter_rows(x1p, dest, n_rows)
    yp = moe_mlp(xp, block_expert, n_valid, n_used, seg_slot, next_expert, layer, w1, b1, w2, b2)
    tp = t // COMBINE_PARTS
    out = None
    for part in range(COMBINE_PARTS):
        y4p = sc_gather_rows(yp, dest[:, part * tp:(part + 1) * tp].reshape(-1))
        out = combine_block(x1, y4p, rg, g2, b2n, part, out)
    return out


def kernel(x, a_w_in, a_b_in, a_ln_g, a_ln_b, a_w_s, a_b_s, a_w_out, a_b_out, b_w_in, b_conv_w, b_conv_b, b_dt_bias, b_a_log, b_d, b_norm_w, b_w_out, moe_w_router, moe_b_router, moe_w1, moe_b1, moe_w2, moe_b2, ln1_g, ln1_b, ln2_g, ln2_b):
    bsz, s_len, d = x.shape
    t = bsz * s_len
    xt = x.reshape(t, d)
    a_w_in_b, a_w_out_b = a_w_in.astype(BF16), a_w_out.astype(BF16)
    b_w_in_b, b_w_out_b = b_w_in.astype(BF16), b_w_out.astype(BF16)
    dt_col0 = b_w_in.shape[2] - SSM_HEADS
    for i in range(DEPTH):
        j = i // 2
        w_r32 = jnp.pad(moe_w_router[i], ((0, 0), (0, LANES - N_EXPERTS)))
        w_r_hi = w_r32.astype(BF16)
        w_r = jnp.stack([w_r_hi, (w_r32 - w_r_hi.astype(F32)).astype(BF16)])
        b_r = moe_b_router[i].reshape(N_EXPERTS, 1)
        if i % 2 == 0:
            x1, x1p, ri, rg, cnt = mixer_a_block(
                xt, j, a_w_in_b, a_b_in[j], a_ln_g[j], a_ln_b[j], a_w_s[j], a_b_s[j],
                a_w_out_b, a_b_out[j], ln1_g[i], ln1_b[i], w_r, b_r)
        else:
            x3 = xt.reshape(bsz, s_len, d)
            x1, x1p, ri, rg, cnt = mamba_block(
                x3, j, b_w_in_b, b_w_in[j, :, dt_col0:], b_conv_w[j], b_conv_b[j], b_dt_bias[j], b_a_log[j],
                b_d[j], b_norm_w[j], b_w_out_b, ln1_g[i], ln1_b[i], w_r, b_r)
            x1 = x1.reshape(t, d)
            x1p = x1p.reshape(t, d // 2)
            rg = rg.reshape(t, LANES)
        xt = moe_block(x1, x1p, ri, rg, cnt, i, moe_w1, moe_b1, moe_w2, moe_b2, ln2_g[i], ln2_b[i])
    return xt.reshape(bsz, s_len, d)
```
